```python
import math
import numpy as np
import jax
import jax.numpy as jnp
from jax import lax

D_MODEL = 1024
BATCH = 1
SEQ = 16384
DEPTH = 1

NSA_HEADS = 8
NSA_KV_GROUPS = 2
NSA_HPG = NSA_HEADS // NSA_KV_GROUPS
NSA_HEAD_DIM = 64
CMP_LEN = 32
CMP_STRIDE = 16
CMP_HIDDEN = 256
SEL_BLOCK = 64
SEL_TOP_N = 16
N_LOCAL_SEL = 2
WINDOW = 512
Q_BLOCK = 128
RET_HEADS = 4
RET_QK_DIM = 128
RET_V_DIM = 256
RET_CHUNK = 128
ROPE_BASE = 10000.0
REL_BUCKETS = 32
REL_MAX_DIST = 128
N_GROUPS = 4
EXPERTS_PER_GROUP = 8
N_EXPERTS = N_GROUPS * EXPERTS_PER_GROUP
EXPERT_TOP_K = 2
EXPERT_FF = 256
EPS = 1e-6

NSA_Q_W = NSA_HEADS * NSA_HEAD_DIM
NSA_KV_W = NSA_KV_GROUPS * NSA_HEAD_DIM
RET_QK_W = RET_HEADS * RET_QK_DIM
RET_V_W = RET_HEADS * RET_V_DIM
SPLITS = (NSA_Q_W, 6 * NSA_KV_W, 3 * NSA_HEADS, RET_QK_W, RET_QK_W, RET_V_W, RET_V_W, D_MODEL, D_MODEL)
IN_PROJ_W = NSA_Q_W + 6 * NSA_KV_W + 3 * NSA_HEADS + 2 * RET_QK_W + 2 * RET_V_W + 2 * D_MODEL

kernel_name = 'hybrid_nsa_retention_hmoe_block'


def rmsnorm(x, w):
    xf = x.astype(jnp.float32)
    y = xf * lax.rsqrt(jnp.mean(xf * xf, axis=-1, keepdims=True) + EPS)
    return (y * w.astype(jnp.float32)).astype(x.dtype)


def t5_bucket(dist):
    dist = jnp.maximum(dist, 0)
    max_exact = REL_BUCKETS // 2
    d_f = jnp.maximum(dist, 1).astype(jnp.float32)
    large = max_exact + (jnp.log(d_f / max_exact) / math.log(REL_MAX_DIST / max_exact)
                         * (REL_BUCKETS - max_exact)).astype(jnp.int32)
    large = jnp.minimum(large, REL_BUCKETS - 1)
    return jnp.where(dist < max_exact, dist, large)


def masked_softmax(s, mask, axis):
    s = jnp.where(mask, s, -1e30)
    m = jnp.max(s, axis=axis, keepdims=True)
    e = jnp.where(mask, jnp.exp(s - m), 0.0)
    return e / jnp.maximum(jnp.sum(e, axis=axis, keepdims=True), 1e-30)


def compress_tokens(kv, pos_emb, w1, w2):
    B, S, G, d = kv.shape
    n_cmp = (S - CMP_LEN) // CMP_STRIDE + 1
    idx = np.arange(n_cmp)[:, None] * CMP_STRIDE + np.arange(CMP_LEN)[None, :]
    blocks = kv[:, idx] + pos_emb[None, None, :, None, :]
    blocks = jnp.moveaxis(blocks, 3, 2).reshape(B, n_cmp, G, CMP_LEN * d)
    return jax.nn.gelu(blocks @ w1) @ w2


def nsa_attention(q, kv, gates, rel_bias, pos_k, k_w1, k_w2, pos_v, v_w1, v_w2):
    B, S = q.shape[0], q.shape[1]
    G, HPG, Dh = NSA_KV_GROUPS, NSA_HPG, NSA_HEAD_DIM
    f32 = jnp.float32
    scale = Dh ** -0.5
    k_cmp, v_cmp, k_sel, v_sel, k_win, v_win = [kv[:, :, j] for j in range(6)]
    kc = compress_tokens(k_cmp, pos_k, k_w1, k_w2)
    vc = compress_tokens(v_cmp, pos_v, v_w1, v_w2)
    n_cmp = kc.shape[1]
    cmp_end = jnp.arange(n_cmp) * CMP_STRIDE + (CMP_LEN - 1)
    n_sel = S // SEL_BLOCK
    top_n = min(SEL_TOP_N, n_sel)
    c_start = np.arange(n_cmp) * CMP_STRIDE
    b_start = np.arange(n_sel) * SEL_BLOCK
    overlap = jnp.asarray((c_start[:, None] <= b_start[None, :] + SEL_BLOCK - 1)
                          & (c_start[:, None] + CMP_LEN - 1 >= b_start[None, :]), dtype=f32)
    ks_blk = jnp.moveaxis(k_sel.reshape(B, n_sel, SEL_BLOCK, G, Dh), 3, 1)
    vs_blk = jnp.moveaxis(v_sel.reshape(B, n_sel, SEL_BLOCK, G, Dh), 3, 1)
    kw = jnp.pad(k_win, ((0, 0), (WINDOW, 0), (0, 0), (0, 0)))
    vw = jnp.pad(v_win, ((0, 0), (WINDOW, 0), (0, 0), (0, 0)))
    bias_g = rel_bias.reshape(REL_BUCKETS, G, HPG)
    b_ix = jnp.arange(B)[:, None, None, None]
    g_ix = jnp.arange(G)[None, None, :, None]
    blk_ids = jnp.arange(n_sel)
    n_qb = S // Q_BLOCK
    qb = jnp.moveaxis(q.reshape(B, n_qb, Q_BLOCK, G, HPG, Dh), 1, 0)
    gb = jnp.moveaxis(gates.reshape(B, n_qb, Q_BLOCK, 3, NSA_HEADS), 1, 0)

    def block(args):
        i, qi, gi = args
        start = i * Q_BLOCK
        t = start + jnp.arange(Q_BLOCK)
        s_c = jnp.einsum('bqgjd,bcgd->bqgjc', qi, kc).astype(f32) * scale
        dist_c = t[:, None] - cmp_end[None, :]
        s_c = s_c + jnp.transpose(bias_g[t5_bucket(dist_c)], (0, 2, 3, 1))[None]
        p_c = masked_softmax(s_c, (dist_c >= 0)[None, :, None, None, :], -1)
        o_c = jnp.einsum('bqgjc,bcgd->bqgjd', p_c.astype(vc.dtype), vc)
        imp = jnp.einsum('bqgc,cn->bqgn', p_c.sum(3), overlap)
        cur = t // SEL_BLOCK
        back = cur[:, None] - blk_ids[None, :]
        valid = back >= 0
        forced = (blk_ids[None, :] == 0) | (valid & (back < N_LOCAL_SEL))
        imp = jnp.where(forced[None, :, None, :], 1e9, jnp.where(valid[None, :, None, :], imp, -1e9))
        _, sel = lax.top_k(imp, top_n)
        k_g = ks_blk[b_ix, g_ix, sel]
        v_g = vs_blk[b_ix, g_ix, sel]
        pos_s = sel[..., None] * SEL_BLOCK + jnp.arange(SEL_BLOCK)
        dist_s = t[None, :, None, None, None] - pos_s
        s_s = jnp.einsum('bqgjd,bqgnkd->bqgjnk', qi, k_g).astype(f32) * scale
        bias_s = jnp.moveaxis(bias_g[t5_bucket(dist_s), g_ix[..., None]], -1, 3)
        p_s = masked_softmax(s_s + bias_s, (dist_s >= 0)[:, :, :, None], (-2, -1))
        o_s = jnp.einsum('bqgjnk,bqgnkd->bqgjd', p_s.astype(v_g.dtype), v_g)
        kw_i = lax.dynamic_slice_in_dim(kw, start, WINDOW + Q_BLOCK, axis=1)
        vw_i = lax.dynamic_slice_in_dim(vw, start, WINDOW + Q_BLOCK, axis=1)
        pos_w = start - WINDOW + jnp.arange(WINDOW + Q_BLOCK)
        dist_w = t[:, None] - pos_w[None, :]
        mask_w = (dist_w >= 0) & (dist_w < WINDOW) & (pos_w[None, :] >= 0)
        s_w = jnp.einsum('bqgjd,bkgd->bqgjk', qi, kw_i).astype(f32) * scale
        s_w = s_w + jnp.transpose(bias_g[t5_bucket(dist_w)], (0, 2, 3, 1))[None]
        p_w = masked_softmax(s_w, mask_w[None, :, None, None, :], -1)
        o_w = jnp.einsum('bqgjk,bkgd->bqgjd', p_w.astype(vw_i.dtype), vw_i)
        gi = gi.reshape(B, Q_BLOCK, 3, G, HPG)[..., None]
        o = gi[:, :, 0] * o_c + gi[:, :, 1] * o_s + gi[:, :, 2] * o_w
        return o.reshape(B, Q_BLOCK, NSA_HEADS * Dh)

    out = lax.map(block, (jnp.arange(n_qb), qb, gb))
    return jnp.moveaxis(out, 0, 1).reshape(B, S, NSA_HEADS * Dh)


def rotary(x, pos):
    half = x.shape[-1] // 2
    inv = ROPE_BASE ** (-jnp.arange(half, dtype=jnp.float32) / half)
    ang = pos[:, None].astype(jnp.float32) * inv[None, :]
    cos = jnp.cos(ang)[None, :, None, :]
    sin = jnp.sin(ang)[None, :, None, :]
    x1 = x[..., :half].astype(jnp.float32)
    x2 = x[..., half:].astype(jnp.float32)
    return jnp.concatenate([x1 * cos - x2 * sin, x1 * sin + x2 * cos], axis=-1).astype(x.dtype)


def retention(q, k, v, g, norm_w):
    B, S, H, dk = q.shape
    dv = v.shape[-1]
    C = RET_CHUNK
    nc = S // C
    f32 = jnp.float32
    pos = jnp.arange(S)
    q = rotary(q, pos)
    k = rotary(k, pos) * (dk ** -0.5)
    log_gamma = jnp.log(1.0 - 2.0 ** (-5.0 - jnp.arange(H, dtype=f32)))
    idx = jnp.arange(C, dtype=f32)
    diff = idx[:, None] - idx[None, :]
    decay_mask = jnp.where(diff >= 0, jnp.exp(jnp.maximum(diff, 0.0)[None] * log_gamma[:, None, None]), 0.0)
    q_decay = jnp.exp((idx + 1.0)[None, :] * log_gamma[:, None])[None, :, :, None]
    k_decay = jnp.exp((C - 1.0 - idx)[None, :] * log_gamma[:, None])[None, :, :, None]
    chunk_decay = jnp.exp(C * log_gamma)[None, :, None, None]

    def to_chunks(a):
        return a.reshape(B, nc, C, H, a.shape[-1]).transpose(1, 0, 3, 2, 4).astype(f32)

    def step(R, inp):
        qc, kc, vc = inp
        inner = jnp.einsum('bhid,bhjd->bhij', qc, kc) * decay_mask[None]
        y = jnp.einsum('bhij,bhjv->bhiv', inner, vc) + jnp.einsum('bhid,bhdv->bhiv', qc, R) * q_decay
        R = R * chunk_decay + jnp.einsum('bhjd,bhjv->bhdv', kc * k_decay, vc)
        return R, y

    R0 = jnp.zeros((B, H, dk, dv), f32)
    _, ys = lax.scan(step, R0, (to_chunks(q), to_chunks(k), to_chunks(v)))
    y = ys.transpose(1, 0, 3, 2, 4).reshape(B, S, H, dv)
    mu = jnp.mean(y, axis=-1, keepdims=True)
    var = jnp.mean(jnp.square(y - mu), axis=-1, keepdims=True)
    y = ((y - mu) * lax.rsqrt(var + EPS)).reshape(B, S, H * dv) * norm_w.astype(f32)
    return (jax.nn.silu(g.astype(f32)) * y).astype(g.dtype)


def hierarchical_moe(h, w_rg, b_rg, w_re, b_re, w_gate, w_up, w_down):
    B, S, _ = h.shape
    f32 = jnp.float32
    lg = (h @ w_rg).astype(f32) + b_rg.astype(f32)
    pg = jax.nn.softmax(lg, axis=-1)
    p_grp, grp = lax.top_k(pg, 1)
    le = ((h @ w_re).astype(f32) + b_re.astype(f32)).reshape(B, S, N_GROUPS, EXPERTS_PER_GROUP)
    le_g = jnp.take_along_axis(le, grp[..., None], axis=2)[:, :, 0]
    top_l, top_i = lax.top_k(le_g, EXPERT_TOP_K)
    w_sel = jax.nn.softmax(top_l, axis=-1) * p_grp
    e_idx = grp * EXPERTS_PER_GROUP + top_i
    combine = jnp.sum(jax.nn.one_hot(e_idx, N_EXPERTS, dtype=f32) * w_sel[..., None], axis=-2)

    def expert(acc, inp):
        wg, wu, wd, c = inp
        hid = jax.nn.silu(h @ wg) * (h @ wu)
        return acc + (hid @ wd) * c[..., None].astype(h.dtype), None

    out, _ = lax.scan(expert, jnp.zeros_like(h), (w_gate, w_up, w_down, jnp.moveaxis(combine, -1, 0)))
    return out


def setup_inputs(seed: int = 0) -> dict:
    key = jax.random.key(seed)
    ks = jax.random.split(key, 24)
    f32 = jnp.float32

    def nrm(k, shape, scale):
        return jax.random.normal(k, shape, f32) * scale

    D = D_MODEL
    return {
        'x': nrm(ks[0], (BATCH, SEQ, D), 1.0),
        'rel_bias': nrm(ks[1], (REL_BUCKETS, NSA_HEADS), 0.5),
        'norm_mix': 1.0 + nrm(ks[2], (DEPTH, D), 0.01),
        'w_in': nrm(ks[3], (DEPTH, D, IN_PROJ_W), D ** -0.5),
        'cmp_pos_k': nrm(ks[4], (DEPTH, CMP_LEN, NSA_HEAD_DIM), 0.1),
        'cmp_k_w1': nrm(ks[5], (DEPTH, CMP_LEN * NSA_HEAD_DIM, CMP_HIDDEN), (CMP_LEN * NSA_HEAD_DIM) ** -0.5),
        'cmp_k_w2': nrm(ks[6], (DEPTH, CMP_HIDDEN, NSA_HEAD_DIM), CMP_HIDDEN ** -0.5),
        'cmp_pos_v': nrm(ks[7], (DEPTH, CMP_LEN, NSA_HEAD_DIM), 0.1),
        'cmp_v_w1': nrm(ks[8], (DEPTH, CMP_LEN * NSA_HEAD_DIM, CMP_HIDDEN), (CMP_LEN * NSA_HEAD_DIM) ** -0.5),
        'cmp_v_w2': nrm(ks[9], (DEPTH, CMP_HIDDEN, NSA_HEAD_DIM), CMP_HIDDEN ** -0.5),
        'ret_norm': 1.0 + nrm(ks[10], (DEPTH, RET_V_W), 0.01),
        'w_nsa_up': nrm(ks[11], (DEPTH, NSA_Q_W, D), NSA_Q_W ** -0.5),
        'w_ret_up': nrm(ks[12], (DEPTH, RET_V_W, D), RET_V_W ** -0.5),
        'w_out': nrm(ks[13], (DEPTH, D, D), D ** -0.5),
        'norm_ffn': 1.0 + nrm(ks[14], (DEPTH, D), 0.01),
        'w_router_group': nrm(ks[15], (DEPTH, D, N_GROUPS), D ** -0.5),
        'b_router_group': nrm(ks[16], (DEPTH, N_GROUPS), 0.01),
        'w_router_expert': nrm(ks[17], (DEPTH, D, N_EXPERTS), D ** -0.5),
        'b_router_expert': nrm(ks[18], (DEPTH, N_EXPERTS), 0.01),
        'moe_w_gate': nrm(ks[19], (DEPTH, N_EXPERTS, D, EXPERT_FF), D ** -0.5),
        'moe_w_up': nrm(ks[20], (DEPTH, N_EXPERTS, D, EXPERT_FF), D ** -0.5),
        'moe_w_down': nrm(ks[21], (DEPTH, N_EXPERTS, EXPERT_FF, D), EXPERT_FF ** -0.5),
        'norm_final': 1.0 + nrm(ks[22], (D,), 0.01),
    }


def reference(x, rel_bias, norm_mix, w_in, cmp_pos_k, cmp_k_w1, cmp_k_w2, cmp_pos_v, cmp_v_w1, cmp_v_w2,
              ret_norm, w_nsa_up, w_ret_up, w_out, norm_ffn, w_router_group, b_router_group,
              w_router_expert, b_router_expert, moe_w_gate, moe_w_up, moe_w_down, norm_final):
    B, S, _ = x.shape
    split_idx = np.cumsum(SPLITS)[:-1].tolist()
    for l in range(DEPTH):
        h = rmsnorm(x, norm_mix[l])
        proj = h @ w_in[l]
        q_a, kv_a, g_a, q_r, k_r, v_r, g_r, gate_a, gate_r = jnp.split(proj, split_idx, axis=-1)
        y_a = nsa_attention(q_a.reshape(B, S, NSA_HEADS, NSA_HEAD_DIM),
                            kv_a.reshape(B, S, 6, NSA_KV_GROUPS, NSA_HEAD_DIM),
                            jax.nn.sigmoid(g_a.reshape(B, S, 3, NSA_HEADS)),
                            rel_bias, cmp_pos_k[l], cmp_k_w1[l], cmp_k_w2[l],
                            cmp_pos_v[l], cmp_v_w1[l], cmp_v_w2[l])
        y_r = retention(q_r.reshape(B, S, RET_HEADS, RET_QK_DIM),
                        k_r.reshape(B, S, RET_HEADS, RET_QK_DIM),
                        v_r.reshape(B, S, RET_HEADS, RET_V_DIM), g_r, ret_norm[l])
        merged = jax.nn.sigmoid(gate_a) * (y_a @ w_nsa_up[l]) + jax.nn.sigmoid(gate_r) * (y_r @ w_ret_up[l])
        x = x + merged @ w_out[l]
        h = rmsnorm(x, norm_ffn[l])
        x = x + hierarchical_moe(h, w_router_group[l], b_router_group[l], w_router_expert[l],
                                 b_router_expert[l], moe_w_gate[l], moe_w_up[l], moe_w_down[l])
    return rmsnorm(x, norm_final)
```

```python
import functools
import math

import numpy as np
import jax
import jax.numpy as jnp
from jax import lax
from jax.experimental import pallas as pl
from jax.experimental.pallas import tpu as pltpu

F32 = jnp.float32
BF16 = jnp.bfloat16

D_MODEL = 1024
NSA_HEADS = 8
NSA_KV_GROUPS = 2
NSA_HPG = NSA_HEADS // NSA_KV_GROUPS
NSA_HEAD_DIM = 64
CMP_LEN = 32
CMP_STRIDE = 16
CMP_HIDDEN = 256
SEL_BLOCK = 64
SEL_TOP_N = 16
N_LOCAL_SEL = 2
WINDOW = 512
Q_BLOCK = 128
RET_HEADS = 4
RET_QK_DIM = 128
RET_V_DIM = 256
RET_CHUNK = 128
ROPE_BASE = 10000.0
REL_BUCKETS = 32
REL_MAX_DIST = 128
N_GROUPS = 4
EXPERTS_PER_GROUP = 8
N_EXPERTS = N_GROUPS * EXPERTS_PER_GROUP
EXPERT_FF = 256
EPS = 1e-6

NSA_Q_W = NSA_HEADS * NSA_HEAD_DIM
NSA_KV_W = NSA_KV_GROUPS * NSA_HEAD_DIM
RET_QK_W = RET_HEADS * RET_QK_DIM
RET_V_W = RET_HEADS * RET_V_DIM

LANE = 128
NEG = -1e30
VMEM_LIMIT = 56 * 1024 * 1024

CMP_PAD = 120
CMP_NEAR = 128
SEL_TK = 512
WIN_KEYS = WINDOW + Q_BLOCK


def _dot(a, b):
    return jnp.dot(a, b, preferred_element_type=F32)


def _dot_nt(a, b):
    return lax.dot_general(a, b, (((1,), (1,)), ((), ())), preferred_element_type=F32)


def _const_spec(shape):
    nd = len(shape)
    return pl.BlockSpec(shape, lambda *_: (0,) * nd, pipeline_mode=pl.Buffered(1))


def _params(*sem):
    return pltpu.CompilerParams(dimension_semantics=sem, vmem_limit_bytes=VMEM_LIMIT)


_C_QA = 0
_C_KV = _C_QA + NSA_Q_W
_C_QR = _C_KV + 6 * NSA_KV_W
_C_KR = _C_QR + RET_QK_W
_C_VR = _C_KR + RET_QK_W
_C_GR = _C_VR + RET_V_W
_C_GTA = _C_GR + RET_V_W
_C_GTR = _C_GTA + D_MODEL
_C_GA = _C_GTR + D_MODEL
_C_END = _C_GA + LANE
IN_TM = 256


def _inproj_kernel(x_ref, nw_ref, w_ref, cos_ref, sin_ref,
                   qa_ref, kcmp_ref, vcmp_ref, ksel_ref, vsel_ref, kwin_ref, vwin_ref, ga_ref,
                   qr_ref, kr_ref, vr_ref, gr_ref, gta_ref, gtr_ref):
    x = x_ref[...]
    h = x * lax.rsqrt(jnp.mean(x * x, axis=-1, keepdims=True) + EPS) * nw_ref[...]
    hb = h.astype(BF16)

    def proj(a, width):
        return _dot(hb, w_ref[:, a:a + width])

    qa_ref[...] = (proj(_C_QA, NSA_Q_W) * (NSA_HEAD_DIM ** -0.5)).astype(BF16)
    kv = proj(_C_KV, 6 * NSA_KV_W)
    kcmp_ref[...] = kv[:, 0 * LANE:1 * LANE]
    vcmp_ref[...] = kv[:, 1 * LANE:2 * LANE]
    ksel_ref[...] = kv[:, 2 * LANE:3 * LANE].astype(BF16)
    vsel_ref[...] = kv[:, 3 * LANE:4 * LANE].astype(BF16)
    kwin_ref[...] = kv[:, 4 * LANE:5 * LANE].astype(BF16)
    vwin_ref[...] = kv[:, 5 * LANE:6 * LANE].astype(BF16)
    ga_ref[...] = jax.nn.sigmoid(proj(_C_GA, LANE))

    cos2 = cos_ref[...]
    sin2 = sin_ref[...]
    qr = proj(_C_QR, RET_QK_W)
    kr = proj(_C_KR, RET_QK_W)
    for hd in range(RET_HEADS):
        sl = slice(hd * RET_QK_DIM, (hd + 1) * RET_QK_DIM)
        qh = qr[:, sl]
        kh = kr[:, sl]
        qr_ref[:, sl] = (qh * cos2 + pltpu.roll(qh, RET_QK_DIM // 2, 1) * sin2).astype(BF16)
        kr_ref[:, sl] = (kh * cos2 + pltpu.roll(kh, RET_QK_DIM // 2, 1) * sin2) * (RET_QK_DIM ** -0.5)
    vr_ref[...] = proj(_C_VR, RET_V_W).astype(BF16)
    gr_ref[...] = proj(_C_GR, RET_V_W)
    gta_ref[...] = jax.nn.sigmoid(proj(_C_GTA, D_MODEL))
    gtr_ref[...] = jax.nn.sigmoid(proj(_C_GTR, D_MODEL))


def _input_projection(x2d, norm_w, w_cat, cos2, sin2):
    S = x2d.shape[0]
    tm = IN_TM
    row = lambda w: pl.BlockSpec((tm, w), lambda i: (i, 0))
    outs = [
        (NSA_Q_W, BF16),
        (LANE, F32), (LANE, F32),
        (LANE, BF16), (LANE, BF16),
        (LANE, BF16), (LANE, BF16),
        (LANE, F32),
        (RET_QK_W, BF16), (RET_QK_W, F32),
        (RET_V_W, BF16), (RET_V_W, F32),
        (D_MODEL, F32), (D_MODEL, F32),
    ]
    return pl.pallas_call(
        _inproj_kernel,
        grid=(S // tm,),
        in_specs=[row(D_MODEL), _const_spec((1, D_MODEL)), _const_spec((D_MODEL, _C_END)),
                  row(LANE), row(LANE)],
        out_specs=[row(w) for w, _ in outs],
        out_shape=[jax.ShapeDtypeStruct((S, w), dt) for w, dt in outs],
        compiler_params=_params("parallel"),
        name="in_proj",
    )(x2d, norm_w, w_cat, cos2, sin2)


def _compress_kernel(xa_ref, xb_ref, pos_ref, w1_ref, w2_ref, out_ref):
    half = CMP_STRIDE * NSA_HEAD_DIM
    a = (xa_ref[0, 0] + pos_ref[0, 0:1, :]).astype(BF16)
    b = (xb_ref[0, 0] + pos_ref[0, 1:2, :]).astype(BF16)
    hid = _dot(a, w1_ref[0, :half, :]) + _dot(b, w1_ref[0, half:, :])
    out_ref[0, 0] = _dot(jax.nn.gelu(hid).astype(BF16), w2_ref[0])


def _compress(xa, xb, pos, w1, w2):
    _, G, nch, width = xa.shape
    xspec = pl.BlockSpec((1, 1, nch, width), lambda t, g: (t, g, 0, 0))
    return pl.pallas_call(
        _compress_kernel,
        grid=(2, G),
        in_specs=[xspec, xspec,
                  pl.BlockSpec((1, 2, width), lambda t, g: (t, 0, 0)),
                  pl.BlockSpec((1, 2 * width, CMP_HIDDEN), lambda t, g: (t, 0, 0)),
                  pl.BlockSpec((1, CMP_HIDDEN, NSA_HEAD_DIM), lambda t, g: (t, 0, 0))],
        out_specs=pl.BlockSpec((1, 1, nch, NSA_HEAD_DIM), lambda t, g: (t, g, 0, 0)),
        out_shape=jax.ShapeDtypeStruct((2, G, nch, NSA_HEAD_DIM), F32),
        compiler_params=_params("parallel", "parallel"),
        name="nsa_compress",
    )(xa, xb, pos, w1, w2)


def _t5_bucket_np(dist):
    dist = np.maximum(dist, 0)
    max_exact = REL_BUCKETS // 2
    d_f = np.maximum(dist, 1).astype(np.float32)
    large = max_exact + (np.log(d_f / np.float32(max_exact)) / np.float32(math.log(REL_MAX_DIST / max_exact))
                         * np.float32(REL_BUCKETS - max_exact)).astype(np.int32)
    large = np.minimum(large, REL_BUCKETS - 1)
    return np.where(dist < max_exact, dist, large)


def _bias_table(rel_bias, dist):
    tab = jnp.transpose(rel_bias[_t5_bucket_np(dist)], (2, 0, 1))
    return jnp.where(jnp.asarray(dist >= 0)[None], tab, NEG).astype(F32)


def _nsa_kernel(n_keys,
                q_ref, ga_ref, kcf_ref, vcf_ref, ovf_ref, kcp_ref, vcp_ref, ovp_ref, tcmp_ref, b31_ref,
                kselT_ref, vsel_ref, tnear_ref, kwinT_ref, vwin_ref, twin_ref, out_ref):
    i = pl.program_id(0)
    Q4 = NSA_HPG * Q_BLOCK
    n_sel = n_keys // SEL_BLOCK
    n_cmp_pad = n_keys // CMP_STRIDE

    def stack4(x):
        return jnp.concatenate([x] * NSA_HPG, axis=0)

    def head_sum(x):
        return sum(x[j * Q_BLOCK:(j + 1) * Q_BLOCK] for j in range(NSA_HPG))

    for g in range(NSA_KV_GROUPS):
        hs = slice(NSA_HPG * g, NSA_HPG * (g + 1))
        vcols = slice(NSA_HEAD_DIM * g, NSA_HEAD_DIM * (g + 1))
        q = q_ref[hs].reshape(Q4, NSA_HEAD_DIM)
        b31 = b31_ref[g][:, 0:1]

        c0 = pl.multiple_of(i * (Q_BLOCK // CMP_STRIDE), 8)
        kc_far = kcf_ref[g]
        kc_near = kcp_ref[g, pl.ds(c0, CMP_NEAR), :].astype(BF16)
        s_f = _dot_nt(q, kc_far) + b31
        c_far = lax.broadcasted_iota(jnp.int32, (Q4, n_cmp_pad), 1)
        mask_f = c_far < c0 - CMP_PAD
        s_f = jnp.where(mask_f, s_f, NEG)
        tab = tcmp_ref[hs].reshape(Q4, CMP_NEAR)
        c_near = lax.broadcasted_iota(jnp.int32, (Q4, CMP_NEAR), 1)
        mask_n = (tab > 0.5 * NEG) & (c_near + c0 >= CMP_PAD)
        s_n = jnp.where(mask_n, _dot_nt(q, kc_near) + tab, NEG)
        m = jnp.maximum(jnp.max(s_f, axis=-1, keepdims=True), jnp.max(s_n, axis=-1, keepdims=True))
        e_f = jnp.where(mask_f, jnp.exp(s_f - m), 0.0)
        e_n = jnp.where(mask_n, jnp.exp(s_n - m), 0.0)
        l = jnp.sum(e_f, axis=-1, keepdims=True) + jnp.sum(e_n, axis=-1, keepdims=True)
        inv = 1.0 / jnp.maximum(l, 1e-30)
        p_f = e_f * inv
        p_n = e_n * inv
        o_c = (_dot(p_f.astype(BF16), vcf_ref[...])
               + _dot(p_n.astype(BF16), vcp_ref[pl.ds(c0, CMP_NEAR), :].astype(BF16)))[:, vcols]

        imp = (_dot(head_sum(p_f).astype(BF16), ovf_ref[...])
               + _dot(head_sum(p_n).astype(BF16), ovp_ref[pl.ds(c0, CMP_NEAR), :].astype(BF16)))
        r_io = lax.broadcasted_iota(jnp.int32, (Q_BLOCK, n_sel), 0)
        n_io = lax.broadcasted_iota(jnp.int32, (Q_BLOCK, n_sel), 1)
        back = (i * Q_BLOCK + r_io) // SEL_BLOCK - n_io
        valid = back >= 0
        forced = (n_io == 0) | (valid & (back < N_LOCAL_SEL))
        imp = jnp.where(forced, 1e9, jnp.where(valid, imp, -1e9))
        n_f = n_io.astype(F32)
        sel = jnp.zeros((Q_BLOCK, n_sel), F32)
        for _ in range(min(SEL_TOP_N, n_sel)):
            mx = jnp.max(imp, axis=-1, keepdims=True)
            first = jnp.min(jnp.where(imp == mx, n_f, float(n_sel)), axis=-1, keepdims=True)
            pick = n_f == first
            sel = jnp.where(pick, 1.0, sel)
            imp = jnp.where(pick, -jnp.inf, imp)
        sel_b = sel.astype(BF16)

        far_end = (i - 1) * Q_BLOCK

        def key_mask(k0, width):
            pos = k0 + lax.broadcasted_iota(jnp.int32, (n_sel, width), 1)
            blk = lax.broadcasted_iota(jnp.int32, (n_sel, width), 0)
            expand = jnp.where((blk * SEL_BLOCK <= pos) & (pos < (blk + 1) * SEL_BLOCK), 1.0, 0.0)
            return _dot(sel_b, expand.astype(BF16))

        def flash_step(carry, s, v):
            m_i, l_i, acc = carry
            m_new = jnp.maximum(m_i, jnp.max(s, axis=-1, keepdims=True))
            alpha = jnp.exp(m_i - m_new)
            p = jnp.exp(s - m_new)
            l_new = alpha * l_i + jnp.sum(p, axis=-1, keepdims=True)
            acc_new = alpha * acc + _dot(p.astype(BF16), v)
            return m_new, l_new, acc_new

        def far_body(kt, carry):
            k0 = pl.multiple_of(kt * SEL_TK, SEL_TK)
            kT = kselT_ref[g, :, pl.ds(k0 + Q_BLOCK, SEL_TK)]
            v = vsel_ref[pl.ds(k0 + Q_BLOCK, SEL_TK), :]
            pos = k0 + lax.broadcasted_iota(jnp.int32, (Q_BLOCK, SEL_TK), 1)
            msk = stack4(jnp.where(pos < far_end, key_mask(k0, SEL_TK), 0.0)) > 0.5
            s = jnp.where(msk, _dot(q, kT) + b31, NEG)
            return flash_step(carry, s, v)

        n_far = (jnp.maximum(far_end, 0) + SEL_TK - 1) // SEL_TK
        carry = (jnp.full((Q4, 1), NEG, F32), jnp.zeros((Q4, 1), F32), jnp.zeros((Q4, LANE), F32))
        carry = lax.fori_loop(0, n_far, far_body, carry)
        k0n = pl.multiple_of(i * Q_BLOCK, Q_BLOCK)
        kT = kselT_ref[g, :, pl.ds(k0n, 2 * Q_BLOCK)]
        v = vsel_ref[pl.ds(k0n, 2 * Q_BLOCK), :]
        tab = tnear_ref[hs].reshape(Q4, 2 * Q_BLOCK)
        msk = (stack4(key_mask(far_end, 2 * Q_BLOCK)) > 0.5) & (tab > 0.5 * NEG)
        s = jnp.where(msk, _dot(q, kT) + tab, NEG)
        m_s, l_s, acc_s = flash_step(carry, s, v)
        o_s = (acc_s / jnp.maximum(l_s, 1e-30))[:, vcols]

        kT = kwinT_ref[g, :, pl.ds(k0n, WIN_KEYS)]
        v = vwin_ref[pl.ds(k0n, WIN_KEYS), :]
        tab = twin_ref[hs].reshape(Q4, WIN_KEYS)
        pos = lax.broadcasted_iota(jnp.int32, (Q4, WIN_KEYS), 1) + (i * Q_BLOCK - WINDOW)
        msk = (tab > 0.5 * NEG) & (pos >= 0)
        s = jnp.where(msk, _dot(q, kT) + tab, NEG)
        m_w = jnp.max(s, axis=-1, keepdims=True)
        e = jnp.where(msk, jnp.exp(s - m_w), 0.0)
        p = e / jnp.maximum(jnp.sum(e, axis=-1, keepdims=True), 1e-30)
        o_w = _dot(p.astype(BF16), v)[:, vcols]

        gates = ga_ref[...]
        for j in range(NSA_HPG):
            hd = NSA_HPG * g + j
            rows = slice(j * Q_BLOCK, (j + 1) * Q_BLOCK)
            o = (gates[:, hd:hd + 1] * o_c[rows]
                 + gates[:, NSA_HEADS + hd:NSA_HEADS + hd + 1] * o_s[rows]
                 + gates[:, 2 * NSA_HEADS + hd:2 * NSA_HEADS + hd + 1] * o_w[rows])
            out_ref[hd] = o.astype(out_ref.dtype)


def _nsa_attention(qh, gates, kcf, vcf, ovf, kcp, vcp, ovp, tcmp, b31, kselT, vsel, tnear, kwinT, vwin, twin):
    H, S, dh = qh.shape
    n_qb = S // Q_BLOCK
    full = lambda a: _const_spec(a.shape)
    return pl.pallas_call(
        functools.partial(_nsa_kernel, S),
        grid=(n_qb,),
        in_specs=[pl.BlockSpec((H, Q_BLOCK, dh), lambda i: (0, i, 0)),
                  pl.BlockSpec((Q_BLOCK, LANE), lambda i: (i, 0)),
                  full(kcf), full(vcf), full(ovf), full(kcp), full(vcp), full(ovp), full(tcmp), full(b31),
                  full(kselT), full(vsel), full(tnear), full(kwinT), full(vwin), full(twin)],
        out_specs=pl.BlockSpec((H, Q_BLOCK, dh), lambda i: (0, i, 0)),
        out_shape=jax.ShapeDtypeStruct((H, S, dh), BF16),
        compiler_params=_params("parallel"),
        name="nsa_attention",
    )(qh, gates, kcf, vcf, ovf, kcp, vcp, ovp, tcmp, b31, kselT, vsel, tnear, kwinT, vwin, twin)


def _retention_kernel(q_ref, k_ref, v_ref, g_ref, nw_ref, dmask_ref, qd_ref, kd_ref, cd_ref, out_ref, state_ref):
    @pl.when(pl.program_id(0) == 0)
    def _():
        state_ref[...] = jnp.zeros_like(state_ref)

    for hd in range(RET_HEADS):
        ks = slice(hd * RET_QK_DIM, (hd + 1) * RET_QK_DIM)
        vs = slice(hd * RET_V_DIM, (hd + 1) * RET_V_DIM)
        q = q_ref[:, ks]
        k = k_ref[:, ks]
        v = v_ref[:, vs]
        state = state_ref[hd]
        inner = _dot_nt(q, k.astype(BF16)) * dmask_ref[hd]
        y = _dot(inner.astype(BF16), v) + _dot(q, state.astype(BF16)) * qd_ref[hd]
        kd = k * kd_ref[hd]
        state_ref[hd] = state * cd_ref[hd] + _dot(kd.T.astype(BF16), v)
        mu = jnp.mean(y, axis=-1, keepdims=True)
        yc = y - mu
        var = jnp.mean(yc * yc, axis=-1, keepdims=True)
        yn = yc * lax.rsqrt(var + EPS) * nw_ref[:, vs]
        out_ref[:, vs] = (jax.nn.silu(g_ref[:, vs]) * yn).astype(out_ref.dtype)


def _retention(q, k, v, g, norm_w, dmask, qd, kd, cd):
    S = q.shape[0]
    C = RET_CHUNK
    row = lambda w: pl.BlockSpec((C, w), lambda i: (i, 0))
    full = lambda a: _const_spec(a.shape)
    return pl.pallas_call(
        _retention_kernel,
        grid=(S // C,),
        in_specs=[row(RET_QK_W), row(RET_QK_W), row(RET_V_W), row(RET_V_W),
                  full(norm_w), full(dmask), full(qd), full(kd), full(cd)],
        out_specs=row(RET_V_W),
        out_shape=jax.ShapeDtypeStruct((S, RET_V_W), BF16),
        scratch_shapes=[pltpu.VMEM((RET_HEADS, RET_QK_DIM, RET_V_DIM), F32)],
        compiler_params=_params("arbitrary"),
        name="retention",
    )(q, k, v, g, norm_w, dmask, qd, kd, cd)


MERGE_TM = 256


def _merge_kernel(x_ref, ya_ref, yr_ref, gta_ref, gtr_ref, wa_ref, wr_ref, wo_ref, nw_ref, wrt_ref, brt_ref,
                  x1_ref, h_ref, comb_ref):
    merged = gta_ref[...] * _dot(ya_ref[...], wa_ref[...]) + gtr_ref[...] * _dot(yr_ref[...], wr_ref[...])
    x1 = x_ref[...] + _dot(merged.astype(BF16), wo_ref[...])
    x1_ref[...] = x1
    h = (x1 * lax.rsqrt(jnp.mean(x1 * x1, axis=-1, keepdims=True) + EPS) * nw_ref[...]).astype(BF16)
    h_ref[...] = h
    logits = _dot(h, wrt_ref[...]) + brt_ref[...]
    lane = lax.broadcasted_iota(jnp.int32, logits.shape, 1)
    lane_f = lane.astype(F32)
    big = float(LANE)
    is_grp = (lane >= N_EXPERTS) & (lane < N_EXPERTS + N_GROUPS)
    lg = jnp.where(is_grp, logits, -jnp.inf)
    eg = jnp.where(is_grp, jnp.exp(lg - jnp.max(lg, axis=-1, keepdims=True)), 0.0)
    pg = eg / jnp.sum(eg, axis=-1, keepdims=True)
    p_grp = jnp.max(pg, axis=-1, keepdims=True)
    grp = jnp.min(jnp.where(is_grp & (pg == p_grp), lane_f, big), axis=-1, keepdims=True) - N_EXPERTS
    in_grp = (lane_f >= grp * EXPERTS_PER_GROUP) & (lane_f < (grp + 1.0) * EXPERTS_PER_GROUP)
    le = jnp.where(in_grp, logits, -jnp.inf)
    m1 = jnp.max(le, axis=-1, keepdims=True)
    i1 = jnp.min(jnp.where(le == m1, lane_f, big), axis=-1, keepdims=True)
    le2 = jnp.where(lane_f == i1, -jnp.inf, le)
    m2 = jnp.max(le2, axis=-1, keepdims=True)
    i2 = jnp.min(jnp.where(le2 == m2, lane_f, big), axis=-1, keepdims=True)
    e2 = jnp.exp(m2 - m1)
    den = 1.0 + e2
    comb_ref[...] = (jnp.where(lane_f == i1, (1.0 / den) * p_grp, 0.0)
                     + jnp.where(lane_f == i2, (e2 / den) * p_grp, 0.0))


def _merge(x2d, ya, yr, gta, gtr, wa, wr, wo, nw, wrt, brt):
    S = x2d.shape[0]
    tm = MERGE_TM
    row = lambda w: pl.BlockSpec((tm, w), lambda i: (i, 0))
    full = lambda a: _const_spec(a.shape)
    return pl.pallas_call(
        _merge_kernel,
        grid=(S // tm,),
        in_specs=[row(D_MODEL), row(NSA_Q_W), row(RET_V_W), row(D_MODEL), row(D_MODEL),
                  full(wa), full(wr), full(wo), full(nw), full(wrt), full(brt)],
        out_specs=[row(D_MODEL), row(D_MODEL), row(LANE)],
        out_shape=[jax.ShapeDtypeStruct((S, D_MODEL), F32), jax.ShapeDtypeStruct((S, D_MODEL), BF16),
                   jax.ShapeDtypeStruct((S, LANE), F32)],
        compiler_params=_params("parallel"),
        name="merge_router",
    )(x2d, ya, yr, gta, gtr, wa, wr, wo, nw, wrt, brt)


MOE_TM = 1024


def _moe_kernel(h_ref, comb_ref, x1_ref, wg_ref, wu_ref, wd_ref, nw_ref, out_ref, acc_ref):
    e = pl.program_id(1)

    @pl.when(e == 0)
    def _():
        acc_ref[...] = jnp.zeros_like(acc_ref)

    h = h_ref[...]
    comb = comb_ref[...]
    lane = lax.broadcasted_iota(jnp.int32, comb.shape, 1)
    c = jnp.sum(jnp.where(lane == e, comb, 0.0), axis=-1, keepdims=True)
    hid = jax.nn.silu(_dot(h, wg_ref[0])) * _dot(h, wu_ref[0])
    acc_ref[...] += _dot(hid.astype(BF16), wd_ref[0]) * c

    @pl.when(e == pl.num_programs(1) - 1)
    def _():
        x2 = x1_ref[...] + acc_ref[...]
        out_ref[...] = x2 * lax.rsqrt(jnp.mean(x2 * x2, axis=-1, keepdims=True) + EPS) * nw_ref[...]


def _moe(h, comb, x1, wg, wu, wd, nw):
    S = h.shape[0]
    tm = MOE_TM
    row = lambda w: pl.BlockSpec((tm, w), lambda i, e: (i, 0))
    return pl.pallas_call(
        _moe_kernel,
        grid=(S // tm, N_EXPERTS),
        in_specs=[row(D_MODEL), row(LANE), row(D_MODEL),
                  pl.BlockSpec((1, D_MODEL, EXPERT_FF), lambda i, e: (e, 0, 0)),
                  pl.BlockSpec((1, D_MODEL, EXPERT_FF), lambda i, e: (e, 0, 0)),
                  pl.BlockSpec((1, EXPERT_FF, D_MODEL), lambda i, e: (e, 0, 0)),
                  _const_spec((1, D_MODEL))],
        out_specs=row(D_MODEL),
        out_shape=jax.ShapeDtypeStruct((S, D_MODEL), F32),
        scratch_shapes=[pltpu.VMEM((tm, D_MODEL), F32)],
        compiler_params=_params("parallel", "arbitrary"),
        name="moe_experts",
    )(h, comb, x1, wg, wu, wd, nw)


def _positional_tables(S):
    half = RET_QK_DIM // 2
    inv = ROPE_BASE ** (-jnp.arange(half, dtype=F32) / half)
    ang = jnp.arange(S)[:, None].astype(F32) * inv[None, :]
    cos, sin = jnp.cos(ang), jnp.sin(ang)
    cos2 = jnp.concatenate([cos, cos], axis=-1)
    sin2 = jnp.concatenate([-sin, sin], axis=-1)
    C, H = RET_CHUNK, RET_HEADS
    log_gamma = jnp.log(1.0 - 2.0 ** (-5.0 - jnp.arange(H, dtype=F32)))
    idx = jnp.arange(C, dtype=F32)
    diff = idx[:, None] - idx[None, :]
    dmask = jnp.where(diff >= 0, jnp.exp(jnp.maximum(diff, 0.0)[None] * log_gamma[:, None, None]), 0.0)
    qd = jnp.exp((idx + 1.0)[None, :] * log_gamma[:, None])
    kd = jnp.exp((C - 1.0 - idx)[None, :] * log_gamma[:, None])
    cd = jnp.exp(C * log_gamma)
    qd = jnp.broadcast_to(qd[:, :, None], (H, C, RET_V_DIM))
    kd = jnp.broadcast_to(kd[:, :, None], (H, C, RET_QK_DIM))
    cd = jnp.broadcast_to(cd[:, None, None], (H, 1, RET_V_DIM))
    return cos2, sin2, dmask, qd, kd, cd


def _layer(x2d, rel_bias, norm_mix, w_in, cmp_pos_k, cmp_k_w1, cmp_k_w2, cmp_pos_v, cmp_v_w1, cmp_v_w2,
           ret_norm, w_nsa_up, w_ret_up, w_out, norm_ffn, w_rg, b_rg, w_re, b_re, moe_wg, moe_wu, moe_wd,
           final_norm_w):
    S = x2d.shape[0]
    G, dh = NSA_KV_GROUPS, NSA_HEAD_DIM
    assert S % SEL_TK == 0 and S // SEL_BLOCK >= SEL_TOP_N

    o = np.cumsum((0,) + (NSA_Q_W, 6 * NSA_KV_W, 3 * NSA_HEADS, RET_QK_W, RET_QK_W, RET_V_W, RET_V_W,
                          D_MODEL, D_MODEL))
    seg = lambda n: w_in[:, o[n]:o[n + 1]]
    w_cat = jnp.concatenate([seg(0), seg(1), seg(3), seg(4), seg(5), seg(6), seg(7), seg(8), seg(2),
                             jnp.zeros((D_MODEL, LANE - 3 * NSA_HEADS), F32)], axis=1).astype(BF16)
    cos2, sin2, dmask, qd, kd, cd = _positional_tables(S)

    (qa, kcmp, vcmp, ksel, vsel, kwin, vwin, ga, qr, kr, vr, gr, gta, gtr) = _input_projection(
        x2d, norm_mix.reshape(1, D_MODEL), w_cat, cos2, sin2)

    nch = S // CMP_STRIDE
    chunks = lambda a: a.reshape(nch, CMP_STRIDE, G, dh).transpose(2, 0, 1, 3).reshape(G, nch, CMP_STRIDE * dh)
    xa = jnp.stack([chunks(kcmp), chunks(vcmp)])
    xb = jnp.concatenate([xa[:, :, 1:], jnp.zeros_like(xa[:, :, :1])], axis=2)
    pos = jnp.stack([cmp_pos_k, cmp_pos_v]).reshape(2, 2, CMP_STRIDE * dh)
    w1 = jnp.stack([cmp_k_w1, cmp_v_w1]).astype(BF16)
    w2 = jnp.stack([cmp_k_w2, cmp_v_w2]).astype(BF16)
    cmp_tok = _compress(xa, xb, pos, w1, w2)
    kc = cmp_tok[0]
    vc = cmp_tok[1].transpose(1, 0, 2).reshape(nch, G * dh)
    kcp = jnp.pad(kc, ((0, 0), (CMP_PAD, 0), (0, 0)))
    vcp = jnp.pad(vc, ((CMP_PAD, 0), (0, 0)))

    n_sel = S // SEL_BLOCK
    c_start = np.arange(nch) * CMP_STRIDE
    b_start = np.arange(n_sel) * SEL_BLOCK
    overlap = ((c_start[:, None] <= b_start[None, :] + SEL_BLOCK - 1)
               & (c_start[:, None] + CMP_LEN - 1 >= b_start[None, :]))
    overlap[nch - 1] = False
    ovf = jnp.asarray(overlap, dtype=BF16)
    ovp = jnp.asarray(np.pad(overlap, ((CMP_PAD, 0), (0, 0))), dtype=F32)
    r = np.arange(Q_BLOCK)[:, None]
    tcmp = _bias_table(rel_bias, r - CMP_STRIDE * (np.arange(CMP_NEAR)[None, :] - CMP_PAD) - (CMP_LEN - 1))
    tnear = _bias_table(rel_bias, r + Q_BLOCK - np.arange(2 * Q_BLOCK)[None, :])
    dw = r + WINDOW - np.arange(WIN_KEYS)[None, :]
    twin = _bias_table(rel_bias, np.where(dw < WINDOW, dw, -1))
    b31 = jnp.broadcast_to(rel_bias[REL_BUCKETS - 1].reshape(G, NSA_HPG, 1, 1),
                           (G, NSA_HPG, Q_BLOCK, LANE)).reshape(G, NSA_HPG * Q_BLOCK, LANE)

    heads = lambda a: a.reshape(S, G, dh).transpose(1, 2, 0)
    kselT = jnp.pad(heads(ksel), ((0, 0), (0, 0), (Q_BLOCK, 0)))
    vselp = jnp.pad(vsel, ((Q_BLOCK, 0), (0, 0)))
    kwinT = jnp.pad(heads(kwin), ((0, 0), (0, 0), (WINDOW, 0)))
    vwinp = jnp.pad(vwin, ((WINDOW, 0), (0, 0)))
    qh = qa.reshape(S, NSA_HEADS, dh).transpose(1, 0, 2)
    ya = _nsa_attention(qh, ga, kc.astype(BF16), vc.astype(BF16), ovf, kcp, vcp, ovp, tcmp, b31,
                        kselT, vselp, tnear, kwinT, vwinp, twin)
    ya = ya.transpose(1, 0, 2).reshape(S, NSA_Q_W)

    yr = _retention(qr, kr, vr, gr, ret_norm.reshape(1, RET_V_W), dmask, qd, kd, cd)

    wrt = jnp.concatenate([w_re, w_rg, jnp.zeros((D_MODEL, LANE - N_EXPERTS - N_GROUPS), F32)], axis=1).astype(BF16)
    brt = jnp.concatenate([b_re, b_rg, jnp.zeros((LANE - N_EXPERTS - N_GROUPS,), F32)]).reshape(1, LANE)
    x1, h2, comb = _merge(x2d, ya, yr, gta, gtr, w_nsa_up.astype(BF16), w_ret_up.astype(BF16),
                          w_out.astype(BF16), norm_ffn.reshape(1, D_MODEL), wrt, brt)
    return _moe(h2, comb, x1, moe_wg.astype(BF16), moe_wu.astype(BF16), moe_wd.astype(BF16), final_norm_w)


def kernel(x, rel_bias, norm_mix, w_in, cmp_pos_k, cmp_k_w1, cmp_k_w2, cmp_pos_v, cmp_v_w1, cmp_v_w2, ret_norm, w_nsa_up, w_ret_up, w_out, norm_ffn, w_router_group, b_router_group, w_router_expert, b_router_expert, moe_w_gate, moe_w_up, moe_w_down, norm_final):
    B, S, D = x.shape
    assert B == 1 and norm_mix.shape[0] == 1, "single sequence, depth 1"
    out = _layer(x[0], rel_bias, norm_mix[0], w_in[0], cmp_pos_k[0], cmp_k_w1[0], cmp_k_w2[0],
                 cmp_pos_v[0], cmp_v_w1[0], cmp_v_w2[0], ret_norm[0], w_nsa_up[0], w_ret_up[0], w_out[0],
                 norm_ffn[0], w_router_group[0], b_router_group[0], w_router_expert[0], b_router_expert[0],
                 moe_w_gate[0], moe_w_up[0], moe_w_down[0], norm_final.reshape(1, D))
    return out[None]
```

```python
import functools
import math

import numpy as np
import jax
import jax.numpy as jnp
from jax import lax
from jax.experimental import pallas as pl
from jax.experimental.pallas import tpu as pltpu

F32 = jnp.float32
BF16 = jnp.bfloat16

D_MODEL = 1024
NSA_HEADS = 8
NSA_KV_GROUPS = 2
NSA_HPG = NSA_HEADS // NSA_KV_GROUPS
NSA_HEAD_DIM = 64
CMP_LEN = 32
CMP_STRIDE = 16
CMP_HIDDEN = 256
SEL_BLOCK = 64
SEL_TOP_N = 16
N_LOCAL_SEL = 2
WINDOW = 512
Q_BLOCK = 128
RET_HEADS = 4
RET_QK_DIM = 128
RET_V_DIM = 256
RET_CHUNK = 128
ROPE_BASE = 10000.0
REL_BUCKETS = 32
REL_MAX_DIST = 128
N_GROUPS = 4
EXPERTS_PER_GROUP = 8
N_EXPERTS = N_GROUPS * EXPERTS_PER_GROUP
EXPERT_FF = 256
EPS = 1e-6

NSA_Q_W = NSA_HEADS * NSA_HEAD_DIM
NSA_KV_W = NSA_KV_GROUPS * NSA_HEAD_DIM
RET_QK_W = RET_HEADS * RET_QK_DIM
RET_V_W = RET_HEADS * RET_V_DIM

LANE = 128
NEG = -1e30
VMEM_LIMIT = 56 * 1024 * 1024

CMP_TILE = 128
QB_PER_CMP_TILE = CMP_TILE * CMP_STRIDE // Q_BLOCK
CMP_TOK_PER_QB = Q_BLOCK // CMP_STRIDE
SCR_PAD = 8
SEL_TK = 512
SEL_WIN = 128
SEL_WIN_TILES = SEL_WIN * SEL_BLOCK // SEL_TK
PV_ROWS = 80
WIN_KEYS = WINDOW + Q_BLOCK
QW = NSA_HPG * Q_BLOCK


def _dot(a, b):
    return jnp.dot(a, b, preferred_element_type=F32)


def _dot_nt(a, b):
    return lax.dot_general(a, b, (((1,), (1,)), ((), ())), preferred_element_type=F32)


def _const_spec(shape):
    nd = len(shape)
    return pl.BlockSpec(shape, lambda *_: (0,) * nd, pipeline_mode=pl.Buffered(1))


def _params(*sem):
    return pltpu.CompilerParams(dimension_semantics=sem, vmem_limit_bytes=VMEM_LIMIT)


_C_QA = 0
_C_KV = _C_QA + NSA_Q_W
_C_QR = _C_KV + 6 * NSA_KV_W
_C_KR = _C_QR + RET_QK_W
_C_VR = _C_KR + RET_QK_W
_C_GR = _C_VR + RET_V_W
_C_GTA = _C_GR + RET_V_W
_C_GTR = _C_GTA + D_MODEL
_C_GA = _C_GTR + D_MODEL
_C_END = _C_GA + LANE
IN_TM = 256


def _inproj_kernel(x_ref, nw_ref, w_ref, cos_ref, sin_ref,
                   qa_ref, kcmp_ref, vcmp_ref, ksel_ref, vsel_ref, kwin_ref, vwin_ref, ga_ref,
                   qr_ref, kr_ref, vr_ref, gr_ref, gta_ref, gtr_ref):
    x = x_ref[...]
    h = x * lax.rsqrt(jnp.mean(x * x, axis=-1, keepdims=True) + EPS) * nw_ref[...]
    hb = h.astype(BF16)

    def proj(a, width):
        return _dot(hb, w_ref[:, a:a + width])

    qa_ref[...] = (proj(_C_QA, NSA_Q_W) * (NSA_HEAD_DIM ** -0.5)).astype(BF16)
    kv = proj(_C_KV, 6 * NSA_KV_W)
    kcmp_ref[...] = kv[:, 0 * LANE:1 * LANE]
    vcmp_ref[...] = kv[:, 1 * LANE:2 * LANE]
    ksel_ref[...] = kv[:, 2 * LANE:3 * LANE].astype(BF16)
    vsel_ref[...] = kv[:, 3 * LANE:4 * LANE].astype(BF16)
    kwin_ref[...] = kv[:, 4 * LANE:5 * LANE].astype(BF16)
    vwin_ref[...] = kv[:, 5 * LANE:6 * LANE].astype(BF16)
    ga_ref[...] = jax.nn.sigmoid(proj(_C_GA, LANE))

    cos2 = cos_ref[...]
    sin2 = sin_ref[...]
    qr = proj(_C_QR, RET_QK_W)
    kr = proj(_C_KR, RET_QK_W)
    for hd in range(RET_HEADS):
        sl = slice(hd * RET_QK_DIM, (hd + 1) * RET_QK_DIM)
        qh = qr[:, sl]
        kh = kr[:, sl]
        qr_ref[:, sl] = (qh * cos2 + pltpu.roll(qh, RET_QK_DIM // 2, 1) * sin2).astype(BF16)
        kr_ref[:, sl] = (kh * cos2 + pltpu.roll(kh, RET_QK_DIM // 2, 1) * sin2) * (RET_QK_DIM ** -0.5)
    vr_ref[...] = proj(_C_VR, RET_V_W).astype(BF16)
    gr_ref[...] = proj(_C_GR, RET_V_W)
    gta_ref[...] = jax.nn.sigmoid(proj(_C_GTA, D_MODEL))
    gtr_ref[...] = jax.nn.sigmoid(proj(_C_GTR, D_MODEL))


def _input_projection(x2d, norm_w, w_cat, cos2, sin2):
    S = x2d.shape[0]
    tm = IN_TM
    row = lambda w: pl.BlockSpec((tm, w), lambda i: (i, 0))
    outs = [
        (NSA_Q_W, BF16),
        (LANE, F32), (LANE, F32),
        (LANE, BF16), (LANE, BF16),
        (LANE, BF16), (LANE, BF16),
        (LANE, F32),
        (RET_QK_W, BF16), (RET_QK_W, F32),
        (RET_V_W, BF16), (RET_V_W, F32),
        (D_MODEL, F32), (D_MODEL, F32),
    ]
    return pl.pallas_call(
        _inproj_kernel,
        grid=(S // tm,),
        in_specs=[row(D_MODEL), _const_spec((1, D_MODEL)), _const_spec((D_MODEL, _C_END)),
                  row(LANE), row(LANE)],
        out_specs=[row(w) for w, _ in outs],
        out_shape=[jax.ShapeDtypeStruct((S, w), dt) for w, dt in outs],
        compiler_params=_params("parallel"),
        name="in_proj",
    )(x2d, norm_w, w_cat, cos2, sin2)


def _compress_kernel(xa_ref, xb_ref, pos_ref, w1_ref, w2_ref, out_ref):
    half = CMP_STRIDE * NSA_HEAD_DIM
    a = (xa_ref[0, 0] + pos_ref[0, 0:1, :]).astype(BF16)
    b = (xb_ref[0, 0] + pos_ref[0, 1:2, :]).astype(BF16)
    hid = _dot(a, w1_ref[0, :half, :]) + _dot(b, w1_ref[0, half:, :])
    out_ref[0, 0] = _dot(jax.nn.gelu(hid).astype(BF16), w2_ref[0])


def _compress(xa, xb, pos, w1, w2):
    _, G, nch, width = xa.shape
    xspec = pl.BlockSpec((1, 1, nch, width), lambda t, g: (t, g, 0, 0))
    return pl.pallas_call(
        _compress_kernel,
        grid=(2, G),
        in_specs=[xspec, xspec,
                  pl.BlockSpec((1, 2, width), lambda t, g: (t, 0, 0)),
                  pl.BlockSpec((1, 2 * width, CMP_HIDDEN), lambda t, g: (t, 0, 0)),
                  pl.BlockSpec((1, CMP_HIDDEN, NSA_HEAD_DIM), lambda t, g: (t, 0, 0))],
        out_specs=pl.BlockSpec((1, 1, nch, NSA_HEAD_DIM), lambda t, g: (t, g, 0, 0)),
        out_shape=jax.ShapeDtypeStruct((2, G, nch, NSA_HEAD_DIM), F32),
        compiler_params=_params("parallel", "parallel"),
        name="nsa_compress",
    )(xa, xb, pos, w1, w2)


def _t5_bucket_np(dist):
    dist = np.maximum(dist, 0)
    max_exact = REL_BUCKETS // 2
    d_f = np.maximum(dist, 1).astype(np.float32)
    large = max_exact + (np.log(d_f / np.float32(max_exact)) / np.float32(math.log(REL_MAX_DIST / max_exact))
                         * np.float32(REL_BUCKETS - max_exact)).astype(np.int32)
    large = np.minimum(large, REL_BUCKETS - 1)
    return np.where(dist < max_exact, dist, large)


def _bias_table_t(rel_bias, dist, shift):
    bucket = jnp.asarray(_t5_bucket_np(dist), jnp.int32)
    rb = rel_bias - rel_bias[REL_BUCKETS - 1][None, :] if shift else rel_bias
    tab = jnp.zeros((NSA_HEADS,) + dist.shape, F32)
    for b in range(REL_BUCKETS):
        tab = jnp.where(bucket[None] == b, rb[b][:, None, None], tab)
    tab = jnp.where(jnp.asarray(dist == -1)[None], NEG, tab)
    tab = jnp.where(jnp.asarray(dist == -2)[None], 0.0, tab)
    K, Q = dist.shape
    return tab.reshape(NSA_KV_GROUPS, NSA_HPG, K, Q).transpose(0, 2, 1, 3).reshape(NSA_KV_GROUPS, K, NSA_HPG * Q)


def _nsa_kernel(n_keys, q_ref, ga_ref, kc_ref, vct_ref, tct_ref, ksel_ref, vselt_ref, et_ref, tnt_ref,
                kwin_ref, vwint_ref, twt_ref, out_ref, sc_ref, ps_ref):
    i = pl.program_id(0)
    dh = NSA_HEAD_DIM
    nb = (ps_ref.shape[0] - 2 * SCR_PAD) // (SEL_BLOCK // CMP_STRIDE)

    @pl.when(i == 0)
    def _():
        sc_ref[...] = jnp.zeros_like(sc_ref)
        ps_ref[...] = jnp.zeros_like(ps_ref)

    def colmax(x):
        return jnp.max(x, axis=0, keepdims=True)

    def tile4(x):
        return jnp.concatenate([x] * NSA_HPG, axis=1)

    gates = ga_ref[...]
    tl = i // QB_PER_CMP_TILE
    sub = i % QB_PER_CMP_TILE
    tp = tl - 1

    for g in range(NSA_KV_GROUPS):
        q_t = jnp.concatenate([q_ref[NSA_HPG * g + j] for j in range(NSA_HPG)], axis=1)
        zeros = jnp.zeros((dh, QW), BF16)
        wq = jnp.concatenate([q_t, zeros] if g == 0 else [zeros, q_t], axis=0)

        def cmp_scores(t):
            return _dot(kc_ref[g, pl.ds(pl.multiple_of(t * CMP_TILE, CMP_TILE), CMP_TILE), :], q_t)

        def sc_rows(t):
            return pl.ds(pl.multiple_of(SCR_PAD + t * CMP_TILE, 8), CMP_TILE)

        def cmp_far(t, m):
            s = cmp_scores(t)
            sc_ref[sc_rows(t), :] = s
            return jnp.maximum(m, colmax(s))

        m_c = lax.fori_loop(0, jnp.maximum(tp, 0), cmp_far, jnp.full((1, QW), NEG, F32))
        tpc = jnp.maximum(tp, 0)
        off = CMP_TILE - CMP_TOK_PER_QB - CMP_TOK_PER_QB * sub
        s = (cmp_scores(tpc) + tct_ref[g, pl.ds(pl.multiple_of(off, 8), CMP_TILE), :]
             + jnp.where(tp < 0, NEG, 0.0))
        sc_ref[sc_rows(tpc), :] = s
        m_c = jnp.maximum(m_c, colmax(s))
        s = cmp_scores(tl) + tct_ref[g, pl.ds(pl.multiple_of(off + CMP_TILE, 8), CMP_TILE), :]
        sc_ref[sc_rows(tl), :] = s
        m_c = jnp.maximum(m_c, colmax(s))

        def cmp_exp(t, carry):
            l, acc = carry
            e = jnp.exp(sc_ref[sc_rows(t), :] - m_c)
            sc_ref[sc_rows(t), :] = e
            vt = vct_ref[g, :, pl.ds(pl.multiple_of(t * CMP_TILE, CMP_TILE), CMP_TILE)]
            return l + jnp.sum(e, axis=0, keepdims=True), acc + _dot(vt, e.astype(BF16))

        l_c, acc_c = lax.fori_loop(0, tl + 1, cmp_exp, (jnp.zeros((1, QW), F32), jnp.zeros((dh, QW), F32)))
        inv_c = jnp.where(m_c > 0.5 * NEG, 1.0 / jnp.maximum(l_c, 1e-30), 0.0)
        o_c = acc_c * inv_c

        def cmp_psum(t, _):
            p = sc_ref[sc_rows(t), :] * inv_c
            ps = p[:, 0:Q_BLOCK]
            for j in range(1, NSA_HPG):
                ps = ps + p[:, j * Q_BLOCK:(j + 1) * Q_BLOCK]
            ps_ref[sc_rows(t), :] = ps.astype(BF16).astype(F32)
            return 0

        lax.fori_loop(0, tl + 1, cmp_psum, 0)
        per_blk = SEL_BLOCK // CMP_STRIDE
        imp = ps_ref[pl.ds(SCR_PAD - 1, nb, stride=per_blk), :]
        for k in range(1, CMP_LEN // CMP_STRIDE + per_blk - 1):
            imp = imp + ps_ref[pl.ds(SCR_PAD - 1 + k, nb, stride=per_blk), :]

        n_io = lax.broadcasted_iota(jnp.int32, (nb, Q_BLOCK), 0)
        r_io = lax.broadcasted_iota(jnp.int32, (nb, Q_BLOCK), 1)
        back = (i * Q_BLOCK + r_io) // SEL_BLOCK - n_io
        valid = back >= 0
        forced = (n_io == 0) | (valid & (back < N_LOCAL_SEL))
        work = jnp.where(forced, -jnp.inf, jnp.where(valid, imp, -1e9))
        sel = jnp.where(forced, 1.0, 0.0)
        n_f = n_io.astype(F32)
        for _ in range(SEL_TOP_N - 1 - N_LOCAL_SEL):
            mx = colmax(work)
            first = jnp.min(jnp.where(work == mx, n_f, float(nb)), axis=0, keepdims=True)
            pick = n_f == first
            sel = jnp.where(pick, 1.0, sel)
            work = jnp.where(pick, -jnp.inf, work)

        far_blocks = (i - 1) * (Q_BLOCK // SEL_BLOCK)
        selm = (jnp.where(n_io < far_blocks, sel, 0.0) - 1.0).astype(BF16)
        far_end = (i - 1) * Q_BLOCK
        n_far = (jnp.maximum(far_end, 0) + SEL_TK - 1) // SEL_TK
        carry = (jnp.full((1, QW), NEG, F32), jnp.zeros((PV_ROWS, QW), F32))

        def flash_step(carry, s, vt):
            m_i, acc = carry
            m_new = jnp.maximum(m_i, colmax(s))
            alpha = jnp.exp(m_i - m_new)
            p = jnp.exp(s - m_new).astype(BF16)
            return m_new, alpha * acc + _dot(vt, p)

        for w in range(nb // SEL_WIN):
            w_far = jnp.concatenate([tile4(selm[w * SEL_WIN:(w + 1) * SEL_WIN]), wq], axis=0)

            def far_body(kt, carry, w=w, w_far=w_far):
                k0 = pl.multiple_of(kt * SEL_TK, SEL_TK)
                e0 = pl.multiple_of((kt - w * SEL_WIN_TILES) * SEL_TK, SEL_TK)
                lhs = jnp.concatenate([et_ref[pl.ds(e0, SEL_TK), :],
                                       ksel_ref[pl.ds(k0 + Q_BLOCK, SEL_TK), :]], axis=1)
                return flash_step(carry, _dot(lhs, w_far), vselt_ref[g, :, pl.ds(k0 + Q_BLOCK, SEL_TK)])

            carry = lax.fori_loop(w * SEL_WIN_TILES, jnp.minimum(n_far, (w + 1) * SEL_WIN_TILES), far_body, carry)

        k0n = pl.multiple_of(i * Q_BLOCK, Q_BLOCK)
        near_blocks = 2 * Q_BLOCK // SEL_BLOCK
        pb = lax.broadcasted_iota(jnp.int32, (16, nb), 0)
        pn = lax.broadcasted_iota(jnp.int32, (16, nb), 1)
        pick_rows = jnp.where((pn == far_blocks + pb) & (pb < near_blocks), 1.0, 0.0).astype(BF16)
        sel_rows = _dot(pick_rows, sel.astype(BF16))
        add_near = jnp.concatenate(
            [jnp.broadcast_to((sel_rows[b:b + 1, :] - 1.0) * (-NEG), (SEL_BLOCK, Q_BLOCK)) for b in range(near_blocks)],
            axis=0)
        s = _dot(ksel_ref[pl.ds(k0n, 2 * Q_BLOCK), :], wq) + tnt_ref[g] + tile4(add_near)
        m_s, acc_s = flash_step(carry, s, vselt_ref[g, :, pl.ds(k0n, 2 * Q_BLOCK)])
        o_s = acc_s[0:dh] * (1.0 / jnp.maximum(acc_s[dh:dh + 1], 1e-30))

        s = _dot(kwin_ref[pl.ds(k0n, WIN_KEYS), :], wq) + twt_ref[g]
        pos = lax.broadcasted_iota(jnp.int32, (WIN_KEYS, QW), 0) + (i * Q_BLOCK - WINDOW)
        s = jnp.where(pos >= 0, s, NEG)
        e = jnp.exp(s - colmax(s)).astype(BF16)
        acc_w = _dot(vwint_ref[g, :, pl.ds(k0n, WIN_KEYS)], e)
        o_w = acc_w[0:dh] * (1.0 / jnp.maximum(acc_w[dh:dh + 1], 1e-30))

        for j in range(NSA_HPG):
            hd = NSA_HPG * g + j
            cols = slice(j * Q_BLOCK, (j + 1) * Q_BLOCK)
            o = (gates[hd:hd + 1] * o_c[:, cols]
                 + gates[NSA_HEADS + hd:NSA_HEADS + hd + 1] * o_s[:, cols]
                 + gates[2 * NSA_HEADS + hd:2 * NSA_HEADS + hd + 1] * o_w[:, cols])
            out_ref[hd] = o.astype(out_ref.dtype)


def _nsa_attention(q_t, gates_t, kc, vct, tct, ksel, vselt, et, tnt, kwin, vwint, twt):
    H, dh, S = q_t.shape
    n_qb = S // Q_BLOCK
    n_tok = S // CMP_STRIDE
    nb = -(-(S // SEL_BLOCK) // SEL_WIN) * SEL_WIN
    ps_rows = nb * (SEL_BLOCK // CMP_STRIDE) + 2 * SCR_PAD
    full = lambda a: _const_spec(a.shape)
    return pl.pallas_call(
        functools.partial(_nsa_kernel, S),
        grid=(n_qb,),
        in_specs=[pl.BlockSpec((H, dh, Q_BLOCK), lambda i: (0, 0, i)),
                  pl.BlockSpec((gates_t.shape[0], Q_BLOCK), lambda i: (0, i)),
                  full(kc), full(vct), full(tct), full(ksel), full(vselt), full(et), full(tnt),
                  full(kwin), full(vwint), full(twt)],
        out_specs=pl.BlockSpec((H, dh, Q_BLOCK), lambda i: (0, 0, i)),
        out_shape=jax.ShapeDtypeStruct((H, dh, S), BF16),
        scratch_shapes=[pltpu.VMEM((SCR_PAD + n_tok, QW), F32),
                        pltpu.VMEM((ps_rows, Q_BLOCK), F32)],
        compiler_params=_params("arbitrary"),
        name="nsa_attention",
    )(q_t, gates_t, kc, vct, tct, ksel, vselt, et, tnt, kwin, vwint, twt)


def _retention_kernel(q_ref, k_ref, v_ref, g_ref, nw_ref, dmask_ref, qd_ref, kd_ref, cd_ref, out_ref, state_ref):
    @pl.when(pl.program_id(0) == 0)
    def _():
        state_ref[...] = jnp.zeros_like(state_ref)

    for hd in range(RET_HEADS):
        ks = slice(hd * RET_QK_DIM, (hd + 1) * RET_QK_DIM)
        vs = slice(hd * RET_V_DIM, (hd + 1) * RET_V_DIM)
        q = q_ref[:, ks]
        k = k_ref[:, ks]
        v = v_ref[:, vs]
        state = state_ref[hd]
        inner = _dot_nt(q, k.astype(BF16)) * dmask_ref[hd]
        y = _dot(inner.astype(BF16), v) + _dot(q, state.astype(BF16)) * qd_ref[hd]
        kd = k * kd_ref[hd]
        state_ref[hd] = state * cd_ref[hd] + _dot(kd.T.astype(BF16), v)
        mu = jnp.mean(y, axis=-1, keepdims=True)
        yc = y - mu
        var = jnp.mean(yc * yc, axis=-1, keepdims=True)
        yn = yc * lax.rsqrt(var + EPS) * nw_ref[:, vs]
        out_ref[:, vs] = (jax.nn.silu(g_ref[:, vs]) * yn).astype(out_ref.dtype)


def _retention(q, k, v, g, norm_w, dmask, qd, kd, cd):
    S = q.shape[0]
    C = RET_CHUNK
    row = lambda w: pl.BlockSpec((C, w), lambda i: (i, 0))
    full = lambda a: _const_spec(a.shape)
    return pl.pallas_call(
        _retention_kernel,
        grid=(S // C,),
        in_specs=[row(RET_QK_W), row(RET_QK_W), row(RET_V_W), row(RET_V_W),
                  full(norm_w), full(dmask), full(qd), full(kd), full(cd)],
        out_specs=row(RET_V_W),
        out_shape=jax.ShapeDtypeStruct((S, RET_V_W), BF16),
        scratch_shapes=[pltpu.VMEM((RET_HEADS, RET_QK_DIM, RET_V_DIM), F32)],
        compiler_params=_params("arbitrary"),
        name="retention",
    )(q, k, v, g, norm_w, dmask, qd, kd, cd)


MERGE_TM = 256


def _merge_kernel(x_ref, ya_ref, yr_ref, gta_ref, gtr_ref, wa_ref, wr_ref, wo_ref, nw_ref, wrt_ref, brt_ref,
                  x1_ref, h_ref, comb_ref):
    merged = gta_ref[...] * _dot(ya_ref[...], wa_ref[...]) + gtr_ref[...] * _dot(yr_ref[...], wr_ref[...])
    x1 = x_ref[...] + _dot(merged.astype(BF16), wo_ref[...])
    x1_ref[...] = x1
    h = (x1 * lax.rsqrt(jnp.mean(x1 * x1, axis=-1, keepdims=True) + EPS) * nw_ref[...]).astype(BF16)
    h_ref[...] = h
    logits = _dot(h, wrt_ref[...]) + brt_ref[...]
    lane = lax.broadcasted_iota(jnp.int32, logits.shape, 1)
    lane_f = lane.astype(F32)
    big = float(LANE)
    is_grp = (lane >= N_EXPERTS) & (lane < N_EXPERTS + N_GROUPS)
    lg = jnp.where(is_grp, logits, -jnp.inf)
    eg = jnp.where(is_grp, jnp.exp(lg - jnp.max(lg, axis=-1, keepdims=True)), 0.0)
    pg = eg / jnp.sum(eg, axis=-1, keepdims=True)
    p_grp = jnp.max(pg, axis=-1, keepdims=True)
    grp = jnp.min(jnp.where(is_grp & (pg == p_grp), lane_f, big), axis=-1, keepdims=True) - N_EXPERTS
    in_grp = (lane_f >= grp * EXPERTS_PER_GROUP) & (lane_f < (grp + 1.0) * EXPERTS_PER_GROUP)
    le = jnp.where(in_grp, logits, -jnp.inf)
    m1 = jnp.max(le, axis=-1, keepdims=True)
    i1 = jnp.min(jnp.where(le == m1, lane_f, big), axis=-1, keepdims=True)
    le2 = jnp.where(lane_f == i1, -jnp.inf, le)
    m2 = jnp.max(le2, axis=-1, keepdims=True)
    i2 = jnp.min(jnp.where(le2 == m2, lane_f, big), axis=-1, keepdims=True)
    e2 = jnp.exp(m2 - m1)
    den = 1.0 + e2
    comb_ref[...] = (jnp.where(lane_f == i1, (1.0 / den) * p_grp, 0.0)
                     + jnp.where(lane_f == i2, (e2 / den) * p_grp, 0.0))


def _merge(x2d, ya, yr, gta, gtr, wa, wr, wo, nw, wrt, brt):
    S = x2d.shape[0]
    tm = MERGE_TM
    row = lambda w: pl.BlockSpec((tm, w), lambda i: (i, 0))
    full = lambda a: _const_spec(a.shape)
    return pl.pallas_call(
        _merge_kernel,
        grid=(S // tm,),
        in_specs=[row(D_MODEL), row(NSA_Q_W), row(RET_V_W), row(D_MODEL), row(D_MODEL),
                  full(wa), full(wr), full(wo), full(nw), full(wrt), full(brt)],
        out_specs=[row(D_MODEL), row(D_MODEL), row(LANE)],
        out_shape=[jax.ShapeDtypeStruct((S, D_MODEL), F32), jax.ShapeDtypeStruct((S, D_MODEL), BF16),
                   jax.ShapeDtypeStruct((S, LANE), F32)],
        compiler_params=_params("parallel"),
        name="merge_router",
    )(x2d, ya, yr, gta, gtr, wa, wr, wo, nw, wrt, brt)


MOE_TM = 1024


def _moe_kernel(h_ref, comb_ref, x1_ref, wg_ref, wu_ref, wd_ref, nw_ref, out_ref, acc_ref):
    e = pl.program_id(1)

    @pl.when(e == 0)
    def _():
        acc_ref[...] = jnp.zeros_like(acc_ref)

    h = h_ref[...]
    comb = comb_ref[...]
    lane = lax.broadcasted_iota(jnp.int32, comb.shape, 1)
    c = jnp.sum(jnp.where(lane == e, comb, 0.0), axis=-1, keepdims=True)
    hid = jax.nn.silu(_dot(h, wg_ref[0])) * _dot(h, wu_ref[0])
    acc_ref[...] += _dot(hid.astype(BF16), wd_ref[0]) * c

    @pl.when(e == pl.num_programs(1) - 1)
    def _():
        x2 = x1_ref[...] + acc_ref[...]
        out_ref[...] = x2 * lax.rsqrt(jnp.mean(x2 * x2, axis=-1, keepdims=True) + EPS) * nw_ref[...]


def _moe(h, comb, x1, wg, wu, wd, nw):
    S = h.shape[0]
    tm = MOE_TM
    row = lambda w: pl.BlockSpec((tm, w), lambda i, e: (i, 0))
    return pl.pallas_call(
        _moe_kernel,
        grid=(S // tm, N_EXPERTS),
        in_specs=[row(D_MODEL), row(LANE), row(D_MODEL),
                  pl.BlockSpec((1, D_MODEL, EXPERT_FF), lambda i, e: (e, 0, 0)),
                  pl.BlockSpec((1, D_MODEL, EXPERT_FF), lambda i, e: (e, 0, 0)),
                  pl.BlockSpec((1, EXPERT_FF, D_MODEL), lambda i, e: (e, 0, 0)),
                  _const_spec((1, D_MODEL))],
        out_specs=row(D_MODEL),
        out_shape=jax.ShapeDtypeStruct((S, D_MODEL), F32),
        scratch_shapes=[pltpu.VMEM((tm, D_MODEL), F32)],
        compiler_params=_params("parallel", "arbitrary"),
        name="moe_experts",
    )(h, comb, x1, wg, wu, wd, nw)


def _positional_tables(S):
    half = RET_QK_DIM // 2
    inv = ROPE_BASE ** (-jnp.arange(half, dtype=F32) / half)
    ang = jnp.arange(S)[:, None].astype(F32) * inv[None, :]
    cos, sin = jnp.cos(ang), jnp.sin(ang)
    cos2 = jnp.concatenate([cos, cos], axis=-1)
    sin2 = jnp.concatenate([-sin, sin], axis=-1)
    C, H = RET_CHUNK, RET_HEADS
    log_gamma = jnp.log(1.0 - 2.0 ** (-5.0 - jnp.arange(H, dtype=F32)))
    idx = jnp.arange(C, dtype=F32)
    diff = idx[:, None] - idx[None, :]
    dmask = jnp.where(diff >= 0, jnp.exp(jnp.maximum(diff, 0.0)[None] * log_gamma[:, None, None]), 0.0)
    qd = jnp.exp((idx + 1.0)[None, :] * log_gamma[:, None])
    kd = jnp.exp((C - 1.0 - idx)[None, :] * log_gamma[:, None])
    cd = jnp.exp(C * log_gamma)
    qd = jnp.broadcast_to(qd[:, :, None], (H, C, RET_V_DIM))
    kd = jnp.broadcast_to(kd[:, :, None], (H, C, RET_QK_DIM))
    cd = jnp.broadcast_to(cd[:, None, None], (H, 1, RET_V_DIM))
    return cos2, sin2, dmask, qd, kd, cd


def _attention_tables(rel_bias):
    r = np.arange(Q_BLOCK)[None, :]
    cc = np.arange(3 * CMP_TILE)[:, None] - CMP_TILE
    d = r - CMP_STRIDE * (cc - (CMP_TILE - CMP_TOK_PER_QB)) - (CMP_LEN - 1)
    d = np.where(cc < 0, -2, np.where((cc >= CMP_TILE) | (d < 0), -1, d))
    tct = _bias_table_t(rel_bias, d, shift=True)
    d = r + Q_BLOCK - np.arange(2 * Q_BLOCK)[:, None]
    tnt = _bias_table_t(rel_bias, np.where(d < 0, -1, d), shift=True)
    d = r + WINDOW - np.arange(WIN_KEYS)[:, None]
    twt = _bias_table_t(rel_bias, np.where((d < 0) | (d >= WINDOW), -1, d), shift=False)
    kk = np.arange(SEL_WIN * SEL_BLOCK)[:, None] // SEL_BLOCK
    et = jnp.where(jnp.asarray(kk == np.arange(SEL_WIN)[None, :]), -NEG, 0.0).astype(BF16)
    return tct, tnt, twt, et


def _layer(x2d, rel_bias, norm_mix, w_in, cmp_pos_k, cmp_k_w1, cmp_k_w2, cmp_pos_v, cmp_v_w1, cmp_v_w2,
           ret_norm, w_nsa_up, w_ret_up, w_out, norm_ffn, w_rg, b_rg, w_re, b_re, moe_wg, moe_wu, moe_wd,
           final_norm_w):
    S = x2d.shape[0]
    G, dh = NSA_KV_GROUPS, NSA_HEAD_DIM
    assert S % (CMP_TILE * CMP_STRIDE) == 0 and S // SEL_BLOCK >= SEL_TOP_N

    o = np.cumsum((0,) + (NSA_Q_W, 6 * NSA_KV_W, 3 * NSA_HEADS, RET_QK_W, RET_QK_W, RET_V_W, RET_V_W,
                          D_MODEL, D_MODEL))
    seg = lambda n: w_in[:, o[n]:o[n + 1]]
    w_cat = jnp.concatenate([seg(0), seg(1), seg(3), seg(4), seg(5), seg(6), seg(7), seg(8), seg(2),
                             jnp.zeros((D_MODEL, LANE - 3 * NSA_HEADS), F32)], axis=1).astype(BF16)
    cos2, sin2, dmask, qd, kd, cd = _positional_tables(S)

    (qa, kcmp, vcmp, ksel, vsel, kwin, vwin, ga, qr, kr, vr, gr, gta, gtr) = _input_projection(
        x2d, norm_mix.reshape(1, D_MODEL), w_cat, cos2, sin2)

    nch = S // CMP_STRIDE
    chunks = lambda a: a.reshape(nch, CMP_STRIDE, G, dh).transpose(2, 0, 1, 3).reshape(G, nch, CMP_STRIDE * dh)
    xa = jnp.stack([chunks(kcmp), chunks(vcmp)])
    xb = jnp.concatenate([xa[:, :, 1:], jnp.zeros_like(xa[:, :, :1])], axis=2)
    pos = jnp.stack([cmp_pos_k, cmp_pos_v]).reshape(2, 2, CMP_STRIDE * dh)
    w1 = jnp.stack([cmp_k_w1, cmp_v_w1]).astype(BF16)
    w2 = jnp.stack([cmp_k_w2, cmp_v_w2]).astype(BF16)
    cmp_tok = _compress(xa, xb, pos, w1, w2)
    kc = cmp_tok[0].astype(BF16)
    vct = cmp_tok[1].transpose(0, 2, 1).astype(BF16)

    tct, tnt, twt, et = _attention_tables(rel_bias)

    def values_t(v, pad):
        vt = v.reshape(S, G, dh).transpose(1, 2, 0)
        extra = jnp.concatenate([jnp.ones((G, 1, S), BF16), jnp.zeros((G, PV_ROWS - dh - 1, S), BF16)], axis=1)
        return jnp.pad(jnp.concatenate([vt, extra], axis=1), ((0, 0), (0, 0), (pad, 0)))

    q_t = qa.T.reshape(NSA_HEADS, dh, S)
    gates_t = ga.T[:4 * NSA_HEADS]
    ya_t = _nsa_attention(q_t, gates_t, kc, vct, tct,
                          jnp.pad(ksel, ((Q_BLOCK, 0), (0, 0))), values_t(vsel, Q_BLOCK), et, tnt,
                          jnp.pad(kwin, ((WINDOW, 0), (0, 0))), values_t(vwin, WINDOW), twt)
    ya = ya_t.reshape(NSA_Q_W, S).T

    yr = _retention(qr, kr, vr, gr, ret_norm.reshape(1, RET_V_W), dmask, qd, kd, cd)

    wrt = jnp.concatenate([w_re, w_rg, jnp.zeros((D_MODEL, LANE - N_EXPERTS - N_GROUPS), F32)], axis=1).astype(BF16)
    brt = jnp.concatenate([b_re, b_rg, jnp.zeros((LANE - N_EXPERTS - N_GROUPS,), F32)]).reshape(1, LANE)
    x1, h2, comb = _merge(x2d, ya, yr, gta, gtr, w_nsa_up.astype(BF16), w_ret_up.astype(BF16),
                          w_out.astype(BF16), norm_ffn.reshape(1, D_MODEL), wrt, brt)
    return _moe(h2, comb, x1, moe_wg.astype(BF16), moe_wu.astype(BF16), moe_wd.astype(BF16), final_norm_w)


def kernel(x, rel_bias, norm_mix, w_in, cmp_pos_k, cmp_k_w1, cmp_k_w2, cmp_pos_v, cmp_v_w1, cmp_v_w2, ret_norm, w_nsa_up, w_ret_up, w_out, norm_ffn, w_router_group, b_router_group, w_router_expert, b_router_expert, moe_w_gate, moe_w_up, moe_w_down, norm_final):
    B, S, D = x.shape
    assert B == 1 and norm_mix.shape[0] == 1, "single sequence, depth 1"
    out = _layer(x[0], rel_bias, norm_mix[0], w_in[0], cmp_pos_k[0], cmp_k_w1[0], cmp_k_w2[0],
                 cmp_pos_v[0], cmp_v_w1[0], cmp_v_w2[0], ret_norm[0], w_nsa_up[0], w_ret_up[0], w_out[0],
                 norm_ffn[0], w_router_group[0], b_router_group[0], w_router_expert[0], b_router_expert[0],
                 moe_w_gate[0], moe_w_up[0], moe_w_down[0], norm_final.reshape(1, D))
    return out[None]
```

```python
import functools
import math

import numpy as np
import jax
import jax.numpy as jnp
from jax import lax
from jax.experimental import pallas as pl
from jax.experimental.pallas import tpu as pltpu

F32 = jnp.float32
BF16 = jnp.bfloat16

D_MODEL = 1024
NSA_HEADS = 8
NSA_KV_GROUPS = 2
NSA_HPG = NSA_HEADS // NSA_KV_GROUPS
NSA_HEAD_DIM = 64
CMP_LEN = 32
CMP_STRIDE = 16
CMP_HIDDEN = 256
SEL_BLOCK = 64
SEL_TOP_N = 16
N_LOCAL_SEL = 2
WINDOW = 512
Q_BLOCK = 128
RET_HEADS = 4
RET_QK_DIM = 128
RET_V_DIM = 256
RET_CHUNK = 128
ROPE_BASE = 10000.0
REL_BUCKETS = 32
REL_MAX_DIST = 128
N_GROUPS = 4
EXPERTS_PER_GROUP = 8
N_EXPERTS = N_GROUPS * EXPERTS_PER_GROUP
EXPERT_FF = 256
EPS = 1e-6

NSA_Q_W = NSA_HEADS * NSA_HEAD_DIM
NSA_KV_W = NSA_KV_GROUPS * NSA_HEAD_DIM
RET_QK_W = RET_HEADS * RET_QK_DIM
RET_V_W = RET_HEADS * RET_V_DIM

LANE = 128
NEG = -1e30
VMEM_LIMIT = 56 * 1024 * 1024

CMP_TILE = 128
QB_PER_CMP_TILE = CMP_TILE * CMP_STRIDE // Q_BLOCK
CMP_TOK_PER_QB = Q_BLOCK // CMP_STRIDE
SCR_PAD = 8
SEL_TK = 512
SEL_WIN = 128
SEL_WIN_TILES = SEL_WIN * SEL_BLOCK // SEL_TK
PV_ROWS = 80
WIN_KEYS = WINDOW + Q_BLOCK
QW = NSA_HPG * Q_BLOCK


def _dot(a, b):
    return jnp.dot(a, b, preferred_element_type=F32)


def _dot_nt(a, b):
    return lax.dot_general(a, b, (((1,), (1,)), ((), ())), preferred_element_type=F32)


def _const_spec(shape):
    nd = len(shape)
    return pl.BlockSpec(shape, lambda *_: (0,) * nd, pipeline_mode=pl.Buffered(1))


def _params(*sem):
    return pltpu.CompilerParams(dimension_semantics=sem, vmem_limit_bytes=VMEM_LIMIT)


_C_QA = 0
_C_KV = _C_QA + NSA_Q_W
_C_QR = _C_KV + 6 * NSA_KV_W
_C_KR = _C_QR + RET_QK_W
_C_VR = _C_KR + RET_QK_W
_C_GR = _C_VR + RET_V_W
_C_GTA = _C_GR + RET_V_W
_C_GTR = _C_GTA + D_MODEL
_C_GA = _C_GTR + D_MODEL
_C_END = _C_GA + LANE
IN_TM = 256


def _inproj_kernel(x_ref, nw_ref, w_ref, cos_ref, sin_ref,
                   qa_ref, kcmp_ref, vcmp_ref, ksel_ref, vsel_ref, kwin_ref, vwin_ref, ga_ref,
                   qr_ref, kr_ref, vr_ref, gr_ref, gta_ref, gtr_ref):
    x = x_ref[...]
    h = x * lax.rsqrt(jnp.mean(x * x, axis=-1, keepdims=True) + EPS) * nw_ref[...]
    hb = h.astype(BF16)

    def proj(a, width):
        return _dot(hb, w_ref[:, a:a + width])

    qa_ref[...] = (proj(_C_QA, NSA_Q_W) * (NSA_HEAD_DIM ** -0.5)).astype(BF16)
    kv = proj(_C_KV, 6 * NSA_KV_W)
    kcmp_ref[...] = kv[:, 0 * LANE:1 * LANE]
    vcmp_ref[...] = kv[:, 1 * LANE:2 * LANE]
    ksel_ref[...] = kv[:, 2 * LANE:3 * LANE].astype(BF16)
    vsel_ref[...] = kv[:, 3 * LANE:4 * LANE].astype(BF16)
    kwin_ref[...] = kv[:, 4 * LANE:5 * LANE].astype(BF16)
    vwin_ref[...] = kv[:, 5 * LANE:6 * LANE].astype(BF16)
    ga_ref[...] = jax.nn.sigmoid(proj(_C_GA, LANE))

    cos2 = cos_ref[...]
    sin2 = sin_ref[...]
    qr = proj(_C_QR, RET_QK_W)
    kr = proj(_C_KR, RET_QK_W)
    for hd in range(RET_HEADS):
        sl = slice(hd * RET_QK_DIM, (hd + 1) * RET_QK_DIM)
        qh = qr[:, sl]
        kh = kr[:, sl]
        qr_ref[:, sl] = (qh * cos2 + pltpu.roll(qh, RET_QK_DIM // 2, 1) * sin2).astype(BF16)
        kr_ref[:, sl] = (kh * cos2 + pltpu.roll(kh, RET_QK_DIM // 2, 1) * sin2) * (RET_QK_DIM ** -0.5)
    vr_ref[...] = proj(_C_VR, RET_V_W).astype(BF16)
    gr_ref[...] = proj(_C_GR, RET_V_W)
    gta_ref[...] = jax.nn.sigmoid(proj(_C_GTA, D_MODEL))
    gtr_ref[...] = jax.nn.sigmoid(proj(_C_GTR, D_MODEL))


def _input_projection(x2d, norm_w, w_cat, cos2, sin2):
    S = x2d.shape[0]
    tm = IN_TM
    row = lambda w: pl.BlockSpec((tm, w), lambda i: (i, 0))
    outs = [
        (NSA_Q_W, BF16),
        (LANE, F32), (LANE, F32),
        (LANE, BF16), (LANE, BF16),
        (LANE, BF16), (LANE, BF16),
        (LANE, F32),
        (RET_QK_W, BF16), (RET_QK_W, F32),
        (RET_V_W, BF16), (RET_V_W, F32),
        (D_MODEL, F32), (D_MODEL, F32),
    ]
    return pl.pallas_call(
        _inproj_kernel,
        grid=(S // tm,),
        in_specs=[row(D_MODEL), _const_spec((1, D_MODEL)), _const_spec((D_MODEL, _C_END)),
                  row(LANE), row(LANE)],
        out_specs=[row(w) for w, _ in outs],
        out_shape=[jax.ShapeDtypeStruct((S, w), dt) for w, dt in outs],
        compiler_params=_params("parallel"),
        name="in_proj",
    )(x2d, norm_w, w_cat, cos2, sin2)


def _compress_kernel(xa_ref, xb_ref, pos_ref, w1_ref, w2_ref, out_ref):
    half = CMP_STRIDE * NSA_HEAD_DIM
    a = (xa_ref[0, 0] + pos_ref[0, 0:1, :]).astype(BF16)
    b = (xb_ref[0, 0] + pos_ref[0, 1:2, :]).astype(BF16)
    hid = _dot(a, w1_ref[0, :half, :]) + _dot(b, w1_ref[0, half:, :])
    out_ref[0, 0] = _dot(jax.nn.gelu(hid).astype(BF16), w2_ref[0])


def _compress(xa, xb, pos, w1, w2):
    _, G, nch, width = xa.shape
    xspec = pl.BlockSpec((1, 1, nch, width), lambda t, g: (t, g, 0, 0))
    return pl.pallas_call(
        _compress_kernel,
        grid=(2, G),
        in_specs=[xspec, xspec,
                  pl.BlockSpec((1, 2, width), lambda t, g: (t, 0, 0)),
                  pl.BlockSpec((1, 2 * width, CMP_HIDDEN), lambda t, g: (t, 0, 0)),
                  pl.BlockSpec((1, CMP_HIDDEN, NSA_HEAD_DIM), lambda t, g: (t, 0, 0))],
        out_specs=pl.BlockSpec((1, 1, nch, NSA_HEAD_DIM), lambda t, g: (t, g, 0, 0)),
        out_shape=jax.ShapeDtypeStruct((2, G, nch, NSA_HEAD_DIM), F32),
        compiler_params=_params("parallel", "parallel"),
        name="nsa_compress",
    )(xa, xb, pos, w1, w2)


def _t5_bucket_np(dist):
    dist = np.maximum(dist, 0)
    max_exact = REL_BUCKETS // 2
    d_f = np.maximum(dist, 1).astype(np.float32)
    large = max_exact + (np.log(d_f / np.float32(max_exact)) / np.float32(math.log(REL_MAX_DIST / max_exact))
                         * np.float32(REL_BUCKETS - max_exact)).astype(np.int32)
    large = np.minimum(large, REL_BUCKETS - 1)
    return np.where(dist < max_exact, dist, large)


def _bias_table_t(rel_bias, dist, shift):
    bucket = jnp.asarray(_t5_bucket_np(dist), jnp.int32)
    rb = rel_bias - rel_bias[REL_BUCKETS - 1][None, :] if shift else rel_bias
    tab = jnp.zeros((NSA_HEADS,) + dist.shape, F32)
    for b in range(REL_BUCKETS):
        tab = jnp.where(bucket[None] == b, rb[b][:, None, None], tab)
    tab = jnp.where(jnp.asarray(dist == -1)[None], NEG, tab)
    tab = jnp.where(jnp.asarray(dist == -2)[None], 0.0, tab)
    K, Q = dist.shape
    return tab.reshape(NSA_KV_GROUPS, NSA_HPG, K, Q).transpose(0, 2, 1, 3).reshape(NSA_KV_GROUPS, K, NSA_HPG * Q)


def _nsa_kernel(n_keys, q_ref, ga_ref, kc_ref, vct_ref, tct_ref, ksel_ref, vselt_ref, et_ref, tnt_ref,
                kwin_ref, vwint_ref, twt_ref, out_ref, sc_ref, ps_ref, sa_ref, sb_ref):
    i = pl.program_id(0)
    dh = NSA_HEAD_DIM
    nb = (ps_ref.shape[0] - 2 * SCR_PAD) // (SEL_BLOCK // CMP_STRIDE)

    @pl.when(i == 0)
    def _():
        sc_ref[...] = jnp.zeros_like(sc_ref)
        ps_ref[...] = jnp.zeros_like(ps_ref)

    def colmax(x):
        return jnp.max(x, axis=0, keepdims=True)

    def tile4(x):
        return jnp.concatenate([x] * NSA_HPG, axis=1)

    gates = ga_ref[...]
    tl = i // QB_PER_CMP_TILE
    sub = i % QB_PER_CMP_TILE
    tp = tl - 1
    groups = range(NSA_KV_GROUPS)
    wq_g, o_c_g, sel_g, selm_g = [], [], [], []

    for g in groups:
        q_t = jnp.concatenate([q_ref[NSA_HPG * g + j] for j in range(NSA_HPG)], axis=1)
        zeros = jnp.zeros((dh, QW), BF16)
        wq = jnp.concatenate([q_t, zeros] if g == 0 else [zeros, q_t], axis=0)

        def cmp_scores(t):
            return _dot(kc_ref[g, pl.ds(pl.multiple_of(t * CMP_TILE, CMP_TILE), CMP_TILE), :], q_t)

        def sc_rows(t):
            return pl.ds(pl.multiple_of(SCR_PAD + t * CMP_TILE, 8), CMP_TILE)

        def cmp_far(t, m):
            s = cmp_scores(t)
            sc_ref[sc_rows(t), :] = s
            return jnp.maximum(m, colmax(s))

        m_c = lax.fori_loop(0, jnp.maximum(tp, 0), cmp_far, jnp.full((1, QW), NEG, F32))
        tpc = jnp.maximum(tp, 0)
        off = CMP_TILE - CMP_TOK_PER_QB - CMP_TOK_PER_QB * sub
        s = (cmp_scores(tpc) + tct_ref[g, pl.ds(pl.multiple_of(off, 8), CMP_TILE), :]
             + jnp.where(tp < 0, NEG, 0.0))
        sc_ref[sc_rows(tpc), :] = s
        m_c = jnp.maximum(m_c, colmax(s))
        s = cmp_scores(tl) + tct_ref[g, pl.ds(pl.multiple_of(off + CMP_TILE, 8), CMP_TILE), :]
        sc_ref[sc_rows(tl), :] = s
        m_c = jnp.maximum(m_c, colmax(s))

        def cmp_exp(t, carry):
            l, acc = carry
            e = jnp.exp(sc_ref[sc_rows(t), :] - m_c)
            sc_ref[sc_rows(t), :] = e
            vt = vct_ref[g, :, pl.ds(pl.multiple_of(t * CMP_TILE, CMP_TILE), CMP_TILE)]
            return l + jnp.sum(e, axis=0, keepdims=True), acc + _dot(vt, e.astype(BF16))

        l_c, acc_c = lax.fori_loop(0, tl + 1, cmp_exp, (jnp.zeros((1, QW), F32), jnp.zeros((dh, QW), F32)))
        inv_c = jnp.where(m_c > 0.5 * NEG, 1.0 / jnp.maximum(l_c, 1e-30), 0.0)
        o_c = acc_c * inv_c

        def cmp_psum(t, _):
            p = sc_ref[sc_rows(t), :] * inv_c
            ps = p[:, 0:Q_BLOCK]
            for j in range(1, NSA_HPG):
                ps = ps + p[:, j * Q_BLOCK:(j + 1) * Q_BLOCK]
            ps_ref[sc_rows(t), :] = ps.astype(BF16).astype(F32)
            return 0

        lax.fori_loop(0, tl + 1, cmp_psum, 0)
        per_blk = SEL_BLOCK // CMP_STRIDE
        imp = ps_ref[pl.ds(SCR_PAD - 1, nb, stride=per_blk), :]
        for k in range(1, CMP_LEN // CMP_STRIDE + per_blk - 1):
            imp = imp + ps_ref[pl.ds(SCR_PAD - 1 + k, nb, stride=per_blk), :]

        n_io = lax.broadcasted_iota(jnp.int32, (nb, Q_BLOCK), 0)
        r_io = lax.broadcasted_iota(jnp.int32, (nb, Q_BLOCK), 1)
        back = (i * Q_BLOCK + r_io) // SEL_BLOCK - n_io
        valid = back >= 0
        forced = (n_io == 0) | (valid & (back < N_LOCAL_SEL))
        work = jnp.where(forced, -jnp.inf, jnp.where(valid, imp, -1e9))
        sel = jnp.where(forced, 1.0, 0.0)
        n_f = n_io.astype(F32)
        for _ in range(SEL_TOP_N - 1 - N_LOCAL_SEL):
            mx = colmax(work)
            first = jnp.min(jnp.where(work == mx, n_f, float(nb)), axis=0, keepdims=True)
            pick = n_f == first
            sel = jnp.where(pick, 1.0, sel)
            work = jnp.where(pick, -jnp.inf, work)

        far_blocks = (i - 1) * (Q_BLOCK // SEL_BLOCK)
        wq_g.append(wq)
        o_c_g.append(o_c)
        sel_g.append(sel)
        selm_g.append((jnp.where(n_io < far_blocks, sel, 0.0) - 1.0).astype(BF16))

    far_end = (i - 1) * Q_BLOCK
    n_far = (jnp.maximum(far_end, 0) + SEL_TK - 1) // SEL_TK

    def flash_step(carry, s, vt):
        m_i, acc = carry
        m_new = jnp.maximum(m_i, colmax(s))
        alpha = jnp.exp(m_i - m_new)
        p = jnp.exp(s - m_new).astype(BF16)
        return m_new, alpha * acc + _dot(vt, p)

    carry = tuple((jnp.full((1, QW), NEG, F32), jnp.zeros((PV_ROWS, QW), F32)) for _ in groups)
    for w in range(nb // SEL_WIN):
        w_far = [jnp.concatenate([tile4(selm_g[g][w * SEL_WIN:(w + 1) * SEL_WIN]), wq_g[g]], axis=0)
                 for g in groups]

        lo = w * SEL_WIN_TILES
        last = min((w + 1) * SEL_WIN_TILES, n_keys // SEL_TK) - 1
        trips = (jnp.minimum(n_far, last + 1) - lo + 1) // 2

        def far_scores(kt, s_ref, w=w, w_far=w_far):
            k0 = pl.multiple_of(kt * SEL_TK, SEL_TK)
            e0 = pl.multiple_of((kt - w * SEL_WIN_TILES) * SEL_TK, SEL_TK)
            lhs = jnp.concatenate([et_ref[pl.ds(e0, SEL_TK), :],
                                   ksel_ref[pl.ds(k0 + Q_BLOCK, SEL_TK), :]], axis=1)
            for g in groups:
                s_ref[g] = _dot(lhs, w_far[g])

        def far_update(kt, s_ref, carry):
            k0 = pl.multiple_of(kt * SEL_TK, SEL_TK)
            return tuple(flash_step(carry[g], s_ref[g], vselt_ref[g, :, pl.ds(k0 + Q_BLOCK, SEL_TK)])
                         for g in groups)

        def far_body(u, carry, lo=lo, last=last, far_scores=far_scores):
            kt = lo + 2 * u
            far_scores(kt + 1, sb_ref)
            carry = far_update(kt, sa_ref, carry)
            far_scores(jnp.minimum(kt + 2, last), sa_ref)
            return far_update(kt + 1, sb_ref, carry)

        far_scores(jnp.int32(lo), sa_ref)
        carry = lax.fori_loop(0, trips, far_body, carry)

    for g in groups:
        wq, o_c, sel = wq_g[g], o_c_g[g], sel_g[g]
        k0n = pl.multiple_of(i * Q_BLOCK, Q_BLOCK)
        near_blocks = 2 * Q_BLOCK // SEL_BLOCK
        pb = lax.broadcasted_iota(jnp.int32, (16, nb), 0)
        pn = lax.broadcasted_iota(jnp.int32, (16, nb), 1)
        pick_rows = jnp.where((pn == far_blocks + pb) & (pb < near_blocks), 1.0, 0.0).astype(BF16)
        sel_rows = _dot(pick_rows, sel.astype(BF16))
        add_near = jnp.concatenate(
            [jnp.broadcast_to((sel_rows[b:b + 1, :] - 1.0) * (-NEG), (SEL_BLOCK, Q_BLOCK)) for b in range(near_blocks)],
            axis=0)
        s = _dot(ksel_ref[pl.ds(k0n, 2 * Q_BLOCK), :], wq) + tnt_ref[g] + tile4(add_near)
        m_s, acc_s = flash_step(carry[g], s, vselt_ref[g, :, pl.ds(k0n, 2 * Q_BLOCK)])
        o_s = acc_s[0:dh] * (1.0 / jnp.maximum(acc_s[dh:dh + 1], 1e-30))

        s = _dot(kwin_ref[pl.ds(k0n, WIN_KEYS), :], wq) + twt_ref[g]
        pos = lax.broadcasted_iota(jnp.int32, (WIN_KEYS, QW), 0) + (i * Q_BLOCK - WINDOW)
        s = jnp.where(pos >= 0, s, NEG)
        e = jnp.exp(s - colmax(s)).astype(BF16)
        acc_w = _dot(vwint_ref[g, :, pl.ds(k0n, WIN_KEYS)], e)
        o_w = acc_w[0:dh] * (1.0 / jnp.maximum(acc_w[dh:dh + 1], 1e-30))

        for j in range(NSA_HPG):
            hd = NSA_HPG * g + j
            cols = slice(j * Q_BLOCK, (j + 1) * Q_BLOCK)
            o = (gates[hd:hd + 1] * o_c[:, cols]
                 + gates[NSA_HEADS + hd:NSA_HEADS + hd + 1] * o_s[:, cols]
                 + gates[2 * NSA_HEADS + hd:2 * NSA_HEADS + hd + 1] * o_w[:, cols])
            out_ref[hd] = o.astype(out_ref.dtype)


def _nsa_attention(q_t, gates_t, kc, vct, tct, ksel, vselt, et, tnt, kwin, vwint, twt):
    H, dh, S = q_t.shape
    n_qb = S // Q_BLOCK
    n_tok = S // CMP_STRIDE
    nb = -(-(S // SEL_BLOCK) // SEL_WIN) * SEL_WIN
    ps_rows = nb * (SEL_BLOCK // CMP_STRIDE) + 2 * SCR_PAD
    full = lambda a: _const_spec(a.shape)
    return pl.pallas_call(
        functools.partial(_nsa_kernel, S),
        grid=(n_qb,),
        in_specs=[pl.BlockSpec((H, dh, Q_BLOCK), lambda i: (0, 0, i)),
                  pl.BlockSpec((gates_t.shape[0], Q_BLOCK), lambda i: (0, i)),
                  full(kc), full(vct), full(tct), full(ksel), full(vselt), full(et), full(tnt),
                  full(kwin), full(vwint), full(twt)],
        out_specs=pl.BlockSpec((H, dh, Q_BLOCK), lambda i: (0, 0, i)),
        out_shape=jax.ShapeDtypeStruct((H, dh, S), BF16),
        scratch_shapes=[pltpu.VMEM((SCR_PAD + n_tok, QW), F32),
                        pltpu.VMEM((ps_rows, Q_BLOCK), F32),
                        pltpu.VMEM((NSA_KV_GROUPS, SEL_TK, QW), F32),
                        pltpu.VMEM((NSA_KV_GROUPS, SEL_TK, QW), F32)],
        compiler_params=_params("arbitrary"),
        name="nsa_attention",
    )(q_t, gates_t, kc, vct, tct, ksel, vselt, et, tnt, kwin, vwint, twt)


def _retention_kernel(q_ref, k_ref, v_ref, g_ref, nw_ref, dmask_ref, qd_ref, kd_ref, cd_ref, out_ref, state_ref):
    @pl.when(pl.program_id(0) == 0)
    def _():
        state_ref[...] = jnp.zeros_like(state_ref)

    for hd in range(RET_HEADS):
        ks = slice(hd * RET_QK_DIM, (hd + 1) * RET_QK_DIM)
        vs = slice(hd * RET_V_DIM, (hd + 1) * RET_V_DIM)
        q = q_ref[:, ks]
        k = k_ref[:, ks]
        v = v_ref[:, vs]
        state = state_ref[hd]
        inner = _dot_nt(q, k.astype(BF16)) * dmask_ref[hd]
        y = _dot(inner.astype(BF16), v) + _dot(q, state.astype(BF16)) * qd_ref[hd]
        kd = k * kd_ref[hd]
        state_ref[hd] = state * cd_ref[hd] + _dot(kd.T.astype(BF16), v)
        mu = jnp.mean(y, axis=-1, keepdims=True)
        yc = y - mu
        var = jnp.mean(yc * yc, axis=-1, keepdims=True)
        yn = yc * lax.rsqrt(var + EPS) * nw_ref[:, vs]
        out_ref[:, vs] = (jax.nn.silu(g_ref[:, vs]) * yn).astype(out_ref.dtype)


def _retention(q, k, v, g, norm_w, dmask, qd, kd, cd):
    S = q.shape[0]
    C = RET_CHUNK
    row = lambda w: pl.BlockSpec((C, w), lambda i: (i, 0))
    full = lambda a: _const_spec(a.shape)
    return pl.pallas_call(
        _retention_kernel,
        grid=(S // C,),
        in_specs=[row(RET_QK_W), row(RET_QK_W), row(RET_V_W), row(RET_V_W),
                  full(norm_w), full(dmask), full(qd), full(kd), full(cd)],
        out_specs=row(RET_V_W),
        out_shape=jax.ShapeDtypeStruct((S, RET_V_W), BF16),
        scratch_shapes=[pltpu.VMEM((RET_HEADS, RET_QK_DIM, RET_V_DIM), F32)],
        compiler_params=_params("arbitrary"),
        name="retention",
    )(q, k, v, g, norm_w, dmask, qd, kd, cd)


MERGE_TM = 256


def _merge_kernel(x_ref, ya_ref, yr_ref, gta_ref, gtr_ref, wa_ref, wr_ref, wo_ref, nw_ref, wrt_ref, brt_ref,
                  x1_ref, h_ref, comb_ref):
    merged = gta_ref[...] * _dot(ya_ref[...], wa_ref[...]) + gtr_ref[...] * _dot(yr_ref[...], wr_ref[...])
    x1 = x_ref[...] + _dot(merged.astype(BF16), wo_ref[...])
    x1_ref[...] = x1
    h = (x1 * lax.rsqrt(jnp.mean(x1 * x1, axis=-1, keepdims=True) + EPS) * nw_ref[...]).astype(BF16)
    h_ref[...] = h
    logits = _dot(h, wrt_ref[...]) + brt_ref[...]
    lane = lax.broadcasted_iota(jnp.int32, logits.shape, 1)
    lane_f = lane.astype(F32)
    big = float(LANE)
    is_grp = (lane >= N_EXPERTS) & (lane < N_EXPERTS + N_GROUPS)
    lg = jnp.where(is_grp, logits, -jnp.inf)
    eg = jnp.where(is_grp, jnp.exp(lg - jnp.max(lg, axis=-1, keepdims=True)), 0.0)
    pg = eg / jnp.sum(eg, axis=-1, keepdims=True)
    p_grp = jnp.max(pg, axis=-1, keepdims=True)
    grp = jnp.min(jnp.where(is_grp & (pg == p_grp), lane_f, big), axis=-1, keepdims=True) - N_EXPERTS
    in_grp = (lane_f >= grp * EXPERTS_PER_GROUP) & (lane_f < (grp + 1.0) * EXPERTS_PER_GROUP)
    le = jnp.where(in_grp, logits, -jnp.inf)
    m1 = jnp.max(le, axis=-1, keepdims=True)
    i1 = jnp.min(jnp.where(le == m1, lane_f, big), axis=-1, keepdims=True)
    le2 = jnp.where(lane_f == i1, -jnp.inf, le)
    m2 = jnp.max(le2, axis=-1, keepdims=True)
    i2 = jnp.min(jnp.where(le2 == m2, lane_f, big), axis=-1, keepdims=True)
    e2 = jnp.exp(m2 - m1)
    den = 1.0 + e2
    comb_ref[...] = (jnp.where(lane_f == i1, (1.0 / den) * p_grp, 0.0)
                     + jnp.where(lane_f == i2, (e2 / den) * p_grp, 0.0))


def _merge(x2d, ya, yr, gta, gtr, wa, wr, wo, nw, wrt, brt):
    S = x2d.shape[0]
    tm = MERGE_TM
    row = lambda w: pl.BlockSpec((tm, w), lambda i: (i, 0))
    full = lambda a: _const_spec(a.shape)
    return pl.pallas_call(
        _merge_kernel,
        grid=(S // tm,),
        in_specs=[row(D_MODEL), row(NSA_Q_W), row(RET_V_W), row(D_MODEL), row(D_MODEL),
                  full(wa), full(wr), full(wo), full(nw), full(wrt), full(brt)],
        out_specs=[row(D_MODEL), row(D_MODEL), row(LANE)],
        out_shape=[jax.ShapeDtypeStruct((S, D_MODEL), F32), jax.ShapeDtypeStruct((S, D_MODEL), BF16),
                   jax.ShapeDtypeStruct((S, LANE), F32)],
        compiler_params=_params("parallel"),
        name="merge_router",
    )(x2d, ya, yr, gta, gtr, wa, wr, wo, nw, wrt, brt)


MOE_TM = 1024


def _moe_kernel(h_ref, comb_ref, x1_ref, wg_ref, wu_ref, wd_ref, nw_ref, out_ref, acc_ref):
    e = pl.program_id(1)

    @pl.when(e == 0)
    def _():
        acc_ref[...] = jnp.zeros_like(acc_ref)

    h = h_ref[...]
    comb = comb_ref[...]
    lane = lax.broadcasted_iota(jnp.int32, comb.shape, 1)
    c = jnp.sum(jnp.where(lane == e, comb, 0.0), axis=-1, keepdims=True)
    hid = jax.nn.silu(_dot(h, wg_ref[0])) * _dot(h, wu_ref[0])
    acc_ref[...] += _dot(hid.astype(BF16), wd_ref[0]) * c

    @pl.when(e == pl.num_programs(1) - 1)
    def _():
        x2 = x1_ref[...] + acc_ref[...]
        out_ref[...] = x2 * lax.rsqrt(jnp.mean(x2 * x2, axis=-1, keepdims=True) + EPS) * nw_ref[...]


def _moe(h, comb, x1, wg, wu, wd, nw):
    S = h.shape[0]
    tm = MOE_TM
    row = lambda w: pl.BlockSpec((tm, w), lambda i, e: (i, 0))
    return pl.pallas_call(
        _moe_kernel,
        grid=(S // tm, N_EXPERTS),
        in_specs=[row(D_MODEL), row(LANE), row(D_MODEL),
                  pl.BlockSpec((1, D_MODEL, EXPERT_FF), lambda i, e: (e, 0, 0)),
                  pl.BlockSpec((1, D_MODEL, EXPERT_FF), lambda i, e: (e, 0, 0)),
                  pl.BlockSpec((1, EXPERT_FF, D_MODEL), lambda i, e: (e, 0, 0)),
                  _const_spec((1, D_MODEL))],
        out_specs=row(D_MODEL),
        out_shape=jax.ShapeDtypeStruct((S, D_MODEL), F32),
        scratch_shapes=[pltpu.VMEM((tm, D_MODEL), F32)],
        compiler_params=_params("parallel", "arbitrary"),
        name="moe_experts",
    )(h, comb, x1, wg, wu, wd, nw)


def _positional_tables(S):
    half = RET_QK_DIM // 2
    inv = ROPE_BASE ** (-jnp.arange(half, dtype=F32) / half)
    ang = jnp.arange(S)[:, None].astype(F32) * inv[None, :]
    cos, sin = jnp.cos(ang), jnp.sin(ang)
    cos2 = jnp.concatenate([cos, cos], axis=-1)
    sin2 = jnp.concatenate([-sin, sin], axis=-1)
    C, H = RET_CHUNK, RET_HEADS
    log_gamma = jnp.log(1.0 - 2.0 ** (-5.0 - jnp.arange(H, dtype=F32)))
    idx = jnp.arange(C, dtype=F32)
    diff = idx[:, None] - idx[None, :]
    dmask = jnp.where(diff >= 0, jnp.exp(jnp.maximum(diff, 0.0)[None] * log_gamma[:, None, None]), 0.0)
    qd = jnp.exp((idx + 1.0)[None, :] * log_gamma[:, None])
    kd = jnp.exp((C - 1.0 - idx)[None, :] * log_gamma[:, None])
    cd = jnp.exp(C * log_gamma)
    qd = jnp.broadcast_to(qd[:, :, None], (H, C, RET_V_DIM))
    kd = jnp.broadcast_to(kd[:, :, None], (H, C, RET_QK_DIM))
    cd = jnp.broadcast_to(cd[:, None, None], (H, 1, RET_V_DIM))
    return cos2, sin2, dmask, qd, kd, cd


def _attention_tables(rel_bias):
    r = np.arange(Q_BLOCK)[None, :]
    cc = np.arange(3 * CMP_TILE)[:, None] - CMP_TILE
    d = r - CMP_STRIDE * (cc - (CMP_TILE - CMP_TOK_PER_QB)) - (CMP_LEN - 1)
    d = np.where(cc < 0, -2, np.where((cc >= CMP_TILE) | (d < 0), -1, d))
    tct = _bias_table_t(rel_bias, d, shift=True)
    d = r + Q_BLOCK - np.arange(2 * Q_BLOCK)[:, None]
    tnt = _bias_table_t(rel_bias, np.where(d < 0, -1, d), shift=True)
    d = r + WINDOW - np.arange(WIN_KEYS)[:, None]
    twt = _bias_table_t(rel_bias, np.where((d < 0) | (d >= WINDOW), -1, d), shift=False)
    kk = np.arange(SEL_WIN * SEL_BLOCK)[:, None] // SEL_BLOCK
    et = jnp.where(jnp.asarray(kk == np.arange(SEL_WIN)[None, :]), -NEG, 0.0).astype(BF16)
    return tct, tnt, twt, et


def _layer(x2d, rel_bias, norm_mix, w_in, cmp_pos_k, cmp_k_w1, cmp_k_w2, cmp_pos_v, cmp_v_w1, cmp_v_w2,
           ret_norm, w_nsa_up, w_ret_up, w_out, norm_ffn, w_rg, b_rg, w_re, b_re, moe_wg, moe_wu, moe_wd,
           final_norm_w):
    S = x2d.shape[0]
    G, dh = NSA_KV_GROUPS, NSA_HEAD_DIM
    assert S % (CMP_TILE * CMP_STRIDE) == 0 and S // SEL_BLOCK >= SEL_TOP_N

    o = np.cumsum((0,) + (NSA_Q_W, 6 * NSA_KV_W, 3 * NSA_HEADS, RET_QK_W, RET_QK_W, RET_V_W, RET_V_W,
                          D_MODEL, D_MODEL))
    seg = lambda n: w_in[:, o[n]:o[n + 1]]
    w_cat = jnp.concatenate([seg(0), seg(1), seg(3), seg(4), seg(5), seg(6), seg(7), seg(8), seg(2),
                             jnp.zeros((D_MODEL, LANE - 3 * NSA_HEADS), F32)], axis=1).astype(BF16)
    cos2, sin2, dmask, qd, kd, cd = _positional_tables(S)

    (qa, kcmp, vcmp, ksel, vsel, kwin, vwin, ga, qr, kr, vr, gr, gta, gtr) = _input_projection(
        x2d, norm_mix.reshape(1, D_MODEL), w_cat, cos2, sin2)

    nch = S // CMP_STRIDE
    chunks = lambda a: a.reshape(nch, CMP_STRIDE, G, dh).transpose(2, 0, 1, 3).reshape(G, nch, CMP_STRIDE * dh)
    xa = jnp.stack([chunks(kcmp), chunks(vcmp)])
    xb = jnp.concatenate([xa[:, :, 1:], jnp.zeros_like(xa[:, :, :1])], axis=2)
    pos = jnp.stack([cmp_pos_k, cmp_pos_v]).reshape(2, 2, CMP_STRIDE * dh)
    w1 = jnp.stack([cmp_k_w1, cmp_v_w1]).astype(BF16)
    w2 = jnp.stack([cmp_k_w2, cmp_v_w2]).astype(BF16)
    cmp_tok = _compress(xa, xb, pos, w1, w2)
    kc = cmp_tok[0].astype(BF16)
    vct = cmp_tok[1].transpose(0, 2, 1).astype(BF16)

    tct, tnt, twt, et = _attention_tables(rel_bias)

    def values_t(v, pad):
        vt = v.reshape(S, G, dh).transpose(1, 2, 0)
        extra = jnp.concatenate([jnp.ones((G, 1, S), BF16), jnp.zeros((G, PV_ROWS - dh - 1, S), BF16)], axis=1)
        return jnp.pad(jnp.concatenate([vt, extra], axis=1), ((0, 0), (0, 0), (pad, 0)))

    q_t = qa.T.reshape(NSA_HEADS, dh, S)
    gates_t = ga.T[:4 * NSA_HEADS]
    ya_t = _nsa_attention(q_t, gates_t, kc, vct, tct,
                          jnp.pad(ksel, ((Q_BLOCK, 0), (0, 0))), values_t(vsel, Q_BLOCK), et, tnt,
                          jnp.pad(kwin, ((WINDOW, 0), (0, 0))), values_t(vwin, WINDOW), twt)
    ya = ya_t.reshape(NSA_Q_W, S).T

    yr = _retention(qr, kr, vr, gr, ret_norm.reshape(1, RET_V_W), dmask, qd, kd, cd)

    wrt = jnp.concatenate([w_re, w_rg, jnp.zeros((D_MODEL, LANE - N_EXPERTS - N_GROUPS), F32)], axis=1).astype(BF16)
    brt = jnp.concatenate([b_re, b_rg, jnp.zeros((LANE - N_EXPERTS - N_GROUPS,), F32)]).reshape(1, LANE)
    x1, h2, comb = _merge(x2d, ya, yr, gta, gtr, w_nsa_up.astype(BF16), w_ret_up.astype(BF16),
                          w_out.astype(BF16), norm_ffn.reshape(1, D_MODEL), wrt, brt)
    return _moe(h2, comb, x1, moe_wg.astype(BF16), moe_wu.astype(BF16), moe_wd.astype(BF16), final_norm_w)


def kernel(x, rel_bias, norm_mix, w_in, cmp_pos_k, cmp_k_w1, cmp_k_w2, cmp_pos_v, cmp_v_w1, cmp_v_w2, ret_norm, w_nsa_up, w_ret_up, w_out, norm_ffn, w_router_group, b_router_group, w_router_expert, b_router_expert, moe_w_gate, moe_w_up, moe_w_down, norm_final):
    B, S, D = x.shape
    assert B == 1 and norm_mix.shape[0] == 1, "single sequence, depth 1"
    out = _layer(x[0], rel_bias, norm_mix[0], w_in[0], cmp_pos_k[0], cmp_k_w1[0], cmp_k_w2[0],
                 cmp_pos_v[0], cmp_v_w1[0], cmp_v_w2[0], ret_norm[0], w_nsa_up[0], w_ret_up[0], w_out[0],
                 norm_ffn[0], w_router_group[0], b_router_group[0], w_router_expert[0], b_router_expert[0],
                 moe_w_gate[0], moe_w_up[0], moe_w_down[0], norm_final.reshape(1, D))
    return out[None]
```

```python
import functools
import math

import numpy as np
import jax
import jax.numpy as jnp
from jax import lax
from jax.experimental import pallas as pl
from jax.experimental.pallas import tpu as pltpu

F32 = jnp.float32
BF16 = jnp.bfloat16

D_MODEL = 1024
NSA_HEADS = 8
NSA_KV_GROUPS = 2
NSA_HPG = NSA_HEADS // NSA_KV_GROUPS
NSA_HEAD_DIM = 64
CMP_LEN = 32
CMP_STRIDE = 16
CMP_HIDDEN = 256
SEL_BLOCK = 64
SEL_TOP_N = 16
N_LOCAL_SEL = 2
WINDOW = 512
Q_BLOCK = 128
RET_HEADS = 4
RET_QK_DIM = 128
RET_V_DIM = 256
RET_CHUNK = 128
ROPE_BASE = 10000.0
REL_BUCKETS = 32
REL_MAX_DIST = 128
N_GROUPS = 4
EXPERTS_PER_GROUP = 8
N_EXPERTS = N_GROUPS * EXPERTS_PER_GROUP
EXPERT_FF = 256
EPS = 1e-6

NSA_Q_W = NSA_HEADS * NSA_HEAD_DIM
NSA_KV_W = NSA_KV_GROUPS * NSA_HEAD_DIM
RET_QK_W = RET_HEADS * RET_QK_DIM
RET_V_W = RET_HEADS * RET_V_DIM

LANE = 128
NEG = -1e30
VMEM_LIMIT = 56 * 1024 * 1024

CMP_TILE = 128
QB_PER_CMP_TILE = CMP_TILE * CMP_STRIDE // Q_BLOCK
CMP_TOK_PER_QB = Q_BLOCK // CMP_STRIDE
SCR_PAD = 8
SEL_TK = 512
SEL_WIN = 128
SEL_WIN_TILES = SEL_WIN * SEL_BLOCK // SEL_TK
PV_ROWS = 80
WIN_KEYS = WINDOW + Q_BLOCK
QW = NSA_HPG * Q_BLOCK


def _dot(a, b):
    return jnp.dot(a, b, preferred_element_type=F32)


def _dot_nt(a, b):
    return lax.dot_general(a, b, (((1,), (1,)), ((), ())), preferred_element_type=F32)


def _const_spec(shape):
    nd = len(shape)
    return pl.BlockSpec(shape, lambda *_: (0,) * nd, pipeline_mode=pl.Buffered(1))


def _params(*sem):
    return pltpu.CompilerParams(dimension_semantics=sem, vmem_limit_bytes=VMEM_LIMIT)


_C_QA = 0
_C_KV = _C_QA + NSA_Q_W
_C_QR = _C_KV + 6 * NSA_KV_W
_C_KR = _C_QR + RET_QK_W
_C_VR = _C_KR + RET_QK_W
_C_GR = _C_VR + RET_V_W
_C_GTA = _C_GR + RET_V_W
_C_GTR = _C_GTA + D_MODEL
_C_GA = _C_GTR + D_MODEL
_C_END = _C_GA + LANE
IN_TM = 256


def _inproj_kernel(x_ref, nw_ref, w_ref, cos_ref, sin_ref,
                   qa_ref, kcmp_ref, vcmp_ref, ksel_ref, vsel_ref, kwin_ref, vwin_ref, ga_ref,
                   qr_ref, kr_ref, vr_ref, gr_ref, gta_ref, gtr_ref):
    x = x_ref[...]
    h = x * lax.rsqrt(jnp.mean(x * x, axis=-1, keepdims=True) + EPS) * nw_ref[...]
    hb = h.astype(BF16)

    def proj(a, width):
        return _dot(hb, w_ref[:, a:a + width])

    qa_ref[...] = (proj(_C_QA, NSA_Q_W) * (NSA_HEAD_DIM ** -0.5)).astype(BF16)
    kv = proj(_C_KV, 6 * NSA_KV_W)
    kcmp_ref[...] = kv[:, 0 * LANE:1 * LANE]
    vcmp_ref[...] = kv[:, 1 * LANE:2 * LANE]
    ksel_ref[...] = kv[:, 2 * LANE:3 * LANE].astype(BF16)
    vsel_ref[...] = kv[:, 3 * LANE:4 * LANE].astype(BF16)
    kwin_ref[...] = kv[:, 4 * LANE:5 * LANE].astype(BF16)
    vwin_ref[...] = kv[:, 5 * LANE:6 * LANE].astype(BF16)
    ga_ref[...] = jax.nn.sigmoid(proj(_C_GA, LANE))

    cos2 = cos_ref[...]
    sin2 = sin_ref[...]
    qr = proj(_C_QR, RET_QK_W)
    kr = proj(_C_KR, RET_QK_W)
    for hd in range(RET_HEADS):
        sl = slice(hd * RET_QK_DIM, (hd + 1) * RET_QK_DIM)
        qh = qr[:, sl]
        kh = kr[:, sl]
        qr_ref[:, sl] = (qh * cos2 + pltpu.roll(qh, RET_QK_DIM // 2, 1) * sin2).astype(BF16)
        kr_ref[:, sl] = (kh * cos2 + pltpu.roll(kh, RET_QK_DIM // 2, 1) * sin2) * (RET_QK_DIM ** -0.5)
    vr_ref[...] = proj(_C_VR, RET_V_W).astype(BF16)
    gr_ref[...] = proj(_C_GR, RET_V_W)
    gta_ref[...] = jax.nn.sigmoid(proj(_C_GTA, D_MODEL))
    gtr_ref[...] = jax.nn.sigmoid(proj(_C_GTR, D_MODEL))


def _input_projection(x2d, norm_w, w_cat, cos2, sin2):
    S = x2d.shape[0]
    tm = IN_TM
    row = lambda w: pl.BlockSpec((tm, w), lambda i: (i, 0))
    outs = [
        (NSA_Q_W, BF16),
        (LANE, F32), (LANE, F32),
        (LANE, BF16), (LANE, BF16),
        (LANE, BF16), (LANE, BF16),
        (LANE, F32),
        (RET_QK_W, BF16), (RET_QK_W, F32),
        (RET_V_W, BF16), (RET_V_W, F32),
        (D_MODEL, F32), (D_MODEL, F32),
    ]
    return pl.pallas_call(
        _inproj_kernel,
        grid=(S // tm,),
        in_specs=[row(D_MODEL), _const_spec((1, D_MODEL)), _const_spec((D_MODEL, _C_END)),
                  row(LANE), row(LANE)],
        out_specs=[row(w) for w, _ in outs],
        out_shape=[jax.ShapeDtypeStruct((S, w), dt) for w, dt in outs],
        compiler_params=_params("parallel"),
        name="in_proj",
    )(x2d, norm_w, w_cat, cos2, sin2)


def _compress_kernel(xa_ref, xb_ref, pos_ref, w1_ref, w2_ref, out_ref):
    half = CMP_STRIDE * NSA_HEAD_DIM
    a = (xa_ref[0, 0] + pos_ref[0, 0:1, :]).astype(BF16)
    b = (xb_ref[0, 0] + pos_ref[0, 1:2, :]).astype(BF16)
    hid = _dot(a, w1_ref[0, :half, :]) + _dot(b, w1_ref[0, half:, :])
    out_ref[0, 0] = _dot(jax.nn.gelu(hid).astype(BF16), w2_ref[0])


def _compress(xa, xb, pos, w1, w2):
    _, G, nch, width = xa.shape
    xspec = pl.BlockSpec((1, 1, nch, width), lambda t, g: (t, g, 0, 0))
    return pl.pallas_call(
        _compress_kernel,
        grid=(2, G),
        in_specs=[xspec, xspec,
                  pl.BlockSpec((1, 2, width), lambda t, g: (t, 0, 0)),
                  pl.BlockSpec((1, 2 * width, CMP_HIDDEN), lambda t, g: (t, 0, 0)),
                  pl.BlockSpec((1, CMP_HIDDEN, NSA_HEAD_DIM), lambda t, g: (t, 0, 0))],
        out_specs=pl.BlockSpec((1, 1, nch, NSA_HEAD_DIM), lambda t, g: (t, g, 0, 0)),
        out_shape=jax.ShapeDtypeStruct((2, G, nch, NSA_HEAD_DIM), F32),
        compiler_params=_params("parallel", "parallel"),
        name="nsa_compress",
    )(xa, xb, pos, w1, w2)


def _t5_bucket_np(dist):
    dist = np.maximum(dist, 0)
    max_exact = REL_BUCKETS // 2
    d_f = np.maximum(dist, 1).astype(np.float32)
    large = max_exact + (np.log(d_f / np.float32(max_exact)) / np.float32(math.log(REL_MAX_DIST / max_exact))
                         * np.float32(REL_BUCKETS - max_exact)).astype(np.int32)
    large = np.minimum(large, REL_BUCKETS - 1)
    return np.where(dist < max_exact, dist, large)


def _bias_table_t(rel_bias, dist, shift):
    bucket = jnp.asarray(_t5_bucket_np(dist), jnp.int32)
    rb = rel_bias - rel_bias[REL_BUCKETS - 1][None, :] if shift else rel_bias
    tab = jnp.zeros((NSA_HEADS,) + dist.shape, F32)
    for b in range(REL_BUCKETS):
        tab = jnp.where(bucket[None] == b, rb[b][:, None, None], tab)
    tab = jnp.where(jnp.asarray(dist == -1)[None], NEG, tab)
    tab = jnp.where(jnp.asarray(dist == -2)[None], 0.0, tab)
    K, Q = dist.shape
    return tab.reshape(NSA_KV_GROUPS, NSA_HPG, K, Q).transpose(0, 2, 1, 3).reshape(NSA_KV_GROUPS, K, NSA_HPG * Q)


def _nsa_kernel(n_keys, q_ref, ga_ref, kc_ref, vct_ref, tct_ref, ksel_ref, vselt_ref, et_ref, tnt_ref,
                kwin_ref, vwint_ref, twt_ref, out_ref, sc_ref, ps_ref, sa_ref, sb_ref):
    i = pl.program_id(0)
    dh = NSA_HEAD_DIM
    nb = (ps_ref.shape[0] - 2 * SCR_PAD) // (SEL_BLOCK // CMP_STRIDE)

    @pl.when(i == 0)
    def _():
        sc_ref[...] = jnp.zeros_like(sc_ref)
        ps_ref[...] = jnp.zeros_like(ps_ref)

    def colmax(x):
        return jnp.max(x, axis=0, keepdims=True)

    def tile4(x):
        return jnp.concatenate([x] * NSA_HPG, axis=1)

    gates = ga_ref[...]
    tl = i // QB_PER_CMP_TILE
    sub = i % QB_PER_CMP_TILE
    tp = tl - 1
    groups = range(NSA_KV_GROUPS)
    wq_g, o_c_g, sel_g, selm_g = [], [], [], []

    for g in groups:
        q_t = jnp.concatenate([q_ref[NSA_HPG * g + j] for j in range(NSA_HPG)], axis=1)
        zeros = jnp.zeros((dh, QW), BF16)
        wq = jnp.concatenate([q_t, zeros] if g == 0 else [zeros, q_t], axis=0)

        def cmp_scores(t):
            return _dot(kc_ref[g, pl.ds(pl.multiple_of(t * CMP_TILE, CMP_TILE), CMP_TILE), :], q_t)

        def sc_rows(t):
            return pl.ds(pl.multiple_of(SCR_PAD + t * CMP_TILE, 8), CMP_TILE)

        def cmp_far(t, m):
            s = cmp_scores(t)
            sc_ref[sc_rows(t), :] = s
            return jnp.maximum(m, colmax(s))

        m_c = lax.fori_loop(0, jnp.maximum(tp, 0), cmp_far, jnp.full((1, QW), NEG, F32))
        tpc = jnp.maximum(tp, 0)
        off = CMP_TILE - CMP_TOK_PER_QB - CMP_TOK_PER_QB * sub
        s = (cmp_scores(tpc) + tct_ref[g, pl.ds(pl.multiple_of(off, 8), CMP_TILE), :]
             + jnp.where(tp < 0, NEG, 0.0))
        sc_ref[sc_rows(tpc), :] = s
        m_c = jnp.maximum(m_c, colmax(s))
        s = cmp_scores(tl) + tct_ref[g, pl.ds(pl.multiple_of(off + CMP_TILE, 8), CMP_TILE), :]
        sc_ref[sc_rows(tl), :] = s
        m_c = jnp.maximum(m_c, colmax(s))

        def cmp_exp(t, carry):
            l, acc = carry
            e = jnp.exp(sc_ref[sc_rows(t), :] - m_c)
            sc_ref[sc_rows(t), :] = e
            vt = vct_ref[g, :, pl.ds(pl.multiple_of(t * CMP_TILE, CMP_TILE), CMP_TILE)]
            return l + jnp.sum(e, axis=0, keepdims=True), acc + _dot(vt, e.astype(BF16))

        l_c, acc_c = lax.fori_loop(0, tl + 1, cmp_exp, (jnp.zeros((1, QW), F32), jnp.zeros((dh, QW), F32)))
        inv_c = jnp.where(m_c > 0.5 * NEG, 1.0 / jnp.maximum(l_c, 1e-30), 0.0)
        o_c = acc_c * inv_c

        def cmp_psum(t, _):
            p = sc_ref[sc_rows(t), :] * inv_c
            ps = p[:, 0:Q_BLOCK]
            for j in range(1, NSA_HPG):
                ps = ps + p[:, j * Q_BLOCK:(j + 1) * Q_BLOCK]
            ps_ref[sc_rows(t), :] = ps.astype(BF16).astype(F32)
            return 0

        lax.fori_loop(0, tl + 1, cmp_psum, 0)
        per_blk = SEL_BLOCK // CMP_STRIDE
        imp = ps_ref[pl.ds(SCR_PAD - 1, nb, stride=per_blk), :]
        for k in range(1, CMP_LEN // CMP_STRIDE + per_blk - 1):
            imp = imp + ps_ref[pl.ds(SCR_PAD - 1 + k, nb, stride=per_blk), :]

        n_io = lax.broadcasted_iota(jnp.int32, (nb, Q_BLOCK), 0)
        r_io = lax.broadcasted_iota(jnp.int32, (nb, Q_BLOCK), 1)
        back = (i * Q_BLOCK + r_io) // SEL_BLOCK - n_io
        valid = back >= 0
        forced = (n_io == 0) | (valid & (back < N_LOCAL_SEL))
        work = jnp.where(forced, -jnp.inf, jnp.where(valid, imp, -1e9))
        sel = jnp.where(forced, 1.0, 0.0)
        n_f = n_io.astype(F32)
        for _ in range(SEL_TOP_N - 1 - N_LOCAL_SEL):
            mx = colmax(work)
            first = jnp.min(jnp.where(work == mx, n_f, float(nb)), axis=0, keepdims=True)
            pick = n_f == first
            sel = jnp.where(pick, 1.0, sel)
            work = jnp.where(pick, -jnp.inf, work)

        far_blocks = (i - 1) * (Q_BLOCK // SEL_BLOCK)
        wq_g.append(wq)
        o_c_g.append(o_c)
        sel_g.append(sel)
        selm_g.append((jnp.where(n_io < far_blocks, sel, 0.0) - 1.0).astype(BF16))

    far_end = (i - 1) * Q_BLOCK
    n_far = (jnp.maximum(far_end, 0) + SEL_TK - 1) // SEL_TK

    def flash_step(carry, s, vt):
        m_i, acc = carry
        m_new = jnp.maximum(m_i, colmax(s))
        alpha = jnp.exp(m_i - m_new)
        p = jnp.exp(s - m_new).astype(BF16)
        return m_new, alpha * acc + _dot(vt, p)

    carry = tuple((jnp.full((1, QW), NEG, F32), jnp.zeros((PV_ROWS, QW), F32)) for _ in groups)
    for w in range(nb // SEL_WIN):
        w_far = [jnp.concatenate([tile4(selm_g[g][w * SEL_WIN:(w + 1) * SEL_WIN]), wq_g[g]], axis=0)
                 for g in groups]

        lo = w * SEL_WIN_TILES
        last = min((w + 1) * SEL_WIN_TILES, n_keys // SEL_TK) - 1
        trips = (jnp.minimum(n_far, last + 1) - lo + 1) // 2

        def far_scores(kt, s_ref, w=w, w_far=w_far):
            k0 = pl.multiple_of(kt * SEL_TK, SEL_TK)
            e0 = pl.multiple_of((kt - w * SEL_WIN_TILES) * SEL_TK, SEL_TK)
            lhs = jnp.concatenate([et_ref[pl.ds(e0, SEL_TK), :],
                                   ksel_ref[pl.ds(k0 + Q_BLOCK, SEL_TK), :]], axis=1)
            for g in groups:
                s_ref[g] = _dot(lhs, w_far[g])

        def far_update(kt, s_ref, carry):
            k0 = pl.multiple_of(kt * SEL_TK, SEL_TK)
            return tuple(flash_step(carry[g], s_ref[g], vselt_ref[g, :, pl.ds(k0 + Q_BLOCK, SEL_TK)])
                         for g in groups)

        def far_body(u, carry, lo=lo, last=last, far_scores=far_scores):
            kt = lo + 2 * u
            far_scores(kt + 1, sb_ref)
            carry = far_update(kt, sa_ref, carry)
            far_scores(jnp.minimum(kt + 2, last), sa_ref)
            return far_update(kt + 1, sb_ref, carry)

        far_scores(jnp.int32(lo), sa_ref)
        carry = lax.fori_loop(0, trips, far_body, carry)

    for g in groups:
        wq, o_c, sel = wq_g[g], o_c_g[g], sel_g[g]
        k0n = pl.multiple_of(i * Q_BLOCK, Q_BLOCK)
        near_blocks = 2 * Q_BLOCK // SEL_BLOCK
        pb = lax.broadcasted_iota(jnp.int32, (16, nb), 0)
        pn = lax.broadcasted_iota(jnp.int32, (16, nb), 1)
        pick_rows = jnp.where((pn == far_blocks + pb) & (pb < near_blocks), 1.0, 0.0).astype(BF16)
        sel_rows = _dot(pick_rows, sel.astype(BF16))
        add_near = jnp.concatenate(
            [jnp.broadcast_to((sel_rows[b:b + 1, :] - 1.0) * (-NEG), (SEL_BLOCK, Q_BLOCK)) for b in range(near_blocks)],
            axis=0)
        s = _dot(ksel_ref[pl.ds(k0n, 2 * Q_BLOCK), :], wq) + tnt_ref[g] + tile4(add_near)
        m_s, acc_s = flash_step(carry[g], s, vselt_ref[g, :, pl.ds(k0n, 2 * Q_BLOCK)])
        o_s = acc_s[0:dh] * (1.0 / jnp.maximum(acc_s[dh:dh + 1], 1e-30))

        s = _dot(kwin_ref[pl.ds(k0n, WIN_KEYS), :], wq) + twt_ref[g]
        pos = lax.broadcasted_iota(jnp.int32, (WIN_KEYS, QW), 0) + (i * Q_BLOCK - WINDOW)
        s = jnp.where(pos >= 0, s, NEG)
        e = jnp.exp(s - colmax(s)).astype(BF16)
        acc_w = _dot(vwint_ref[g, :, pl.ds(k0n, WIN_KEYS)], e)
        o_w = acc_w[0:dh] * (1.0 / jnp.maximum(acc_w[dh:dh + 1], 1e-30))

        for j in range(NSA_HPG):
            hd = NSA_HPG * g + j
            cols = slice(j * Q_BLOCK, (j + 1) * Q_BLOCK)
            o = (gates[hd:hd + 1] * o_c[:, cols]
                 + gates[NSA_HEADS + hd:NSA_HEADS + hd + 1] * o_s[:, cols]
                 + gates[2 * NSA_HEADS + hd:2 * NSA_HEADS + hd + 1] * o_w[:, cols])
            out_ref[hd] = o.astype(out_ref.dtype)


def _nsa_attention(q_t, gates_t, kc, vct, tct, ksel, vselt, et, tnt, kwin, vwint, twt):
    H, dh, S = q_t.shape
    n_qb = S // Q_BLOCK
    n_tok = S // CMP_STRIDE
    nb = -(-(S // SEL_BLOCK) // SEL_WIN) * SEL_WIN
    ps_rows = nb * (SEL_BLOCK // CMP_STRIDE) + 2 * SCR_PAD
    full = lambda a: _const_spec(a.shape)
    return pl.pallas_call(
        functools.partial(_nsa_kernel, S),
        grid=(n_qb,),
        in_specs=[pl.BlockSpec((H, dh, Q_BLOCK), lambda i: (0, 0, i)),
                  pl.BlockSpec((gates_t.shape[0], Q_BLOCK), lambda i: (0, i)),
                  full(kc), full(vct), full(tct), full(ksel), full(vselt), full(et), full(tnt),
                  full(kwin), full(vwint), full(twt)],
        out_specs=pl.BlockSpec((H, dh, Q_BLOCK), lambda i: (0, 0, i)),
        out_shape=jax.ShapeDtypeStruct((H, dh, S), BF16),
        scratch_shapes=[pltpu.VMEM((SCR_PAD + n_tok, QW), F32),
                        pltpu.VMEM((ps_rows, Q_BLOCK), F32),
                        pltpu.VMEM((NSA_KV_GROUPS, SEL_TK, QW), F32),
                        pltpu.VMEM((NSA_KV_GROUPS, SEL_TK, QW), F32)],
        compiler_params=_params("arbitrary"),
        name="nsa_attention",
    )(q_t, gates_t, kc, vct, tct, ksel, vselt, et, tnt, kwin, vwint, twt)


def _retention_kernel(q_ref, k_ref, v_ref, g_ref, nw_ref, dmask_ref, qd_ref, kd_ref, cd_ref, out_ref, state_ref):
    @pl.when(pl.program_id(0) == 0)
    def _():
        state_ref[...] = jnp.zeros_like(state_ref)

    for hd in range(RET_HEADS):
        ks = slice(hd * RET_QK_DIM, (hd + 1) * RET_QK_DIM)
        vs = slice(hd * RET_V_DIM, (hd + 1) * RET_V_DIM)
        q = q_ref[:, ks]
        k = k_ref[:, ks]
        v = v_ref[:, vs]
        state = state_ref[hd]
        inner = _dot_nt(q, k.astype(BF16)) * dmask_ref[hd]
        y = _dot(inner.astype(BF16), v) + _dot(q, state.astype(BF16)) * qd_ref[hd]
        kd = k * kd_ref[hd]
        state_ref[hd] = state * cd_ref[hd] + _dot(kd.T.astype(BF16), v)
        mu = jnp.mean(y, axis=-1, keepdims=True)
        yc = y - mu
        var = jnp.mean(yc * yc, axis=-1, keepdims=True)
        yn = yc * lax.rsqrt(var + EPS) * nw_ref[:, vs]
        out_ref[:, vs] = (jax.nn.silu(g_ref[:, vs]) * yn).astype(out_ref.dtype)


def _retention(q, k, v, g, norm_w, dmask, qd, kd, cd):
    S = q.shape[0]
    C = RET_CHUNK
    row = lambda w: pl.BlockSpec((C, w), lambda i: (i, 0))
    full = lambda a: _const_spec(a.shape)
    return pl.pallas_call(
        _retention_kernel,
        grid=(S // C,),
        in_specs=[row(RET_QK_W), row(RET_QK_W), row(RET_V_W), row(RET_V_W),
                  full(norm_w), full(dmask), full(qd), full(kd), full(cd)],
        out_specs=row(RET_V_W),
        out_shape=jax.ShapeDtypeStruct((S, RET_V_W), BF16),
        scratch_shapes=[pltpu.VMEM((RET_HEADS, RET_QK_DIM, RET_V_DIM), F32)],
        compiler_params=_params("arbitrary"),
        name="retention",
    )(q, k, v, g, norm_w, dmask, qd, kd, cd)


MERGE_TM = 256
GRP_LANE = 40


def _merge_kernel(x_ref, ya_ref, yr_ref, gta_ref, gtr_ref, wa_ref, wr_ref, wo_ref, nw_ref, wrt_ref, brt_ref,
                  x1_ref, h_ref, comb_ref):
    merged = gta_ref[...] * _dot(ya_ref[...], wa_ref[...]) + gtr_ref[...] * _dot(yr_ref[...], wr_ref[...])
    x1 = x_ref[...] + _dot(merged.astype(BF16), wo_ref[...])
    x1_ref[...] = x1
    h = (x1 * lax.rsqrt(jnp.mean(x1 * x1, axis=-1, keepdims=True) + EPS) * nw_ref[...]).astype(BF16)
    h_ref[...] = _pack_bf16_pairs(h.astype(F32))
    logits = _dot(h, wrt_ref[...]) + brt_ref[...]
    lane = lax.broadcasted_iota(jnp.int32, logits.shape, 1)
    lane_f = lane.astype(F32)
    big = float(LANE)
    is_grp = (lane >= N_EXPERTS) & (lane < N_EXPERTS + N_GROUPS)
    lg = jnp.where(is_grp, logits, -jnp.inf)
    eg = jnp.where(is_grp, jnp.exp(lg - jnp.max(lg, axis=-1, keepdims=True)), 0.0)
    pg = eg / jnp.sum(eg, axis=-1, keepdims=True)
    p_grp = jnp.max(pg, axis=-1, keepdims=True)
    grp = jnp.min(jnp.where(is_grp & (pg == p_grp), lane_f, big), axis=-1, keepdims=True) - N_EXPERTS
    in_grp = (lane_f >= grp * EXPERTS_PER_GROUP) & (lane_f < (grp + 1.0) * EXPERTS_PER_GROUP)
    le = jnp.where(in_grp, logits, -jnp.inf)
    m1 = jnp.max(le, axis=-1, keepdims=True)
    i1 = jnp.min(jnp.where(le == m1, lane_f, big), axis=-1, keepdims=True)
    le2 = jnp.where(lane_f == i1, -jnp.inf, le)
    m2 = jnp.max(le2, axis=-1, keepdims=True)
    i2 = jnp.min(jnp.where(le2 == m2, lane_f, big), axis=-1, keepdims=True)
    e2 = jnp.exp(m2 - m1)
    den = 1.0 + e2
    comb_ref[...] = (jnp.where(lane_f == i1, (1.0 / den) * p_grp, 0.0)
                     + jnp.where(lane_f == i2, (e2 / den) * p_grp, 0.0)
                     + jnp.where(lane == GRP_LANE, grp, 0.0))


def _pack_bf16_pairs(x):
    n = x.shape[1] // 2
    bits = pltpu.bitcast(x, jnp.uint32)
    return (bits[:, :n] >> 16) | (bits[:, n:] & jnp.uint32(0xFFFF0000))


def _unpack_bf16_pairs(w):
    return pltpu.bitcast(w << 16, F32), pltpu.bitcast(w & jnp.uint32(0xFFFF0000), F32)


def _merge(x2d, ya, yr, gta, gtr, wa, wr, wo, nw, wrt, brt):
    S = x2d.shape[0]
    tm = MERGE_TM
    row = lambda w: pl.BlockSpec((tm, w), lambda i: (i, 0))
    full = lambda a: _const_spec(a.shape)
    return pl.pallas_call(
        _merge_kernel,
        grid=(S // tm,),
        in_specs=[row(D_MODEL), row(NSA_Q_W), row(RET_V_W), row(D_MODEL), row(D_MODEL),
                  full(wa), full(wr), full(wo), full(nw), full(wrt), full(brt)],
        out_specs=[row(D_MODEL), row(D_MODEL // 2), row(LANE)],
        out_shape=[jax.ShapeDtypeStruct((S, D_MODEL), F32), jax.ShapeDtypeStruct((S, D_MODEL // 2), jnp.uint32),
                   jax.ShapeDtypeStruct((S, LANE), F32)],
        compiler_params=_params("parallel"),
        name="merge_router",
    )(x2d, ya, yr, gta, gtr, wa, wr, wo, nw, wrt, brt)


MOE_TM = 512
RANK_TM = 512
HALF = D_MODEL // 2


def _rank_kernel(comb_ref, tri_ref, rank_ref, cnt_ref, carry_ref):
    @pl.when(pl.program_id(0) == 0)
    def _():
        carry_ref[...] = jnp.zeros_like(carry_ref)

    comb = comb_ref[...]
    lane_f = lax.broadcasted_iota(jnp.int32, comb.shape, 1).astype(F32)
    onehot = jnp.where(lane_f == comb[:, GRP_LANE:GRP_LANE + 1], 1.0, 0.0)
    before = _dot(tri_ref[...], onehot.astype(BF16)) + carry_ref[...]
    rank_ref[...] = jnp.broadcast_to(jnp.sum(onehot * before, axis=-1, keepdims=True), comb.shape)
    carry_ref[...] += jnp.sum(onehot, axis=0, keepdims=True)
    cnt_ref[...] = jnp.broadcast_to(carry_ref[...], cnt_ref.shape)


def _group_rank(comb):
    S = comb.shape[0]
    tm = RANK_TM
    tri = jnp.asarray(np.tril(np.ones((tm, tm), np.float32), -1), BF16)
    return pl.pallas_call(
        _rank_kernel,
        grid=(S // tm,),
        in_specs=[pl.BlockSpec((tm, LANE), lambda i: (i, 0)), _const_spec((tm, tm))],
        out_specs=[pl.BlockSpec((tm, LANE), lambda i: (i, 0)), pl.BlockSpec((8, LANE), lambda i: (0, 0))],
        out_shape=[jax.ShapeDtypeStruct((S, LANE), F32), jax.ShapeDtypeStruct((8, LANE), F32)],
        scratch_shapes=[pltpu.VMEM((1, LANE), F32)],
        compiler_params=_params("arbitrary"),
        name="moe_group_rank",
    )(comb, tri)


def _permute_kernel(pos_ref, h_ref, comb_ref, hs_ref, cs_ref):
    i = pl.program_id(0)

    @pl.when(i == 0)
    def _():
        hs_ref[...] = jnp.zeros_like(hs_ref)
        cs_ref[...] = jnp.zeros_like(cs_ref)

    tm = h_ref.shape[0]

    def body(r, _):
        d = pos_ref[i * tm + r]
        hs_ref[pl.ds(d, 1), :] = h_ref[pl.ds(r, 1), :]
        cs_ref[pl.ds(d, 1), :] = comb_ref[pl.ds(r, 1), :]
        return 0

    lax.fori_loop(0, tm, body, 0, unroll=8)


def _permute(pos, hu, comb, n_sorted):
    S = hu.shape[0]
    tm = RANK_TM
    resident = lambda w: pl.BlockSpec((n_sorted, w), lambda i, p: (0, 0), pipeline_mode=pl.Buffered(1))
    return pl.pallas_call(
        _permute_kernel,
        grid_spec=pltpu.PrefetchScalarGridSpec(
            num_scalar_prefetch=1, grid=(S // tm,),
            in_specs=[pl.BlockSpec((tm, HALF), lambda i, p: (i, 0)), pl.BlockSpec((tm, LANE), lambda i, p: (i, 0))],
            out_specs=[resident(HALF), resident(LANE)]),
        out_shape=[jax.ShapeDtypeStruct((n_sorted, HALF), jnp.uint32), jax.ShapeDtypeStruct((n_sorted, LANE), F32)],
        compiler_params=_params("arbitrary"),
        name="moe_permute",
    )(pos, hu, comb)


def _moe_kernel(tg_ref, tv_ref, hs_ref, cs_ref, wg_ref, wu_ref, wd_ref, ys_ref, lo_ref, hi_ref, acc_ref):
    i = pl.program_id(0)
    e = pl.program_id(1)

    @pl.when(e == 0)
    def _():
        lo, hi = _unpack_bf16_pairs(hs_ref[...])
        lo_ref[...] = lo.astype(BF16)
        hi_ref[...] = hi.astype(BF16)
        acc_ref[...] = jnp.zeros_like(acc_ref)

    @pl.when(tv_ref[i] > 0)
    def _():
        lo, hi = lo_ref[...], hi_ref[...]
        gate = _dot(lo, wg_ref[0, :HALF, :]) + _dot(hi, wg_ref[0, HALF:, :])
        up = _dot(lo, wu_ref[0, :HALF, :]) + _dot(hi, wu_ref[0, HALF:, :])
        comb = cs_ref[...]
        lane = lax.broadcasted_iota(jnp.int32, comb.shape, 1)
        c = jnp.sum(jnp.where(lane == tg_ref[i] * EXPERTS_PER_GROUP + e, comb, 0.0), axis=-1, keepdims=True)
        acc_ref[...] += _dot((jax.nn.silu(gate) * up).astype(BF16), wd_ref[0]) * c

    @pl.when(e == pl.num_programs(1) - 1)
    def _():
        ys_ref[...] = _pack_bf16_pairs(acc_ref[...].astype(BF16).astype(F32))


def _moe(tile_group, tile_valid, hs, cs, wg, wu, wd):
    n_sorted = hs.shape[0]
    tm = MOE_TM
    row = lambda w: pl.BlockSpec((tm, w), lambda i, e, tg, tv: (i, 0))
    wspec = lambda a, b: pl.BlockSpec((1, a, b), lambda i, e, tg, tv: (tg[i] * EXPERTS_PER_GROUP + e, 0, 0))
    return pl.pallas_call(
        _moe_kernel,
        grid_spec=pltpu.PrefetchScalarGridSpec(
            num_scalar_prefetch=2, grid=(n_sorted // tm, EXPERTS_PER_GROUP),
            in_specs=[row(HALF), row(LANE), wspec(D_MODEL, EXPERT_FF), wspec(D_MODEL, EXPERT_FF),
                      wspec(EXPERT_FF, D_MODEL)],
            out_specs=row(HALF),
            scratch_shapes=[pltpu.VMEM((tm, HALF), BF16), pltpu.VMEM((tm, HALF), BF16),
                            pltpu.VMEM((tm, D_MODEL), F32)]),
        out_shape=jax.ShapeDtypeStruct((n_sorted, HALF), jnp.uint32),
        compiler_params=_params("parallel", "arbitrary"),
        name="moe_experts",
    )(tile_group, tile_valid, hs, cs, wg, wu, wd)


def _unpermute_kernel(pos_ref, ys_ref, x1_ref, nw_ref, out_ref, stage_ref):
    i = pl.program_id(0)
    tm = x1_ref.shape[0]

    def body(r, _):
        stage_ref[pl.ds(r, 1), :] = ys_ref[pl.ds(pos_ref[i * tm + r], 1), :]
        return 0

    lax.fori_loop(0, tm, body, 0, unroll=8)
    lo, hi = _unpack_bf16_pairs(stage_ref[...])
    x2 = x1_ref[...] + jnp.concatenate([lo, hi], axis=1)
    out_ref[...] = x2 * lax.rsqrt(jnp.mean(x2 * x2, axis=-1, keepdims=True) + EPS) * nw_ref[...]


def _unpermute_residual_norm(pos, ys, x1, nw):
    S = x1.shape[0]
    tm = RANK_TM
    return pl.pallas_call(
        _unpermute_kernel,
        grid_spec=pltpu.PrefetchScalarGridSpec(
            num_scalar_prefetch=1, grid=(S // tm,),
            in_specs=[pl.BlockSpec(ys.shape, lambda i, p: (0, 0), pipeline_mode=pl.Buffered(1)),
                      pl.BlockSpec((tm, D_MODEL), lambda i, p: (i, 0)),
                      pl.BlockSpec((1, D_MODEL), lambda i, p: (0, 0))],
            out_specs=pl.BlockSpec((tm, D_MODEL), lambda i, p: (i, 0)),
            scratch_shapes=[pltpu.VMEM((tm, HALF), jnp.uint32)]),
        out_shape=jax.ShapeDtypeStruct((S, D_MODEL), F32),
        compiler_params=_params("parallel"),
        name="moe_unpermute_norm",
    )(pos, ys, x1, nw)


def _hierarchical_moe(hu, comb, x1, wg, wu, wd, final_norm_w):
    S = hu.shape[0]
    tm = MOE_TM
    n_sorted = S + N_GROUPS * tm
    rank, cnt = _group_rank(comb)
    grp = comb[:, GRP_LANE].astype(jnp.int32)
    counts = cnt[0, :N_GROUPS].astype(jnp.int32)
    padded = (counts + tm - 1) // tm * tm
    ends = jnp.cumsum(padded)
    starts = ends - padded
    pos = rank[:, 0].astype(jnp.int32) + jnp.sum(
        jnp.where(grp[:, None] == jnp.arange(N_GROUPS)[None, :], starts[None, :], 0), axis=1)
    tile_start = jnp.arange(n_sorted // tm, dtype=jnp.int32) * tm
    tile_group = jnp.minimum(jnp.sum((tile_start[:, None] >= ends[None, :]).astype(jnp.int32), axis=1), N_GROUPS - 1)
    tile_valid = (tile_start < ends[N_GROUPS - 1]).astype(jnp.int32)
    hs, cs = _permute(pos, hu, comb, n_sorted)
    ys = _moe(tile_group, tile_valid, hs, cs, wg, wu, wd)
    return _unpermute_residual_norm(pos, ys, x1, final_norm_w)


def _positional_tables(S):
    half = RET_QK_DIM // 2
    inv = ROPE_BASE ** (-jnp.arange(half, dtype=F32) / half)
    ang = jnp.arange(S)[:, None].astype(F32) * inv[None, :]
    cos, sin = jnp.cos(ang), jnp.sin(ang)
    cos2 = jnp.concatenate([cos, cos], axis=-1)
    sin2 = jnp.concatenate([-sin, sin], axis=-1)
    C, H = RET_CHUNK, RET_HEADS
    log_gamma = jnp.log(1.0 - 2.0 ** (-5.0 - jnp.arange(H, dtype=F32)))
    idx = jnp.arange(C, dtype=F32)
    diff = idx[:, None] - idx[None, :]
    dmask = jnp.where(diff >= 0, jnp.exp(jnp.maximum(diff, 0.0)[None] * log_gamma[:, None, None]), 0.0)
    qd = jnp.exp((idx + 1.0)[None, :] * log_gamma[:, None])
    kd = jnp.exp((C - 1.0 - idx)[None, :] * log_gamma[:, None])
    cd = jnp.exp(C * log_gamma)
    qd = jnp.broadcast_to(qd[:, :, None], (H, C, RET_V_DIM))
    kd = jnp.broadcast_to(kd[:, :, None], (H, C, RET_QK_DIM))
    cd = jnp.broadcast_to(cd[:, None, None], (H, 1, RET_V_DIM))
    return cos2, sin2, dmask, qd, kd, cd


def _attention_tables(rel_bias):
    r = np.arange(Q_BLOCK)[None, :]
    cc = np.arange(3 * CMP_TILE)[:, None] - CMP_TILE
    d = r - CMP_STRIDE * (cc - (CMP_TILE - CMP_TOK_PER_QB)) - (CMP_LEN - 1)
    d = np.where(cc < 0, -2, np.where((cc >= CMP_TILE) | (d < 0), -1, d))
    tct = _bias_table_t(rel_bias, d, shift=True)
    d = r + Q_BLOCK - np.arange(2 * Q_BLOCK)[:, None]
    tnt = _bias_table_t(rel_bias, np.where(d < 0, -1, d), shift=True)
    d = r + WINDOW - np.arange(WIN_KEYS)[:, None]
    twt = _bias_table_t(rel_bias, np.where((d < 0) | (d >= WINDOW), -1, d), shift=False)
    kk = np.arange(SEL_WIN * SEL_BLOCK)[:, None] // SEL_BLOCK
    et = jnp.where(jnp.asarray(kk == np.arange(SEL_WIN)[None, :]), -NEG, 0.0).astype(BF16)
    return tct, tnt, twt, et


def _layer(x2d, rel_bias, norm_mix, w_in, cmp_pos_k, cmp_k_w1, cmp_k_w2, cmp_pos_v, cmp_v_w1, cmp_v_w2,
           ret_norm, w_nsa_up, w_ret_up, w_out, norm_ffn, w_rg, b_rg, w_re, b_re, moe_wg, moe_wu, moe_wd,
           final_norm_w):
    S = x2d.shape[0]
    G, dh = NSA_KV_GROUPS, NSA_HEAD_DIM
    assert S % (CMP_TILE * CMP_STRIDE) == 0 and S // SEL_BLOCK >= SEL_TOP_N

    o = np.cumsum((0,) + (NSA_Q_W, 6 * NSA_KV_W, 3 * NSA_HEADS, RET_QK_W, RET_QK_W, RET_V_W, RET_V_W,
                          D_MODEL, D_MODEL))
    seg = lambda n: w_in[:, o[n]:o[n + 1]]
    w_cat = jnp.concatenate([seg(0), seg(1), seg(3), seg(4), seg(5), seg(6), seg(7), seg(8), seg(2),
                             jnp.zeros((D_MODEL, LANE - 3 * NSA_HEADS), F32)], axis=1).astype(BF16)
    cos2, sin2, dmask, qd, kd, cd = _positional_tables(S)

    (qa, kcmp, vcmp, ksel, vsel, kwin, vwin, ga, qr, kr, vr, gr, gta, gtr) = _input_projection(
        x2d, norm_mix.reshape(1, D_MODEL), w_cat, cos2, sin2)

    nch = S // CMP_STRIDE
    chunks = lambda a: a.reshape(nch, CMP_STRIDE, G, dh).transpose(2, 0, 1, 3).reshape(G, nch, CMP_STRIDE * dh)
    xa = jnp.stack([chunks(kcmp), chunks(vcmp)])
    xb = jnp.concatenate([xa[:, :, 1:], jnp.zeros_like(xa[:, :, :1])], axis=2)
    pos = jnp.stack([cmp_pos_k, cmp_pos_v]).reshape(2, 2, CMP_STRIDE * dh)
    w1 = jnp.stack([cmp_k_w1, cmp_v_w1]).astype(BF16)
    w2 = jnp.stack([cmp_k_w2, cmp_v_w2]).astype(BF16)
    cmp_tok = _compress(xa, xb, pos, w1, w2)
    kc = cmp_tok[0].astype(BF16)
    vct = cmp_tok[1].transpose(0, 2, 1).astype(BF16)

    tct, tnt, twt, et = _attention_tables(rel_bias)

    def values_t(v, pad):
        vt = v.reshape(S, G, dh).transpose(1, 2, 0)
        extra = jnp.concatenate([jnp.ones((G, 1, S), BF16), jnp.zeros((G, PV_ROWS - dh - 1, S), BF16)], axis=1)
        return jnp.pad(jnp.concatenate([vt, extra], axis=1), ((0, 0), (0, 0), (pad, 0)))

    q_t = qa.T.reshape(NSA_HEADS, dh, S)
    gates_t = ga.T[:4 * NSA_HEADS]
    ya_t = _nsa_attention(q_t, gates_t, kc, vct, tct,
                          jnp.pad(ksel, ((Q_BLOCK, 0), (0, 0))), values_t(vsel, Q_BLOCK), et, tnt,
                          jnp.pad(kwin, ((WINDOW, 0), (0, 0))), values_t(vwin, WINDOW), twt)
    ya = ya_t.reshape(NSA_Q_W, S).T

    yr = _retention(qr, kr, vr, gr, ret_norm.reshape(1, RET_V_W), dmask, qd, kd, cd)

    wrt = jnp.concatenate([w_re, w_rg, jnp.zeros((D_MODEL, LANE - N_EXPERTS - N_GROUPS), F32)], axis=1).astype(BF16)
    brt = jnp.concatenate([b_re, b_rg, jnp.zeros((LANE - N_EXPERTS - N_GROUPS,), F32)]).reshape(1, LANE)
    x1, h2, comb = _merge(x2d, ya, yr, gta, gtr, w_nsa_up.astype(BF16), w_ret_up.astype(BF16),
                          w_out.astype(BF16), norm_ffn.reshape(1, D_MODEL), wrt, brt)
    return _hierarchical_moe(h2, comb, x1, moe_wg.astype(BF16), moe_wu.astype(BF16), moe_wd.astype(BF16),
                             final_norm_w)


def kernel(x, rel_bias, norm_mix, w_in, cmp_pos_k, cmp_k_w1, cmp_k_w2, cmp_pos_v, cmp_v_w1, cmp_v_w2, ret_norm, w_nsa_up, w_ret_up, w_out, norm_ffn, w_router_group, b_router_group, w_router_expert, b_router_expert, moe_w_gate, moe_w_up, moe_w_down, norm_final):
    B, S, D = x.shape
    assert B == 1 and norm_mix.shape[0] == 1, "single sequence, depth 1"
    out = _layer(x[0], rel_bias, norm_mix[0], w_in[0], cmp_pos_k[0], cmp_k_w1[0], cmp_k_w2[0],
                 cmp_pos_v[0], cmp_v_w1[0], cmp_v_w2[0], ret_norm[0], w_nsa_up[0], w_ret_up[0], w_out[0],
                 norm_ffn[0], w_router_group[0], b_router_group[0], w_router_expert[0], b_router_expert[0],
                 moe_w_gate[0], moe_w_up[0], moe_w_down[0], norm_final.reshape(1, D))
    return out[None]
```

```python
import functools
import math

import numpy as np
import jax
import jax.numpy as jnp
from jax import lax
from jax.experimental import pallas as pl
from jax.experimental.pallas import tpu as pltpu

F32 = jnp.float32
BF16 = jnp.bfloat16

D_MODEL = 1024
NSA_HEADS = 8
NSA_KV_GROUPS = 2
NSA_HPG = NSA_HEADS // NSA_KV_GROUPS
NSA_HEAD_DIM = 64
CMP_LEN = 32
CMP_STRIDE = 16
CMP_HIDDEN = 256
SEL_BLOCK = 64
SEL_TOP_N = 16
N_LOCAL_SEL = 2
WINDOW = 512
Q_BLOCK = 128
RET_HEADS = 4
RET_QK_DIM = 128
RET_V_DIM = 256
RET_CHUNK = 128
ROPE_BASE = 10000.0
REL_BUCKETS = 32
REL_MAX_DIST = 128
N_GROUPS = 4
EXPERTS_PER_GROUP = 8
N_EXPERTS = N_GROUPS * EXPERTS_PER_GROUP
EXPERT_FF = 256
EPS = 1e-6

NSA_Q_W = NSA_HEADS * NSA_HEAD_DIM
NSA_KV_W = NSA_KV_GROUPS * NSA_HEAD_DIM
RET_QK_W = RET_HEADS * RET_QK_DIM
RET_V_W = RET_HEADS * RET_V_DIM

LANE = 128
NEG = -1e30
VMEM_LIMIT = 56 * 1024 * 1024

CMP_TILE = 128
QB_PER_CMP_TILE = CMP_TILE * CMP_STRIDE // Q_BLOCK
CMP_TOK_PER_QB = Q_BLOCK // CMP_STRIDE
SCR_PAD = 8
SEL_TK = 512
SEL_WIN = 128
SEL_WIN_TILES = SEL_WIN * SEL_BLOCK // SEL_TK
PV_ROWS = 80
WIN_KEYS = WINDOW + Q_BLOCK
QW = NSA_HPG * Q_BLOCK


def _dot(a, b):
    return jnp.dot(a, b, preferred_element_type=F32)


def _dot_nt(a, b):
    return lax.dot_general(a, b, (((1,), (1,)), ((), ())), preferred_element_type=F32)


def _const_spec(shape):
    nd = len(shape)
    return pl.BlockSpec(shape, lambda *_: (0,) * nd, pipeline_mode=pl.Buffered(1))


def _params(*sem):
    return pltpu.CompilerParams(dimension_semantics=sem, vmem_limit_bytes=VMEM_LIMIT)


_C_QA = 0
_C_KV = _C_QA + NSA_Q_W
_C_QR = _C_KV + 6 * NSA_KV_W
_C_KR = _C_QR + RET_QK_W
_C_VR = _C_KR + RET_QK_W
_C_GR = _C_VR + RET_V_W
_C_GTA = _C_GR + RET_V_W
_C_GTR = _C_GTA + D_MODEL
_C_GA = _C_GTR + D_MODEL
_C_END = _C_GA + LANE
IN_TM = 256


def _inproj_kernel(x_ref, nw_ref, w_ref, cos_ref, sin_ref,
                   qa_ref, kcmp_ref, vcmp_ref, ksel_ref, vsel_ref, kwin_ref, vwin_ref, ga_ref,
                   qr_ref, kr_ref, vr_ref, gr_ref, gta_ref, gtr_ref):
    x = x_ref[...]
    h = x * lax.rsqrt(jnp.mean(x * x, axis=-1, keepdims=True) + EPS) * nw_ref[...]
    hb = h.astype(BF16)

    def proj(a, width):
        return _dot(hb, w_ref[:, a:a + width])

    qa_ref[...] = (proj(_C_QA, NSA_Q_W) * (NSA_HEAD_DIM ** -0.5)).astype(BF16)
    kv = proj(_C_KV, 6 * NSA_KV_W)
    kcmp_ref[...] = kv[:, 0 * LANE:1 * LANE]
    vcmp_ref[...] = kv[:, 1 * LANE:2 * LANE]
    ksel_ref[...] = kv[:, 2 * LANE:3 * LANE].astype(BF16)
    vsel_ref[...] = kv[:, 3 * LANE:4 * LANE].astype(BF16)
    kwin_ref[...] = kv[:, 4 * LANE:5 * LANE].astype(BF16)
    vwin_ref[...] = kv[:, 5 * LANE:6 * LANE].astype(BF16)
    ga_ref[...] = jax.nn.sigmoid(proj(_C_GA, LANE))

    cos2 = cos_ref[...]
    sin2 = sin_ref[...]
    qr = proj(_C_QR, RET_QK_W)
    kr = proj(_C_KR, RET_QK_W)
    for hd in range(RET_HEADS):
        sl = slice(hd * RET_QK_DIM, (hd + 1) * RET_QK_DIM)
        qh = qr[:, sl]
        kh = kr[:, sl]
        qr_ref[:, sl] = (qh * cos2 + pltpu.roll(qh, RET_QK_DIM // 2, 1) * sin2).astype(BF16)
        kr_ref[:, sl] = (kh * cos2 + pltpu.roll(kh, RET_QK_DIM // 2, 1) * sin2) * (RET_QK_DIM ** -0.5)
    vr_ref[...] = proj(_C_VR, RET_V_W).astype(BF16)
    gr_ref[...] = proj(_C_GR, RET_V_W)
    gta_ref[...] = jax.nn.sigmoid(proj(_C_GTA, D_MODEL))
    gtr_ref[...] = jax.nn.sigmoid(proj(_C_GTR, D_MODEL))


def _input_projection(x2d, norm_w, w_cat, cos2, sin2):
    S = x2d.shape[0]
    tm = IN_TM
    row = lambda w: pl.BlockSpec((tm, w), lambda i: (i, 0))
    outs = [
        (NSA_Q_W, BF16),
        (LANE, F32), (LANE, F32),
        (LANE, BF16), (LANE, BF16),
        (LANE, BF16), (LANE, BF16),
        (LANE, F32),
        (RET_QK_W, BF16), (RET_QK_W, F32),
        (RET_V_W, BF16), (RET_V_W, F32),
        (D_MODEL, F32), (D_MODEL, F32),
    ]
    return pl.pallas_call(
        _inproj_kernel,
        grid=(S // tm,),
        in_specs=[row(D_MODEL), _const_spec((1, D_MODEL)), _const_spec((D_MODEL, _C_END)),
                  row(LANE), row(LANE)],
        out_specs=[row(w) for w, _ in outs],
        out_shape=[jax.ShapeDtypeStruct((S, w), dt) for w, dt in outs],
        compiler_params=_params("parallel"),
        name="in_proj",
    )(x2d, norm_w, w_cat, cos2, sin2)


def _compress_kernel(x_ref, pos_ref, w1_ref, w2_ref, out_ref, hid_ref):
    n_tok = out_ref.shape[0]
    hid_ref[...] = jnp.zeros_like(hid_ref)

    def body(l, _):
        rows = x_ref[pl.ds(l, n_tok, stride=CMP_STRIDE), :] + pos_ref[pl.ds(l, 1), :]
        hid_ref[...] += _dot(rows.astype(BF16), w1_ref[l])
        return 0

    lax.fori_loop(0, CMP_LEN, body, 0)
    out_ref[...] = _dot(jax.nn.gelu(hid_ref[...]).astype(BF16), w2_ref[...])


def _compress(x, pos, w1, w2):
    S, width = x.shape
    n_tok = S // CMP_STRIDE
    G = NSA_KV_GROUPS
    w1 = w1.reshape(CMP_LEN, NSA_HEAD_DIM, CMP_HIDDEN)
    w1_bd = jnp.zeros((CMP_LEN, G, NSA_HEAD_DIM, G, CMP_HIDDEN), F32)
    w2_bd = jnp.zeros((G, CMP_HIDDEN, G, NSA_HEAD_DIM), F32)
    for g in range(G):
        w1_bd = w1_bd.at[:, g, :, g, :].set(w1)
        w2_bd = w2_bd.at[g, :, g, :].set(w2)
    w1_bd = w1_bd.reshape(CMP_LEN, width, G * CMP_HIDDEN).astype(BF16)
    w2_bd = w2_bd.reshape(G * CMP_HIDDEN, width).astype(BF16)
    x_pad = jnp.pad(x, ((0, CMP_LEN - CMP_STRIDE), (0, 0)))
    full = lambda a: _const_spec(a.shape)
    pos2 = jnp.concatenate([pos] * G, axis=1)
    return pl.pallas_call(
        _compress_kernel,
        grid=(1,),
        in_specs=[full(x_pad), full(pos2), full(w1_bd), full(w2_bd)],
        out_specs=pl.BlockSpec((n_tok, width), lambda i: (0, 0)),
        out_shape=jax.ShapeDtypeStruct((n_tok, width), F32),
        scratch_shapes=[pltpu.VMEM((n_tok, G * CMP_HIDDEN), F32)],
        compiler_params=_params("arbitrary"),
        name="nsa_compress",
    )(x_pad, pos2, w1_bd, w2_bd)


def _t5_bucket_np(dist):
    dist = np.maximum(dist, 0)
    max_exact = REL_BUCKETS // 2
    d_f = np.maximum(dist, 1).astype(np.float32)
    large = max_exact + (np.log(d_f / np.float32(max_exact)) / np.float32(math.log(REL_MAX_DIST / max_exact))
                         * np.float32(REL_BUCKETS - max_exact)).astype(np.int32)
    large = np.minimum(large, REL_BUCKETS - 1)
    return np.where(dist < max_exact, dist, large)


def _bias_table_t(rel_bias, dist, shift):
    bucket = jnp.asarray(_t5_bucket_np(dist), jnp.int32)
    rb = rel_bias - rel_bias[REL_BUCKETS - 1][None, :] if shift else rel_bias
    tab = jnp.zeros((NSA_HEADS,) + dist.shape, F32)
    for b in range(REL_BUCKETS):
        tab = jnp.where(bucket[None] == b, rb[b][:, None, None], tab)
    tab = jnp.where(jnp.asarray(dist == -1)[None], NEG, tab)
    tab = jnp.where(jnp.asarray(dist == -2)[None], 0.0, tab)
    K, Q = dist.shape
    return tab.reshape(NSA_KV_GROUPS, NSA_HPG, K, Q).transpose(0, 2, 1, 3).reshape(NSA_KV_GROUPS, K, NSA_HPG * Q)


def _nsa_kernel(n_keys, q_ref, ga_ref, kc_ref, vct_ref, tct_ref, ksel_ref, vselt_ref, et_ref, tnt_ref,
                kwin_ref, vwint_ref, twt_ref, out_ref, sc_ref, ps_ref, sa_ref, sb_ref):
    i = pl.program_id(0)
    dh = NSA_HEAD_DIM
    nb = (ps_ref.shape[1] - 2 * SCR_PAD) // (SEL_BLOCK // CMP_STRIDE)

    @pl.when(i == 0)
    def _():
        sc_ref[...] = jnp.zeros_like(sc_ref)
        ps_ref[...] = jnp.zeros_like(ps_ref)

    def colmax(x):
        return jnp.max(x, axis=0, keepdims=True)

    def tile4(x):
        return jnp.concatenate([x] * NSA_HPG, axis=1)

    gates = ga_ref[...]
    tl = i // QB_PER_CMP_TILE
    sub = i % QB_PER_CMP_TILE
    tp = tl - 1
    groups = range(NSA_KV_GROUPS)
    q_t_g = [jnp.concatenate([q_ref[NSA_HPG * g + j] for j in range(NSA_HPG)], axis=1) for g in groups]
    zeros = jnp.zeros((dh, QW), BF16)
    wq_g = [jnp.concatenate([q_t_g[g] if k == g else zeros for k in groups], axis=0) for g in groups]

    def cmp_scores(g, t):
        return _dot(kc_ref[pl.ds(pl.multiple_of(t * CMP_TILE, CMP_TILE), CMP_TILE), :], wq_g[g])

    def sc_rows(t):
        return pl.ds(pl.multiple_of(SCR_PAD + t * CMP_TILE, 8), CMP_TILE)

    def cmp_far(t, m):
        out = []
        for g in groups:
            s = cmp_scores(g, t)
            sc_ref[g, sc_rows(t), :] = s
            out.append(jnp.maximum(m[g], colmax(s)))
        return tuple(out)

    m_c = lax.fori_loop(0, jnp.maximum(tp, 0), cmp_far, tuple(jnp.full((1, QW), NEG, F32) for _ in groups))
    tpc = jnp.maximum(tp, 0)
    off = CMP_TILE - CMP_TOK_PER_QB - CMP_TOK_PER_QB * sub
    m_c = list(m_c)
    for g in groups:
        s = (cmp_scores(g, tpc) + tct_ref[g, pl.ds(pl.multiple_of(off, 8), CMP_TILE), :]
             + jnp.where(tp < 0, NEG, 0.0))
        sc_ref[g, sc_rows(tpc), :] = s
        m_c[g] = jnp.maximum(m_c[g], colmax(s))
    for g in groups:
        s = cmp_scores(g, tl) + tct_ref[g, pl.ds(pl.multiple_of(off + CMP_TILE, 8), CMP_TILE), :]
        sc_ref[g, sc_rows(tl), :] = s
        m_c[g] = jnp.maximum(m_c[g], colmax(s))

    def cmp_exp(t, carry):
        out = []
        for g in groups:
            l, acc = carry[g]
            e = jnp.exp(sc_ref[g, sc_rows(t), :] - m_c[g])
            sc_ref[g, sc_rows(t), :] = e
            vt = vct_ref[g, :, pl.ds(pl.multiple_of(t * CMP_TILE, CMP_TILE), CMP_TILE)]
            out.append((l + jnp.sum(e, axis=0, keepdims=True), acc + _dot(vt, e.astype(BF16))))
        return tuple(out)

    cmp_out = lax.fori_loop(0, tl + 1, cmp_exp,
                            tuple((jnp.zeros((1, QW), F32), jnp.zeros((dh, QW), F32)) for _ in groups))
    inv_c = [jnp.where(m_c[g] > 0.5 * NEG, 1.0 / jnp.maximum(cmp_out[g][0], 1e-30), 0.0) for g in groups]
    o_c_g = [cmp_out[g][1] * inv_c[g] for g in groups]

    def cmp_psum(t, _):
        for g in groups:
            p = sc_ref[g, sc_rows(t), :] * inv_c[g]
            ps = p[:, 0:Q_BLOCK]
            for j in range(1, NSA_HPG):
                ps = ps + p[:, j * Q_BLOCK:(j + 1) * Q_BLOCK]
            ps_ref[g, sc_rows(t), :] = ps.astype(BF16).astype(F32)
        return 0

    lax.fori_loop(0, tl + 1, cmp_psum, 0)
    per_blk = SEL_BLOCK // CMP_STRIDE
    imp_g = []
    for g in groups:
        imp = ps_ref[g, pl.ds(SCR_PAD - 1, nb, stride=per_blk), :]
        for k in range(1, CMP_LEN // CMP_STRIDE + per_blk - 1):
            imp = imp + ps_ref[g, pl.ds(SCR_PAD - 1 + k, nb, stride=per_blk), :]
        imp_g.append(imp)

    n_io = lax.broadcasted_iota(jnp.int32, (nb, Q_BLOCK), 0)
    r_io = lax.broadcasted_iota(jnp.int32, (nb, Q_BLOCK), 1)
    back = (i * Q_BLOCK + r_io) // SEL_BLOCK - n_io
    valid = back >= 0
    forced = (n_io == 0) | (valid & (back < N_LOCAL_SEL))
    work_g = [jnp.where(forced, -jnp.inf, jnp.where(valid, imp_g[g], -1e9)) for g in groups]
    sel_g = [jnp.where(forced, 1.0, 0.0) for _ in groups]
    n_f = n_io.astype(F32)
    for _ in range(SEL_TOP_N - 1 - N_LOCAL_SEL):
        for g in groups:
            mx = colmax(work_g[g])
            first = jnp.min(jnp.where(work_g[g] == mx, n_f, float(nb)), axis=0, keepdims=True)
            pick = n_f == first
            sel_g[g] = jnp.where(pick, 1.0, sel_g[g])
            work_g[g] = jnp.where(pick, -jnp.inf, work_g[g])

    far_blocks = (i - 1) * (Q_BLOCK // SEL_BLOCK)
    selm_g = [(jnp.where(n_io < far_blocks, sel_g[g], 0.0) - 1.0).astype(BF16) for g in groups]

    far_end = (i - 1) * Q_BLOCK
    n_far = (jnp.maximum(far_end, 0) + SEL_TK - 1) // SEL_TK

    def flash_step(carry, s, vt):
        m_i, acc = carry
        m_new = jnp.maximum(m_i, colmax(s))
        alpha = jnp.exp(m_i - m_new)
        p = jnp.exp(s - m_new).astype(BF16)
        return m_new, alpha * acc + _dot(vt, p)

    carry = tuple((jnp.full((1, QW), NEG, F32), jnp.zeros((PV_ROWS, QW), F32)) for _ in groups)
    for w in range(nb // SEL_WIN):
        w_far = [jnp.concatenate([tile4(selm_g[g][w * SEL_WIN:(w + 1) * SEL_WIN]), wq_g[g]], axis=0)
                 for g in groups]

        lo = w * SEL_WIN_TILES
        last = min((w + 1) * SEL_WIN_TILES, n_keys // SEL_TK) - 1
        trips = (jnp.minimum(n_far, last + 1) - lo + 1) // 2

        def far_scores(kt, s_ref, g, w=w, w_far=w_far):
            k0 = pl.multiple_of(kt * SEL_TK, SEL_TK)
            e0 = pl.multiple_of((kt - w * SEL_WIN_TILES) * SEL_TK, SEL_TK)
            lhs = jnp.concatenate([et_ref[pl.ds(e0, SEL_TK), :],
                                   ksel_ref[pl.ds(k0 + Q_BLOCK, SEL_TK), :]], axis=1)
            s_ref[g] = _dot(lhs, w_far[g])

        def far_update(kt, s_ref, g, carry_g):
            k0 = pl.multiple_of(kt * SEL_TK, SEL_TK)
            return flash_step(carry_g, s_ref[g], vselt_ref[g, :, pl.ds(k0 + Q_BLOCK, SEL_TK)])

        def far_body(u, carry, lo=lo, last=last, far_scores=far_scores):
            kt = lo + 2 * u
            carry = list(carry)
            for g in groups:
                far_scores(kt + 1, sb_ref, g)
                carry[g] = far_update(kt, sa_ref, g, carry[g])
            for g in groups:
                far_scores(jnp.minimum(kt + 2, last), sa_ref, g)
                carry[g] = far_update(kt + 1, sb_ref, g, carry[g])
            return tuple(carry)

        for g in groups:
            far_scores(jnp.int32(lo), sa_ref, g)
        carry = lax.fori_loop(0, trips, far_body, carry)

    heads_out = []
    for g in groups:
        wq, o_c, sel = wq_g[g], o_c_g[g], sel_g[g]
        k0n = pl.multiple_of(i * Q_BLOCK, Q_BLOCK)
        near_blocks = 2 * Q_BLOCK // SEL_BLOCK
        pb = lax.broadcasted_iota(jnp.int32, (16, nb), 0)
        pn = lax.broadcasted_iota(jnp.int32, (16, nb), 1)
        pick_rows = jnp.where((pn == far_blocks + pb) & (pb < near_blocks), 1.0, 0.0).astype(BF16)
        sel_rows = _dot(pick_rows, sel.astype(BF16))
        add_near = jnp.concatenate(
            [jnp.broadcast_to((sel_rows[b:b + 1, :] - 1.0) * (-NEG), (SEL_BLOCK, Q_BLOCK)) for b in range(near_blocks)],
            axis=0)
        s = _dot(ksel_ref[pl.ds(k0n, 2 * Q_BLOCK), :], wq) + tnt_ref[g] + tile4(add_near)
        m_s, acc_s = flash_step(carry[g], s, vselt_ref[g, :, pl.ds(k0n, 2 * Q_BLOCK)])
        o_s = acc_s[0:dh] * (1.0 / jnp.maximum(acc_s[dh:dh + 1], 1e-30))

        s = _dot(kwin_ref[pl.ds(k0n, WIN_KEYS), :], wq) + twt_ref[g]
        pos = lax.broadcasted_iota(jnp.int32, (WIN_KEYS, QW), 0) + (i * Q_BLOCK - WINDOW)
        s = jnp.where(pos >= 0, s, NEG)
        e = jnp.exp(s - colmax(s)).astype(BF16)
        acc_w = _dot(vwint_ref[g, :, pl.ds(k0n, WIN_KEYS)], e)
        o_w = acc_w[0:dh] * (1.0 / jnp.maximum(acc_w[dh:dh + 1], 1e-30))

        for j in range(NSA_HPG):
            hd = NSA_HPG * g + j
            cols = slice(j * Q_BLOCK, (j + 1) * Q_BLOCK)
            heads_out.append(gates[hd:hd + 1] * o_c[:, cols]
                             + gates[NSA_HEADS + hd:NSA_HEADS + hd + 1] * o_s[:, cols]
                             + gates[2 * NSA_HEADS + hd:2 * NSA_HEADS + hd + 1] * o_w[:, cols])

    out_ref[...] = jnp.concatenate(heads_out, axis=0).T.astype(out_ref.dtype)


def _nsa_attention(q_t, gates_t, kc, vct, tct, ksel, vselt, et, tnt, kwin, vwint, twt):
    H, dh, S = q_t.shape
    n_qb = S // Q_BLOCK
    n_tok = S // CMP_STRIDE
    nb = -(-(S // SEL_BLOCK) // SEL_WIN) * SEL_WIN
    ps_rows = nb * (SEL_BLOCK // CMP_STRIDE) + 2 * SCR_PAD
    full = lambda a: _const_spec(a.shape)
    return pl.pallas_call(
        functools.partial(_nsa_kernel, S),
        grid=(n_qb,),
        in_specs=[pl.BlockSpec((H, dh, Q_BLOCK), lambda i: (0, 0, i)),
                  pl.BlockSpec((gates_t.shape[0], Q_BLOCK), lambda i: (0, i)),
                  full(kc), full(vct), full(tct), full(ksel), full(vselt), full(et), full(tnt),
                  full(kwin), full(vwint), full(twt)],
        out_specs=pl.BlockSpec((Q_BLOCK, H * dh), lambda i: (i, 0)),
        out_shape=jax.ShapeDtypeStruct((S, H * dh), BF16),
        scratch_shapes=[pltpu.VMEM((NSA_KV_GROUPS, SCR_PAD + n_tok, QW), F32),
                        pltpu.VMEM((NSA_KV_GROUPS, ps_rows, Q_BLOCK), F32),
                        pltpu.VMEM((NSA_KV_GROUPS, SEL_TK, QW), F32),
                        pltpu.VMEM((NSA_KV_GROUPS, SEL_TK, QW), F32)],
        compiler_params=_params("arbitrary"),
        name="nsa_attention",
    )(q_t, gates_t, kc, vct, tct, ksel, vselt, et, tnt, kwin, vwint, twt)


def _retention_kernel(q_ref, k_ref, v_ref, g_ref, nw_ref, dmask_ref, qd_ref, kd_ref, cd_ref, out_ref, state_ref):
    @pl.when(pl.program_id(0) == 0)
    def _():
        state_ref[...] = jnp.zeros_like(state_ref)

    for hd in range(RET_HEADS):
        ks = slice(hd * RET_QK_DIM, (hd + 1) * RET_QK_DIM)
        vs = slice(hd * RET_V_DIM, (hd + 1) * RET_V_DIM)
        q = q_ref[:, ks]
        k = k_ref[:, ks]
        v = v_ref[:, vs]
        state = state_ref[hd]
        inner = _dot_nt(q, k.astype(BF16)) * dmask_ref[hd]
        y = _dot(inner.astype(BF16), v) + _dot(q, state.astype(BF16)) * qd_ref[hd]
        kd = k * kd_ref[hd]
        state_ref[hd] = state * cd_ref[hd] + _dot(kd.T.astype(BF16), v)
        mu = jnp.mean(y, axis=-1, keepdims=True)
        yc = y - mu
        var = jnp.mean(yc * yc, axis=-1, keepdims=True)
        yn = yc * lax.rsqrt(var + EPS) * nw_ref[:, vs]
        out_ref[:, vs] = (jax.nn.silu(g_ref[:, vs]) * yn).astype(out_ref.dtype)


def _retention(q, k, v, g, norm_w, dmask, qd, kd, cd):
    S = q.shape[0]
    C = RET_CHUNK
    row = lambda w: pl.BlockSpec((C, w), lambda i: (i, 0))
    full = lambda a: _const_spec(a.shape)
    return pl.pallas_call(
        _retention_kernel,
        grid=(S // C,),
        in_specs=[row(RET_QK_W), row(RET_QK_W), row(RET_V_W), row(RET_V_W),
                  full(norm_w), full(dmask), full(qd), full(kd), full(cd)],
        out_specs=row(RET_V_W),
        out_shape=jax.ShapeDtypeStruct((S, RET_V_W), BF16),
        scratch_shapes=[pltpu.VMEM((RET_HEADS, RET_QK_DIM, RET_V_DIM), F32)],
        compiler_params=_params("arbitrary"),
        name="retention",
    )(q, k, v, g, norm_w, dmask, qd, kd, cd)


MERGE_TM = 256
GRP_LANE = 40


def _merge_kernel(x_ref, ya_ref, yr_ref, gta_ref, gtr_ref, wa_ref, wr_ref, wo_ref, nw_ref, wrt_ref, brt_ref,
                  x1_ref, h_ref, comb_ref):
    merged = gta_ref[...] * _dot(ya_ref[...], wa_ref[...]) + gtr_ref[...] * _dot(yr_ref[...], wr_ref[...])
    x1 = x_ref[...] + _dot(merged.astype(BF16), wo_ref[...])
    x1_ref[...] = x1
    h = (x1 * lax.rsqrt(jnp.mean(x1 * x1, axis=-1, keepdims=True) + EPS) * nw_ref[...]).astype(BF16)
    h_ref[...] = _pack_bf16_pairs(h.astype(F32))
    logits = _dot(h, wrt_ref[...]) + brt_ref[...]
    lane = lax.broadcasted_iota(jnp.int32, logits.shape, 1)
    lane_f = lane.astype(F32)
    big = float(LANE)
    is_grp = (lane >= N_EXPERTS) & (lane < N_EXPERTS + N_GROUPS)
    lg = jnp.where(is_grp, logits, -jnp.inf)
    eg = jnp.where(is_grp, jnp.exp(lg - jnp.max(lg, axis=-1, keepdims=True)), 0.0)
    pg = eg / jnp.sum(eg, axis=-1, keepdims=True)
    p_grp = jnp.max(pg, axis=-1, keepdims=True)
    grp = jnp.min(jnp.where(is_grp & (pg == p_grp), lane_f, big), axis=-1, keepdims=True) - N_EXPERTS
    in_grp = (lane_f >= grp * EXPERTS_PER_GROUP) & (lane_f < (grp + 1.0) * EXPERTS_PER_GROUP)
    le = jnp.where(in_grp, logits, -jnp.inf)
    m1 = jnp.max(le, axis=-1, keepdims=True)
    i1 = jnp.min(jnp.where(le == m1, lane_f, big), axis=-1, keepdims=True)
    le2 = jnp.where(lane_f == i1, -jnp.inf, le)
    m2 = jnp.max(le2, axis=-1, keepdims=True)
    i2 = jnp.min(jnp.where(le2 == m2, lane_f, big), axis=-1, keepdims=True)
    e2 = jnp.exp(m2 - m1)
    den = 1.0 + e2
    comb_ref[...] = (jnp.where(lane_f == i1, (1.0 / den) * p_grp, 0.0)
                     + jnp.where(lane_f == i2, (e2 / den) * p_grp, 0.0)
                     + jnp.where(lane == GRP_LANE, grp, 0.0))


def _pack_bf16_pairs(x):
    n = x.shape[1] // 2
    bits = pltpu.bitcast(x, jnp.uint32)
    return (bits[:, :n] >> 16) | (bits[:, n:] & jnp.uint32(0xFFFF0000))


def _unpack_bf16_pairs(w):
    return pltpu.bitcast(w << 16, F32), pltpu.bitcast(w & jnp.uint32(0xFFFF0000), F32)


def _merge(x2d, ya, yr, gta, gtr, wa, wr, wo, nw, wrt, brt):
    S = x2d.shape[0]
    tm = MERGE_TM
    row = lambda w: pl.BlockSpec((tm, w), lambda i: (i, 0))
    full = lambda a: _const_spec(a.shape)
    return pl.pallas_call(
        _merge_kernel,
        grid=(S // tm,),
        in_specs=[row(D_MODEL), row(NSA_Q_W), row(RET_V_W), row(D_MODEL), row(D_MODEL),
                  full(wa), full(wr), full(wo), full(nw), full(wrt), full(brt)],
        out_specs=[row(D_MODEL), row(D_MODEL // 2), row(LANE)],
        out_shape=[jax.ShapeDtypeStruct((S, D_MODEL), F32), jax.ShapeDtypeStruct((S, D_MODEL // 2), jnp.uint32),
                   jax.ShapeDtypeStruct((S, LANE), F32)],
        compiler_params=_params("parallel"),
        name="merge_router",
    )(x2d, ya, yr, gta, gtr, wa, wr, wo, nw, wrt, brt)


MOE_TM = 1024
RANK_TM = 512
HALF = D_MODEL // 2


def _rank_kernel(comb_ref, tri_ref, rank_ref, cnt_ref, carry_ref):
    @pl.when(pl.program_id(0) == 0)
    def _():
        carry_ref[...] = jnp.zeros_like(carry_ref)

    comb = comb_ref[...]
    lane_f = lax.broadcasted_iota(jnp.int32, comb.shape, 1).astype(F32)
    onehot = jnp.where(lane_f == comb[:, GRP_LANE:GRP_LANE + 1], 1.0, 0.0)
    before = _dot(tri_ref[...], onehot.astype(BF16)) + carry_ref[...]
    rank_ref[...] = jnp.broadcast_to(jnp.sum(onehot * before, axis=-1, keepdims=True), comb.shape)
    carry_ref[...] += jnp.sum(onehot, axis=0, keepdims=True)
    cnt_ref[...] = jnp.broadcast_to(carry_ref[...], cnt_ref.shape)


def _group_rank(comb):
    S = comb.shape[0]
    tm = RANK_TM
    tri = jnp.asarray(np.tril(np.ones((tm, tm), np.float32), -1), BF16)
    return pl.pallas_call(
        _rank_kernel,
        grid=(S // tm,),
        in_specs=[pl.BlockSpec((tm, LANE), lambda i: (i, 0)), _const_spec((tm, tm))],
        out_specs=[pl.BlockSpec((tm, LANE), lambda i: (i, 0)), pl.BlockSpec((8, LANE), lambda i: (0, 0))],
        out_shape=[jax.ShapeDtypeStruct((S, LANE), F32), jax.ShapeDtypeStruct((8, LANE), F32)],
        scratch_shapes=[pltpu.VMEM((1, LANE), F32)],
        compiler_params=_params("arbitrary"),
        name="moe_group_rank",
    )(comb, tri)


def _permute_kernel(pos_ref, h_ref, comb_ref, hs_ref, cs_ref):
    i = pl.program_id(0)

    @pl.when(i == 0)
    def _():
        hs_ref[...] = jnp.zeros_like(hs_ref)
        cs_ref[...] = jnp.zeros_like(cs_ref)

    tm = h_ref.shape[0]

    def body(r, _):
        d = pos_ref[i * tm + r]
        hs_ref[pl.ds(d, 1), :] = h_ref[pl.ds(r, 1), :]
        cs_ref[pl.ds(d, 1), :] = comb_ref[pl.ds(r, 1), :]
        return 0

    lax.fori_loop(0, tm, body, 0, unroll=8)


def _permute(pos, hu, comb, n_sorted):
    S = hu.shape[0]
    tm = RANK_TM
    resident = lambda w: pl.BlockSpec((n_sorted, w), lambda i, p: (0, 0), pipeline_mode=pl.Buffered(1))
    return pl.pallas_call(
        _permute_kernel,
        grid_spec=pltpu.PrefetchScalarGridSpec(
            num_scalar_prefetch=1, grid=(S // tm,),
            in_specs=[pl.BlockSpec((tm, HALF), lambda i, p: (i, 0)), pl.BlockSpec((tm, LANE), lambda i, p: (i, 0))],
            out_specs=[resident(HALF), resident(LANE)]),
        out_shape=[jax.ShapeDtypeStruct((n_sorted, HALF), jnp.uint32), jax.ShapeDtypeStruct((n_sorted, LANE), F32)],
        compiler_params=_params("arbitrary"),
        name="moe_permute",
    )(pos, hu, comb)


def _moe_kernel(tg_ref, tv_ref, hs_ref, cs_ref, wg_ref, wu_ref, wd_ref, ys_ref, lo_ref, hi_ref, acc_ref):
    i = pl.program_id(0)
    e = pl.program_id(1)

    @pl.when(e == 0)
    def _():
        lo, hi = _unpack_bf16_pairs(hs_ref[...])
        lo_ref[...] = lo.astype(BF16)
        hi_ref[...] = hi.astype(BF16)
        acc_ref[...] = jnp.zeros_like(acc_ref)

    @pl.when(tv_ref[i] > 0)
    def _():
        lo, hi = lo_ref[...], hi_ref[...]
        wg = wg_ref[0].astype(BF16)
        wu = wu_ref[0].astype(BF16)
        gate = _dot(lo, wg[:HALF]) + _dot(hi, wg[HALF:])
        up = _dot(lo, wu[:HALF]) + _dot(hi, wu[HALF:])
        comb = cs_ref[...]
        lane = lax.broadcasted_iota(jnp.int32, comb.shape, 1)
        c = jnp.sum(jnp.where(lane == tg_ref[i] * EXPERTS_PER_GROUP + e, comb, 0.0), axis=-1, keepdims=True)
        acc_ref[...] += _dot((jax.nn.silu(gate) * up).astype(BF16), wd_ref[0].astype(BF16)) * c

    @pl.when(e == pl.num_programs(1) - 1)
    def _():
        ys_ref[...] = _pack_bf16_pairs(acc_ref[...].astype(BF16).astype(F32))


def _moe(tile_group, tile_valid, hs, cs, wg, wu, wd):
    n_sorted = hs.shape[0]
    tm = MOE_TM
    row = lambda w: pl.BlockSpec((tm, w), lambda i, e, tg, tv: (i, 0))
    wspec = lambda a, b: pl.BlockSpec((1, a, b), lambda i, e, tg, tv: (tg[i] * EXPERTS_PER_GROUP + e, 0, 0))
    return pl.pallas_call(
        _moe_kernel,
        grid_spec=pltpu.PrefetchScalarGridSpec(
            num_scalar_prefetch=2, grid=(n_sorted // tm, EXPERTS_PER_GROUP),
            in_specs=[row(HALF), row(LANE), wspec(D_MODEL, EXPERT_FF), wspec(D_MODEL, EXPERT_FF),
                      wspec(EXPERT_FF, D_MODEL)],
            out_specs=row(HALF),
            scratch_shapes=[pltpu.VMEM((tm, HALF), BF16), pltpu.VMEM((tm, HALF), BF16),
                            pltpu.VMEM((tm, D_MODEL), F32)]),
        out_shape=jax.ShapeDtypeStruct((n_sorted, HALF), jnp.uint32),
        compiler_params=_params("parallel", "arbitrary"),
        name="moe_experts",
    )(tile_group, tile_valid, hs, cs, wg, wu, wd)


def _unpermute_kernel(pos_ref, ys_ref, x1_ref, nw_ref, out_ref, stage_ref):
    i = pl.program_id(0)
    tm = x1_ref.shape[0]

    def body(r, _):
        stage_ref[pl.ds(r, 1), :] = ys_ref[pl.ds(pos_ref[i * tm + r], 1), :]
        return 0

    lax.fori_loop(0, tm, body, 0, unroll=8)
    lo, hi = _unpack_bf16_pairs(stage_ref[...])
    x2 = x1_ref[...] + jnp.concatenate([lo, hi], axis=1)
    out_ref[...] = x2 * lax.rsqrt(jnp.mean(x2 * x2, axis=-1, keepdims=True) + EPS) * nw_ref[...]


def _unpermute_residual_norm(pos, ys, x1, nw):
    S = x1.shape[0]
    tm = RANK_TM
    return pl.pallas_call(
        _unpermute_kernel,
        grid_spec=pltpu.PrefetchScalarGridSpec(
            num_scalar_prefetch=1, grid=(S // tm,),
            in_specs=[pl.BlockSpec(ys.shape, lambda i, p: (0, 0), pipeline_mode=pl.Buffered(1)),
                      pl.BlockSpec((tm, D_MODEL), lambda i, p: (i, 0)),
                      pl.BlockSpec((1, D_MODEL), lambda i, p: (0, 0))],
            out_specs=pl.BlockSpec((tm, D_MODEL), lambda i, p: (i, 0)),
            scratch_shapes=[pltpu.VMEM((tm, HALF), jnp.uint32)]),
        out_shape=jax.ShapeDtypeStruct((S, D_MODEL), F32),
        compiler_params=_params("parallel"),
        name="moe_unpermute_norm",
    )(pos, ys, x1, nw)


def _hierarchical_moe(hu, comb, x1, wg, wu, wd, final_norm_w):
    S = hu.shape[0]
    tm = MOE_TM
    n_sorted = S + N_GROUPS * tm
    rank, cnt = _group_rank(comb)
    grp = comb[:, GRP_LANE].astype(jnp.int32)
    counts = cnt[0, :N_GROUPS].astype(jnp.int32)
    padded = (counts + tm - 1) // tm * tm
    ends = jnp.cumsum(padded)
    starts = ends - padded
    pos = rank[:, 0].astype(jnp.int32) + jnp.sum(
        jnp.where(grp[:, None] == jnp.arange(N_GROUPS)[None, :], starts[None, :], 0), axis=1)
    tile_start = jnp.arange(n_sorted // tm, dtype=jnp.int32) * tm
    tile_group = jnp.minimum(jnp.sum((tile_start[:, None] >= ends[None, :]).astype(jnp.int32), axis=1), N_GROUPS - 1)
    tile_valid = (tile_start < ends[N_GROUPS - 1]).astype(jnp.int32)
    hs, cs = _permute(pos, hu, comb, n_sorted)
    ys = _moe(tile_group, tile_valid, hs, cs, wg, wu, wd)
    return _unpermute_residual_norm(pos, ys, x1, final_norm_w)


def _positional_tables(S):
    half = RET_QK_DIM // 2
    inv = ROPE_BASE ** (-jnp.arange(half, dtype=F32) / half)
    ang = jnp.arange(S)[:, None].astype(F32) * inv[None, :]
    cos, sin = jnp.cos(ang), jnp.sin(ang)
    cos2 = jnp.concatenate([cos, cos], axis=-1)
    sin2 = jnp.concatenate([-sin, sin], axis=-1)
    C, H = RET_CHUNK, RET_HEADS
    log_gamma = jnp.log(1.0 - 2.0 ** (-5.0 - jnp.arange(H, dtype=F32)))
    idx = jnp.arange(C, dtype=F32)
    diff = idx[:, None] - idx[None, :]
    dmask = jnp.where(diff >= 0, jnp.exp(jnp.maximum(diff, 0.0)[None] * log_gamma[:, None, None]), 0.0)
    qd = jnp.exp((idx + 1.0)[None, :] * log_gamma[:, None])
    kd = jnp.exp((C - 1.0 - idx)[None, :] * log_gamma[:, None])
    cd = jnp.exp(C * log_gamma)
    qd = jnp.broadcast_to(qd[:, :, None], (H, C, RET_V_DIM))
    kd = jnp.broadcast_to(kd[:, :, None], (H, C, RET_QK_DIM))
    cd = jnp.broadcast_to(cd[:, None, None], (H, 1, RET_V_DIM))
    return cos2, sin2, dmask, qd, kd, cd


def _attention_tables(rel_bias):
    r = np.arange(Q_BLOCK)[None, :]
    cc = np.arange(3 * CMP_TILE)[:, None] - CMP_TILE
    d = r - CMP_STRIDE * (cc - (CMP_TILE - CMP_TOK_PER_QB)) - (CMP_LEN - 1)
    d = np.where(cc < 0, -2, np.where((cc >= CMP_TILE) | (d < 0), -1, d))
    tct = _bias_table_t(rel_bias, d, shift=True)
    d = r + Q_BLOCK - np.arange(2 * Q_BLOCK)[:, None]
    tnt = _bias_table_t(rel_bias, np.where(d < 0, -1, d), shift=True)
    d = r + WINDOW - np.arange(WIN_KEYS)[:, None]
    twt = _bias_table_t(rel_bias, np.where((d < 0) | (d >= WINDOW), -1, d), shift=False)
    kk = np.arange(SEL_WIN * SEL_BLOCK)[:, None] // SEL_BLOCK
    et = jnp.where(jnp.asarray(kk == np.arange(SEL_WIN)[None, :]), -NEG, 0.0).astype(BF16)
    return tct, tnt, twt, et


def _layer(x2d, rel_bias, norm_mix, w_in, cmp_pos_k, cmp_k_w1, cmp_k_w2, cmp_pos_v, cmp_v_w1, cmp_v_w2,
           ret_norm, w_nsa_up, w_ret_up, w_out, norm_ffn, w_rg, b_rg, w_re, b_re, moe_wg, moe_wu, moe_wd,
           final_norm_w):
    S = x2d.shape[0]
    G, dh = NSA_KV_GROUPS, NSA_HEAD_DIM
    assert S % (CMP_TILE * CMP_STRIDE) == 0 and S // SEL_BLOCK >= SEL_TOP_N

    o = np.cumsum((0,) + (NSA_Q_W, 6 * NSA_KV_W, 3 * NSA_HEADS, RET_QK_W, RET_QK_W, RET_V_W, RET_V_W,
                          D_MODEL, D_MODEL))
    seg = lambda n: w_in[:, o[n]:o[n + 1]]
    w_cat = jnp.concatenate([seg(0), seg(1), seg(3), seg(4), seg(5), seg(6), seg(7), seg(8), seg(2),
                             jnp.zeros((D_MODEL, LANE - 3 * NSA_HEADS), F32)], axis=1).astype(BF16)
    cos2, sin2, dmask, qd, kd, cd = _positional_tables(S)

    (qa, kcmp, vcmp, ksel, vsel, kwin, vwin, ga, qr, kr, vr, gr, gta, gtr) = _input_projection(
        x2d, norm_mix.reshape(1, D_MODEL), w_cat, cos2, sin2)

    nch = S // CMP_STRIDE
    kc = _compress(kcmp, cmp_pos_k, cmp_k_w1, cmp_k_w2).astype(BF16)
    vct = _compress(vcmp, cmp_pos_v, cmp_v_w1, cmp_v_w2).T.reshape(G, dh, nch).astype(BF16)

    tct, tnt, twt, et = _attention_tables(rel_bias)

    def values_t(v, pad):
        vt = v.reshape(S, G, dh).transpose(1, 2, 0)
        extra = jnp.concatenate([jnp.ones((G, 1, S), BF16), jnp.zeros((G, PV_ROWS - dh - 1, S), BF16)], axis=1)
        return jnp.pad(jnp.concatenate([vt, extra], axis=1), ((0, 0), (0, 0), (pad, 0)))

    q_t = qa.T.reshape(NSA_HEADS, dh, S)
    gates_t = ga.T[:4 * NSA_HEADS]
    ya = _nsa_attention(q_t, gates_t, kc, vct, tct,
                        jnp.pad(ksel, ((Q_BLOCK, 0), (0, 0))), values_t(vsel, Q_BLOCK), et, tnt,
                        jnp.pad(kwin, ((WINDOW, 0), (0, 0))), values_t(vwin, WINDOW), twt)

    yr = _retention(qr, kr, vr, gr, ret_norm.reshape(1, RET_V_W), dmask, qd, kd, cd)

    wrt = jnp.concatenate([w_re, w_rg, jnp.zeros((D_MODEL, LANE - N_EXPERTS - N_GROUPS), F32)], axis=1).astype(BF16)
    brt = jnp.concatenate([b_re, b_rg, jnp.zeros((LANE - N_EXPERTS - N_GROUPS,), F32)]).reshape(1, LANE)
    x1, h2, comb = _merge(x2d, ya, yr, gta, gtr, w_nsa_up.astype(BF16), w_ret_up.astype(BF16),
                          w_out.astype(BF16), norm_ffn.reshape(1, D_MODEL), wrt, brt)
    return _hierarchical_moe(h2, comb, x1, moe_wg, moe_wu, moe_wd, final_norm_w)


def kernel(x, rel_bias, norm_mix, w_in, cmp_pos_k, cmp_k_w1, cmp_k_w2, cmp_pos_v, cmp_v_w1, cmp_v_w2, ret_norm, w_nsa_up, w_ret_up, w_out, norm_ffn, w_router_group, b_router_group, w_router_expert, b_router_expert, moe_w_gate, moe_w_up, moe_w_down, norm_final):
    B, S, D = x.shape
    assert B == 1 and norm_mix.shape[0] == 1, "single sequence, depth 1"
    out = _layer(x[0], rel_bias, norm_mix[0], w_in[0], cmp_pos_k[0], cmp_k_w1[0], cmp_k_w2[0],
                 cmp_pos_v[0], cmp_v_w1[0], cmp_v_w2[0], ret_norm[0], w_nsa_up[0], w_ret_up[0], w_out[0],
                 norm_ffn[0], w_router_group[0], b_router_group[0], w_router_expert[0], b_router_expert[0],
                 moe_w_gate[0], moe_w_up[0], moe_w_down[0], norm_final.reshape(1, D))
    return out[None]
```

```python
import functools
import math

import numpy as np
import jax
import jax.numpy as jnp
from jax import lax
from jax.experimental import pallas as pl
from jax.experimental.pallas import tpu as pltpu

F32 = jnp.float32
BF16 = jnp.bfloat16

D_MODEL = 1024
NSA_HEADS = 8
NSA_KV_GROUPS = 2
NSA_HPG = NSA_HEADS // NSA_KV_GROUPS
NSA_HEAD_DIM = 64
CMP_LEN = 32
CMP_STRIDE = 16
CMP_HIDDEN = 256
SEL_BLOCK = 64
SEL_TOP_N = 16
N_LOCAL_SEL = 2
WINDOW = 512
Q_BLOCK = 128
RET_HEADS = 4
RET_QK_DIM = 128
RET_V_DIM = 256
RET_CHUNK = 128
ROPE_BASE = 10000.0
REL_BUCKETS = 32
REL_MAX_DIST = 128
N_GROUPS = 4
EXPERTS_PER_GROUP = 8
N_EXPERTS = N_GROUPS * EXPERTS_PER_GROUP
EXPERT_FF = 256
EPS = 1e-6

NSA_Q_W = NSA_HEADS * NSA_HEAD_DIM
NSA_KV_W = NSA_KV_GROUPS * NSA_HEAD_DIM
RET_QK_W = RET_HEADS * RET_QK_DIM
RET_V_W = RET_HEADS * RET_V_DIM

LANE = 128
NEG = -1e30
VMEM_LIMIT = 56 * 1024 * 1024

CMP_TILE = 128
QB_PER_CMP_TILE = CMP_TILE * CMP_STRIDE // Q_BLOCK
CMP_TOK_PER_QB = Q_BLOCK // CMP_STRIDE
SCR_PAD = 8
SEL_TK = 512
SEL_WIN = 128
SEL_WIN_TILES = SEL_WIN * SEL_BLOCK // SEL_TK
PV_ROWS = 80
WIN_KEYS = WINDOW + Q_BLOCK
QW = NSA_HPG * Q_BLOCK


def _dot(a, b):
    return jnp.dot(a, b, preferred_element_type=F32)


def _dot_nt(a, b):
    return lax.dot_general(a, b, (((1,), (1,)), ((), ())), preferred_element_type=F32)


def _const_spec(shape):
    nd = len(shape)
    return pl.BlockSpec(shape, lambda *_: (0,) * nd, pipeline_mode=pl.Buffered(1))


def _params(*sem):
    return pltpu.CompilerParams(dimension_semantics=sem, vmem_limit_bytes=VMEM_LIMIT)


_C_QA = 0
_C_KV = _C_QA + NSA_Q_W
_C_QR = _C_KV + 6 * NSA_KV_W
_C_KR = _C_QR + RET_QK_W
_C_VR = _C_KR + RET_QK_W
_C_GR = _C_VR + RET_V_W
_C_GTA = _C_GR + RET_V_W
_C_GTR = _C_GTA + D_MODEL
_C_GA = _C_GTR + D_MODEL
_C_END = _C_GA + LANE
IN_TM = 256


def _inproj_kernel(x_ref, nw_ref, w_ref, cos_ref, sin_ref,
                   qa_ref, kcmp_ref, vcmp_ref, ksel_ref, vsel_ref, kwin_ref, vwin_ref, ga_ref,
                   qr_ref, kr_ref, vr_ref, gr_ref, gta_ref, gtr_ref):
    x = x_ref[...]
    h = x * lax.rsqrt(jnp.mean(x * x, axis=-1, keepdims=True) + EPS) * nw_ref[...]
    hb = h.astype(BF16)

    def proj(a, width):
        return _dot(hb, w_ref[:, a:a + width])

    qa_ref[...] = (proj(_C_QA, NSA_Q_W) * (NSA_HEAD_DIM ** -0.5)).astype(BF16)
    kv = proj(_C_KV, 6 * NSA_KV_W)
    kcmp_ref[...] = kv[:, 0 * LANE:1 * LANE]
    vcmp_ref[...] = kv[:, 1 * LANE:2 * LANE]
    ksel_ref[...] = kv[:, 2 * LANE:3 * LANE].astype(BF16)
    vsel_ref[...] = kv[:, 3 * LANE:4 * LANE].astype(BF16)
    kwin_ref[...] = kv[:, 4 * LANE:5 * LANE].astype(BF16)
    vwin_ref[...] = kv[:, 5 * LANE:6 * LANE].astype(BF16)
    ga_ref[...] = jax.nn.sigmoid(proj(_C_GA, LANE))

    cos2 = cos_ref[...]
    sin2 = sin_ref[...]
    qr = proj(_C_QR, RET_QK_W)
    kr = proj(_C_KR, RET_QK_W)
    for hd in range(RET_HEADS):
        sl = slice(hd * RET_QK_DIM, (hd + 1) * RET_QK_DIM)
        qh = qr[:, sl]
        kh = kr[:, sl]
        qr_ref[:, sl] = (qh * cos2 + pltpu.roll(qh, RET_QK_DIM // 2, 1) * sin2).astype(BF16)
        kr_ref[:, sl] = (kh * cos2 + pltpu.roll(kh, RET_QK_DIM // 2, 1) * sin2) * (RET_QK_DIM ** -0.5)
    vr_ref[...] = proj(_C_VR, RET_V_W).astype(BF16)
    gr_ref[...] = proj(_C_GR, RET_V_W)
    gta_ref[...] = jax.nn.sigmoid(proj(_C_GTA, D_MODEL))
    gtr_ref[...] = jax.nn.sigmoid(proj(_C_GTR, D_MODEL))


def _input_projection(x2d, norm_w, w_cat, cos2, sin2):
    S = x2d.shape[0]
    tm = IN_TM
    row = lambda w: pl.BlockSpec((tm, w), lambda i: (i, 0))
    outs = [
        (NSA_Q_W, BF16),
        (LANE, F32), (LANE, F32),
        (LANE, BF16), (LANE, BF16),
        (LANE, BF16), (LANE, BF16),
        (LANE, F32),
        (RET_QK_W, BF16), (RET_QK_W, F32),
        (RET_V_W, BF16), (RET_V_W, F32),
        (D_MODEL, F32), (D_MODEL, F32),
    ]
    return pl.pallas_call(
        _inproj_kernel,
        grid=(S // tm,),
        in_specs=[row(D_MODEL), _const_spec((1, D_MODEL)), _const_spec((D_MODEL, _C_END)),
                  row(LANE), row(LANE)],
        out_specs=[row(w) for w, _ in outs],
        out_shape=[jax.ShapeDtypeStruct((S, w), dt) for w, dt in outs],
        compiler_params=_params("parallel"),
        name="in_proj",
    )(x2d, norm_w, w_cat, cos2, sin2)


def _compress_kernel(x_ref, pos_ref, w1_ref, w2_ref, out_ref, hid_ref):
    n_tok = out_ref.shape[0]
    hid_ref[...] = jnp.zeros_like(hid_ref)

    def body(l, _):
        rows = x_ref[pl.ds(l, n_tok, stride=CMP_STRIDE), :] + pos_ref[pl.ds(l, 1), :]
        hid_ref[...] += _dot(rows.astype(BF16), w1_ref[l])
        return 0

    lax.fori_loop(0, CMP_LEN, body, 0)
    out_ref[...] = _dot(jax.nn.gelu(hid_ref[...]).astype(BF16), w2_ref[...])


def _compress(x, pos, w1, w2):
    S, width = x.shape
    n_tok = S // CMP_STRIDE
    G = NSA_KV_GROUPS
    def block_diag(w):
        w = w.astype(BF16)
        z = jnp.zeros_like(w)
        return jnp.concatenate([jnp.concatenate([w if k == g else z for k in range(G)], axis=-1)
                                for g in range(G)], axis=-2)

    w1_bd = block_diag(w1.reshape(CMP_LEN, NSA_HEAD_DIM, CMP_HIDDEN))
    w2_bd = block_diag(w2)
    x_pad = jnp.pad(x, ((0, CMP_LEN - CMP_STRIDE), (0, 0)))
    full = lambda a: _const_spec(a.shape)
    pos2 = jnp.concatenate([pos] * G, axis=1)
    return pl.pallas_call(
        _compress_kernel,
        grid=(1,),
        in_specs=[full(x_pad), full(pos2), full(w1_bd), full(w2_bd)],
        out_specs=pl.BlockSpec((n_tok, width), lambda i: (0, 0)),
        out_shape=jax.ShapeDtypeStruct((n_tok, width), F32),
        scratch_shapes=[pltpu.VMEM((n_tok, G * CMP_HIDDEN), F32)],
        compiler_params=_params("arbitrary"),
        name="nsa_compress",
    )(x_pad, pos2, w1_bd, w2_bd)


def _t5_bucket_np(dist):
    dist = np.maximum(dist, 0)
    max_exact = REL_BUCKETS // 2
    d_f = np.maximum(dist, 1).astype(np.float32)
    large = max_exact + (np.log(d_f / np.float32(max_exact)) / np.float32(math.log(REL_MAX_DIST / max_exact))
                         * np.float32(REL_BUCKETS - max_exact)).astype(np.int32)
    large = np.minimum(large, REL_BUCKETS - 1)
    return np.where(dist < max_exact, dist, large)


def _bias_table_t(rel_bias, dist, shift):
    bucket = jnp.asarray(np.where(dist >= 0, _t5_bucket_np(dist), -1), jnp.int32)
    rb = rel_bias - rel_bias[REL_BUCKETS - 1][None, :] if shift else rel_bias
    onehot = (bucket[None] == jnp.arange(REL_BUCKETS, dtype=jnp.int32)[:, None, None]).astype(F32)
    tab = jnp.einsum('bh,bkq->hkq', rb, onehot, precision=lax.Precision.HIGHEST)
    tab = jnp.where(jnp.asarray(dist == -1)[None], NEG, tab)
    K, Q = dist.shape
    return tab.reshape(NSA_KV_GROUPS, NSA_HPG, K, Q).transpose(0, 2, 1, 3).reshape(NSA_KV_GROUPS, K, NSA_HPG * Q)


def _nsa_kernel(n_keys, q_ref, ga_ref, kc_ref, vct_ref, tct_ref, ksel_ref, vselt_ref, et_ref, tnt_ref,
                kwin_ref, vwint_ref, twt_ref, out_ref, sc_ref, ps_ref, sa_ref, sb_ref):
    i = pl.program_id(0)
    dh = NSA_HEAD_DIM
    nb = (ps_ref.shape[1] - 2 * SCR_PAD) // (SEL_BLOCK // CMP_STRIDE)

    @pl.when(i == 0)
    def _():
        sc_ref[...] = jnp.zeros_like(sc_ref)
        ps_ref[...] = jnp.zeros_like(ps_ref)

    def colmax(x):
        return jnp.max(x, axis=0, keepdims=True)

    def tile4(x):
        return jnp.concatenate([x] * NSA_HPG, axis=1)

    gates = ga_ref[...]
    tl = i // QB_PER_CMP_TILE
    sub = i % QB_PER_CMP_TILE
    tp = tl - 1
    groups = range(NSA_KV_GROUPS)
    q_t_g = [jnp.concatenate([q_ref[NSA_HPG * g + j] for j in range(NSA_HPG)], axis=1) for g in groups]
    zeros = jnp.zeros((dh, QW), BF16)
    wq_g = [jnp.concatenate([q_t_g[g] if k == g else zeros for k in groups], axis=0) for g in groups]

    def cmp_scores(g, t):
        return _dot(kc_ref[pl.ds(pl.multiple_of(t * CMP_TILE, CMP_TILE), CMP_TILE), :], wq_g[g])

    def sc_rows(t):
        return pl.ds(pl.multiple_of(SCR_PAD + t * CMP_TILE, 8), CMP_TILE)

    def cmp_far(t, m):
        out = []
        for g in groups:
            s = cmp_scores(g, t)
            sc_ref[g, sc_rows(t), :] = s
            out.append(jnp.maximum(m[g], colmax(s)))
        return tuple(out)

    m_c = lax.fori_loop(0, jnp.maximum(tp, 0), cmp_far, tuple(jnp.full((1, QW), NEG, F32) for _ in groups))
    tpc = jnp.maximum(tp, 0)
    off = CMP_TILE - CMP_TOK_PER_QB - CMP_TOK_PER_QB * sub
    m_c = list(m_c)
    for g in groups:
        s = (cmp_scores(g, tpc) + tct_ref[g, pl.ds(pl.multiple_of(off, 8), CMP_TILE), :]
             + jnp.where(tp < 0, NEG, 0.0))
        sc_ref[g, sc_rows(tpc), :] = s
        m_c[g] = jnp.maximum(m_c[g], colmax(s))
    for g in groups:
        s = cmp_scores(g, tl) + tct_ref[g, pl.ds(pl.multiple_of(off + CMP_TILE, 8), CMP_TILE), :]
        sc_ref[g, sc_rows(tl), :] = s
        m_c[g] = jnp.maximum(m_c[g], colmax(s))

    def cmp_exp(t, carry):
        out = []
        for g in groups:
            l, acc = carry[g]
            e = jnp.exp(sc_ref[g, sc_rows(t), :] - m_c[g])
            sc_ref[g, sc_rows(t), :] = e
            vt = vct_ref[g, :, pl.ds(pl.multiple_of(t * CMP_TILE, CMP_TILE), CMP_TILE)]
            out.append((l + jnp.sum(e, axis=0, keepdims=True), acc + _dot(vt, e.astype(BF16))))
        return tuple(out)

    cmp_out = lax.fori_loop(0, tl + 1, cmp_exp,
                            tuple((jnp.zeros((1, QW), F32), jnp.zeros((dh, QW), F32)) for _ in groups))
    inv_c = [jnp.where(m_c[g] > 0.5 * NEG, 1.0 / jnp.maximum(cmp_out[g][0], 1e-30), 0.0) for g in groups]
    o_c_g = [cmp_out[g][1] * inv_c[g] for g in groups]

    def cmp_psum(t, _):
        for g in groups:
            p = sc_ref[g, sc_rows(t), :] * inv_c[g]
            ps = p[:, 0:Q_BLOCK]
            for j in range(1, NSA_HPG):
                ps = ps + p[:, j * Q_BLOCK:(j + 1) * Q_BLOCK]
            ps_ref[g, sc_rows(t), :] = ps.astype(BF16).astype(F32)
        return 0

    lax.fori_loop(0, tl + 1, cmp_psum, 0)
    per_blk = SEL_BLOCK // CMP_STRIDE
    imp_g = []
    for g in groups:
        imp = ps_ref[g, pl.ds(SCR_PAD - 1, nb, stride=per_blk), :]
        for k in range(1, CMP_LEN // CMP_STRIDE + per_blk - 1):
            imp = imp + ps_ref[g, pl.ds(SCR_PAD - 1 + k, nb, stride=per_blk), :]
        imp_g.append(imp)

    k0n = pl.multiple_of(i * Q_BLOCK, Q_BLOCK)
    o_w_g = []
    for g in groups:
        s = _dot(kwin_ref[pl.ds(k0n, WIN_KEYS), :], wq_g[g]) + twt_ref[g]
        pos = lax.broadcasted_iota(jnp.int32, (WIN_KEYS, QW), 0) + (i * Q_BLOCK - WINDOW)
        s = jnp.where(pos >= 0, s, NEG)
        e = jnp.exp(s - colmax(s)).astype(BF16)
        acc_w = _dot(vwint_ref[g, :, pl.ds(k0n, WIN_KEYS)], e)
        o_w_g.append(acc_w[0:dh] * (1.0 / jnp.maximum(acc_w[dh:dh + 1], 1e-30)))

    n_io = lax.broadcasted_iota(jnp.int32, (nb, Q_BLOCK), 0)
    r_io = lax.broadcasted_iota(jnp.int32, (nb, Q_BLOCK), 1)
    back = (i * Q_BLOCK + r_io) // SEL_BLOCK - n_io
    valid = back >= 0
    forced = (n_io == 0) | (valid & (back < N_LOCAL_SEL))
    work_g = [jnp.where(forced, -jnp.inf, jnp.where(valid, imp_g[g], -1e9)) for g in groups]
    n_f = n_io.astype(F32)
    for _ in range(SEL_TOP_N - 1 - N_LOCAL_SEL):
        for g in groups:
            mx = colmax(work_g[g])
            first = jnp.min(jnp.where(work_g[g] == mx, n_f, float(nb)), axis=0, keepdims=True)
            work_g[g] = jnp.where(n_f == first, -jnp.inf, work_g[g])
    sel_g = [jnp.where(work_g[g] == -jnp.inf, 1.0, 0.0) for g in groups]

    far_blocks = (i - 1) * (Q_BLOCK // SEL_BLOCK)
    selm_g = [(jnp.where(n_io < far_blocks, sel_g[g], 0.0) - 1.0).astype(BF16) for g in groups]

    far_end = (i - 1) * Q_BLOCK
    n_far = (jnp.maximum(far_end, 0) + SEL_TK - 1) // SEL_TK

    def flash_step(carry, s, vt):
        m_i, acc = carry
        m_new = jnp.maximum(m_i, colmax(s))
        alpha = jnp.exp(m_i - m_new)
        p = jnp.exp(s - m_new).astype(BF16)
        return m_new, alpha * acc + _dot(vt, p)

    carry = tuple((jnp.full((1, QW), NEG, F32), jnp.zeros((PV_ROWS, QW), F32)) for _ in groups)
    for w in range(nb // SEL_WIN):
        w_far = [jnp.concatenate([tile4(selm_g[g][w * SEL_WIN:(w + 1) * SEL_WIN]), wq_g[g]], axis=0)
                 for g in groups]

        lo = w * SEL_WIN_TILES
        last = min((w + 1) * SEL_WIN_TILES, n_keys // SEL_TK) - 1
        trips = (jnp.minimum(n_far, last + 1) - lo + 1) // 2

        def far_scores(kt, s_ref, g, w=w, w_far=w_far):
            k0 = pl.multiple_of(kt * SEL_TK, SEL_TK)
            e0 = pl.multiple_of((kt - w * SEL_WIN_TILES) * SEL_TK, SEL_TK)
            lhs = jnp.concatenate([et_ref[pl.ds(e0, SEL_TK), :],
                                   ksel_ref[pl.ds(k0 + Q_BLOCK, SEL_TK), :]], axis=1)
            s_ref[g] = _dot(lhs, w_far[g])

        def far_update(kt, s_ref, g, carry_g):
            k0 = pl.multiple_of(kt * SEL_TK, SEL_TK)
            return flash_step(carry_g, s_ref[g], vselt_ref[g, :, pl.ds(k0 + Q_BLOCK, SEL_TK)])

        def far_body(u, carry, lo=lo, last=last, far_scores=far_scores):
            kt = lo + 2 * u
            carry = list(carry)
            for g in groups:
                far_scores(kt + 1, sb_ref, g)
                carry[g] = far_update(kt, sa_ref, g, carry[g])
            for g in groups:
                far_scores(jnp.minimum(kt + 2, last), sa_ref, g)
                carry[g] = far_update(kt + 1, sb_ref, g, carry[g])
            return tuple(carry)

        @pl.when(trips > 0)
        def _(lo=lo, far_scores=far_scores):
            for g in groups:
                far_scores(jnp.int32(lo), sa_ref, g)

        carry = lax.fori_loop(0, trips, far_body, carry)

    heads_out = []
    for g in groups:
        wq, o_c, sel = wq_g[g], o_c_g[g], sel_g[g]
        k0n = pl.multiple_of(i * Q_BLOCK, Q_BLOCK)
        near_blocks = 2 * Q_BLOCK // SEL_BLOCK
        pb = lax.broadcasted_iota(jnp.int32, (16, nb), 0)
        pn = lax.broadcasted_iota(jnp.int32, (16, nb), 1)
        pick_rows = jnp.where((pn == far_blocks + pb) & (pb < near_blocks), 1.0, 0.0).astype(BF16)
        sel_rows = _dot(pick_rows, sel.astype(BF16))
        add_near = jnp.concatenate(
            [jnp.broadcast_to((sel_rows[b:b + 1, :] - 1.0) * (-NEG), (SEL_BLOCK, Q_BLOCK)) for b in range(near_blocks)],
            axis=0)
        s = _dot(ksel_ref[pl.ds(k0n, 2 * Q_BLOCK), :], wq) + tnt_ref[g] + tile4(add_near)
        m_s, acc_s = flash_step(carry[g], s, vselt_ref[g, :, pl.ds(k0n, 2 * Q_BLOCK)])
        o_s = acc_s[0:dh] * (1.0 / jnp.maximum(acc_s[dh:dh + 1], 1e-30))

        o_w = o_w_g[g]

        for j in range(NSA_HPG):
            hd = NSA_HPG * g + j
            cols = slice(j * Q_BLOCK, (j + 1) * Q_BLOCK)
            heads_out.append(gates[hd:hd + 1] * o_c[:, cols]
                             + gates[NSA_HEADS + hd:NSA_HEADS + hd + 1] * o_s[:, cols]
                             + gates[2 * NSA_HEADS + hd:2 * NSA_HEADS + hd + 1] * o_w[:, cols])

    out_ref[...] = jnp.concatenate(heads_out, axis=0).T.astype(out_ref.dtype)


def _nsa_attention(q_t, gates_t, kc, vct, tct, ksel, vselt, et, tnt, kwin, vwint, twt):
    H, dh, S = q_t.shape
    n_qb = S // Q_BLOCK
    n_tok = S // CMP_STRIDE
    nb = -(-(S // SEL_BLOCK) // SEL_WIN) * SEL_WIN
    ps_rows = nb * (SEL_BLOCK // CMP_STRIDE) + 2 * SCR_PAD
    full = lambda a: _const_spec(a.shape)
    return pl.pallas_call(
        functools.partial(_nsa_kernel, S),
        grid=(n_qb,),
        in_specs=[pl.BlockSpec((H, dh, Q_BLOCK), lambda i: (0, 0, i)),
                  pl.BlockSpec((gates_t.shape[0], Q_BLOCK), lambda i: (0, i)),
                  full(kc), full(vct), full(tct), full(ksel), full(vselt), full(et), full(tnt),
                  full(kwin), full(vwint), full(twt)],
        out_specs=pl.BlockSpec((Q_BLOCK, H * dh), lambda i: (i, 0)),
        out_shape=jax.ShapeDtypeStruct((S, H * dh), BF16),
        scratch_shapes=[pltpu.VMEM((NSA_KV_GROUPS, SCR_PAD + n_tok, QW), F32),
                        pltpu.VMEM((NSA_KV_GROUPS, ps_rows, Q_BLOCK), F32),
                        pltpu.VMEM((NSA_KV_GROUPS, SEL_TK, QW), F32),
                        pltpu.VMEM((NSA_KV_GROUPS, SEL_TK, QW), F32)],
        compiler_params=_params("arbitrary"),
        name="nsa_attention",
    )(q_t, gates_t, kc, vct, tct, ksel, vselt, et, tnt, kwin, vwint, twt)


def _retention_kernel(q_ref, k_ref, v_ref, g_ref, nw_ref, dmask_ref, qd_ref, kd_ref, cd_ref, out_ref, state_ref):
    @pl.when(pl.program_id(0) == 0)
    def _():
        state_ref[...] = jnp.zeros_like(state_ref)

    for hd in range(RET_HEADS):
        ks = slice(hd * RET_QK_DIM, (hd + 1) * RET_QK_DIM)
        vs = slice(hd * RET_V_DIM, (hd + 1) * RET_V_DIM)
        q = q_ref[:, ks]
        k = k_ref[:, ks]
        v = v_ref[:, vs]
        state = state_ref[hd]
        inner = _dot_nt(q, k.astype(BF16)) * dmask_ref[hd]
        y = _dot(inner.astype(BF16), v) + _dot(q, state.astype(BF16)) * qd_ref[hd]
        kd = k * kd_ref[hd]
        state_ref[hd] = state * cd_ref[hd] + _dot(kd.T.astype(BF16), v)
        mu = jnp.mean(y, axis=-1, keepdims=True)
        yc = y - mu
        var = jnp.mean(yc * yc, axis=-1, keepdims=True)
        yn = yc * lax.rsqrt(var + EPS) * nw_ref[:, vs]
        out_ref[:, vs] = (jax.nn.silu(g_ref[:, vs]) * yn).astype(out_ref.dtype)


def _retention(q, k, v, g, norm_w, dmask, qd, kd, cd):
    S = q.shape[0]
    C = RET_CHUNK
    row = lambda w: pl.BlockSpec((C, w), lambda i: (i, 0))
    full = lambda a: _const_spec(a.shape)
    return pl.pallas_call(
        _retention_kernel,
        grid=(S // C,),
        in_specs=[row(RET_QK_W), row(RET_QK_W), row(RET_V_W), row(RET_V_W),
                  full(norm_w), full(dmask), full(qd), full(kd), full(cd)],
        out_specs=row(RET_V_W),
        out_shape=jax.ShapeDtypeStruct((S, RET_V_W), BF16),
        scratch_shapes=[pltpu.VMEM((RET_HEADS, RET_QK_DIM, RET_V_DIM), F32)],
        compiler_params=_params("arbitrary"),
        name="retention",
    )(q, k, v, g, norm_w, dmask, qd, kd, cd)


MERGE_TM = 256
GRP_LANE = 40


def _merge_kernel(x_ref, ya_ref, yr_ref, gta_ref, gtr_ref, wa_ref, wr_ref, wo_ref, nw_ref, wrt_ref, brt_ref,
                  x1_ref, h_ref, comb_ref):
    merged = gta_ref[...] * _dot(ya_ref[...], wa_ref[...]) + gtr_ref[...] * _dot(yr_ref[...], wr_ref[...])
    x1 = x_ref[...] + _dot(merged.astype(BF16), wo_ref[...])
    x1_ref[...] = x1
    h = (x1 * lax.rsqrt(jnp.mean(x1 * x1, axis=-1, keepdims=True) + EPS) * nw_ref[...]).astype(BF16)
    h_ref[...] = _pack_bf16_pairs(h.astype(F32))
    logits = _dot(h, wrt_ref[...]) + brt_ref[...]
    lane = lax.broadcasted_iota(jnp.int32, logits.shape, 1)
    lane_f = lane.astype(F32)
    big = float(LANE)
    is_grp = (lane >= N_EXPERTS) & (lane < N_EXPERTS + N_GROUPS)
    lg = jnp.where(is_grp, logits, -jnp.inf)
    eg = jnp.where(is_grp, jnp.exp(lg - jnp.max(lg, axis=-1, keepdims=True)), 0.0)
    pg = eg / jnp.sum(eg, axis=-1, keepdims=True)
    p_grp = jnp.max(pg, axis=-1, keepdims=True)
    grp = jnp.min(jnp.where(is_grp & (pg == p_grp), lane_f, big), axis=-1, keepdims=True) - N_EXPERTS
    in_grp = (lane_f >= grp * EXPERTS_PER_GROUP) & (lane_f < (grp + 1.0) * EXPERTS_PER_GROUP)
    le = jnp.where(in_grp, logits, -jnp.inf)
    m1 = jnp.max(le, axis=-1, keepdims=True)
    i1 = jnp.min(jnp.where(le == m1, lane_f, big), axis=-1, keepdims=True)
    le2 = jnp.where(lane_f == i1, -jnp.inf, le)
    m2 = jnp.max(le2, axis=-1, keepdims=True)
    i2 = jnp.min(jnp.where(le2 == m2, lane_f, big), axis=-1, keepdims=True)
    e2 = jnp.exp(m2 - m1)
    den = 1.0 + e2
    comb_ref[...] = (jnp.where(lane_f == i1, (1.0 / den) * p_grp, 0.0)
                     + jnp.where(lane_f == i2, (e2 / den) * p_grp, 0.0)
                     + jnp.where(lane == GRP_LANE, grp, 0.0))


def _pack_bf16_pairs(x):
    n = x.shape[1] // 2
    bits = pltpu.bitcast(x, jnp.uint32)
    return (bits[:, :n] >> 16) | (bits[:, n:] & jnp.uint32(0xFFFF0000))


def _unpack_bf16_pairs(w):
    return pltpu.bitcast(w << 16, F32), pltpu.bitcast(w & jnp.uint32(0xFFFF0000), F32)


def _merge(x2d, ya, yr, gta, gtr, wa, wr, wo, nw, wrt, brt):
    S = x2d.shape[0]
    tm = MERGE_TM
    row = lambda w: pl.BlockSpec((tm, w), lambda i: (i, 0))
    full = lambda a: _const_spec(a.shape)
    return pl.pallas_call(
        _merge_kernel,
        grid=(S // tm,),
        in_specs=[row(D_MODEL), row(NSA_Q_W), row(RET_V_W), row(D_MODEL), row(D_MODEL),
                  full(wa), full(wr), full(wo), full(nw), full(wrt), full(brt)],
        out_specs=[row(D_MODEL), row(D_MODEL // 2), row(LANE)],
        out_shape=[jax.ShapeDtypeStruct((S, D_MODEL), F32), jax.ShapeDtypeStruct((S, D_MODEL // 2), jnp.uint32),
                   jax.ShapeDtypeStruct((S, LANE), F32)],
        compiler_params=_params("parallel"),
        name="merge_router",
    )(x2d, ya, yr, gta, gtr, wa, wr, wo, nw, wrt, brt)


MOE_TM = 1024
RANK_TM = 512
HALF = D_MODEL // 2


def _rank_kernel(comb_ref, tri_ref, rank_ref, cnt_ref, carry_ref):
    @pl.when(pl.program_id(0) == 0)
    def _():
        carry_ref[...] = jnp.zeros_like(carry_ref)

    comb = comb_ref[...]
    lane_f = lax.broadcasted_iota(jnp.int32, comb.shape, 1).astype(F32)
    onehot = jnp.where(lane_f == comb[:, GRP_LANE:GRP_LANE + 1], 1.0, 0.0)
    before = _dot(tri_ref[...], onehot.astype(BF16)) + carry_ref[...]
    rank_ref[...] = jnp.broadcast_to(jnp.sum(onehot * before, axis=-1, keepdims=True), comb.shape)
    carry_ref[...] += jnp.sum(onehot, axis=0, keepdims=True)
    cnt_ref[...] = jnp.broadcast_to(carry_ref[...], cnt_ref.shape)


def _group_rank(comb):
    S = comb.shape[0]
    tm = RANK_TM
    tri = jnp.asarray(np.tril(np.ones((tm, tm), np.float32), -1), BF16)
    return pl.pallas_call(
        _rank_kernel,
        grid=(S // tm,),
        in_specs=[pl.BlockSpec((tm, LANE), lambda i: (i, 0)), _const_spec((tm, tm))],
        out_specs=[pl.BlockSpec((tm, LANE), lambda i: (i, 0)), pl.BlockSpec((8, LANE), lambda i: (0, 0))],
        out_shape=[jax.ShapeDtypeStruct((S, LANE), F32), jax.ShapeDtypeStruct((8, LANE), F32)],
        scratch_shapes=[pltpu.VMEM((1, LANE), F32)],
        compiler_params=_params("arbitrary"),
        name="moe_group_rank",
    )(comb, tri)


def _permute_kernel(pos_ref, h_ref, comb_ref, hs_ref, cs_ref):
    i = pl.program_id(0)

    @pl.when(i == 0)
    def _():
        hs_ref[...] = jnp.zeros_like(hs_ref)
        cs_ref[...] = jnp.zeros_like(cs_ref)

    tm = h_ref.shape[0]

    def body(r, _):
        d = pos_ref[i * tm + r]
        hs_ref[pl.ds(d, 1), :] = h_ref[pl.ds(r, 1), :]
        cs_ref[pl.ds(d, 1), :] = comb_ref[pl.ds(r, 1), :]
        return 0

    lax.fori_loop(0, tm, body, 0, unroll=8)


def _permute(pos, hu, comb, n_sorted):
    S = hu.shape[0]
    tm = RANK_TM
    resident = lambda w: pl.BlockSpec((n_sorted, w), lambda i, p: (0, 0), pipeline_mode=pl.Buffered(1))
    return pl.pallas_call(
        _permute_kernel,
        grid_spec=pltpu.PrefetchScalarGridSpec(
            num_scalar_prefetch=1, grid=(S // tm,),
            in_specs=[pl.BlockSpec((tm, HALF), lambda i, p: (i, 0)), pl.BlockSpec((tm, LANE), lambda i, p: (i, 0))],
            out_specs=[resident(HALF), resident(LANE)]),
        out_shape=[jax.ShapeDtypeStruct((n_sorted, HALF), jnp.uint32), jax.ShapeDtypeStruct((n_sorted, LANE), F32)],
        compiler_params=_params("arbitrary"),
        name="moe_permute",
    )(pos, hu, comb)


def _moe_kernel(tg_ref, tv_ref, hs_ref, cs_ref, wg_ref, wu_ref, wd_ref, ys_ref, lo_ref, hi_ref, acc_ref):
    i = pl.program_id(0)
    e = pl.program_id(1)

    @pl.when(e == 0)
    def _():
        lo, hi = _unpack_bf16_pairs(hs_ref[...])
        lo_ref[...] = lo.astype(BF16)
        hi_ref[...] = hi.astype(BF16)
        acc_ref[...] = jnp.zeros_like(acc_ref)

    @pl.when(tv_ref[i] > 0)
    def _():
        lo, hi = lo_ref[...], hi_ref[...]
        wg = wg_ref[0].astype(BF16)
        wu = wu_ref[0].astype(BF16)
        gate = _dot(lo, wg[:HALF]) + _dot(hi, wg[HALF:])
        up = _dot(lo, wu[:HALF]) + _dot(hi, wu[HALF:])
        comb = cs_ref[...]
        lane = lax.broadcasted_iota(jnp.int32, comb.shape, 1)
        c = jnp.sum(jnp.where(lane == tg_ref[i] * EXPERTS_PER_GROUP + e, comb, 0.0), axis=-1, keepdims=True)
        acc_ref[...] += _dot((jax.nn.silu(gate) * up).astype(BF16), wd_ref[0].astype(BF16)) * c

    @pl.when(e == pl.num_programs(1) - 1)
    def _():
        ys_ref[...] = _pack_bf16_pairs(acc_ref[...].astype(BF16).astype(F32))


def _moe(tile_group, tile_valid, hs, cs, wg, wu, wd):
    n_sorted = hs.shape[0]
    tm = MOE_TM
    row = lambda w: pl.BlockSpec((tm, w), lambda i, e, tg, tv: (i, 0))
    wspec = lambda a, b: pl.BlockSpec((1, a, b), lambda i, e, tg, tv: (tg[i] * EXPERTS_PER_GROUP + e, 0, 0))
    return pl.pallas_call(
        _moe_kernel,
        grid_spec=pltpu.PrefetchScalarGridSpec(
            num_scalar_prefetch=2, grid=(n_sorted // tm, EXPERTS_PER_GROUP),
            in_specs=[row(HALF), row(LANE), wspec(D_MODEL, EXPERT_FF), wspec(D_MODEL, EXPERT_FF),
                      wspec(EXPERT_FF, D_MODEL)],
            out_specs=row(HALF),
            scratch_shapes=[pltpu.VMEM((tm, HALF), BF16), pltpu.VMEM((tm, HALF), BF16),
                            pltpu.VMEM((tm, D_MODEL), F32)]),
        out_shape=jax.ShapeDtypeStruct((n_sorted, HALF), jnp.uint32),
        compiler_params=_params("parallel", "arbitrary"),
        name="moe_experts",
    )(tile_group, tile_valid, hs, cs, wg, wu, wd)


def _unpermute_kernel(pos_ref, ys_ref, x1_ref, nw_ref, out_ref, stage_ref):
    i = pl.program_id(0)
    tm = x1_ref.shape[0]

    def body(r, _):
        stage_ref[pl.ds(r, 1), :] = ys_ref[pl.ds(pos_ref[i * tm + r], 1), :]
        return 0

    lax.fori_loop(0, tm, body, 0, unroll=8)
    lo, hi = _unpack_bf16_pairs(stage_ref[...])
    x2 = x1_ref[...] + jnp.concatenate([lo, hi], axis=1)
    out_ref[...] = x2 * lax.rsqrt(jnp.mean(x2 * x2, axis=-1, keepdims=True) + EPS) * nw_ref[...]


def _unpermute_residual_norm(pos, ys, x1, nw):
    S = x1.shape[0]
    tm = RANK_TM
    return pl.pallas_call(
        _unpermute_kernel,
        grid_spec=pltpu.PrefetchScalarGridSpec(
            num_scalar_prefetch=1, grid=(S // tm,),
            in_specs=[pl.BlockSpec(ys.shape, lambda i, p: (0, 0), pipeline_mode=pl.Buffered(1)),
                      pl.BlockSpec((tm, D_MODEL), lambda i, p: (i, 0)),
                      pl.BlockSpec((1, D_MODEL), lambda i, p: (0, 0))],
            out_specs=pl.BlockSpec((tm, D_MODEL), lambda i, p: (i, 0)),
            scratch_shapes=[pltpu.VMEM((tm, HALF), jnp.uint32)]),
        out_shape=jax.ShapeDtypeStruct((S, D_MODEL), F32),
        compiler_params=_params("parallel"),
        name="moe_unpermute_norm",
    )(pos, ys, x1, nw)


def _hierarchical_moe(hu, comb, x1, wg, wu, wd, final_norm_w):
    S = hu.shape[0]
    tm = MOE_TM
    n_sorted = S + N_GROUPS * tm
    rank, cnt = _group_rank(comb)
    grp = comb[:, GRP_LANE].astype(jnp.int32)
    counts = cnt[0, :N_GROUPS].astype(jnp.int32)
    padded = (counts + tm - 1) // tm * tm
    ends = jnp.cumsum(padded)
    starts = ends - padded
    pos = rank[:, 0].astype(jnp.int32) + jnp.sum(
        jnp.where(grp[:, None] == jnp.arange(N_GROUPS)[None, :], starts[None, :], 0), axis=1)
    tile_start = jnp.arange(n_sorted // tm, dtype=jnp.int32) * tm
    tile_group = jnp.minimum(jnp.sum((tile_start[:, None] >= ends[None, :]).astype(jnp.int32), axis=1), N_GROUPS - 1)
    tile_valid = (tile_start < ends[N_GROUPS - 1]).astype(jnp.int32)
    hs, cs = _permute(pos, hu, comb, n_sorted)
    ys = _moe(tile_group, tile_valid, hs, cs, wg, wu, wd)
    return _unpermute_residual_norm(pos, ys, x1, final_norm_w)


def _positional_tables(S):
    half = RET_QK_DIM // 2
    inv = ROPE_BASE ** (-jnp.arange(half, dtype=F32) / half)
    ang = jnp.arange(S)[:, None].astype(F32) * inv[None, :]
    cos, sin = jnp.cos(ang), jnp.sin(ang)
    cos2 = jnp.concatenate([cos, cos], axis=-1)
    sin2 = jnp.concatenate([-sin, sin], axis=-1)
    C, H = RET_CHUNK, RET_HEADS
    log_gamma = jnp.log(1.0 - 2.0 ** (-5.0 - jnp.arange(H, dtype=F32)))
    idx = jnp.arange(C, dtype=F32)
    diff = idx[:, None] - idx[None, :]
    dmask = jnp.where(diff >= 0, jnp.exp(jnp.maximum(diff, 0.0)[None] * log_gamma[:, None, None]), 0.0)
    qd = jnp.exp((idx + 1.0)[None, :] * log_gamma[:, None])
    kd = jnp.exp((C - 1.0 - idx)[None, :] * log_gamma[:, None])
    cd = jnp.exp(C * log_gamma)
    qd = jnp.broadcast_to(qd[:, :, None], (H, C, RET_V_DIM))
    kd = jnp.broadcast_to(kd[:, :, None], (H, C, RET_QK_DIM))
    cd = jnp.broadcast_to(cd[:, None, None], (H, 1, RET_V_DIM))
    return cos2, sin2, dmask, qd, kd, cd


def _attention_tables(rel_bias):
    r = np.arange(Q_BLOCK)[None, :]
    cc = np.arange(3 * CMP_TILE)[:, None] - CMP_TILE
    d = r - CMP_STRIDE * (cc - (CMP_TILE - CMP_TOK_PER_QB)) - (CMP_LEN - 1)
    d = np.where(cc < 0, -2, np.where((cc >= CMP_TILE) | (d < 0), -1, d))
    tct = _bias_table_t(rel_bias, d, shift=True)
    d = r + Q_BLOCK - np.arange(2 * Q_BLOCK)[:, None]
    tnt = _bias_table_t(rel_bias, np.where(d < 0, -1, d), shift=True)
    d = r + WINDOW - np.arange(WIN_KEYS)[:, None]
    twt = _bias_table_t(rel_bias, np.where((d < 0) | (d >= WINDOW), -1, d), shift=False)
    kk = np.arange(SEL_WIN * SEL_BLOCK)[:, None] // SEL_BLOCK
    et = jnp.where(jnp.asarray(kk == np.arange(SEL_WIN)[None, :]), -NEG, 0.0).astype(BF16)
    return tct, tnt, twt, et


def _layer(x2d, rel_bias, norm_mix, w_in, cmp_pos_k, cmp_k_w1, cmp_k_w2, cmp_pos_v, cmp_v_w1, cmp_v_w2,
           ret_norm, w_nsa_up, w_ret_up, w_out, norm_ffn, w_rg, b_rg, w_re, b_re, moe_wg, moe_wu, moe_wd,
           final_norm_w):
    S = x2d.shape[0]
    G, dh = NSA_KV_GROUPS, NSA_HEAD_DIM
    assert S % (CMP_TILE * CMP_STRIDE) == 0 and S // SEL_BLOCK >= SEL_TOP_N

    o = np.cumsum((0,) + (NSA_Q_W, 6 * NSA_KV_W, 3 * NSA_HEADS, RET_QK_W, RET_QK_W, RET_V_W, RET_V_W,
                          D_MODEL, D_MODEL))
    seg = lambda n: w_in[:, o[n]:o[n + 1]]
    w_cat = jnp.concatenate([seg(0), seg(1), seg(3), seg(4), seg(5), seg(6), seg(7), seg(8), seg(2),
                             jnp.zeros((D_MODEL, LANE - 3 * NSA_HEADS), F32)], axis=1).astype(BF16)
    cos2, sin2, dmask, qd, kd, cd = _positional_tables(S)

    (qa, kcmp, vcmp, ksel, vsel, kwin, vwin, ga, qr, kr, vr, gr, gta, gtr) = _input_projection(
        x2d, norm_mix.reshape(1, D_MODEL), w_cat, cos2, sin2)

    nch = S // CMP_STRIDE
    kc = _compress(kcmp, cmp_pos_k, cmp_k_w1, cmp_k_w2).astype(BF16)
    vct = _compress(vcmp, cmp_pos_v, cmp_v_w1, cmp_v_w2).T.reshape(G, dh, nch).astype(BF16)

    tct, tnt, twt, et = _attention_tables(rel_bias)

    def values_t(v, pad):
        vt = v.reshape(S, G, dh).transpose(1, 2, 0)
        extra = jnp.concatenate([jnp.ones((G, 1, S), BF16), jnp.zeros((G, PV_ROWS - dh - 1, S), BF16)], axis=1)
        return jnp.pad(jnp.concatenate([vt, extra], axis=1), ((0, 0), (0, 0), (pad, 0)))

    q_t = qa.T.reshape(NSA_HEADS, dh, S)
    gates_t = ga.T[:4 * NSA_HEADS]
    ya = _nsa_attention(q_t, gates_t, kc, vct, tct,
                        jnp.pad(ksel, ((Q_BLOCK, 0), (0, 0))), values_t(vsel, Q_BLOCK), et, tnt,
                        jnp.pad(kwin, ((WINDOW, 0), (0, 0))), values_t(vwin, WINDOW), twt)

    yr = _retention(qr, kr, vr, gr, ret_norm.reshape(1, RET_V_W), dmask, qd, kd, cd)

    wrt = jnp.concatenate([w_re, w_rg, jnp.zeros((D_MODEL, LANE - N_EXPERTS - N_GROUPS), F32)], axis=1).astype(BF16)
    brt = jnp.concatenate([b_re, b_rg, jnp.zeros((LANE - N_EXPERTS - N_GROUPS,), F32)]).reshape(1, LANE)
    x1, h2, comb = _merge(x2d, ya, yr, gta, gtr, w_nsa_up.astype(BF16), w_ret_up.astype(BF16),
                          w_out.astype(BF16), norm_ffn.reshape(1, D_MODEL), wrt, brt)
    return _hierarchical_moe(h2, comb, x1, moe_wg, moe_wu, moe_wd, final_norm_w)


def kernel(x, rel_bias, norm_mix, w_in, cmp_pos_k, cmp_k_w1, cmp_k_w2, cmp_pos_v, cmp_v_w1, cmp_v_w2, ret_norm, w_nsa_up, w_ret_up, w_out, norm_ffn, w_router_group, b_router_group, w_router_expert, b_router_expert, moe_w_gate, moe_w_up, moe_w_down, norm_final):
    B, S, D = x.shape
    assert B == 1 and norm_mix.shape[0] == 1, "single sequence, depth 1"
    out = _layer(x[0], rel_bias, norm_mix[0], w_in[0], cmp_pos_k[0], cmp_k_w1[0], cmp_k_w2[0],
                 cmp_pos_v[0], cmp_v_w1[0], cmp_v_w2[0], ret_norm[0], w_nsa_up[0], w_ret_up[0], w_out[0],
                 norm_ffn[0], w_router_group[0], b_router_group[0], w_router_expert[0], b_router_expert[0],
                 moe_w_gate[0], moe_w_up[0], moe_w_down[0], norm_final.reshape(1, D))
    return out[None]
```

```python
import functools
import math

import numpy as np
import jax
import jax.numpy as jnp
from jax import lax
from jax.experimental import pallas as pl
from jax.experimental.pallas import tpu as pltpu

F32 = jnp.float32
BF16 = jnp.bfloat16

D_MODEL = 1024
NSA_HEADS = 8
NSA_KV_GROUPS = 2
NSA_HPG = NSA_HEADS // NSA_KV_GROUPS
NSA_HEAD_DIM = 64
CMP_LEN = 32
CMP_STRIDE = 16
CMP_HIDDEN = 256
SEL_BLOCK = 64
SEL_TOP_N = 16
N_LOCAL_SEL = 2
WINDOW = 512
Q_BLOCK = 128
RET_HEADS = 4
RET_QK_DIM = 128
RET_V_DIM = 256
RET_CHUNK = 128
ROPE_BASE = 10000.0
REL_BUCKETS = 32
REL_MAX_DIST = 128
N_GROUPS = 4
EXPERTS_PER_GROUP = 8
N_EXPERTS = N_GROUPS * EXPERTS_PER_GROUP
EXPERT_FF = 256
EPS = 1e-6

NSA_Q_W = NSA_HEADS * NSA_HEAD_DIM
NSA_KV_W = NSA_KV_GROUPS * NSA_HEAD_DIM
RET_QK_W = RET_HEADS * RET_QK_DIM
RET_V_W = RET_HEADS * RET_V_DIM

LANE = 128
NEG = -1e30
VMEM_LIMIT = 56 * 1024 * 1024

CMP_TILE = 128
QB_PER_CMP_TILE = CMP_TILE * CMP_STRIDE // Q_BLOCK
CMP_TOK_PER_QB = Q_BLOCK // CMP_STRIDE
SCR_PAD = 8
SEL_TK = 512
SEL_WIN = 128
SEL_WIN_TILES = SEL_WIN * SEL_BLOCK // SEL_TK
PV_ROWS = 80
WIN_KEYS = WINDOW + Q_BLOCK
QW = NSA_HPG * Q_BLOCK


def _dot(a, b):
    return jnp.dot(a, b, preferred_element_type=F32)


def _dot_nt(a, b):
    return lax.dot_general(a, b, (((1,), (1,)), ((), ())), preferred_element_type=F32)


def _const_spec(shape):
    nd = len(shape)
    return pl.BlockSpec(shape, lambda *_: (0,) * nd, pipeline_mode=pl.Buffered(1))


def _params(*sem):
    return pltpu.CompilerParams(dimension_semantics=sem, vmem_limit_bytes=VMEM_LIMIT)


_C_QA = 0
_C_KV = _C_QA + NSA_Q_W
_C_QR = _C_KV + 6 * NSA_KV_W
_C_KR = _C_QR + RET_QK_W
_C_VR = _C_KR + RET_QK_W
_C_GR = _C_VR + RET_V_W
_C_GTA = _C_GR + RET_V_W
_C_GTR = _C_GTA + D_MODEL
_C_GA = _C_GTR + D_MODEL
_C_END = _C_GA + LANE
IN_TM = 256


def _inproj_kernel(x_ref, nw_ref, w_ref, inv_ref,
                   qat_ref, kcmp_ref, vcmp_ref, ksel_ref, vselt_ref, kwin_ref, vwint_ref, gat_ref,
                   qr_ref, kr_ref, vr_ref, gr_ref, gta_ref, gtr_ref):
    tm = x_ref.shape[0]
    x = x_ref[...]
    h = x * lax.rsqrt(jnp.mean(x * x, axis=-1, keepdims=True) + EPS) * nw_ref[...]
    hb = h.astype(BF16)

    def proj(a, width):
        return _dot(hb, w_ref[:, a:a + width])

    def values_t(v):
        vt = v.T.astype(BF16)
        aux = jnp.where(lax.broadcasted_iota(jnp.int32, (PV_ROWS - NSA_HEAD_DIM, tm), 0) == 0, 1.0, 0.0).astype(BF16)
        parts = []
        for g in range(NSA_KV_GROUPS):
            parts += [vt[g * NSA_HEAD_DIM:(g + 1) * NSA_HEAD_DIM], aux]
        return jnp.concatenate(parts, axis=0)

    qat_ref[...] = (proj(_C_QA, NSA_Q_W) * (NSA_HEAD_DIM ** -0.5)).T.astype(BF16)
    kv = proj(_C_KV, 6 * NSA_KV_W)
    kcmp_ref[...] = kv[:, 0 * LANE:1 * LANE]
    vcmp_ref[...] = kv[:, 1 * LANE:2 * LANE]
    ksel_ref[...] = kv[:, 2 * LANE:3 * LANE].astype(BF16)
    vselt_ref[...] = values_t(kv[:, 3 * LANE:4 * LANE])
    kwin_ref[...] = kv[:, 4 * LANE:5 * LANE].astype(BF16)
    vwint_ref[...] = values_t(kv[:, 5 * LANE:6 * LANE])
    gat_ref[...] = jax.nn.sigmoid(proj(_C_GA, LANE)).T

    pos = (pl.program_id(0) * tm + lax.broadcasted_iota(jnp.int32, (tm, 1), 0)).astype(F32)
    ang = pos * inv_ref[...]
    first_half = lax.broadcasted_iota(jnp.int32, (tm, RET_QK_DIM), 1) < RET_QK_DIM // 2
    cos2 = jnp.cos(ang)
    sin2 = jnp.where(first_half, -jnp.sin(ang), jnp.sin(ang))
    qr = proj(_C_QR, RET_QK_W)
    kr = proj(_C_KR, RET_QK_W)
    for hd in range(RET_HEADS):
        sl = slice(hd * RET_QK_DIM, (hd + 1) * RET_QK_DIM)
        qh = qr[:, sl]
        kh = kr[:, sl]
        qr_ref[:, sl] = (qh * cos2 + pltpu.roll(qh, RET_QK_DIM // 2, 1) * sin2).astype(BF16)
        kr_ref[:, sl] = (kh * cos2 + pltpu.roll(kh, RET_QK_DIM // 2, 1) * sin2) * (RET_QK_DIM ** -0.5)
    vr_ref[...] = proj(_C_VR, RET_V_W).astype(BF16)
    gr_ref[...] = proj(_C_GR, RET_V_W)
    gta_ref[...] = jax.nn.sigmoid(proj(_C_GTA, D_MODEL))
    gtr_ref[...] = jax.nn.sigmoid(proj(_C_GTR, D_MODEL))


def _input_projection(x2d, norm_w, w_cat, inv2):
    S = x2d.shape[0]
    tm = IN_TM
    vt_rows = NSA_KV_GROUPS * PV_ROWS
    outs = [
        (NSA_Q_W, BF16, True),
        (LANE, F32, False), (LANE, F32, False),
        (LANE, BF16, False), (vt_rows, BF16, True),
        (LANE, BF16, False), (vt_rows, BF16, True),
        (LANE, F32, True),
        (RET_QK_W, BF16, False), (RET_QK_W, F32, False),
        (RET_V_W, BF16, False), (RET_V_W, F32, False),
        (D_MODEL, F32, False), (D_MODEL, F32, False),
    ]
    row = lambda w: pl.BlockSpec((tm, w), lambda i: (i, 0))
    col = lambda w: pl.BlockSpec((w, tm), lambda i: (0, i))
    return pl.pallas_call(
        _inproj_kernel,
        grid=(S // tm,),
        in_specs=[row(D_MODEL), _const_spec((1, D_MODEL)), _const_spec((D_MODEL, _C_END)),
                  _const_spec((1, RET_QK_DIM))],
        out_specs=[col(w) if t else row(w) for w, _, t in outs],
        out_shape=[jax.ShapeDtypeStruct((w, S) if t else (S, w), dt) for w, dt, t in outs],
        compiler_params=_params("parallel"),
        name="in_proj",
    )(x2d, norm_w, w_cat, inv2)


def _compress_kernel(x_ref, pos_ref, w1_ref, w2_ref, out_ref, hid_ref):
    n_tok = out_ref.shape[0]
    hid_ref[...] = jnp.zeros_like(hid_ref)

    def body(l, _):
        rows = x_ref[pl.ds(l, n_tok, stride=CMP_STRIDE), :] + pos_ref[pl.ds(l, 1), :]
        hid_ref[...] += _dot(rows.astype(BF16), w1_ref[l])
        return 0

    lax.fori_loop(0, CMP_LEN, body, 0)
    out_ref[...] = _dot(jax.nn.gelu(hid_ref[...]).astype(BF16), w2_ref[...])


def _compress(x, pos, w1, w2):
    S, width = x.shape
    n_tok = S // CMP_STRIDE
    G = NSA_KV_GROUPS
    def block_diag(w):
        w = w.astype(BF16)
        z = jnp.zeros_like(w)
        return jnp.concatenate([jnp.concatenate([w if k == g else z for k in range(G)], axis=-1)
                                for g in range(G)], axis=-2)

    w1_bd = block_diag(w1.reshape(CMP_LEN, NSA_HEAD_DIM, CMP_HIDDEN))
    w2_bd = block_diag(w2)
    x_pad = jnp.pad(x, ((0, CMP_LEN - CMP_STRIDE), (0, 0)))
    full = lambda a: _const_spec(a.shape)
    pos2 = jnp.concatenate([pos] * G, axis=1)
    return pl.pallas_call(
        _compress_kernel,
        grid=(1,),
        in_specs=[full(x_pad), full(pos2), full(w1_bd), full(w2_bd)],
        out_specs=pl.BlockSpec((n_tok, width), lambda i: (0, 0)),
        out_shape=jax.ShapeDtypeStruct((n_tok, width), F32),
        scratch_shapes=[pltpu.VMEM((n_tok, G * CMP_HIDDEN), F32)],
        compiler_params=_params("arbitrary"),
        name="nsa_compress",
    )(x_pad, pos2, w1_bd, w2_bd)


def _t5_bucket_np(dist):
    dist = np.maximum(dist, 0)
    max_exact = REL_BUCKETS // 2
    d_f = np.maximum(dist, 1).astype(np.float32)
    large = max_exact + (np.log(d_f / np.float32(max_exact)) / np.float32(math.log(REL_MAX_DIST / max_exact))
                         * np.float32(REL_BUCKETS - max_exact)).astype(np.int32)
    large = np.minimum(large, REL_BUCKETS - 1)
    return np.where(dist < max_exact, dist, large)


def _bias_table_t(rel_bias, dist, shift):
    bucket = jnp.asarray(np.where(dist >= 0, _t5_bucket_np(dist), -1), jnp.int32)
    rb = rel_bias - rel_bias[REL_BUCKETS - 1][None, :] if shift else rel_bias
    onehot = (bucket[None] == jnp.arange(REL_BUCKETS, dtype=jnp.int32)[:, None, None]).astype(F32)
    tab = jnp.einsum('bh,bkq->hkq', rb, onehot, precision=lax.Precision.HIGHEST)
    tab = jnp.where(jnp.asarray(dist == -1)[None], NEG, tab)
    K, Q = dist.shape
    return tab.reshape(NSA_KV_GROUPS, NSA_HPG, K, Q).transpose(0, 2, 1, 3).reshape(NSA_KV_GROUPS, K, NSA_HPG * Q)


def _nsa_kernel(n_keys, q_ref, ga_ref, kc_ref, vct_ref, tct_ref, ksel_ref, vselt_ref, et_ref, tnt_ref,
                kwin_ref, vwint_ref, twt_ref, out_ref, sc_ref, ps_ref, sa_ref, sb_ref):
    i = pl.program_id(0)
    dh = NSA_HEAD_DIM
    nb = (ps_ref.shape[1] - 2 * SCR_PAD) // (SEL_BLOCK // CMP_STRIDE)

    @pl.when(i == 0)
    def _():
        sc_ref[...] = jnp.zeros_like(sc_ref)
        ps_ref[...] = jnp.zeros_like(ps_ref)

    def colmax(x):
        return jnp.max(x, axis=0, keepdims=True)

    def tile4(x):
        return jnp.concatenate([x] * NSA_HPG, axis=1)

    gates = ga_ref[...]
    tl = i // QB_PER_CMP_TILE
    sub = i % QB_PER_CMP_TILE
    tp = tl - 1
    groups = range(NSA_KV_GROUPS)
    q_t_g = [jnp.concatenate([q_ref[NSA_HPG * g + j] for j in range(NSA_HPG)], axis=1) for g in groups]
    zeros = jnp.zeros((dh, QW), BF16)
    wq_g = [jnp.concatenate([q_t_g[g] if k == g else zeros for k in groups], axis=0) for g in groups]

    def cmp_scores(g, t):
        return _dot(kc_ref[pl.ds(pl.multiple_of(t * CMP_TILE, CMP_TILE), CMP_TILE), :], wq_g[g])

    def sc_rows(t):
        return pl.ds(pl.multiple_of(SCR_PAD + t * CMP_TILE, 8), CMP_TILE)

    def cmp_far(t, m):
        out = []
        for g in groups:
            s = cmp_scores(g, t)
            sc_ref[g, sc_rows(t), :] = s
            out.append(jnp.maximum(m[g], colmax(s)))
        return tuple(out)

    m_c = lax.fori_loop(0, jnp.maximum(tp, 0), cmp_far, tuple(jnp.full((1, QW), NEG, F32) for _ in groups))
    tpc = jnp.maximum(tp, 0)
    off = CMP_TILE - CMP_TOK_PER_QB - CMP_TOK_PER_QB * sub
    m_c = list(m_c)
    for g in groups:
        s = (cmp_scores(g, tpc) + tct_ref[g, pl.ds(pl.multiple_of(off, 8), CMP_TILE), :]
             + jnp.where(tp < 0, NEG, 0.0))
        sc_ref[g, sc_rows(tpc), :] = s
        m_c[g] = jnp.maximum(m_c[g], colmax(s))
    for g in groups:
        s = cmp_scores(g, tl) + tct_ref[g, pl.ds(pl.multiple_of(off + CMP_TILE, 8), CMP_TILE), :]
        sc_ref[g, sc_rows(tl), :] = s
        m_c[g] = jnp.maximum(m_c[g], colmax(s))

    def cmp_exp(t, carry):
        out = []
        for g in groups:
            l, acc = carry[g]
            e = jnp.exp(sc_ref[g, sc_rows(t), :] - m_c[g])
            sc_ref[g, sc_rows(t), :] = e
            vt = vct_ref[g, :, pl.ds(pl.multiple_of(t * CMP_TILE, CMP_TILE), CMP_TILE)]
            out.append((l + jnp.sum(e, axis=0, keepdims=True), acc + _dot(vt, e.astype(BF16))))
        return tuple(out)

    cmp_out = lax.fori_loop(0, tl + 1, cmp_exp,
                            tuple((jnp.zeros((1, QW), F32), jnp.zeros((dh, QW), F32)) for _ in groups))
    inv_c = [jnp.where(m_c[g] > 0.5 * NEG, 1.0 / jnp.maximum(cmp_out[g][0], 1e-30), 0.0) for g in groups]
    o_c_g = [cmp_out[g][1] * inv_c[g] for g in groups]

    def cmp_psum(t, _):
        for g in groups:
            p = sc_ref[g, sc_rows(t), :] * inv_c[g]
            ps = p[:, 0:Q_BLOCK]
            for j in range(1, NSA_HPG):
                ps = ps + p[:, j * Q_BLOCK:(j + 1) * Q_BLOCK]
            ps_ref[g, sc_rows(t), :] = ps.astype(BF16).astype(F32)
        return 0

    lax.fori_loop(0, tl + 1, cmp_psum, 0)
    per_blk = SEL_BLOCK // CMP_STRIDE
    imp_g = []
    for g in groups:
        imp = ps_ref[g, pl.ds(SCR_PAD - 1, nb, stride=per_blk), :]
        for k in range(1, CMP_LEN // CMP_STRIDE + per_blk - 1):
            imp = imp + ps_ref[g, pl.ds(SCR_PAD - 1 + k, nb, stride=per_blk), :]
        imp_g.append(imp)

    kw0 = pl.multiple_of(jnp.maximum(i * Q_BLOCK - WINDOW, 0), Q_BLOCK)
    tw0 = pl.multiple_of(jnp.maximum(WINDOW - i * Q_BLOCK, 0), Q_BLOCK)
    o_w_g = []
    for g in groups:
        s = _dot(kwin_ref[pl.ds(kw0, WIN_KEYS), :], wq_g[g]) + twt_ref[g, pl.ds(tw0, WIN_KEYS), :]
        e = jnp.exp(s - colmax(s)).astype(BF16)
        acc_w = _dot(vwint_ref[g, :, pl.ds(kw0, WIN_KEYS)], e)
        o_w_g.append(acc_w[0:dh] * (1.0 / jnp.maximum(acc_w[dh:dh + 1], 1e-30)))

    n_io = lax.broadcasted_iota(jnp.int32, (nb, Q_BLOCK), 0)
    r_io = lax.broadcasted_iota(jnp.int32, (nb, Q_BLOCK), 1)
    back = (i * Q_BLOCK + r_io) // SEL_BLOCK - n_io
    valid = back >= 0
    forced = (n_io == 0) | (valid & (back < N_LOCAL_SEL))
    work_g = [jnp.where(forced, -jnp.inf, jnp.where(valid, imp_g[g], -1e9)) for g in groups]
    n_f = n_io.astype(F32)
    for _ in range(SEL_TOP_N - 1 - N_LOCAL_SEL):
        for g in groups:
            mx = colmax(work_g[g])
            first = jnp.min(jnp.where(work_g[g] == mx, n_f, float(nb)), axis=0, keepdims=True)
            work_g[g] = jnp.where(n_f == first, -jnp.inf, work_g[g])
    sel_g = [jnp.where(work_g[g] == -jnp.inf, 1.0, 0.0) for g in groups]

    far_blocks = (i - 1) * (Q_BLOCK // SEL_BLOCK)
    selm_g = [(jnp.where(n_io < far_blocks, sel_g[g], 0.0) - 1.0).astype(BF16) for g in groups]

    far_end = (i - 1) * Q_BLOCK
    n_far = (jnp.maximum(far_end, 0) + SEL_TK - 1) // SEL_TK

    def flash_step(carry, s, vt):
        m_i, acc = carry
        m_new = jnp.maximum(m_i, colmax(s))
        alpha = jnp.exp(m_i - m_new)
        p = jnp.exp(s - m_new).astype(BF16)
        return m_new, alpha * acc + _dot(vt, p)

    carry = tuple((jnp.full((1, QW), NEG, F32), jnp.zeros((PV_ROWS, QW), F32)) for _ in groups)
    for w in range(nb // SEL_WIN):
        w_far = [jnp.concatenate([tile4(selm_g[g][w * SEL_WIN:(w + 1) * SEL_WIN]), wq_g[g]], axis=0)
                 for g in groups]

        lo = w * SEL_WIN_TILES
        last = min((w + 1) * SEL_WIN_TILES, n_keys // SEL_TK) - 1
        trips = (jnp.minimum(n_far, last + 1) - lo + 1) // 2

        def far_scores(kt, s_ref, g, w=w, w_far=w_far):
            k0 = pl.multiple_of(kt * SEL_TK, SEL_TK)
            e0 = pl.multiple_of((kt - w * SEL_WIN_TILES) * SEL_TK, SEL_TK)
            lhs = jnp.concatenate([et_ref[pl.ds(e0, SEL_TK), :],
                                   ksel_ref[pl.ds(k0, SEL_TK), :]], axis=1)
            s_ref[g] = _dot(lhs, w_far[g])

        def far_update(kt, s_ref, g, carry_g):
            k0 = pl.multiple_of(kt * SEL_TK, SEL_TK)
            return flash_step(carry_g, s_ref[g], vselt_ref[g, :, pl.ds(k0, SEL_TK)])

        def far_body(u, carry, lo=lo, last=last, far_scores=far_scores):
            kt = lo + 2 * u
            carry = list(carry)
            for g in groups:
                far_scores(kt + 1, sb_ref, g)
                carry[g] = far_update(kt, sa_ref, g, carry[g])
            for g in groups:
                far_scores(jnp.minimum(kt + 2, last), sa_ref, g)
                carry[g] = far_update(kt + 1, sb_ref, g, carry[g])
            return tuple(carry)

        @pl.when(trips > 0)
        def _(lo=lo, far_scores=far_scores):
            for g in groups:
                far_scores(jnp.int32(lo), sa_ref, g)

        carry = lax.fori_loop(0, trips, far_body, carry)

    heads_out = []
    for g in groups:
        wq, o_c, sel = wq_g[g], o_c_g[g], sel_g[g]
        k0n = pl.multiple_of(jnp.maximum(far_end, 0), Q_BLOCK)
        t0n = pl.multiple_of(jnp.maximum(-far_end, 0), Q_BLOCK)
        near_blocks = 2 * Q_BLOCK // SEL_BLOCK
        pb = lax.broadcasted_iota(jnp.int32, (16, nb), 0)
        pn = lax.broadcasted_iota(jnp.int32, (16, nb), 1)
        pick_rows = jnp.where((pn == jnp.maximum(far_blocks, 0) + pb) & (pb < near_blocks), 1.0, 0.0).astype(BF16)
        sel_rows = _dot(pick_rows, sel.astype(BF16))
        add_near = jnp.concatenate(
            [jnp.broadcast_to((sel_rows[b:b + 1, :] - 1.0) * (-NEG), (SEL_BLOCK, Q_BLOCK)) for b in range(near_blocks)],
            axis=0)
        s = (_dot(ksel_ref[pl.ds(k0n, 2 * Q_BLOCK), :], wq) + tnt_ref[g, pl.ds(t0n, 2 * Q_BLOCK), :]
             + tile4(add_near))
        m_s, acc_s = flash_step(carry[g], s, vselt_ref[g, :, pl.ds(k0n, 2 * Q_BLOCK)])
        o_s = acc_s[0:dh] * (1.0 / jnp.maximum(acc_s[dh:dh + 1], 1e-30))

        o_w = o_w_g[g]

        for j in range(NSA_HPG):
            hd = NSA_HPG * g + j
            cols = slice(j * Q_BLOCK, (j + 1) * Q_BLOCK)
            heads_out.append(gates[hd:hd + 1] * o_c[:, cols]
                             + gates[NSA_HEADS + hd:NSA_HEADS + hd + 1] * o_s[:, cols]
                             + gates[2 * NSA_HEADS + hd:2 * NSA_HEADS + hd + 1] * o_w[:, cols])

    out_ref[...] = jnp.concatenate(heads_out, axis=0).T.astype(out_ref.dtype)


def _nsa_attention(q_t, gates_t, kc, vct, tct, ksel, vselt, et, tnt, kwin, vwint, twt):
    H, dh, S = q_t.shape
    n_qb = S // Q_BLOCK
    n_tok = S // CMP_STRIDE
    nb = -(-(S // SEL_BLOCK) // SEL_WIN) * SEL_WIN
    ps_rows = nb * (SEL_BLOCK // CMP_STRIDE) + 2 * SCR_PAD
    full = lambda a: _const_spec(a.shape)
    return pl.pallas_call(
        functools.partial(_nsa_kernel, S),
        grid=(n_qb,),
        in_specs=[pl.BlockSpec((H, dh, Q_BLOCK), lambda i: (0, 0, i)),
                  pl.BlockSpec((gates_t.shape[0], Q_BLOCK), lambda i: (0, i)),
                  full(kc), full(vct), full(tct), full(ksel), full(vselt), full(et), full(tnt),
                  full(kwin), full(vwint), full(twt)],
        out_specs=pl.BlockSpec((Q_BLOCK, H * dh), lambda i: (i, 0)),
        out_shape=jax.ShapeDtypeStruct((S, H * dh), BF16),
        scratch_shapes=[pltpu.VMEM((NSA_KV_GROUPS, SCR_PAD + n_tok, QW), F32),
                        pltpu.VMEM((NSA_KV_GROUPS, ps_rows, Q_BLOCK), F32),
                        pltpu.VMEM((NSA_KV_GROUPS, SEL_TK, QW), F32),
                        pltpu.VMEM((NSA_KV_GROUPS, SEL_TK, QW), F32)],
        compiler_params=_params("arbitrary"),
        name="nsa_attention",
    )(q_t, gates_t, kc, vct, tct, ksel, vselt, et, tnt, kwin, vwint, twt)


def _retention_kernel(q_ref, k_ref, v_ref, g_ref, nw_ref, dmask_ref, qd_ref, kd_ref, cd_ref, out_ref, state_ref):
    @pl.when(pl.program_id(0) == 0)
    def _():
        state_ref[...] = jnp.zeros_like(state_ref)

    for hd in range(RET_HEADS):
        ks = slice(hd * RET_QK_DIM, (hd + 1) * RET_QK_DIM)
        vs = slice(hd * RET_V_DIM, (hd + 1) * RET_V_DIM)
        q = q_ref[:, ks]
        k = k_ref[:, ks]
        v = v_ref[:, vs]
        state = state_ref[hd]
        inner = _dot_nt(q, k.astype(BF16)) * dmask_ref[hd]
        y = _dot(inner.astype(BF16), v) + _dot(q, state.astype(BF16)) * qd_ref[hd]
        kd = k * kd_ref[hd]
        state_ref[hd] = state * cd_ref[hd] + _dot(kd.T.astype(BF16), v)
        mu = jnp.mean(y, axis=-1, keepdims=True)
        yc = y - mu
        var = jnp.mean(yc * yc, axis=-1, keepdims=True)
        yn = yc * lax.rsqrt(var + EPS) * nw_ref[:, vs]
        out_ref[:, vs] = (jax.nn.silu(g_ref[:, vs]) * yn).astype(out_ref.dtype)


def _retention(q, k, v, g, norm_w, dmask, qd, kd, cd):
    S = q.shape[0]
    C = RET_CHUNK
    row = lambda w: pl.BlockSpec((C, w), lambda i: (i, 0))
    full = lambda a: _const_spec(a.shape)
    return pl.pallas_call(
        _retention_kernel,
        grid=(S // C,),
        in_specs=[row(RET_QK_W), row(RET_QK_W), row(RET_V_W), row(RET_V_W),
                  full(norm_w), full(dmask), full(qd), full(kd), full(cd)],
        out_specs=row(RET_V_W),
        out_shape=jax.ShapeDtypeStruct((S, RET_V_W), BF16),
        scratch_shapes=[pltpu.VMEM((RET_HEADS, RET_QK_DIM, RET_V_DIM), F32)],
        compiler_params=_params("arbitrary"),
        name="retention",
    )(q, k, v, g, norm_w, dmask, qd, kd, cd)


MERGE_TM = 256
GRP_LANE = 40


def _merge_kernel(x_ref, ya_ref, yr_ref, gta_ref, gtr_ref, wa_ref, wr_ref, wo_ref, nw_ref, wrt_ref, brt_ref,
                  x1_ref, h_ref, comb_ref):
    merged = gta_ref[...] * _dot(ya_ref[...], wa_ref[...]) + gtr_ref[...] * _dot(yr_ref[...], wr_ref[...])
    x1 = x_ref[...] + _dot(merged.astype(BF16), wo_ref[...])
    x1_ref[...] = x1
    h = (x1 * lax.rsqrt(jnp.mean(x1 * x1, axis=-1, keepdims=True) + EPS) * nw_ref[...]).astype(BF16)
    h_ref[...] = _pack_bf16_pairs(h.astype(F32))
    logits = _dot(h, wrt_ref[...]) + brt_ref[...]
    lane = lax.broadcasted_iota(jnp.int32, logits.shape, 1)
    lane_f = lane.astype(F32)
    big = float(LANE)
    is_grp = (lane >= N_EXPERTS) & (lane < N_EXPERTS + N_GROUPS)
    lg = jnp.where(is_grp, logits, -jnp.inf)
    eg = jnp.where(is_grp, jnp.exp(lg - jnp.max(lg, axis=-1, keepdims=True)), 0.0)
    pg = eg / jnp.sum(eg, axis=-1, keepdims=True)
    p_grp = jnp.max(pg, axis=-1, keepdims=True)
    grp = jnp.min(jnp.where(is_grp & (pg == p_grp), lane_f, big), axis=-1, keepdims=True) - N_EXPERTS
    in_grp = (lane_f >= grp * EXPERTS_PER_GROUP) & (lane_f < (grp + 1.0) * EXPERTS_PER_GROUP)
    le = jnp.where(in_grp, logits, -jnp.inf)
    m1 = jnp.max(le, axis=-1, keepdims=True)
    i1 = jnp.min(jnp.where(le == m1, lane_f, big), axis=-1, keepdims=True)
    le2 = jnp.where(lane_f == i1, -jnp.inf, le)
    m2 = jnp.max(le2, axis=-1, keepdims=True)
    i2 = jnp.min(jnp.where(le2 == m2, lane_f, big), axis=-1, keepdims=True)
    e2 = jnp.exp(m2 - m1)
    den = 1.0 + e2
    comb_ref[...] = (jnp.where(lane_f == i1, (1.0 / den) * p_grp, 0.0)
                     + jnp.where(lane_f == i2, (e2 / den) * p_grp, 0.0)
                     + jnp.where(lane == GRP_LANE, grp, 0.0))


def _pack_bf16_pairs(x):
    n = x.shape[1] // 2
    return pltpu.pack_elementwise([x[:, :n], x[:, n:]], packed_dtype=BF16)


def _unpack_bf16_pairs(w):
    return tuple(pltpu.unpack_elementwise(w, index=k, packed_dtype=BF16, unpacked_dtype=F32) for k in range(2))


def _merge(x2d, ya, yr, gta, gtr, wa, wr, wo, nw, wrt, brt):
    S = x2d.shape[0]
    tm = MERGE_TM
    row = lambda w: pl.BlockSpec((tm, w), lambda i: (i, 0))
    full = lambda a: _const_spec(a.shape)
    return pl.pallas_call(
        _merge_kernel,
        grid=(S // tm,),
        in_specs=[row(D_MODEL), row(NSA_Q_W), row(RET_V_W), row(D_MODEL), row(D_MODEL),
                  full(wa), full(wr), full(wo), full(nw), full(wrt), full(brt)],
        out_specs=[row(D_MODEL), row(D_MODEL // 2), row(LANE)],
        out_shape=[jax.ShapeDtypeStruct((S, D_MODEL), F32), jax.ShapeDtypeStruct((S, D_MODEL // 2), jnp.uint32),
                   jax.ShapeDtypeStruct((S, LANE), F32)],
        compiler_params=_params("parallel"),
        name="merge_router",
    )(x2d, ya, yr, gta, gtr, wa, wr, wo, nw, wrt, brt)


MOE_TM = 1024
RANK_TM = 512
HALF = D_MODEL // 2


def _rank_kernel(comb_ref, tri_ref, rank_ref, cnt_ref, carry_ref):
    @pl.when(pl.program_id(0) == 0)
    def _():
        carry_ref[...] = jnp.zeros_like(carry_ref)

    comb = comb_ref[...]
    lane_f = lax.broadcasted_iota(jnp.int32, comb.shape, 1).astype(F32)
    onehot = jnp.where(lane_f == comb[:, GRP_LANE:GRP_LANE + 1], 1.0, 0.0)
    before = _dot(tri_ref[...], onehot.astype(BF16)) + carry_ref[...]
    rank_ref[...] = jnp.broadcast_to(jnp.sum(onehot * before, axis=-1, keepdims=True), comb.shape)
    carry_ref[...] += jnp.sum(onehot, axis=0, keepdims=True)
    cnt_ref[...] = jnp.broadcast_to(carry_ref[...], cnt_ref.shape)


def _group_rank(comb):
    S = comb.shape[0]
    tm = RANK_TM
    tri = jnp.asarray(np.tril(np.ones((tm, tm), np.float32), -1), BF16)
    return pl.pallas_call(
        _rank_kernel,
        grid=(S // tm,),
        in_specs=[pl.BlockSpec((tm, LANE), lambda i: (i, 0)), _const_spec((tm, tm))],
        out_specs=[pl.BlockSpec((tm, LANE), lambda i: (i, 0)), pl.BlockSpec((8, LANE), lambda i: (0, 0))],
        out_shape=[jax.ShapeDtypeStruct((S, LANE), F32), jax.ShapeDtypeStruct((8, LANE), F32)],
        scratch_shapes=[pltpu.VMEM((1, LANE), F32)],
        compiler_params=_params("arbitrary"),
        name="moe_group_rank",
    )(comb, tri)


def _permute_kernel(pos_ref, h_ref, comb_ref, hs_ref, cs_ref):
    i = pl.program_id(0)

    @pl.when(i == 0)
    def _():
        hs_ref[...] = jnp.zeros_like(hs_ref)
        cs_ref[...] = jnp.zeros_like(cs_ref)

    tm = h_ref.shape[0]

    def body(r, _):
        d = pos_ref[i * tm + r]
        hs_ref[pl.ds(d, 1), :] = h_ref[pl.ds(r, 1), :]
        cs_ref[pl.ds(d, 1), :] = comb_ref[pl.ds(r, 1), :]
        return 0

    lax.fori_loop(0, tm, body, 0, unroll=8)


def _permute(pos, hu, comb, n_sorted):
    S = hu.shape[0]
    tm = RANK_TM
    resident = lambda w: pl.BlockSpec((n_sorted, w), lambda i, p: (0, 0), pipeline_mode=pl.Buffered(1))
    return pl.pallas_call(
        _permute_kernel,
        grid_spec=pltpu.PrefetchScalarGridSpec(
            num_scalar_prefetch=1, grid=(S // tm,),
            in_specs=[pl.BlockSpec((tm, HALF), lambda i, p: (i, 0)), pl.BlockSpec((tm, LANE), lambda i, p: (i, 0))],
            out_specs=[resident(HALF), resident(LANE)]),
        out_shape=[jax.ShapeDtypeStruct((n_sorted, HALF), jnp.uint32), jax.ShapeDtypeStruct((n_sorted, LANE), F32)],
        compiler_params=_params("arbitrary"),
        name="moe_permute",
    )(pos, hu, comb)


def _moe_kernel(tg_ref, tv_ref, hs_ref, cs_ref, wg_ref, wu_ref, wd_ref, ys_ref, lo_ref, hi_ref, acc_ref):
    i = pl.program_id(0)
    e = pl.program_id(1)

    @pl.when(e == 0)
    def _():
        lo, hi = _unpack_bf16_pairs(hs_ref[...])
        lo_ref[...] = lo.astype(BF16)
        hi_ref[...] = hi.astype(BF16)
        acc_ref[...] = jnp.zeros_like(acc_ref)

    @pl.when(tv_ref[i] > 0)
    def _():
        lo, hi = lo_ref[...], hi_ref[...]
        wg = wg_ref[0].astype(BF16)
        wu = wu_ref[0].astype(BF16)
        gate = _dot(lo, wg[:HALF]) + _dot(hi, wg[HALF:])
        up = _dot(lo, wu[:HALF]) + _dot(hi, wu[HALF:])
        comb = cs_ref[...]
        lane = lax.broadcasted_iota(jnp.int32, comb.shape, 1)
        c = jnp.sum(jnp.where(lane == tg_ref[i] * EXPERTS_PER_GROUP + e, comb, 0.0), axis=-1, keepdims=True)
        acc_ref[...] += _dot((jax.nn.silu(gate) * up).astype(BF16), wd_ref[0].astype(BF16)) * c

    @pl.when(e == pl.num_programs(1) - 1)
    def _():
        ys_ref[...] = _pack_bf16_pairs(acc_ref[...].astype(BF16).astype(F32))


def _moe(tile_group, tile_valid, hs, cs, wg, wu, wd):
    n_sorted = hs.shape[0]
    tm = MOE_TM
    row = lambda w: pl.BlockSpec((tm, w), lambda i, e, tg, tv: (i, 0))
    wspec = lambda a, b: pl.BlockSpec((1, a, b), lambda i, e, tg, tv: (tg[i] * EXPERTS_PER_GROUP + e, 0, 0))
    return pl.pallas_call(
        _moe_kernel,
        grid_spec=pltpu.PrefetchScalarGridSpec(
            num_scalar_prefetch=2, grid=(n_sorted // tm, EXPERTS_PER_GROUP),
            in_specs=[row(HALF), row(LANE), wspec(D_MODEL, EXPERT_FF), wspec(D_MODEL, EXPERT_FF),
                      wspec(EXPERT_FF, D_MODEL)],
            out_specs=row(HALF),
            scratch_shapes=[pltpu.VMEM((tm, HALF), BF16), pltpu.VMEM((tm, HALF), BF16),
                            pltpu.VMEM((tm, D_MODEL), F32)]),
        out_shape=jax.ShapeDtypeStruct((n_sorted, HALF), jnp.uint32),
        compiler_params=_params("parallel", "arbitrary"),
        name="moe_experts",
    )(tile_group, tile_valid, hs, cs, wg, wu, wd)


def _unpermute_kernel(pos_ref, ys_ref, x1_ref, nw_ref, out_ref, stage_ref):
    i = pl.program_id(0)
    tm = x1_ref.shape[0]

    def body(r, _):
        stage_ref[pl.ds(r, 1), :] = ys_ref[pl.ds(pos_ref[i * tm + r], 1), :]
        return 0

    lax.fori_loop(0, tm, body, 0, unroll=8)
    lo, hi = _unpack_bf16_pairs(stage_ref[...])
    x2 = x1_ref[...] + jnp.concatenate([lo, hi], axis=1)
    out_ref[...] = x2 * lax.rsqrt(jnp.mean(x2 * x2, axis=-1, keepdims=True) + EPS) * nw_ref[...]


def _unpermute_residual_norm(pos, ys, x1, nw):
    S = x1.shape[0]
    tm = RANK_TM
    return pl.pallas_call(
        _unpermute_kernel,
        grid_spec=pltpu.PrefetchScalarGridSpec(
            num_scalar_prefetch=1, grid=(S // tm,),
            in_specs=[pl.BlockSpec(ys.shape, lambda i, p: (0, 0), pipeline_mode=pl.Buffered(1)),
                      pl.BlockSpec((tm, D_MODEL), lambda i, p: (i, 0)),
                      pl.BlockSpec((1, D_MODEL), lambda i, p: (0, 0))],
            out_specs=pl.BlockSpec((tm, D_MODEL), lambda i, p: (i, 0)),
            scratch_shapes=[pltpu.VMEM((tm, HALF), jnp.uint32)]),
        out_shape=jax.ShapeDtypeStruct((S, D_MODEL), F32),
        compiler_params=_params("parallel"),
        name="moe_unpermute_norm",
    )(pos, ys, x1, nw)


def _hierarchical_moe(hu, comb, x1, wg, wu, wd, final_norm_w):
    S = hu.shape[0]
    tm = MOE_TM
    n_sorted = S + N_GROUPS * tm
    rank, cnt = _group_rank(comb)
    grp = comb[:, GRP_LANE].astype(jnp.int32)
    counts = cnt[0, :N_GROUPS].astype(jnp.int32)
    padded = (counts + tm - 1) // tm * tm
    ends = jnp.cumsum(padded)
    starts = ends - padded
    pos = rank[:, 0].astype(jnp.int32) + jnp.sum(
        jnp.where(grp[:, None] == jnp.arange(N_GROUPS)[None, :], starts[None, :], 0), axis=1)
    tile_start = jnp.arange(n_sorted // tm, dtype=jnp.int32) * tm
    tile_group = jnp.minimum(jnp.sum((tile_start[:, None] >= ends[None, :]).astype(jnp.int32), axis=1), N_GROUPS - 1)
    tile_valid = (tile_start < ends[N_GROUPS - 1]).astype(jnp.int32)
    hs, cs = _permute(pos, hu, comb, n_sorted)
    ys = _moe(tile_group, tile_valid, hs, cs, wg, wu, wd)
    return _unpermute_residual_norm(pos, ys, x1, final_norm_w)


def _positional_tables():
    half = RET_QK_DIM // 2
    inv = ROPE_BASE ** (-jnp.arange(half, dtype=F32) / half)
    inv2 = jnp.concatenate([inv, inv]).reshape(1, RET_QK_DIM)
    C, H = RET_CHUNK, RET_HEADS
    log_gamma = jnp.log(1.0 - 2.0 ** (-5.0 - jnp.arange(H, dtype=F32)))
    idx = jnp.arange(C, dtype=F32)
    diff = idx[:, None] - idx[None, :]
    dmask = jnp.where(diff >= 0, jnp.exp(jnp.maximum(diff, 0.0)[None] * log_gamma[:, None, None]), 0.0)
    qd = jnp.exp((idx + 1.0)[None, :] * log_gamma[:, None])
    kd = jnp.exp((C - 1.0 - idx)[None, :] * log_gamma[:, None])
    cd = jnp.exp(C * log_gamma)
    qd = jnp.broadcast_to(qd[:, :, None], (H, C, RET_V_DIM))
    kd = jnp.broadcast_to(kd[:, :, None], (H, C, RET_QK_DIM))
    cd = jnp.broadcast_to(cd[:, None, None], (H, 1, RET_V_DIM))
    return inv2, dmask, qd, kd, cd


def _attention_tables(rel_bias):
    r = np.arange(Q_BLOCK)[None, :]
    cc = np.arange(3 * CMP_TILE)[:, None] - CMP_TILE
    d = r - CMP_STRIDE * (cc - (CMP_TILE - CMP_TOK_PER_QB)) - (CMP_LEN - 1)
    d = np.where(cc < 0, -2, np.where((cc >= CMP_TILE) | (d < 0), -1, d))
    tct = _bias_table_t(rel_bias, d, shift=True)
    d = r + Q_BLOCK - np.arange(3 * Q_BLOCK)[:, None]
    tnt = _bias_table_t(rel_bias, np.where(d < 0, -1, d), shift=True)
    d = r + WINDOW - np.arange(WIN_KEYS + WINDOW)[:, None]
    twt = _bias_table_t(rel_bias, np.where((d < 0) | (d >= WINDOW), -1, d), shift=False)
    kk = np.arange(SEL_WIN * SEL_BLOCK)[:, None] // SEL_BLOCK
    et = jnp.where(jnp.asarray(kk == np.arange(SEL_WIN)[None, :]), -NEG, 0.0).astype(BF16)
    return tct, tnt, twt, et


def _layer(x2d, rel_bias, norm_mix, w_in, cmp_pos_k, cmp_k_w1, cmp_k_w2, cmp_pos_v, cmp_v_w1, cmp_v_w2,
           ret_norm, w_nsa_up, w_ret_up, w_out, norm_ffn, w_rg, b_rg, w_re, b_re, moe_wg, moe_wu, moe_wd,
           final_norm_w):
    S = x2d.shape[0]
    G, dh = NSA_KV_GROUPS, NSA_HEAD_DIM
    assert S % (CMP_TILE * CMP_STRIDE) == 0 and S // SEL_BLOCK >= SEL_TOP_N

    o = np.cumsum((0,) + (NSA_Q_W, 6 * NSA_KV_W, 3 * NSA_HEADS, RET_QK_W, RET_QK_W, RET_V_W, RET_V_W,
                          D_MODEL, D_MODEL))
    seg = lambda n: w_in[:, o[n]:o[n + 1]]
    w_cat = jnp.concatenate([seg(0), seg(1), seg(3), seg(4), seg(5), seg(6), seg(7), seg(8), seg(2),
                             jnp.zeros((D_MODEL, LANE - 3 * NSA_HEADS), F32)], axis=1).astype(BF16)
    inv2, dmask, qd, kd, cd = _positional_tables()

    (qa_t, kcmp, vcmp, ksel, vsel_t, kwin, vwin_t, ga_t, qr, kr, vr, gr, gta, gtr) = _input_projection(
        x2d, norm_mix.reshape(1, D_MODEL), w_cat, inv2)

    nch = S // CMP_STRIDE
    kc = _compress(kcmp, cmp_pos_k, cmp_k_w1, cmp_k_w2).astype(BF16)
    vct = _compress(vcmp, cmp_pos_v, cmp_v_w1, cmp_v_w2).T.reshape(G, dh, nch).astype(BF16)

    tct, tnt, twt, et = _attention_tables(rel_bias)
    ya = _nsa_attention(qa_t.reshape(NSA_HEADS, dh, S), ga_t, kc, vct, tct,
                        ksel, vsel_t.reshape(G, PV_ROWS, S), et, tnt,
                        kwin, vwin_t.reshape(G, PV_ROWS, S), twt)

    yr = _retention(qr, kr, vr, gr, ret_norm.reshape(1, RET_V_W), dmask, qd, kd, cd)

    wrt = jnp.concatenate([w_re, w_rg, jnp.zeros((D_MODEL, LANE - N_EXPERTS - N_GROUPS), F32)], axis=1).astype(BF16)
    brt = jnp.concatenate([b_re, b_rg, jnp.zeros((LANE - N_EXPERTS - N_GROUPS,), F32)]).reshape(1, LANE)
    x1, h2, comb = _merge(x2d, ya, yr, gta, gtr, w_nsa_up.astype(BF16), w_ret_up.astype(BF16),
                          w_out.astype(BF16), norm_ffn.reshape(1, D_MODEL), wrt, brt)
    return _hierarchical_moe(h2, comb, x1, moe_wg, moe_wu, moe_wd, final_norm_w)


def kernel(x, rel_bias, norm_mix, w_in, cmp_pos_k, cmp_k_w1, cmp_k_w2, cmp_pos_v, cmp_v_w1, cmp_v_w2, ret_norm, w_nsa_up, w_ret_up, w_out, norm_ffn, w_router_group, b_router_group, w_router_expert, b_router_expert, moe_w_gate, moe_w_up, moe_w_down, norm_final):
    B, S, D = x.shape
    assert B == 1 and norm_mix.shape[0] == 1, "single sequence, depth 1"
    out = _layer(x[0], rel_bias, norm_mix[0], w_in[0], cmp_pos_k[0], cmp_k_w1[0], cmp_k_w2[0],
                 cmp_pos_v[0], cmp_v_w1[0], cmp_v_w2[0], ret_norm[0], w_nsa_up[0], w_ret_up[0], w_out[0],
                 norm_ffn[0], w_router_group[0], b_router_group[0], w_router_expert[0], b_router_expert[0],
                 moe_w_gate[0], moe_w_up[0], moe_w_down[0], norm_final.reshape(1, D))
    return out[None]
```

```python
import functools
import math

import numpy as np
import jax
import jax.numpy as jnp
from jax import lax
from jax.experimental import pallas as pl
from jax.experimental.pallas import tpu as pltpu

F32 = jnp.float32
BF16 = jnp.bfloat16

D_MODEL = 1024
NSA_HEADS = 8
NSA_KV_GROUPS = 2
NSA_HPG = NSA_HEADS // NSA_KV_GROUPS
NSA_HEAD_DIM = 64
CMP_LEN = 32
CMP_STRIDE = 16
CMP_HIDDEN = 256
SEL_BLOCK = 64
SEL_TOP_N = 16
N_LOCAL_SEL = 2
WINDOW = 512
Q_BLOCK = 128
RET_HEADS = 4
RET_QK_DIM = 128
RET_V_DIM = 256
RET_CHUNK = 128
ROPE_BASE = 10000.0
REL_BUCKETS = 32
REL_MAX_DIST = 128
N_GROUPS = 4
EXPERTS_PER_GROUP = 8
N_EXPERTS = N_GROUPS * EXPERTS_PER_GROUP
EXPERT_FF = 256
EPS = 1e-6

NSA_Q_W = NSA_HEADS * NSA_HEAD_DIM
NSA_KV_W = NSA_KV_GROUPS * NSA_HEAD_DIM
RET_QK_W = RET_HEADS * RET_QK_DIM
RET_V_W = RET_HEADS * RET_V_DIM

LANE = 128
NEG = -1e30
VMEM_LIMIT = 56 * 1024 * 1024

CMP_TILE = 128
QB_PER_CMP_TILE = CMP_TILE * CMP_STRIDE // Q_BLOCK
CMP_TOK_PER_QB = Q_BLOCK // CMP_STRIDE
SCR_PAD = 8
SEL_TK = 512
SEL_WIN = 128
SEL_WIN_TILES = SEL_WIN * SEL_BLOCK // SEL_TK
PV_ROWS = 80
WIN_KEYS = WINDOW + Q_BLOCK
QW = NSA_HPG * Q_BLOCK


def _dot(a, b):
    return jnp.dot(a, b, preferred_element_type=F32)


def _dot_nt(a, b):
    return lax.dot_general(a, b, (((1,), (1,)), ((), ())), preferred_element_type=F32)


def _const_spec(shape):
    nd = len(shape)
    return pl.BlockSpec(shape, lambda *_: (0,) * nd, pipeline_mode=pl.Buffered(1))


def _params(*sem):
    return pltpu.CompilerParams(dimension_semantics=sem, vmem_limit_bytes=VMEM_LIMIT)


_C_QA = 0
_C_KV = _C_QA + NSA_Q_W
_C_QR = _C_KV + 6 * NSA_KV_W
_C_KR = _C_QR + RET_QK_W
_C_VR = _C_KR + RET_QK_W
_C_GR = _C_VR + RET_V_W
_C_GTA = _C_GR + RET_V_W
_C_GTR = _C_GTA + D_MODEL
_C_GA = _C_GTR + D_MODEL
_C_END = _C_GA + LANE
IN_TM = 512


def _inproj_kernel(x_ref, nw_ref, w_ref, inv_ref,
                   qat_ref, kcmp_ref, vcmp_ref, ksel_ref, vselt_ref, kwin_ref, vwint_ref, gat_ref,
                   qr_ref, kr_ref, vr_ref, gr_ref, gta_ref, gtr_ref):
    tm = x_ref.shape[0]
    x = x_ref[...]
    h = x * lax.rsqrt(jnp.mean(x * x, axis=-1, keepdims=True) + EPS) * nw_ref[...]
    hb = h.astype(BF16)

    def proj(a, width):
        return _dot(hb, w_ref[:, a:a + width])

    def values_t(v):
        vt = v.T.astype(BF16)
        aux = jnp.where(lax.broadcasted_iota(jnp.int32, (PV_ROWS - NSA_HEAD_DIM, tm), 0) == 0, 1.0, 0.0).astype(BF16)
        parts = []
        for g in range(NSA_KV_GROUPS):
            parts += [vt[g * NSA_HEAD_DIM:(g + 1) * NSA_HEAD_DIM], aux]
        return jnp.concatenate(parts, axis=0)

    qat_ref[...] = (proj(_C_QA, NSA_Q_W) * (NSA_HEAD_DIM ** -0.5)).T.astype(BF16)
    kv = proj(_C_KV, 6 * NSA_KV_W)
    kcmp_ref[...] = kv[:, 0 * LANE:1 * LANE]
    vcmp_ref[...] = kv[:, 1 * LANE:2 * LANE]
    ksel_ref[...] = kv[:, 2 * LANE:3 * LANE].astype(BF16)
    vselt_ref[...] = values_t(kv[:, 3 * LANE:4 * LANE])
    kwin_ref[...] = kv[:, 4 * LANE:5 * LANE].astype(BF16)
    vwint_ref[...] = values_t(kv[:, 5 * LANE:6 * LANE])
    gat_ref[...] = jax.nn.sigmoid(proj(_C_GA, LANE)).T

    pos = (pl.program_id(0) * tm + lax.broadcasted_iota(jnp.int32, (tm, 1), 0)).astype(F32)
    ang = pos * inv_ref[...]
    first_half = lax.broadcasted_iota(jnp.int32, (tm, RET_QK_DIM), 1) < RET_QK_DIM // 2
    cos2 = jnp.cos(ang)
    sin2 = jnp.where(first_half, -jnp.sin(ang), jnp.sin(ang))
    qr = proj(_C_QR, RET_QK_W)
    kr = proj(_C_KR, RET_QK_W)
    for hd in range(RET_HEADS):
        sl = slice(hd * RET_QK_DIM, (hd + 1) * RET_QK_DIM)
        qh = qr[:, sl]
        kh = kr[:, sl]
        qr_ref[:, sl] = (qh * cos2 + pltpu.roll(qh, RET_QK_DIM // 2, 1) * sin2).astype(BF16)
        kr_ref[:, sl] = (kh * cos2 + pltpu.roll(kh, RET_QK_DIM // 2, 1) * sin2) * (RET_QK_DIM ** -0.5)
    vr_ref[...] = proj(_C_VR, RET_V_W).astype(BF16)
    gr_ref[...] = proj(_C_GR, RET_V_W)
    gta_ref[...] = jax.nn.sigmoid(proj(_C_GTA, D_MODEL))
    gtr_ref[...] = jax.nn.sigmoid(proj(_C_GTR, D_MODEL))


def _input_projection(x2d, norm_w, w_cat, inv2):
    S = x2d.shape[0]
    tm = IN_TM
    vt_rows = NSA_KV_GROUPS * PV_ROWS
    outs = [
        (NSA_Q_W, BF16, True),
        (LANE, F32, False), (LANE, F32, False),
        (LANE, BF16, False), (vt_rows, BF16, True),
        (LANE, BF16, False), (vt_rows, BF16, True),
        (LANE, F32, True),
        (RET_QK_W, BF16, False), (RET_QK_W, F32, False),
        (RET_V_W, BF16, False), (RET_V_W, F32, False),
        (D_MODEL, F32, False), (D_MODEL, F32, False),
    ]
    row = lambda w: pl.BlockSpec((tm, w), lambda i: (i, 0))
    col = lambda w: pl.BlockSpec((w, tm), lambda i: (0, i))
    return pl.pallas_call(
        _inproj_kernel,
        grid=(S // tm,),
        in_specs=[row(D_MODEL), _const_spec((1, D_MODEL)), _const_spec((D_MODEL, _C_END)),
                  _const_spec((1, RET_QK_DIM))],
        out_specs=[col(w) if t else row(w) for w, _, t in outs],
        out_shape=[jax.ShapeDtypeStruct((w, S) if t else (S, w), dt) for w, dt, t in outs],
        compiler_params=_params("parallel"),
        name="in_proj",
    )(x2d, norm_w, w_cat, inv2)


def _compress_kernel(x_ref, pos_ref, w1_ref, w2_ref, out_ref, hid_ref):
    n_tok = out_ref.shape[0]
    hid_ref[...] = jnp.zeros_like(hid_ref)

    def body(l, _):
        rows = x_ref[pl.ds(l, n_tok, stride=CMP_STRIDE), :] + pos_ref[pl.ds(l, 1), :]
        hid_ref[...] += _dot(rows.astype(BF16), w1_ref[l])
        return 0

    lax.fori_loop(0, CMP_LEN, body, 0)
    out_ref[...] = _dot(jax.nn.gelu(hid_ref[...]).astype(BF16), w2_ref[...])


def _compress(x, pos, w1, w2):
    S, width = x.shape
    n_tok = S // CMP_STRIDE
    G = NSA_KV_GROUPS
    def block_diag(w):
        w = w.astype(BF16)
        z = jnp.zeros_like(w)
        return jnp.concatenate([jnp.concatenate([w if k == g else z for k in range(G)], axis=-1)
                                for g in range(G)], axis=-2)

    w1_bd = block_diag(w1.reshape(CMP_LEN, NSA_HEAD_DIM, CMP_HIDDEN))
    w2_bd = block_diag(w2)
    x_pad = jnp.pad(x, ((0, CMP_LEN - CMP_STRIDE), (0, 0)))
    full = lambda a: _const_spec(a.shape)
    pos2 = jnp.concatenate([pos] * G, axis=1)
    return pl.pallas_call(
        _compress_kernel,
        grid=(1,),
        in_specs=[full(x_pad), full(pos2), full(w1_bd), full(w2_bd)],
        out_specs=pl.BlockSpec((n_tok, width), lambda i: (0, 0)),
        out_shape=jax.ShapeDtypeStruct((n_tok, width), F32),
        scratch_shapes=[pltpu.VMEM((n_tok, G * CMP_HIDDEN), F32)],
        compiler_params=_params("arbitrary"),
        name="nsa_compress",
    )(x_pad, pos2, w1_bd, w2_bd)


def _t5_bucket_np(dist):
    dist = np.maximum(dist, 0)
    max_exact = REL_BUCKETS // 2
    d_f = np.maximum(dist, 1).astype(np.float32)
    large = max_exact + (np.log(d_f / np.float32(max_exact)) / np.float32(math.log(REL_MAX_DIST / max_exact))
                         * np.float32(REL_BUCKETS - max_exact)).astype(np.int32)
    large = np.minimum(large, REL_BUCKETS - 1)
    return np.where(dist < max_exact, dist, large)


def _bias_table_t(rel_bias, dist, shift):
    bucket = jnp.asarray(np.where(dist >= 0, _t5_bucket_np(dist), -1), jnp.int32)
    rb = rel_bias - rel_bias[REL_BUCKETS - 1][None, :] if shift else rel_bias
    onehot = (bucket[None] == jnp.arange(REL_BUCKETS, dtype=jnp.int32)[:, None, None]).astype(F32)
    tab = jnp.einsum('bh,bkq->hkq', rb, onehot, precision=lax.Precision.HIGHEST)
    tab = jnp.where(jnp.asarray(dist == -1)[None], NEG, tab)
    K, Q = dist.shape
    return tab.reshape(NSA_KV_GROUPS, NSA_HPG, K, Q).transpose(0, 2, 1, 3).reshape(NSA_KV_GROUPS, K, NSA_HPG * Q)


def _nsa_kernel(n_keys, q_ref, ga_ref, kc_ref, vct_ref, tct_ref, ksel_ref, vselt_ref, et_ref, tnt_ref,
                kwin_ref, vwint_ref, twt_ref, out_ref, sc_ref, ps_ref, sa_ref, sb_ref):
    i = pl.program_id(0)
    dh = NSA_HEAD_DIM
    nb = (ps_ref.shape[1] - 2 * SCR_PAD) // (SEL_BLOCK // CMP_STRIDE)

    @pl.when(i == 0)
    def _():
        sc_ref[...] = jnp.zeros_like(sc_ref)
        ps_ref[...] = jnp.zeros_like(ps_ref)

    def colmax(x):
        return jnp.max(x, axis=0, keepdims=True)

    def tile4(x):
        return jnp.concatenate([x] * NSA_HPG, axis=1)

    gates = ga_ref[...]
    tl = i // QB_PER_CMP_TILE
    sub = i % QB_PER_CMP_TILE
    tp = tl - 1
    groups = range(NSA_KV_GROUPS)
    q_t_g = [jnp.concatenate([q_ref[NSA_HPG * g + j] for j in range(NSA_HPG)], axis=1) for g in groups]
    zeros = jnp.zeros((dh, QW), BF16)
    wq_g = [jnp.concatenate([q_t_g[g] if k == g else zeros for k in groups], axis=0) for g in groups]

    def cmp_scores(g, t):
        return _dot(kc_ref[pl.ds(pl.multiple_of(t * CMP_TILE, CMP_TILE), CMP_TILE), :], wq_g[g])

    def sc_rows(t):
        return pl.ds(pl.multiple_of(SCR_PAD + t * CMP_TILE, 8), CMP_TILE)

    def cmp_far(t, m):
        out = []
        for g in groups:
            s = cmp_scores(g, t)
            sc_ref[g, sc_rows(t), :] = s
            out.append(jnp.maximum(m[g], colmax(s)))
        return tuple(out)

    m_c = lax.fori_loop(0, jnp.maximum(tp, 0), cmp_far, tuple(jnp.full((1, QW), NEG, F32) for _ in groups))
    tpc = jnp.maximum(tp, 0)
    off = CMP_TILE - CMP_TOK_PER_QB - CMP_TOK_PER_QB * sub
    m_c = list(m_c)
    for g in groups:
        s = (cmp_scores(g, tpc) + tct_ref[g, pl.ds(pl.multiple_of(off, 8), CMP_TILE), :]
             + jnp.where(tp < 0, NEG, 0.0))
        sc_ref[g, sc_rows(tpc), :] = s
        m_c[g] = jnp.maximum(m_c[g], colmax(s))
    for g in groups:
        s = cmp_scores(g, tl) + tct_ref[g, pl.ds(pl.multiple_of(off + CMP_TILE, 8), CMP_TILE), :]
        sc_ref[g, sc_rows(tl), :] = s
        m_c[g] = jnp.maximum(m_c[g], colmax(s))

    def cmp_exp(t, carry):
        out = []
        for g in groups:
            l, acc = carry[g]
            e = jnp.exp(sc_ref[g, sc_rows(t), :] - m_c[g])
            sc_ref[g, sc_rows(t), :] = e
            vt = vct_ref[g, :, pl.ds(pl.multiple_of(t * CMP_TILE, CMP_TILE), CMP_TILE)]
            out.append((l + jnp.sum(e, axis=0, keepdims=True), acc + _dot(vt, e.astype(BF16))))
        return tuple(out)

    cmp_out = lax.fori_loop(0, tl + 1, cmp_exp,
                            tuple((jnp.zeros((1, QW), F32), jnp.zeros((dh, QW), F32)) for _ in groups))
    inv_c = [jnp.where(m_c[g] > 0.5 * NEG, 1.0 / jnp.maximum(cmp_out[g][0], 1e-30), 0.0) for g in groups]
    o_c_g = [cmp_out[g][1] * inv_c[g] for g in groups]

    def cmp_psum(t, _):
        for g in groups:
            p = sc_ref[g, sc_rows(t), :] * inv_c[g]
            ps = p[:, 0:Q_BLOCK]
            for j in range(1, NSA_HPG):
                ps = ps + p[:, j * Q_BLOCK:(j + 1) * Q_BLOCK]
            ps_ref[g, sc_rows(t), :] = ps.astype(BF16).astype(F32)
        return 0

    lax.fori_loop(0, tl + 1, cmp_psum, 0)
    per_blk = SEL_BLOCK // CMP_STRIDE
    imp_g = []
    for g in groups:
        imp = ps_ref[g, pl.ds(SCR_PAD - 1, nb, stride=per_blk), :]
        for k in range(1, CMP_LEN // CMP_STRIDE + per_blk - 1):
            imp = imp + ps_ref[g, pl.ds(SCR_PAD - 1 + k, nb, stride=per_blk), :]
        imp_g.append(imp)

    kw0 = pl.multiple_of(jnp.maximum(i * Q_BLOCK - WINDOW, 0), Q_BLOCK)
    tw0 = pl.multiple_of(jnp.maximum(WINDOW - i * Q_BLOCK, 0), Q_BLOCK)
    o_w_g = []
    for g in groups:
        s = _dot(kwin_ref[pl.ds(kw0, WIN_KEYS), :], wq_g[g]) + twt_ref[g, pl.ds(tw0, WIN_KEYS), :]
        e = jnp.exp(s - colmax(s)).astype(BF16)
        acc_w = _dot(vwint_ref[g, :, pl.ds(kw0, WIN_KEYS)], e)
        o_w_g.append(acc_w[0:dh] * (1.0 / jnp.maximum(acc_w[dh:dh + 1], 1e-30)))

    n_io = lax.broadcasted_iota(jnp.int32, (nb, Q_BLOCK), 0)
    r_io = lax.broadcasted_iota(jnp.int32, (nb, Q_BLOCK), 1)
    back = (i * Q_BLOCK + r_io) // SEL_BLOCK - n_io
    valid = back >= 0
    forced = (n_io == 0) | (valid & (back < N_LOCAL_SEL))
    work_g = [jnp.where(forced, -jnp.inf, jnp.where(valid, imp_g[g], -1e9)) for g in groups]
    n_f = n_io.astype(F32)
    for _ in range(SEL_TOP_N - 1 - N_LOCAL_SEL):
        for g in groups:
            mx = colmax(work_g[g])
            first = jnp.min(jnp.where(work_g[g] == mx, n_f, float(nb)), axis=0, keepdims=True)
            work_g[g] = jnp.where(n_f == first, -jnp.inf, work_g[g])
    sel_g = [jnp.where(work_g[g] == -jnp.inf, 1.0, 0.0) for g in groups]

    far_blocks = (i - 1) * (Q_BLOCK // SEL_BLOCK)
    selm_g = [(jnp.where(n_io < far_blocks, sel_g[g], 0.0) - 1.0).astype(BF16) for g in groups]

    far_end = (i - 1) * Q_BLOCK
    n_far = (jnp.maximum(far_end, 0) + SEL_TK - 1) // SEL_TK

    def flash_step(carry, s, vt):
        m_i, acc = carry
        m_new = jnp.maximum(m_i, colmax(s))
        alpha = jnp.exp(m_i - m_new)
        p = jnp.exp(s - m_new).astype(BF16)
        return m_new, alpha * acc + _dot(vt, p)

    carry = tuple((jnp.full((1, QW), NEG, F32), jnp.zeros((PV_ROWS, QW), F32)) for _ in groups)
    for w in range(nb // SEL_WIN):
        w_far = [jnp.concatenate([tile4(selm_g[g][w * SEL_WIN:(w + 1) * SEL_WIN]), wq_g[g]], axis=0)
                 for g in groups]

        lo = w * SEL_WIN_TILES
        last = min((w + 1) * SEL_WIN_TILES, n_keys // SEL_TK) - 1
        trips = (jnp.minimum(n_far, last + 1) - lo + 1) // 2

        def far_scores(kt, s_ref, g, w=w, w_far=w_far):
            k0 = pl.multiple_of(kt * SEL_TK, SEL_TK)
            e0 = pl.multiple_of((kt - w * SEL_WIN_TILES) * SEL_TK, SEL_TK)
            lhs = jnp.concatenate([et_ref[pl.ds(e0, SEL_TK), :],
                                   ksel_ref[pl.ds(k0, SEL_TK), :]], axis=1)
            s_ref[g] = _dot(lhs, w_far[g])

        def far_update(kt, s_ref, g, carry_g):
            k0 = pl.multiple_of(kt * SEL_TK, SEL_TK)
            return flash_step(carry_g, s_ref[g], vselt_ref[g, :, pl.ds(k0, SEL_TK)])

        def far_body(u, carry, lo=lo, last=last, far_scores=far_scores):
            kt = lo + 2 * u
            carry = list(carry)
            for g in groups:
                far_scores(kt + 1, sb_ref, g)
                carry[g] = far_update(kt, sa_ref, g, carry[g])
            for g in groups:
                far_scores(jnp.minimum(kt + 2, last), sa_ref, g)
                carry[g] = far_update(kt + 1, sb_ref, g, carry[g])
            return tuple(carry)

        @pl.when(trips > 0)
        def _(lo=lo, far_scores=far_scores):
            for g in groups:
                far_scores(jnp.int32(lo), sa_ref, g)

        carry = lax.fori_loop(0, trips, far_body, carry)

    heads_out = []
    for g in groups:
        wq, o_c, sel = wq_g[g], o_c_g[g], sel_g[g]
        k0n = pl.multiple_of(jnp.maximum(far_end, 0), Q_BLOCK)
        t0n = pl.multiple_of(jnp.maximum(-far_end, 0), Q_BLOCK)
        near_blocks = 2 * Q_BLOCK // SEL_BLOCK
        pb = lax.broadcasted_iota(jnp.int32, (16, nb), 0)
        pn = lax.broadcasted_iota(jnp.int32, (16, nb), 1)
        pick_rows = jnp.where((pn == jnp.maximum(far_blocks, 0) + pb) & (pb < near_blocks), 1.0, 0.0).astype(BF16)
        sel_rows = _dot(pick_rows, sel.astype(BF16))
        add_near = jnp.concatenate(
            [jnp.broadcast_to((sel_rows[b:b + 1, :] - 1.0) * (-NEG), (SEL_BLOCK, Q_BLOCK)) for b in range(near_blocks)],
            axis=0)
        s = (_dot(ksel_ref[pl.ds(k0n, 2 * Q_BLOCK), :], wq) + tnt_ref[g, pl.ds(t0n, 2 * Q_BLOCK), :]
             + tile4(add_near))
        m_s, acc_s = flash_step(carry[g], s, vselt_ref[g, :, pl.ds(k0n, 2 * Q_BLOCK)])
        o_s = acc_s[0:dh] * (1.0 / jnp.maximum(acc_s[dh:dh + 1], 1e-30))

        o_w = o_w_g[g]

        for j in range(NSA_HPG):
            hd = NSA_HPG * g + j
            cols = slice(j * Q_BLOCK, (j + 1) * Q_BLOCK)
            heads_out.append(gates[hd:hd + 1] * o_c[:, cols]
                             + gates[NSA_HEADS + hd:NSA_HEADS + hd + 1] * o_s[:, cols]
                             + gates[2 * NSA_HEADS + hd:2 * NSA_HEADS + hd + 1] * o_w[:, cols])

    out_ref[...] = jnp.concatenate(heads_out, axis=0).T.astype(out_ref.dtype)


def _nsa_attention(q_t, gates_t, kc, vct, tct, ksel, vselt, et, tnt, kwin, vwint, twt):
    H, dh, S = q_t.shape
    n_qb = S // Q_BLOCK
    n_tok = S // CMP_STRIDE
    nb = -(-(S // SEL_BLOCK) // SEL_WIN) * SEL_WIN
    ps_rows = nb * (SEL_BLOCK // CMP_STRIDE) + 2 * SCR_PAD
    full = lambda a: _const_spec(a.shape)
    return pl.pallas_call(
        functools.partial(_nsa_kernel, S),
        grid=(n_qb,),
        in_specs=[pl.BlockSpec((H, dh, Q_BLOCK), lambda i: (0, 0, i)),
                  pl.BlockSpec((gates_t.shape[0], Q_BLOCK), lambda i: (0, i)),
                  full(kc), full(vct), full(tct), full(ksel), full(vselt), full(et), full(tnt),
                  full(kwin), full(vwint), full(twt)],
        out_specs=pl.BlockSpec((Q_BLOCK, H * dh), lambda i: (i, 0)),
        out_shape=jax.ShapeDtypeStruct((S, H * dh), BF16),
        scratch_shapes=[pltpu.VMEM((NSA_KV_GROUPS, SCR_PAD + n_tok, QW), F32),
                        pltpu.VMEM((NSA_KV_GROUPS, ps_rows, Q_BLOCK), F32),
                        pltpu.VMEM((NSA_KV_GROUPS, SEL_TK, QW), F32),
                        pltpu.VMEM((NSA_KV_GROUPS, SEL_TK, QW), F32)],
        compiler_params=_params("arbitrary"),
        name="nsa_attention",
    )(q_t, gates_t, kc, vct, tct, ksel, vselt, et, tnt, kwin, vwint, twt)


def _retention_kernel(q_ref, k_ref, v_ref, g_ref, nw_ref, dmask_ref, qd_ref, kd_ref, cd_ref, out_ref, state_ref):
    @pl.when(pl.program_id(0) == 0)
    def _():
        state_ref[...] = jnp.zeros_like(state_ref)

    states = [state_ref[hd] for hd in range(RET_HEADS)]
    for c in range(q_ref.shape[0] // RET_CHUNK):
        rows = slice(c * RET_CHUNK, (c + 1) * RET_CHUNK)
        for hd in range(RET_HEADS):
            ks = slice(hd * RET_QK_DIM, (hd + 1) * RET_QK_DIM)
            vs = slice(hd * RET_V_DIM, (hd + 1) * RET_V_DIM)
            q = q_ref[rows, ks]
            k = k_ref[rows, ks]
            v = v_ref[rows, vs]
            state = states[hd]
            inner = _dot_nt(q, k.astype(BF16)) * dmask_ref[hd]
            y = _dot(inner.astype(BF16), v) + _dot(q, state.astype(BF16)) * qd_ref[hd]
            kd = k * kd_ref[hd]
            states[hd] = state * cd_ref[hd] + _dot(kd.T.astype(BF16), v)
            mu = jnp.mean(y, axis=-1, keepdims=True)
            yc = y - mu
            var = jnp.mean(yc * yc, axis=-1, keepdims=True)
            yn = yc * lax.rsqrt(var + EPS) * nw_ref[:, vs]
            out_ref[rows, vs] = (jax.nn.silu(g_ref[rows, vs]) * yn).astype(out_ref.dtype)
    for hd in range(RET_HEADS):
        state_ref[hd] = states[hd]


RET_STEP_CHUNKS = 4


def _retention(q, k, v, g, norm_w, dmask, qd, kd, cd):
    S = q.shape[0]
    C = RET_CHUNK * RET_STEP_CHUNKS
    row = lambda w: pl.BlockSpec((C, w), lambda i: (i, 0))
    full = lambda a: _const_spec(a.shape)
    return pl.pallas_call(
        _retention_kernel,
        grid=(S // C,),
        in_specs=[row(RET_QK_W), row(RET_QK_W), row(RET_V_W), row(RET_V_W),
                  full(norm_w), full(dmask), full(qd), full(kd), full(cd)],
        out_specs=row(RET_V_W),
        out_shape=jax.ShapeDtypeStruct((S, RET_V_W), BF16),
        scratch_shapes=[pltpu.VMEM((RET_HEADS, RET_QK_DIM, RET_V_DIM), F32)],
        compiler_params=_params("arbitrary"),
        name="retention",
    )(q, k, v, g, norm_w, dmask, qd, kd, cd)


MERGE_TM = 256
GRP_LANE = 40
RANK_LANE = 41


def _merge_kernel(x_ref, ya_ref, yr_ref, gta_ref, gtr_ref, wa_ref, wr_ref, wo_ref, nw_ref, wrt_ref, brt_ref,
                  tri_ref, x1_ref, h_ref, comb_ref, cnt_ref, carry_ref):
    merged = gta_ref[...] * _dot(ya_ref[...], wa_ref[...]) + gtr_ref[...] * _dot(yr_ref[...], wr_ref[...])
    x1 = x_ref[...] + _dot(merged.astype(BF16), wo_ref[...])
    x1_ref[...] = x1
    h = (x1 * lax.rsqrt(jnp.mean(x1 * x1, axis=-1, keepdims=True) + EPS) * nw_ref[...]).astype(BF16)
    h_ref[...] = _pack_bf16_pairs(h.astype(F32))
    logits = _dot(h, wrt_ref[...]) + brt_ref[...]
    lane = lax.broadcasted_iota(jnp.int32, logits.shape, 1)
    lane_f = lane.astype(F32)
    big = float(LANE)
    is_grp = (lane >= N_EXPERTS) & (lane < N_EXPERTS + N_GROUPS)
    lg = jnp.where(is_grp, logits, -jnp.inf)
    eg = jnp.where(is_grp, jnp.exp(lg - jnp.max(lg, axis=-1, keepdims=True)), 0.0)
    pg = eg / jnp.sum(eg, axis=-1, keepdims=True)
    p_grp = jnp.max(pg, axis=-1, keepdims=True)
    grp = jnp.min(jnp.where(is_grp & (pg == p_grp), lane_f, big), axis=-1, keepdims=True) - N_EXPERTS
    in_grp = (lane_f >= grp * EXPERTS_PER_GROUP) & (lane_f < (grp + 1.0) * EXPERTS_PER_GROUP)
    le = jnp.where(in_grp, logits, -jnp.inf)
    m1 = jnp.max(le, axis=-1, keepdims=True)
    i1 = jnp.min(jnp.where(le == m1, lane_f, big), axis=-1, keepdims=True)
    le2 = jnp.where(lane_f == i1, -jnp.inf, le)
    m2 = jnp.max(le2, axis=-1, keepdims=True)
    i2 = jnp.min(jnp.where(le2 == m2, lane_f, big), axis=-1, keepdims=True)
    e2 = jnp.exp(m2 - m1)
    den = 1.0 + e2
    @pl.when(pl.program_id(0) == 0)
    def _():
        carry_ref[...] = jnp.zeros_like(carry_ref)

    onehot = jnp.where(lane_f == grp, 1.0, 0.0)
    before = _dot(tri_ref[...], onehot.astype(BF16)) + carry_ref[...]
    rank = jnp.sum(onehot * before, axis=-1, keepdims=True)
    carry_ref[...] += jnp.sum(onehot, axis=0, keepdims=True)
    cnt_ref[...] = jnp.broadcast_to(carry_ref[...], cnt_ref.shape)
    comb_ref[...] = (jnp.where(lane_f == i1, (1.0 / den) * p_grp, 0.0)
                     + jnp.where(lane_f == i2, (e2 / den) * p_grp, 0.0)
                     + jnp.where(lane == GRP_LANE, grp, 0.0)
                     + jnp.where(lane == RANK_LANE, rank, 0.0))


def _pack_bf16_pairs(x):
    n = x.shape[1] // 2
    return pltpu.pack_elementwise([x[:, :n], x[:, n:]], packed_dtype=BF16)


def _unpack_bf16_pairs(w):
    return tuple(pltpu.unpack_elementwise(w, index=k, packed_dtype=BF16, unpacked_dtype=F32) for k in range(2))


def _merge(x2d, ya, yr, gta, gtr, wa, wr, wo, nw, wrt, brt):
    S = x2d.shape[0]
    tm = MERGE_TM
    row = lambda w: pl.BlockSpec((tm, w), lambda i: (i, 0))
    full = lambda a: _const_spec(a.shape)
    tri = jnp.asarray(np.tril(np.ones((tm, tm), np.float32), -1), BF16)
    return pl.pallas_call(
        _merge_kernel,
        grid=(S // tm,),
        in_specs=[row(D_MODEL), row(NSA_Q_W), row(RET_V_W), row(D_MODEL), row(D_MODEL),
                  full(wa), full(wr), full(wo), full(nw), full(wrt), full(brt), full(tri)],
        out_specs=[row(D_MODEL), row(D_MODEL // 2), row(LANE), pl.BlockSpec((8, LANE), lambda i: (0, 0))],
        out_shape=[jax.ShapeDtypeStruct((S, D_MODEL), F32), jax.ShapeDtypeStruct((S, D_MODEL // 2), jnp.uint32),
                   jax.ShapeDtypeStruct((S, LANE), F32), jax.ShapeDtypeStruct((8, LANE), F32)],
        scratch_shapes=[pltpu.VMEM((1, LANE), F32)],
        compiler_params=_params("arbitrary"),
        name="merge_router",
    )(x2d, ya, yr, gta, gtr, wa, wr, wo, nw, wrt, brt, tri)


MOE_TM = 1024
RANK_TM = 512
HALF = D_MODEL // 2


def _permute_kernel(pos_ref, h_ref, comb_ref, hs_ref, cs_ref):
    i = pl.program_id(0)

    @pl.when(i == 0)
    def _():
        hs_ref[...] = jnp.zeros_like(hs_ref)
        cs_ref[...] = jnp.zeros_like(cs_ref)

    tm = h_ref.shape[0]

    def body(r, _):
        d = pos_ref[i * tm + r]
        hs_ref[pl.ds(d, 1), :] = h_ref[pl.ds(r, 1), :]
        cs_ref[pl.ds(d, 1), :] = comb_ref[pl.ds(r, 1), :]
        return 0

    lax.fori_loop(0, tm, body, 0, unroll=8)


def _permute(pos, hu, comb, n_sorted):
    S = hu.shape[0]
    tm = RANK_TM
    resident = lambda w: pl.BlockSpec((n_sorted, w), lambda i, p: (0, 0), pipeline_mode=pl.Buffered(1))
    return pl.pallas_call(
        _permute_kernel,
        grid_spec=pltpu.PrefetchScalarGridSpec(
            num_scalar_prefetch=1, grid=(S // tm,),
            in_specs=[pl.BlockSpec((tm, HALF), lambda i, p: (i, 0)), pl.BlockSpec((tm, LANE), lambda i, p: (i, 0))],
            out_specs=[resident(HALF), resident(LANE)]),
        out_shape=[jax.ShapeDtypeStruct((n_sorted, HALF), jnp.uint32), jax.ShapeDtypeStruct((n_sorted, LANE), F32)],
        compiler_params=_params("arbitrary"),
        name="moe_permute",
    )(pos, hu, comb)


def _moe_kernel(tg_ref, tv_ref, hs_ref, cs_ref, wg_ref, wu_ref, wd_ref, ys_ref, lo_ref, hi_ref, acc_ref):
    i = pl.program_id(0)
    e = pl.program_id(1)

    @pl.when(e == 0)
    def _():
        lo, hi = _unpack_bf16_pairs(hs_ref[...])
        lo_ref[...] = lo.astype(BF16)
        hi_ref[...] = hi.astype(BF16)
        acc_ref[...] = jnp.zeros_like(acc_ref)

    @pl.when(tv_ref[i] > 0)
    def _():
        lo, hi = lo_ref[...], hi_ref[...]
        wg = wg_ref[0].astype(BF16)
        wu = wu_ref[0].astype(BF16)
        gate = _dot(lo, wg[:HALF]) + _dot(hi, wg[HALF:])
        up = _dot(lo, wu[:HALF]) + _dot(hi, wu[HALF:])
        comb = cs_ref[...]
        lane = lax.broadcasted_iota(jnp.int32, comb.shape, 1)
        c = jnp.sum(jnp.where(lane == tg_ref[i] * EXPERTS_PER_GROUP + e, comb, 0.0), axis=-1, keepdims=True)
        acc_ref[...] += _dot((jax.nn.silu(gate) * up).astype(BF16), wd_ref[0].astype(BF16)) * c

    @pl.when(e == pl.num_programs(1) - 1)
    def _():
        ys_ref[...] = _pack_bf16_pairs(acc_ref[...].astype(BF16).astype(F32))


def _moe(tile_group, tile_valid, hs, cs, wg, wu, wd):
    n_sorted = hs.shape[0]
    tm = MOE_TM
    row = lambda w: pl.BlockSpec((tm, w), lambda i, e, tg, tv: (i, 0))
    wspec = lambda a, b: pl.BlockSpec((1, a, b), lambda i, e, tg, tv: (tg[i] * EXPERTS_PER_GROUP + e, 0, 0))
    return pl.pallas_call(
        _moe_kernel,
        grid_spec=pltpu.PrefetchScalarGridSpec(
            num_scalar_prefetch=2, grid=(n_sorted // tm, EXPERTS_PER_GROUP),
            in_specs=[row(HALF), row(LANE), wspec(D_MODEL, EXPERT_FF), wspec(D_MODEL, EXPERT_FF),
                      wspec(EXPERT_FF, D_MODEL)],
            out_specs=row(HALF),
            scratch_shapes=[pltpu.VMEM((tm, HALF), BF16), pltpu.VMEM((tm, HALF), BF16),
                            pltpu.VMEM((tm, D_MODEL), F32)]),
        out_shape=jax.ShapeDtypeStruct((n_sorted, HALF), jnp.uint32),
        compiler_params=_params("parallel", "arbitrary"),
        name="moe_experts",
    )(tile_group, tile_valid, hs, cs, wg, wu, wd)


def _unpermute_kernel(pos_ref, ys_ref, x1_ref, nw_ref, out_ref, stage_ref):
    i = pl.program_id(0)
    tm = x1_ref.shape[0]

    def body(r, _):
        stage_ref[pl.ds(r, 1), :] = ys_ref[pl.ds(pos_ref[i * tm + r], 1), :]
        return 0

    lax.fori_loop(0, tm, body, 0, unroll=8)
    lo, hi = _unpack_bf16_pairs(stage_ref[...])
    x2 = x1_ref[...] + jnp.concatenate([lo, hi], axis=1)
    out_ref[...] = x2 * lax.rsqrt(jnp.mean(x2 * x2, axis=-1, keepdims=True) + EPS) * nw_ref[...]


def _unpermute_residual_norm(pos, ys, x1, nw):
    S = x1.shape[0]
    tm = RANK_TM
    return pl.pallas_call(
        _unpermute_kernel,
        grid_spec=pltpu.PrefetchScalarGridSpec(
            num_scalar_prefetch=1, grid=(S // tm,),
            in_specs=[pl.BlockSpec(ys.shape, lambda i, p: (0, 0), pipeline_mode=pl.Buffered(1)),
                      pl.BlockSpec((tm, D_MODEL), lambda i, p: (i, 0)),
                      pl.BlockSpec((1, D_MODEL), lambda i, p: (0, 0))],
            out_specs=pl.BlockSpec((tm, D_MODEL), lambda i, p: (i, 0)),
            scratch_shapes=[pltpu.VMEM((tm, HALF), jnp.uint32)]),
        out_shape=jax.ShapeDtypeStruct((S, D_MODEL), F32),
        compiler_params=_params("parallel"),
        name="moe_unpermute_norm",
    )(pos, ys, x1, nw)


def _hierarchical_moe(hu, comb, cnt, x1, wg, wu, wd, final_norm_w):
    S = hu.shape[0]
    tm = MOE_TM
    n_sorted = S + N_GROUPS * tm
    rank = comb[:, RANK_LANE:RANK_LANE + 1]
    grp = comb[:, GRP_LANE].astype(jnp.int32)
    counts = cnt[0, :N_GROUPS].astype(jnp.int32)
    padded = (counts + tm - 1) // tm * tm
    ends = jnp.cumsum(padded)
    starts = ends - padded
    pos = rank[:, 0].astype(jnp.int32) + jnp.sum(
        jnp.where(grp[:, None] == jnp.arange(N_GROUPS)[None, :], starts[None, :], 0), axis=1)
    tile_start = jnp.arange(n_sorted // tm, dtype=jnp.int32) * tm
    tile_group = jnp.minimum(jnp.sum((tile_start[:, None] >= ends[None, :]).astype(jnp.int32), axis=1), N_GROUPS - 1)
    tile_valid = (tile_start < ends[N_GROUPS - 1]).astype(jnp.int32)
    hs, cs = _permute(pos, hu, comb, n_sorted)
    ys = _moe(tile_group, tile_valid, hs, cs, wg, wu, wd)
    return _unpermute_residual_norm(pos, ys, x1, final_norm_w)


def _positional_tables():
    half = RET_QK_DIM // 2
    inv = ROPE_BASE ** (-jnp.arange(half, dtype=F32) / half)
    inv2 = jnp.concatenate([inv, inv]).reshape(1, RET_QK_DIM)
    C, H = RET_CHUNK, RET_HEADS
    log_gamma = jnp.log(1.0 - 2.0 ** (-5.0 - jnp.arange(H, dtype=F32)))
    idx = jnp.arange(C, dtype=F32)
    diff = idx[:, None] - idx[None, :]
    dmask = jnp.where(diff >= 0, jnp.exp(jnp.maximum(diff, 0.0)[None] * log_gamma[:, None, None]), 0.0)
    qd = jnp.exp((idx + 1.0)[None, :] * log_gamma[:, None])
    kd = jnp.exp((C - 1.0 - idx)[None, :] * log_gamma[:, None])
    cd = jnp.exp(C * log_gamma)
    qd = jnp.broadcast_to(qd[:, :, None], (H, C, RET_V_DIM))
    kd = jnp.broadcast_to(kd[:, :, None], (H, C, RET_QK_DIM))
    cd = jnp.broadcast_to(cd[:, None, None], (H, 1, RET_V_DIM))
    return inv2, dmask, qd, kd, cd


def _attention_tables(rel_bias):
    r = np.arange(Q_BLOCK)[None, :]
    cc = np.arange(3 * CMP_TILE)[:, None] - CMP_TILE
    d = r - CMP_STRIDE * (cc - (CMP_TILE - CMP_TOK_PER_QB)) - (CMP_LEN - 1)
    d = np.where(cc < 0, -2, np.where((cc >= CMP_TILE) | (d < 0), -1, d))
    tct = _bias_table_t(rel_bias, d, shift=True)
    d = r + Q_BLOCK - np.arange(3 * Q_BLOCK)[:, None]
    tnt = _bias_table_t(rel_bias, np.where(d < 0, -1, d), shift=True)
    d = r + WINDOW - np.arange(WIN_KEYS + WINDOW)[:, None]
    twt = _bias_table_t(rel_bias, np.where((d < 0) | (d >= WINDOW), -1, d), shift=False)
    kk = np.arange(SEL_WIN * SEL_BLOCK)[:, None] // SEL_BLOCK
    et = jnp.where(jnp.asarray(kk == np.arange(SEL_WIN)[None, :]), -NEG, 0.0).astype(BF16)
    return tct, tnt, twt, et


def _layer(x2d, rel_bias, norm_mix, w_in, cmp_pos_k, cmp_k_w1, cmp_k_w2, cmp_pos_v, cmp_v_w1, cmp_v_w2,
           ret_norm, w_nsa_up, w_ret_up, w_out, norm_ffn, w_rg, b_rg, w_re, b_re, moe_wg, moe_wu, moe_wd,
           final_norm_w):
    S = x2d.shape[0]
    G, dh = NSA_KV_GROUPS, NSA_HEAD_DIM
    assert S % (CMP_TILE * CMP_STRIDE) == 0 and S // SEL_BLOCK >= SEL_TOP_N

    o = np.cumsum((0,) + (NSA_Q_W, 6 * NSA_KV_W, 3 * NSA_HEADS, RET_QK_W, RET_QK_W, RET_V_W, RET_V_W,
                          D_MODEL, D_MODEL))
    seg = lambda n: w_in[:, o[n]:o[n + 1]]
    w_cat = jnp.concatenate([seg(0), seg(1), seg(3), seg(4), seg(5), seg(6), seg(7), seg(8), seg(2),
                             jnp.zeros((D_MODEL, LANE - 3 * NSA_HEADS), F32)], axis=1).astype(BF16)
    inv2, dmask, qd, kd, cd = _positional_tables()

    (qa_t, kcmp, vcmp, ksel, vsel_t, kwin, vwin_t, ga_t, qr, kr, vr, gr, gta, gtr) = _input_projection(
        x2d, norm_mix.reshape(1, D_MODEL), w_cat, inv2)

    nch = S // CMP_STRIDE
    kc = _compress(kcmp, cmp_pos_k, cmp_k_w1, cmp_k_w2).astype(BF16)
    vct = _compress(vcmp, cmp_pos_v, cmp_v_w1, cmp_v_w2).T.reshape(G, dh, nch).astype(BF16)

    tct, tnt, twt, et = _attention_tables(rel_bias)
    ya = _nsa_attention(qa_t.reshape(NSA_HEADS, dh, S), ga_t, kc, vct, tct,
                        ksel, vsel_t.reshape(G, PV_ROWS, S), et, tnt,
                        kwin, vwin_t.reshape(G, PV_ROWS, S), twt)

    yr = _retention(qr, kr, vr, gr, ret_norm.reshape(1, RET_V_W), dmask, qd, kd, cd)

    wrt = jnp.concatenate([w_re, w_rg, jnp.zeros((D_MODEL, LANE - N_EXPERTS - N_GROUPS), F32)], axis=1).astype(BF16)
    brt = jnp.concatenate([b_re, b_rg, jnp.zeros((LANE - N_EXPERTS - N_GROUPS,), F32)]).reshape(1, LANE)
    x1, h2, comb, cnt = _merge(x2d, ya, yr, gta, gtr, w_nsa_up.astype(BF16), w_ret_up.astype(BF16),
                               w_out.astype(BF16), norm_ffn.reshape(1, D_MODEL), wrt, brt)
    return _hierarchical_moe(h2, comb, cnt, x1, moe_wg, moe_wu, moe_wd, final_norm_w)


def kernel(x, rel_bias, norm_mix, w_in, cmp_pos_k, cmp_k_w1, cmp_k_w2, cmp_pos_v, cmp_v_w1, cmp_v_w2, ret_norm, w_nsa_up, w_ret_up, w_out, norm_ffn, w_router_group, b_router_group, w_router_expert, b_router_expert, moe_w_gate, moe_w_up, moe_w_down, norm_final):
    B, S, D = x.shape
    assert B == 1 and norm_mix.shape[0] == 1, "single sequence, depth 1"
    out = _layer(x[0], rel_bias, norm_mix[0], w_in[0], cmp_pos_k[0], cmp_k_w1[0], cmp_k_w2[0],
                 cmp_pos_v[0], cmp_v_w1[0], cmp_v_w2[0], ret_norm[0], w_nsa_up[0], w_ret_up[0], w_out[0],
                 norm_ffn[0], w_router_group[0], b_router_group[0], w_router_expert[0], b_router_expert[0],
                 moe_w_gate[0], moe_w_up[0], moe_w_down[0], norm_final.reshape(1, D))
    return out[None]
```

```python
import functools
import math

import numpy as np
import jax
import jax.numpy as jnp
from jax import lax
from jax.experimental import pallas as pl
from jax.experimental.pallas import tpu as pltpu

F32 = jnp.float32
BF16 = jnp.bfloat16

D_MODEL = 1024
NSA_HEADS = 8
NSA_KV_GROUPS = 2
NSA_HPG = NSA_HEADS // NSA_KV_GROUPS
NSA_HEAD_DIM = 64
CMP_LEN = 32
CMP_STRIDE = 16
CMP_HIDDEN = 256
SEL_BLOCK = 64
SEL_TOP_N = 16
N_LOCAL_SEL = 2
WINDOW = 512
Q_BLOCK = 128
RET_HEADS = 4
RET_QK_DIM = 128
RET_V_DIM = 256
RET_CHUNK = 128
ROPE_BASE = 10000.0
REL_BUCKETS = 32
REL_MAX_DIST = 128
N_GROUPS = 4
EXPERTS_PER_GROUP = 8
N_EXPERTS = N_GROUPS * EXPERTS_PER_GROUP
EXPERT_FF = 256
EPS = 1e-6

NSA_Q_W = NSA_HEADS * NSA_HEAD_DIM
NSA_KV_W = NSA_KV_GROUPS * NSA_HEAD_DIM
RET_QK_W = RET_HEADS * RET_QK_DIM
RET_V_W = RET_HEADS * RET_V_DIM

LANE = 128
NEG = -1e30
VMEM_LIMIT = 56 * 1024 * 1024

CMP_TILE = 128
QB_PER_CMP_TILE = CMP_TILE * CMP_STRIDE // Q_BLOCK
CMP_TOK_PER_QB = Q_BLOCK // CMP_STRIDE
SCR_PAD = 8
SEL_TK = 512
SEL_WIN = 128
SEL_WIN_TILES = SEL_WIN * SEL_BLOCK // SEL_TK
PV_ROWS = 80
WIN_KEYS = WINDOW + Q_BLOCK
QW = NSA_HPG * Q_BLOCK


def _dot(a, b):
    return jnp.dot(a, b, preferred_element_type=F32)


def _dot_nt(a, b):
    return lax.dot_general(a, b, (((1,), (1,)), ((), ())), preferred_element_type=F32)


def _const_spec(shape):
    nd = len(shape)
    return pl.BlockSpec(shape, lambda *_: (0,) * nd, pipeline_mode=pl.Buffered(1))


def _params(*sem):
    return pltpu.CompilerParams(dimension_semantics=sem, vmem_limit_bytes=VMEM_LIMIT)


_C_QA = 0
_C_KV = _C_QA + NSA_Q_W
_C_QR = _C_KV + 6 * NSA_KV_W
_C_KR = _C_QR + RET_QK_W
_C_VR = _C_KR + RET_QK_W
_C_GR = _C_VR + RET_V_W
_C_GTA = _C_GR + RET_V_W
_C_GTR = _C_GTA + D_MODEL
_C_GA = _C_GTR + D_MODEL
_C_END = _C_GA + LANE
IN_TM = 512


def _inproj_kernel(x_ref, nw_ref, w_ref, inv_ref,
                   qat_ref, kcmp_ref, vcmp_ref, ksel_ref, vselt_ref, kwin_ref, vwint_ref, gat_ref,
                   qr_ref, kr_ref, vr_ref, gr_ref, gta_ref, gtr_ref):
    tm = x_ref.shape[0]
    x = x_ref[...]
    h = x * lax.rsqrt(jnp.mean(x * x, axis=-1, keepdims=True) + EPS) * nw_ref[...]
    hb = h.astype(BF16)

    def proj(a, width):
        return _dot(hb, w_ref[:, a:a + width])

    def values_t(v):
        vt = v.T.astype(BF16)
        aux = jnp.where(lax.broadcasted_iota(jnp.int32, (PV_ROWS - NSA_HEAD_DIM, tm), 0) == 0, 1.0, 0.0).astype(BF16)
        parts = []
        for g in range(NSA_KV_GROUPS):
            parts += [vt[g * NSA_HEAD_DIM:(g + 1) * NSA_HEAD_DIM], aux]
        return jnp.concatenate(parts, axis=0)

    qat_ref[...] = (proj(_C_QA, NSA_Q_W) * (NSA_HEAD_DIM ** -0.5)).T.astype(BF16)
    kv = proj(_C_KV, 6 * NSA_KV_W)
    kcmp_ref[...] = kv[:, 0 * LANE:1 * LANE]
    vcmp_ref[...] = kv[:, 1 * LANE:2 * LANE]
    ksel_ref[...] = kv[:, 2 * LANE:3 * LANE].astype(BF16)
    vselt_ref[...] = values_t(kv[:, 3 * LANE:4 * LANE])
    kwin_ref[...] = kv[:, 4 * LANE:5 * LANE].astype(BF16)
    vwint_ref[...] = values_t(kv[:, 5 * LANE:6 * LANE])
    gat_ref[...] = jax.nn.sigmoid(proj(_C_GA, LANE)).T

    pos = (pl.program_id(0) * tm + lax.broadcasted_iota(jnp.int32, (tm, 1), 0)).astype(F32)
    ang = pos * inv_ref[...]
    first_half = lax.broadcasted_iota(jnp.int32, (tm, RET_QK_DIM), 1) < RET_QK_DIM // 2
    cos2 = jnp.cos(ang)
    sin2 = jnp.where(first_half, -jnp.sin(ang), jnp.sin(ang))
    qr = proj(_C_QR, RET_QK_W)
    kr = proj(_C_KR, RET_QK_W)
    for hd in range(RET_HEADS):
        sl = slice(hd * RET_QK_DIM, (hd + 1) * RET_QK_DIM)
        qh = qr[:, sl]
        kh = kr[:, sl]
        qr_ref[:, sl] = (qh * cos2 + pltpu.roll(qh, RET_QK_DIM // 2, 1) * sin2).astype(BF16)
        kr_ref[:, sl] = (kh * cos2 + pltpu.roll(kh, RET_QK_DIM // 2, 1) * sin2) * (RET_QK_DIM ** -0.5)
    vr_ref[...] = proj(_C_VR, RET_V_W).astype(BF16)
    gr_ref[...] = proj(_C_GR, RET_V_W)
    gta_ref[...] = jax.nn.sigmoid(proj(_C_GTA, D_MODEL))
    gtr_ref[...] = jax.nn.sigmoid(proj(_C_GTR, D_MODEL))


def _input_projection(x2d, norm_w, w_cat, inv2):
    S = x2d.shape[0]
    tm = IN_TM
    vt_rows = NSA_KV_GROUPS * PV_ROWS
    outs = [
        (NSA_Q_W, BF16, True),
        (LANE, F32, False), (LANE, F32, False),
        (LANE, BF16, False), (vt_rows, BF16, True),
        (LANE, BF16, False), (vt_rows, BF16, True),
        (LANE, F32, True),
        (RET_QK_W, BF16, False), (RET_QK_W, F32, False),
        (RET_V_W, BF16, False), (RET_V_W, F32, False),
        (D_MODEL, F32, False), (D_MODEL, F32, False),
    ]
    row = lambda w: pl.BlockSpec((tm, w), lambda i: (i, 0))
    col = lambda w: pl.BlockSpec((w, tm), lambda i: (0, i))
    return pl.pallas_call(
        _inproj_kernel,
        grid=(S // tm,),
        in_specs=[row(D_MODEL), _const_spec((1, D_MODEL)), _const_spec((D_MODEL, _C_END)),
                  _const_spec((1, RET_QK_DIM))],
        out_specs=[col(w) if t else row(w) for w, _, t in outs],
        out_shape=[jax.ShapeDtypeStruct((w, S) if t else (S, w), dt) for w, dt, t in outs],
        compiler_params=_params("parallel"),
        name="in_proj",
    )(x2d, norm_w, w_cat, inv2)


def _compress_kernel(x_ref, pos_ref, w1_ref, w2_ref, out_ref, hid_ref):
    n_tok = out_ref.shape[0]
    hid_ref[...] = jnp.zeros_like(hid_ref)

    def body(l, _):
        rows = x_ref[pl.ds(l, n_tok, stride=CMP_STRIDE), :] + pos_ref[pl.ds(l, 1), :]
        hid_ref[...] += _dot(rows.astype(BF16), w1_ref[l])
        return 0

    lax.fori_loop(0, CMP_LEN, body, 0)
    out_ref[...] = _dot(jax.nn.gelu(hid_ref[...]).astype(BF16), w2_ref[...])


def _compress(x, pos, w1, w2):
    S, width = x.shape
    n_tok = S // CMP_STRIDE
    G = NSA_KV_GROUPS
    def block_diag(w):
        w = w.astype(BF16)
        z = jnp.zeros_like(w)
        return jnp.concatenate([jnp.concatenate([w if k == g else z for k in range(G)], axis=-1)
                                for g in range(G)], axis=-2)

    w1_bd = block_diag(w1.reshape(CMP_LEN, NSA_HEAD_DIM, CMP_HIDDEN))
    w2_bd = block_diag(w2)
    x_pad = jnp.pad(x, ((0, CMP_LEN - CMP_STRIDE), (0, 0)))
    full = lambda a: _const_spec(a.shape)
    pos2 = jnp.concatenate([pos] * G, axis=1)
    return pl.pallas_call(
        _compress_kernel,
        grid=(1,),
        in_specs=[full(x_pad), full(pos2), full(w1_bd), full(w2_bd)],
        out_specs=pl.BlockSpec((n_tok, width), lambda i: (0, 0)),
        out_shape=jax.ShapeDtypeStruct((n_tok, width), F32),
        scratch_shapes=[pltpu.VMEM((n_tok, G * CMP_HIDDEN), F32)],
        compiler_params=_params("arbitrary"),
        name="nsa_compress",
    )(x_pad, pos2, w1_bd, w2_bd)


def _t5_bucket_np(dist):
    dist = np.maximum(dist, 0)
    max_exact = REL_BUCKETS // 2
    d_f = np.maximum(dist, 1).astype(np.float32)
    large = max_exact + (np.log(d_f / np.float32(max_exact)) / np.float32(math.log(REL_MAX_DIST / max_exact))
                         * np.float32(REL_BUCKETS - max_exact)).astype(np.int32)
    large = np.minimum(large, REL_BUCKETS - 1)
    return np.where(dist < max_exact, dist, large)


def _bias_table_t(rel_bias, dist, shift):
    bucket = jnp.asarray(np.where(dist >= 0, _t5_bucket_np(dist), -1), jnp.int32)
    rb = rel_bias - rel_bias[REL_BUCKETS - 1][None, :] if shift else rel_bias
    onehot = (bucket[None] == jnp.arange(REL_BUCKETS, dtype=jnp.int32)[:, None, None]).astype(F32)
    tab = jnp.einsum('bh,bkq->hkq', rb, onehot, precision=lax.Precision.HIGHEST)
    tab = jnp.where(jnp.asarray(dist == -1)[None], NEG, tab)
    K, Q = dist.shape
    return tab.reshape(NSA_KV_GROUPS, NSA_HPG, K, Q).transpose(0, 2, 1, 3).reshape(NSA_KV_GROUPS, K, NSA_HPG * Q)


def _nsa_kernel(n_keys, q_ref, ga_ref, kc_ref, vct_ref, tct_ref, ksel_ref, vselt_ref, et_ref, tnt_ref,
                kwin_ref, vwint_ref, twt_ref, out_ref, sc_ref, ps_ref, sa_ref, sb_ref):
    i = pl.program_id(0)
    dh = NSA_HEAD_DIM
    nb = (ps_ref.shape[1] - 2 * SCR_PAD) // (SEL_BLOCK // CMP_STRIDE)

    @pl.when(i == 0)
    def _():
        sc_ref[...] = jnp.zeros_like(sc_ref)
        ps_ref[...] = jnp.zeros_like(ps_ref)

    def colmax(x):
        return jnp.max(x, axis=0, keepdims=True)

    def tile4(x):
        return jnp.concatenate([x] * NSA_HPG, axis=1)

    gates = ga_ref[...]
    tl = i // QB_PER_CMP_TILE
    sub = i % QB_PER_CMP_TILE
    tp = tl - 1
    groups = range(NSA_KV_GROUPS)
    q_t_g = [jnp.concatenate([q_ref[NSA_HPG * g + j] for j in range(NSA_HPG)], axis=1) for g in groups]
    zeros = jnp.zeros((dh, QW), BF16)
    wq_g = [jnp.concatenate([q_t_g[g] if k == g else zeros for k in groups], axis=0) for g in groups]

    def cmp_scores(g, t):
        return _dot(kc_ref[pl.ds(pl.multiple_of(t * CMP_TILE, CMP_TILE), CMP_TILE), :], wq_g[g])

    def sc_rows(t):
        return pl.ds(pl.multiple_of(SCR_PAD + t * CMP_TILE, 8), CMP_TILE)

    def cmp_far(t, m):
        out = []
        for g in groups:
            s = cmp_scores(g, t)
            sc_ref[g, sc_rows(t), :] = s
            out.append(jnp.maximum(m[g], colmax(s)))
        return tuple(out)

    m_c = lax.fori_loop(0, jnp.maximum(tp, 0), cmp_far, tuple(jnp.full((1, QW), NEG, F32) for _ in groups))
    tpc = jnp.maximum(tp, 0)
    off = CMP_TILE - CMP_TOK_PER_QB - CMP_TOK_PER_QB * sub
    m_c = list(m_c)
    for g in groups:
        s = (cmp_scores(g, tpc) + tct_ref[g, pl.ds(pl.multiple_of(off, 8), CMP_TILE), :]
             + jnp.where(tp < 0, NEG, 0.0))
        sc_ref[g, sc_rows(tpc), :] = s
        m_c[g] = jnp.maximum(m_c[g], colmax(s))
    for g in groups:
        s = cmp_scores(g, tl) + tct_ref[g, pl.ds(pl.multiple_of(off + CMP_TILE, 8), CMP_TILE), :]
        sc_ref[g, sc_rows(tl), :] = s
        m_c[g] = jnp.maximum(m_c[g], colmax(s))

    def cmp_exp(t, carry):
        out = []
        for g in groups:
            l, acc = carry[g]
            e = jnp.exp(sc_ref[g, sc_rows(t), :] - m_c[g])
            sc_ref[g, sc_rows(t), :] = e
            vt = vct_ref[g, :, pl.ds(pl.multiple_of(t * CMP_TILE, CMP_TILE), CMP_TILE)]
            out.append((l + jnp.sum(e, axis=0, keepdims=True), acc + _dot(vt, e.astype(BF16))))
        return tuple(out)

    cmp_out = lax.fori_loop(0, tl + 1, cmp_exp,
                            tuple((jnp.zeros((1, QW), F32), jnp.zeros((dh, QW), F32)) for _ in groups))
    inv_c = [jnp.where(m_c[g] > 0.5 * NEG, 1.0 / jnp.maximum(cmp_out[g][0], 1e-30), 0.0) for g in groups]
    o_c_g = [cmp_out[g][1] * inv_c[g] for g in groups]

    def cmp_psum(t, _):
        for g in groups:
            p = sc_ref[g, sc_rows(t), :] * inv_c[g]
            ps = p[:, 0:Q_BLOCK]
            for j in range(1, NSA_HPG):
                ps = ps + p[:, j * Q_BLOCK:(j + 1) * Q_BLOCK]
            ps_ref[g, sc_rows(t), :] = ps.astype(BF16).astype(F32)
        return 0

    lax.fori_loop(0, tl + 1, cmp_psum, 0)
    per_blk = SEL_BLOCK // CMP_STRIDE
    imp_g = []
    for g in groups:
        imp = ps_ref[g, pl.ds(SCR_PAD - 1, nb, stride=per_blk), :]
        for k in range(1, CMP_LEN // CMP_STRIDE + per_blk - 1):
            imp = imp + ps_ref[g, pl.ds(SCR_PAD - 1 + k, nb, stride=per_blk), :]
        imp_g.append(imp)

    kw0 = pl.multiple_of(jnp.maximum(i * Q_BLOCK - WINDOW, 0), Q_BLOCK)
    tw0 = pl.multiple_of(jnp.maximum(WINDOW - i * Q_BLOCK, 0), Q_BLOCK)
    o_w_g = []
    for g in groups:
        s = _dot(kwin_ref[pl.ds(kw0, WIN_KEYS), :], wq_g[g]) + twt_ref[g, pl.ds(tw0, WIN_KEYS), :]
        e = jnp.exp(s - colmax(s)).astype(BF16)
        acc_w = _dot(vwint_ref[g, :, pl.ds(kw0, WIN_KEYS)], e)
        o_w_g.append(acc_w[0:dh] * (1.0 / jnp.maximum(acc_w[dh:dh + 1], 1e-30)))

    n_io = lax.broadcasted_iota(jnp.int32, (nb, Q_BLOCK), 0)
    r_io = lax.broadcasted_iota(jnp.int32, (nb, Q_BLOCK), 1)
    back = (i * Q_BLOCK + r_io) // SEL_BLOCK - n_io
    valid = back >= 0
    forced = (n_io == 0) | (valid & (back < N_LOCAL_SEL))
    work_g = [jnp.where(forced, -jnp.inf, jnp.where(valid, imp_g[g], -1e9)) for g in groups]
    n_f = n_io.astype(F32)
    for _ in range(SEL_TOP_N - 1 - N_LOCAL_SEL):
        for g in groups:
            mx = colmax(work_g[g])
            first = jnp.min(jnp.where(work_g[g] == mx, n_f, float(nb)), axis=0, keepdims=True)
            work_g[g] = jnp.where(n_f == first, -jnp.inf, work_g[g])
    sel_g = [jnp.where(work_g[g] == -jnp.inf, 1.0, 0.0) for g in groups]

    far_blocks = (i - 1) * (Q_BLOCK // SEL_BLOCK)
    selm_g = [(jnp.where(n_io < far_blocks, sel_g[g], 0.0) - 1.0).astype(BF16) for g in groups]

    far_end = (i - 1) * Q_BLOCK
    n_far = (jnp.maximum(far_end, 0) + SEL_TK - 1) // SEL_TK

    def flash_step(carry, s, vt):
        m_i, acc = carry
        m_new = jnp.maximum(m_i, colmax(s))
        alpha = jnp.exp(m_i - m_new)
        p = jnp.exp(s - m_new).astype(BF16)
        return m_new, alpha * acc + _dot(vt, p)

    carry = tuple((jnp.full((1, QW), NEG, F32), jnp.zeros((PV_ROWS, QW), F32)) for _ in groups)
    for w in range(nb // SEL_WIN):
        w_far = [jnp.concatenate([tile4(selm_g[g][w * SEL_WIN:(w + 1) * SEL_WIN]), wq_g[g]], axis=0)
                 for g in groups]

        lo = w * SEL_WIN_TILES
        last = min((w + 1) * SEL_WIN_TILES, n_keys // SEL_TK) - 1
        trips = (jnp.minimum(n_far, last + 1) - lo + 1) // 2

        def far_scores(kt, s_ref, g, w=w, w_far=w_far):
            k0 = pl.multiple_of(kt * SEL_TK, SEL_TK)
            e0 = pl.multiple_of((kt - w * SEL_WIN_TILES) * SEL_TK, SEL_TK)
            lhs = jnp.concatenate([et_ref[pl.ds(e0, SEL_TK), :],
                                   ksel_ref[pl.ds(k0, SEL_TK), :]], axis=1)
            s_ref[g] = _dot(lhs, w_far[g])

        def far_update(kt, s_ref, g, carry_g):
            k0 = pl.multiple_of(kt * SEL_TK, SEL_TK)
            return flash_step(carry_g, s_ref[g], vselt_ref[g, :, pl.ds(k0, SEL_TK)])

        def far_body(u, carry, lo=lo, last=last, far_scores=far_scores):
            kt = lo + 2 * u
            carry = list(carry)
            for g in groups:
                far_scores(kt + 1, sb_ref, g)
                carry[g] = far_update(kt, sa_ref, g, carry[g])
            for g in groups:
                far_scores(jnp.minimum(kt + 2, last), sa_ref, g)
                carry[g] = far_update(kt + 1, sb_ref, g, carry[g])
            return tuple(carry)

        @pl.when(trips > 0)
        def _(lo=lo, far_scores=far_scores):
            for g in groups:
                far_scores(jnp.int32(lo), sa_ref, g)

        carry = lax.fori_loop(0, trips, far_body, carry)

    heads_out = []
    for g in groups:
        wq, o_c, sel = wq_g[g], o_c_g[g], sel_g[g]
        k0n = pl.multiple_of(jnp.maximum(far_end, 0), Q_BLOCK)
        t0n = pl.multiple_of(jnp.maximum(-far_end, 0), Q_BLOCK)
        near_blocks = 2 * Q_BLOCK // SEL_BLOCK
        pb = lax.broadcasted_iota(jnp.int32, (16, nb), 0)
        pn = lax.broadcasted_iota(jnp.int32, (16, nb), 1)
        pick_rows = jnp.where((pn == jnp.maximum(far_blocks, 0) + pb) & (pb < near_blocks), 1.0, 0.0).astype(BF16)
        sel_rows = _dot(pick_rows, sel.astype(BF16))
        add_near = jnp.concatenate(
            [jnp.broadcast_to((sel_rows[b:b + 1, :] - 1.0) * (-NEG), (SEL_BLOCK, Q_BLOCK)) for b in range(near_blocks)],
            axis=0)
        s = (_dot(ksel_ref[pl.ds(k0n, 2 * Q_BLOCK), :], wq) + tnt_ref[g, pl.ds(t0n, 2 * Q_BLOCK), :]
             + tile4(add_near))
        m_s, acc_s = flash_step(carry[g], s, vselt_ref[g, :, pl.ds(k0n, 2 * Q_BLOCK)])
        o_s = acc_s[0:dh] * (1.0 / jnp.maximum(acc_s[dh:dh + 1], 1e-30))

        o_w = o_w_g[g]

        for j in range(NSA_HPG):
            hd = NSA_HPG * g + j
            cols = slice(j * Q_BLOCK, (j + 1) * Q_BLOCK)
            heads_out.append(gates[hd:hd + 1] * o_c[:, cols]
                             + gates[NSA_HEADS + hd:NSA_HEADS + hd + 1] * o_s[:, cols]
                             + gates[2 * NSA_HEADS + hd:2 * NSA_HEADS + hd + 1] * o_w[:, cols])

    out_ref[...] = jnp.concatenate(heads_out, axis=0).T.astype(out_ref.dtype)


def _nsa_attention(q_t, gates_t, kc, vct, tct, ksel, vselt, et, tnt, kwin, vwint, twt):
    H, dh, S = q_t.shape
    n_qb = S // Q_BLOCK
    n_tok = S // CMP_STRIDE
    nb = -(-(S // SEL_BLOCK) // SEL_WIN) * SEL_WIN
    ps_rows = nb * (SEL_BLOCK // CMP_STRIDE) + 2 * SCR_PAD
    full = lambda a: _const_spec(a.shape)
    return pl.pallas_call(
        functools.partial(_nsa_kernel, S),
        grid=(n_qb,),
        in_specs=[pl.BlockSpec((H, dh, Q_BLOCK), lambda i: (0, 0, i)),
                  pl.BlockSpec((gates_t.shape[0], Q_BLOCK), lambda i: (0, i)),
                  full(kc), full(vct), full(tct), full(ksel), full(vselt), full(et), full(tnt),
                  full(kwin), full(vwint), full(twt)],
        out_specs=pl.BlockSpec((Q_BLOCK, H * dh), lambda i: (i, 0)),
        out_shape=jax.ShapeDtypeStruct((S, H * dh), BF16),
        scratch_shapes=[pltpu.VMEM((NSA_KV_GROUPS, SCR_PAD + n_tok, QW), F32),
                        pltpu.VMEM((NSA_KV_GROUPS, ps_rows, Q_BLOCK), F32),
                        pltpu.VMEM((NSA_KV_GROUPS, SEL_TK, QW), F32),
                        pltpu.VMEM((NSA_KV_GROUPS, SEL_TK, QW), F32)],
        compiler_params=_params("arbitrary"),
        name="nsa_attention",
    )(q_t, gates_t, kc, vct, tct, ksel, vselt, et, tnt, kwin, vwint, twt)


def _retention_kernel(q_ref, k_ref, v_ref, g_ref, nw_ref, dmask_ref, qd_ref, kd_ref, cd_ref, out_ref, state_ref):
    @pl.when(pl.program_id(0) == 0)
    def _():
        state_ref[...] = jnp.zeros_like(state_ref)

    states = [state_ref[hd] for hd in range(RET_HEADS)]
    for c in range(q_ref.shape[0] // RET_CHUNK):
        rows = slice(c * RET_CHUNK, (c + 1) * RET_CHUNK)
        for hd in range(RET_HEADS):
            ks = slice(hd * RET_QK_DIM, (hd + 1) * RET_QK_DIM)
            vs = slice(hd * RET_V_DIM, (hd + 1) * RET_V_DIM)
            q = q_ref[rows, ks]
            k = k_ref[rows, ks]
            v = v_ref[rows, vs]
            state = states[hd]
            inner = _dot_nt(q, k.astype(BF16)) * dmask_ref[hd]
            y = _dot(inner.astype(BF16), v) + _dot(q, state.astype(BF16)) * qd_ref[hd]
            kd = k * kd_ref[hd]
            states[hd] = state * cd_ref[hd] + _dot(kd.T.astype(BF16), v)
            mu = jnp.mean(y, axis=-1, keepdims=True)
            yc = y - mu
            var = jnp.mean(yc * yc, axis=-1, keepdims=True)
            yn = yc * lax.rsqrt(var + EPS) * nw_ref[:, vs]
            out_ref[rows, vs] = (jax.nn.silu(g_ref[rows, vs]) * yn).astype(out_ref.dtype)
    for hd in range(RET_HEADS):
        state_ref[hd] = states[hd]


RET_STEP_CHUNKS = 8


def _retention(q, k, v, g, norm_w, dmask, qd, kd, cd):
    S = q.shape[0]
    C = RET_CHUNK * RET_STEP_CHUNKS
    row = lambda w: pl.BlockSpec((C, w), lambda i: (i, 0))
    full = lambda a: _const_spec(a.shape)
    return pl.pallas_call(
        _retention_kernel,
        grid=(S // C,),
        in_specs=[row(RET_QK_W), row(RET_QK_W), row(RET_V_W), row(RET_V_W),
                  full(norm_w), full(dmask), full(qd), full(kd), full(cd)],
        out_specs=row(RET_V_W),
        out_shape=jax.ShapeDtypeStruct((S, RET_V_W), BF16),
        scratch_shapes=[pltpu.VMEM((RET_HEADS, RET_QK_DIM, RET_V_DIM), F32)],
        compiler_params=_params("arbitrary"),
        name="retention",
    )(q, k, v, g, norm_w, dmask, qd, kd, cd)


MERGE_TM = 512
MERGE_SUB_ROWS = 256
GRP_LANE = 40
RANK_LANE = 41


def _merge_kernel(x_ref, ya_ref, yr_ref, gta_ref, gtr_ref, wa_ref, wr_ref, wo_ref, nw_ref, wrt_ref, brt_ref,
                  tri_ref, x1_ref, h_ref, comb_ref, cnt_ref, carry_ref):
    @pl.when(pl.program_id(0) == 0)
    def _():
        carry_ref[...] = jnp.zeros_like(carry_ref)

    sub = tri_ref.shape[0]
    carry = carry_ref[...]
    for r in range(x_ref.shape[0] // sub):
        carry = _merge_rows(slice(r * sub, (r + 1) * sub), carry, x_ref, ya_ref, yr_ref, gta_ref, gtr_ref,
                            wa_ref, wr_ref, wo_ref, nw_ref, wrt_ref, brt_ref, tri_ref, x1_ref, h_ref, comb_ref)
    carry_ref[...] = carry
    cnt_ref[...] = jnp.broadcast_to(carry, cnt_ref.shape)


def _merge_rows(rows, carry, x_ref, ya_ref, yr_ref, gta_ref, gtr_ref, wa_ref, wr_ref, wo_ref, nw_ref, wrt_ref,
                brt_ref, tri_ref, x1_ref, h_ref, comb_ref):
    merged = gta_ref[rows] * _dot(ya_ref[rows], wa_ref[...]) + gtr_ref[rows] * _dot(yr_ref[rows], wr_ref[...])
    x1 = x_ref[rows] + _dot(merged.astype(BF16), wo_ref[...])
    x1_ref[rows] = x1
    h = (x1 * lax.rsqrt(jnp.mean(x1 * x1, axis=-1, keepdims=True) + EPS) * nw_ref[...]).astype(BF16)
    h_ref[rows] = _pack_bf16_pairs(h.astype(F32))
    logits = _dot(h, wrt_ref[...]) + brt_ref[...]
    lane = lax.broadcasted_iota(jnp.int32, logits.shape, 1)
    lane_f = lane.astype(F32)
    big = float(LANE)
    is_grp = (lane >= N_EXPERTS) & (lane < N_EXPERTS + N_GROUPS)
    lg = jnp.where(is_grp, logits, -jnp.inf)
    eg = jnp.where(is_grp, jnp.exp(lg - jnp.max(lg, axis=-1, keepdims=True)), 0.0)
    pg = eg / jnp.sum(eg, axis=-1, keepdims=True)
    p_grp = jnp.max(pg, axis=-1, keepdims=True)
    grp = jnp.min(jnp.where(is_grp & (pg == p_grp), lane_f, big), axis=-1, keepdims=True) - N_EXPERTS
    in_grp = (lane_f >= grp * EXPERTS_PER_GROUP) & (lane_f < (grp + 1.0) * EXPERTS_PER_GROUP)
    le = jnp.where(in_grp, logits, -jnp.inf)
    m1 = jnp.max(le, axis=-1, keepdims=True)
    i1 = jnp.min(jnp.where(le == m1, lane_f, big), axis=-1, keepdims=True)
    le2 = jnp.where(lane_f == i1, -jnp.inf, le)
    m2 = jnp.max(le2, axis=-1, keepdims=True)
    i2 = jnp.min(jnp.where(le2 == m2, lane_f, big), axis=-1, keepdims=True)
    e2 = jnp.exp(m2 - m1)
    den = 1.0 + e2
    onehot = jnp.where(lane_f == grp, 1.0, 0.0)
    before = _dot(tri_ref[...], onehot.astype(BF16)) + carry
    rank = jnp.sum(onehot * before, axis=-1, keepdims=True)
    comb_ref[rows] = (jnp.where(lane_f == i1, (1.0 / den) * p_grp, 0.0)
                      + jnp.where(lane_f == i2, (e2 / den) * p_grp, 0.0)
                      + jnp.where(lane == GRP_LANE, grp, 0.0)
                      + jnp.where(lane == RANK_LANE, rank, 0.0))
    return carry + jnp.sum(onehot, axis=0, keepdims=True)


def _pack_bf16_pairs(x):
    n = x.shape[1] // 2
    return pltpu.pack_elementwise([x[:, :n], x[:, n:]], packed_dtype=BF16)


def _unpack_bf16_pairs(w):
    return tuple(pltpu.unpack_elementwise(w, index=k, packed_dtype=BF16, unpacked_dtype=F32) for k in range(2))


def _merge(x2d, ya, yr, gta, gtr, wa, wr, wo, nw, wrt, brt):
    S = x2d.shape[0]
    tm = MERGE_TM
    row = lambda w: pl.BlockSpec((tm, w), lambda i: (i, 0))
    full = lambda a: _const_spec(a.shape)
    sub = MERGE_SUB_ROWS
    tri = jnp.asarray(np.tril(np.ones((sub, sub), np.float32), -1), BF16)
    return pl.pallas_call(
        _merge_kernel,
        grid=(S // tm,),
        in_specs=[row(D_MODEL), row(NSA_Q_W), row(RET_V_W), row(D_MODEL), row(D_MODEL),
                  full(wa), full(wr), full(wo), full(nw), full(wrt), full(brt), full(tri)],
        out_specs=[row(D_MODEL), row(D_MODEL // 2), row(LANE), pl.BlockSpec((8, LANE), lambda i: (0, 0))],
        out_shape=[jax.ShapeDtypeStruct((S, D_MODEL), F32), jax.ShapeDtypeStruct((S, D_MODEL // 2), jnp.uint32),
                   jax.ShapeDtypeStruct((S, LANE), F32), jax.ShapeDtypeStruct((8, LANE), F32)],
        scratch_shapes=[pltpu.VMEM((1, LANE), F32)],
        compiler_params=_params("arbitrary"),
        name="merge_router",
    )(x2d, ya, yr, gta, gtr, wa, wr, wo, nw, wrt, brt, tri)


MOE_TM = 1024
MOE_SUB_TILES = 2
RANK_TM = 512
HALF = D_MODEL // 2


def _permute_kernel(pos_ref, h_ref, comb_ref, hs_ref, cs_ref):
    i = pl.program_id(0)

    @pl.when(i == 0)
    def _():
        hs_ref[...] = jnp.zeros_like(hs_ref)
        cs_ref[...] = jnp.zeros_like(cs_ref)

    tm = h_ref.shape[0]

    def body(r, _):
        d = pos_ref[i * tm + r]
        hs_ref[pl.ds(d, 1), :] = h_ref[pl.ds(r, 1), :]
        cs_ref[pl.ds(d, 1), :] = comb_ref[pl.ds(r, 1), :]
        return 0

    lax.fori_loop(0, tm, body, 0, unroll=8)


def _permute(pos, hu, comb, n_sorted):
    S = hu.shape[0]
    tm = RANK_TM
    resident = lambda w: pl.BlockSpec((n_sorted, w), lambda i, p: (0, 0), pipeline_mode=pl.Buffered(1))
    return pl.pallas_call(
        _permute_kernel,
        grid_spec=pltpu.PrefetchScalarGridSpec(
            num_scalar_prefetch=1, grid=(S // tm,),
            in_specs=[pl.BlockSpec((tm, HALF), lambda i, p: (i, 0)), pl.BlockSpec((tm, LANE), lambda i, p: (i, 0))],
            out_specs=[resident(HALF), resident(LANE)]),
        out_shape=[jax.ShapeDtypeStruct((n_sorted, HALF), jnp.uint32), jax.ShapeDtypeStruct((n_sorted, LANE), F32)],
        compiler_params=_params("arbitrary"),
        name="moe_permute",
    )(pos, hu, comb)


def _moe_kernel(tg_ref, tv_ref, hs_ref, cs_ref, wg_ref, wu_ref, wd_ref, ys_ref, lo_ref, hi_ref, acc_ref):
    i = pl.program_id(0)
    e = pl.program_id(1)

    @pl.when(e == 0)
    def _():
        lo, hi = _unpack_bf16_pairs(hs_ref[...])
        lo_ref[...] = lo.astype(BF16)
        hi_ref[...] = hi.astype(BF16)
        acc_ref[...] = jnp.zeros_like(acc_ref)

    @pl.when(tv_ref[i] > 0)
    def _():
        wg = wg_ref[0].astype(BF16)
        wu = wu_ref[0].astype(BF16)
        wd = wd_ref[0].astype(BF16)
        sub = lo_ref.shape[0] // MOE_SUB_TILES
        for r in range(MOE_SUB_TILES):
            rows = slice(r * sub, (r + 1) * sub)
            lo, hi = lo_ref[rows], hi_ref[rows]
            gate = _dot(lo, wg[:HALF]) + _dot(hi, wg[HALF:])
            up = _dot(lo, wu[:HALF]) + _dot(hi, wu[HALF:])
            comb = cs_ref[rows]
            lane = lax.broadcasted_iota(jnp.int32, comb.shape, 1)
            c = jnp.sum(jnp.where(lane == tg_ref[i] * EXPERTS_PER_GROUP + e, comb, 0.0), axis=-1, keepdims=True)
            acc_ref[rows] += _dot((jax.nn.silu(gate) * up).astype(BF16), wd) * c

    @pl.when(e == pl.num_programs(1) - 1)
    def _():
        ys_ref[...] = _pack_bf16_pairs(acc_ref[...].astype(BF16).astype(F32))


def _moe(tile_group, tile_valid, hs, cs, wg, wu, wd):
    n_sorted = hs.shape[0]
    tm = MOE_TM
    row = lambda w: pl.BlockSpec((tm, w), lambda i, e, tg, tv: (i, 0))
    wspec = lambda a, b: pl.BlockSpec((1, a, b), lambda i, e, tg, tv: (tg[i] * EXPERTS_PER_GROUP + e, 0, 0))
    return pl.pallas_call(
        _moe_kernel,
        grid_spec=pltpu.PrefetchScalarGridSpec(
            num_scalar_prefetch=2, grid=(n_sorted // tm, EXPERTS_PER_GROUP),
            in_specs=[row(HALF), row(LANE), wspec(D_MODEL, EXPERT_FF), wspec(D_MODEL, EXPERT_FF),
                      wspec(EXPERT_FF, D_MODEL)],
            out_specs=row(HALF),
            scratch_shapes=[pltpu.VMEM((tm, HALF), BF16), pltpu.VMEM((tm, HALF), BF16),
                            pltpu.VMEM((tm, D_MODEL), F32)]),
        out_shape=jax.ShapeDtypeStruct((n_sorted, HALF), jnp.uint32),
        compiler_params=_params("parallel", "arbitrary"),
        name="moe_experts",
    )(tile_group, tile_valid, hs, cs, wg, wu, wd)


def _unpermute_kernel(pos_ref, ys_ref, x1_ref, nw_ref, out_ref, stage_ref):
    i = pl.program_id(0)
    tm = x1_ref.shape[0]

    def body(r, _):
        stage_ref[pl.ds(r, 1), :] = ys_ref[pl.ds(pos_ref[i * tm + r], 1), :]
        return 0

    lax.fori_loop(0, tm, body, 0, unroll=8)
    lo, hi = _unpack_bf16_pairs(stage_ref[...])
    x2 = x1_ref[...] + jnp.concatenate([lo, hi], axis=1)
    out_ref[...] = x2 * lax.rsqrt(jnp.mean(x2 * x2, axis=-1, keepdims=True) + EPS) * nw_ref[...]


def _unpermute_residual_norm(pos, ys, x1, nw):
    S = x1.shape[0]
    tm = RANK_TM
    return pl.pallas_call(
        _unpermute_kernel,
        grid_spec=pltpu.PrefetchScalarGridSpec(
            num_scalar_prefetch=1, grid=(S // tm,),
            in_specs=[pl.BlockSpec(ys.shape, lambda i, p: (0, 0), pipeline_mode=pl.Buffered(1)),
                      pl.BlockSpec((tm, D_MODEL), lambda i, p: (i, 0)),
                      pl.BlockSpec((1, D_MODEL), lambda i, p: (0, 0))],
            out_specs=pl.BlockSpec((tm, D_MODEL), lambda i, p: (i, 0)),
            scratch_shapes=[pltpu.VMEM((tm, HALF), jnp.uint32)]),
        out_shape=jax.ShapeDtypeStruct((S, D_MODEL), F32),
        compiler_params=_params("parallel"),
        name="moe_unpermute_norm",
    )(pos, ys, x1, nw)


def _hierarchical_moe(hu, comb, cnt, x1, wg, wu, wd, final_norm_w):
    S = hu.shape[0]
    tm = MOE_TM
    n_sorted = S + N_GROUPS * tm
    rank = comb[:, RANK_LANE:RANK_LANE + 1]
    grp = comb[:, GRP_LANE].astype(jnp.int32)
    counts = cnt[0, :N_GROUPS].astype(jnp.int32)
    padded = (counts + tm - 1) // tm * tm
    ends = jnp.cumsum(padded)
    starts = ends - padded
    pos = rank[:, 0].astype(jnp.int32) + jnp.sum(
        jnp.where(grp[:, None] == jnp.arange(N_GROUPS)[None, :], starts[None, :], 0), axis=1)
    tile_start = jnp.arange(n_sorted // tm, dtype=jnp.int32) * tm
    tile_group = jnp.minimum(jnp.sum((tile_start[:, None] >= ends[None, :]).astype(jnp.int32), axis=1), N_GROUPS - 1)
    tile_valid = (tile_start < ends[N_GROUPS - 1]).astype(jnp.int32)
    hs, cs = _permute(pos, hu, comb, n_sorted)
    ys = _moe(tile_group, tile_valid, hs, cs, wg, wu, wd)
    return _unpermute_residual_norm(pos, ys, x1, final_norm_w)


def _positional_tables():
    half = RET_QK_DIM // 2
    inv = ROPE_BASE ** (-jnp.arange(half, dtype=F32) / half)
    inv2 = jnp.concatenate([inv, inv]).reshape(1, RET_QK_DIM)
    C, H = RET_CHUNK, RET_HEADS
    log_gamma = jnp.log(1.0 - 2.0 ** (-5.0 - jnp.arange(H, dtype=F32)))
    idx = jnp.arange(C, dtype=F32)
    diff = idx[:, None] - idx[None, :]
    dmask = jnp.where(diff >= 0, jnp.exp(jnp.maximum(diff, 0.0)[None] * log_gamma[:, None, None]), 0.0)
    qd = jnp.exp((idx + 1.0)[None, :] * log_gamma[:, None])
    kd = jnp.exp((C - 1.0 - idx)[None, :] * log_gamma[:, None])
    cd = jnp.exp(C * log_gamma)
    qd = jnp.broadcast_to(qd[:, :, None], (H, C, RET_V_DIM))
    kd = jnp.broadcast_to(kd[:, :, None], (H, C, RET_QK_DIM))
    cd = jnp.broadcast_to(cd[:, None, None], (H, 1, RET_V_DIM))
    return inv2, dmask, qd, kd, cd


def _attention_tables(rel_bias):
    r = np.arange(Q_BLOCK)[None, :]
    cc = np.arange(3 * CMP_TILE)[:, None] - CMP_TILE
    d = r - CMP_STRIDE * (cc - (CMP_TILE - CMP_TOK_PER_QB)) - (CMP_LEN - 1)
    d = np.where(cc < 0, -2, np.where((cc >= CMP_TILE) | (d < 0), -1, d))
    tct = _bias_table_t(rel_bias, d, shift=True)
    d = r + Q_BLOCK - np.arange(3 * Q_BLOCK)[:, None]
    tnt = _bias_table_t(rel_bias, np.where(d < 0, -1, d), shift=True)
    d = r + WINDOW - np.arange(WIN_KEYS + WINDOW)[:, None]
    twt = _bias_table_t(rel_bias, np.where((d < 0) | (d >= WINDOW), -1, d), shift=False)
    kk = np.arange(SEL_WIN * SEL_BLOCK)[:, None] // SEL_BLOCK
    et = jnp.where(jnp.asarray(kk == np.arange(SEL_WIN)[None, :]), -NEG, 0.0).astype(BF16)
    return tct, tnt, twt, et


def _layer(x2d, rel_bias, norm_mix, w_in, cmp_pos_k, cmp_k_w1, cmp_k_w2, cmp_pos_v, cmp_v_w1, cmp_v_w2,
           ret_norm, w_nsa_up, w_ret_up, w_out, norm_ffn, w_rg, b_rg, w_re, b_re, moe_wg, moe_wu, moe_wd,
           final_norm_w):
    S = x2d.shape[0]
    G, dh = NSA_KV_GROUPS, NSA_HEAD_DIM
    assert S % (CMP_TILE * CMP_STRIDE) == 0 and S // SEL_BLOCK >= SEL_TOP_N

    o = np.cumsum((0,) + (NSA_Q_W, 6 * NSA_KV_W, 3 * NSA_HEADS, RET_QK_W, RET_QK_W, RET_V_W, RET_V_W,
                          D_MODEL, D_MODEL))
    seg = lambda n: w_in[:, o[n]:o[n + 1]]
    w_cat = jnp.concatenate([seg(0), seg(1), seg(3), seg(4), seg(5), seg(6), seg(7), seg(8), seg(2),
                             jnp.zeros((D_MODEL, LANE - 3 * NSA_HEADS), F32)], axis=1).astype(BF16)
    inv2, dmask, qd, kd, cd = _positional_tables()

    (qa_t, kcmp, vcmp, ksel, vsel_t, kwin, vwin_t, ga_t, qr, kr, vr, gr, gta, gtr) = _input_projection(
        x2d, norm_mix.reshape(1, D_MODEL), w_cat, inv2)

    nch = S // CMP_STRIDE
    kc = _compress(kcmp, cmp_pos_k, cmp_k_w1, cmp_k_w2).astype(BF16)
    vct = _compress(vcmp, cmp_pos_v, cmp_v_w1, cmp_v_w2).T.reshape(G, dh, nch).astype(BF16)

    tct, tnt, twt, et = _attention_tables(rel_bias)
    ya = _nsa_attention(qa_t.reshape(NSA_HEADS, dh, S), ga_t, kc, vct, tct,
                        ksel, vsel_t.reshape(G, PV_ROWS, S), et, tnt,
                        kwin, vwin_t.reshape(G, PV_ROWS, S), twt)

    yr = _retention(qr, kr, vr, gr, ret_norm.reshape(1, RET_V_W), dmask, qd, kd, cd)

    wrt = jnp.concatenate([w_re, w_rg, jnp.zeros((D_MODEL, LANE - N_EXPERTS - N_GROUPS), F32)], axis=1).astype(BF16)
    brt = jnp.concatenate([b_re, b_rg, jnp.zeros((LANE - N_EXPERTS - N_GROUPS,), F32)]).reshape(1, LANE)
    x1, h2, comb, cnt = _merge(x2d, ya, yr, gta, gtr, w_nsa_up.astype(BF16), w_ret_up.astype(BF16),
                               w_out.astype(BF16), norm_ffn.reshape(1, D_MODEL), wrt, brt)
    return _hierarchical_moe(h2, comb, cnt, x1, moe_wg, moe_wu, moe_wd, final_norm_w)


def kernel(x, rel_bias, norm_mix, w_in, cmp_pos_k, cmp_k_w1, cmp_k_w2, cmp_pos_v, cmp_v_w1, cmp_v_w2, ret_norm, w_nsa_up, w_ret_up, w_out, norm_ffn, w_router_group, b_router_group, w_router_expert, b_router_expert, moe_w_gate, moe_w_up, moe_w_down, norm_final):
    B, S, D = x.shape
    assert B == 1 and norm_mix.shape[0] == 1, "single sequence, depth 1"
    out = _layer(x[0], rel_bias, norm_mix[0], w_in[0], cmp_pos_k[0], cmp_k_w1[0], cmp_k_w2[0],
                 cmp_pos_v[0], cmp_v_w1[0], cmp_v_w2[0], ret_norm[0], w_nsa_up[0], w_ret_up[0], w_out[0],
                 norm_ffn[0], w_router_group[0], b_router_group[0], w_router_expert[0], b_router_expert[0],
                 moe_w_gate[0], moe_w_up[0], moe_w_down[0], norm_final.reshape(1, D))
    return out[None]
```

```python
import functools
import math

import numpy as np
import jax
import jax.numpy as jnp
from jax import lax
from jax.experimental import pallas as pl
from jax.experimental.pallas import tpu as pltpu

F32 = jnp.float32
BF16 = jnp.bfloat16

D_MODEL = 1024
NSA_HEADS = 8
NSA_KV_GROUPS = 2
NSA_HPG = NSA_HEADS // NSA_KV_GROUPS
NSA_HEAD_DIM = 64
CMP_LEN = 32
CMP_STRIDE = 16
CMP_HIDDEN = 256
SEL_BLOCK = 64
SEL_TOP_N = 16
N_LOCAL_SEL = 2
WINDOW = 512
Q_BLOCK = 128
RET_HEADS = 4
RET_QK_DIM = 128
RET_V_DIM = 256
RET_CHUNK = 128
ROPE_BASE = 10000.0
REL_BUCKETS = 32
REL_MAX_DIST = 128
N_GROUPS = 4
EXPERTS_PER_GROUP = 8
N_EXPERTS = N_GROUPS * EXPERTS_PER_GROUP
EXPERT_FF = 256
EPS = 1e-6

NSA_Q_W = NSA_HEADS * NSA_HEAD_DIM
NSA_KV_W = NSA_KV_GROUPS * NSA_HEAD_DIM
RET_QK_W = RET_HEADS * RET_QK_DIM
RET_V_W = RET_HEADS * RET_V_DIM

LANE = 128
NEG = -1e30
VMEM_LIMIT = 56 * 1024 * 1024

CMP_TILE = 128
QB_PER_CMP_TILE = CMP_TILE * CMP_STRIDE // Q_BLOCK
CMP_TOK_PER_QB = Q_BLOCK // CMP_STRIDE
SCR_PAD = 8
SEL_TK = 512
SEL_WIN = 128
SEL_WIN_TILES = SEL_WIN * SEL_BLOCK // SEL_TK
SEL_ROWS_STEP = 64
PV_ROWS = 80
WIN_KEYS = WINDOW + Q_BLOCK
QW = NSA_HPG * Q_BLOCK


def _dot(a, b):
    return jnp.dot(a, b, preferred_element_type=F32)


def _dot_nt(a, b):
    return lax.dot_general(a, b, (((1,), (1,)), ((), ())), preferred_element_type=F32)


def _const_spec(shape):
    nd = len(shape)
    return pl.BlockSpec(shape, lambda *_: (0,) * nd, pipeline_mode=pl.Buffered(1))


def _params(*sem):
    return pltpu.CompilerParams(dimension_semantics=sem, vmem_limit_bytes=VMEM_LIMIT)


_C_QA = 0
_C_KV = _C_QA + NSA_Q_W
_C_QR = _C_KV + 6 * NSA_KV_W
_C_KR = _C_QR + RET_QK_W
_C_VR = _C_KR + RET_QK_W
_C_GR = _C_VR + RET_V_W
_C_GTA = _C_GR + RET_V_W
_C_GTR = _C_GTA + D_MODEL
_C_GA = _C_GTR + D_MODEL
_C_END = _C_GA + LANE
IN_TM = 512


def _inproj_kernel(x_ref, nw_ref, w_ref, inv_ref,
                   qat_ref, kcmp_ref, vcmp_ref, ksel_ref, vselt_ref, kwin_ref, vwint_ref, gat_ref,
                   qr_ref, kr_ref, vr_ref, gr_ref, gta_ref, gtr_ref):
    tm = x_ref.shape[0]
    x = x_ref[...]
    h = x * lax.rsqrt(jnp.mean(x * x, axis=-1, keepdims=True) + EPS) * nw_ref[...]
    hb = h.astype(BF16)

    def proj(a, width):
        return _dot(hb, w_ref[:, a:a + width])

    def values_t(v):
        vt = v.T.astype(BF16)
        aux = jnp.where(lax.broadcasted_iota(jnp.int32, (PV_ROWS - NSA_HEAD_DIM, tm), 0) == 0, 1.0, 0.0).astype(BF16)
        parts = []
        for g in range(NSA_KV_GROUPS):
            parts += [vt[g * NSA_HEAD_DIM:(g + 1) * NSA_HEAD_DIM], aux]
        return jnp.concatenate(parts, axis=0)

    qat_ref[...] = (proj(_C_QA, NSA_Q_W) * (NSA_HEAD_DIM ** -0.5)).T.astype(BF16)
    kv = proj(_C_KV, 6 * NSA_KV_W)
    kcmp_ref[...] = kv[:, 0 * LANE:1 * LANE]
    vcmp_ref[...] = kv[:, 1 * LANE:2 * LANE]
    ksel_ref[...] = kv[:, 2 * LANE:3 * LANE].astype(BF16)
    vselt_ref[...] = values_t(kv[:, 3 * LANE:4 * LANE])
    kwin_ref[...] = kv[:, 4 * LANE:5 * LANE].astype(BF16)
    vwint_ref[...] = values_t(kv[:, 5 * LANE:6 * LANE])
    gat_ref[...] = jax.nn.sigmoid(proj(_C_GA, LANE)).T

    hm = tm // 2
    half = RET_QK_DIM // 2
    low = lax.broadcasted_iota(jnp.int32, (hm, RET_QK_DIM), 1) < half
    row = pl.program_id(0) * tm + lax.broadcasted_iota(jnp.int32, (hm, RET_QK_DIM), 0)
    ang = jnp.where(low, row, row + hm).astype(F32) * inv_ref[...]
    c, s = jnp.cos(ang), jnp.sin(ang)
    c_sw, s_sw = pltpu.roll(c, half, 1), pltpu.roll(s, half, 1)
    cos2 = jnp.concatenate([jnp.where(low, c, c_sw), jnp.where(low, c_sw, c)], axis=0)
    sin2 = jnp.concatenate([jnp.where(low, -s, s_sw), jnp.where(low, -s_sw, s)], axis=0)
    qr = proj(_C_QR, RET_QK_W)
    kr = proj(_C_KR, RET_QK_W)
    for hd in range(RET_HEADS):
        sl = slice(hd * RET_QK_DIM, (hd + 1) * RET_QK_DIM)
        qh = qr[:, sl]
        kh = kr[:, sl]
        qr_ref[:, sl] = (qh * cos2 + pltpu.roll(qh, RET_QK_DIM // 2, 1) * sin2).astype(BF16)
        kr_ref[:, sl] = (kh * cos2 + pltpu.roll(kh, RET_QK_DIM // 2, 1) * sin2) * (RET_QK_DIM ** -0.5)
    vr_ref[...] = proj(_C_VR, RET_V_W).astype(BF16)
    gr_ref[...] = proj(_C_GR, RET_V_W)
    gta_ref[...] = jax.nn.sigmoid(proj(_C_GTA, D_MODEL))
    gtr_ref[...] = jax.nn.sigmoid(proj(_C_GTR, D_MODEL))


def _input_projection(x2d, norm_w, w_cat, inv2):
    S = x2d.shape[0]
    tm = IN_TM
    vt_rows = NSA_KV_GROUPS * PV_ROWS
    outs = [
        (NSA_Q_W, BF16, True),
        (LANE, F32, False), (LANE, F32, False),
        (LANE, BF16, False), (vt_rows, BF16, True),
        (LANE, BF16, False), (vt_rows, BF16, True),
        (LANE, F32, True),
        (RET_QK_W, BF16, False), (RET_QK_W, F32, False),
        (RET_V_W, BF16, False), (RET_V_W, F32, False),
        (D_MODEL, F32, False), (D_MODEL, F32, False),
    ]
    row = lambda w: pl.BlockSpec((tm, w), lambda i: (i, 0))
    col = lambda w: pl.BlockSpec((w, tm), lambda i: (0, i))
    return pl.pallas_call(
        _inproj_kernel,
        grid=(S // tm,),
        in_specs=[row(D_MODEL), _const_spec((1, D_MODEL)), _const_spec((D_MODEL, _C_END)),
                  _const_spec((1, RET_QK_DIM))],
        out_specs=[col(w) if t else row(w) for w, _, t in outs],
        out_shape=[jax.ShapeDtypeStruct((w, S) if t else (S, w), dt) for w, dt, t in outs],
        compiler_params=_params("parallel"),
        name="in_proj",
    )(x2d, norm_w, w_cat, inv2)


def _compress_kernel(x_ref, pos_ref, w1_ref, w2_ref, out_ref, hid_ref):
    n_tok = out_ref.shape[0]
    hid_ref[...] = jnp.zeros_like(hid_ref)

    def body(l, _):
        rows = x_ref[pl.ds(l, n_tok, stride=CMP_STRIDE), :] + pos_ref[pl.ds(l, 1), :]
        hid_ref[...] += _dot(rows.astype(BF16), w1_ref[l])
        return 0

    lax.fori_loop(0, CMP_LEN, body, 0)
    out_ref[...] = _dot(jax.nn.gelu(hid_ref[...]).astype(BF16), w2_ref[...])


def _compress(x, pos, w1, w2):
    S, width = x.shape
    n_tok = S // CMP_STRIDE
    G = NSA_KV_GROUPS
    def block_diag(w):
        w = w.astype(BF16)
        z = jnp.zeros_like(w)
        return jnp.concatenate([jnp.concatenate([w if k == g else z for k in range(G)], axis=-1)
                                for g in range(G)], axis=-2)

    w1_bd = block_diag(w1.reshape(CMP_LEN, NSA_HEAD_DIM, CMP_HIDDEN))
    w2_bd = block_diag(w2)
    x_pad = jnp.pad(x, ((0, CMP_LEN - CMP_STRIDE), (0, 0)))
    full = lambda a: _const_spec(a.shape)
    pos2 = jnp.concatenate([pos] * G, axis=1)
    return pl.pallas_call(
        _compress_kernel,
        grid=(1,),
        in_specs=[full(x_pad), full(pos2), full(w1_bd), full(w2_bd)],
        out_specs=pl.BlockSpec((n_tok, width), lambda i: (0, 0)),
        out_shape=jax.ShapeDtypeStruct((n_tok, width), F32),
        scratch_shapes=[pltpu.VMEM((n_tok, G * CMP_HIDDEN), F32)],
        compiler_params=_params("arbitrary"),
        name="nsa_compress",
    )(x_pad, pos2, w1_bd, w2_bd)


def _t5_bucket_np(dist):
    dist = np.maximum(dist, 0)
    max_exact = REL_BUCKETS // 2
    d_f = np.maximum(dist, 1).astype(np.float32)
    large = max_exact + (np.log(d_f / np.float32(max_exact)) / np.float32(math.log(REL_MAX_DIST / max_exact))
                         * np.float32(REL_BUCKETS - max_exact)).astype(np.int32)
    large = np.minimum(large, REL_BUCKETS - 1)
    return np.where(dist < max_exact, dist, large)


def _bias_table_t(rel_bias, dist, shift):
    bucket = jnp.asarray(np.where(dist >= 0, _t5_bucket_np(dist), -1), jnp.int32)
    rb = rel_bias - rel_bias[REL_BUCKETS - 1][None, :] if shift else rel_bias
    onehot = (bucket[None] == jnp.arange(REL_BUCKETS, dtype=jnp.int32)[:, None, None]).astype(F32)
    tab = jnp.einsum('bh,bkq->hkq', rb, onehot, precision=lax.Precision.HIGHEST)
    tab = jnp.where(jnp.asarray(dist == -1)[None], NEG, tab)
    K, Q = dist.shape
    return tab.reshape(NSA_KV_GROUPS, NSA_HPG, K, Q).transpose(0, 2, 1, 3).reshape(NSA_KV_GROUPS, K, NSA_HPG * Q)


def _nsa_kernel(n_keys, q_ref, ga_ref, kc_ref, vct_ref, tct_ref, ksel_ref, vselt_ref, et_ref, tnt_ref,
                kwin_ref, vwint_ref, twt_ref, out_ref, sc_ref, ps_ref, sa_ref, sb_ref):
    i = pl.program_id(0)
    dh = NSA_HEAD_DIM
    nb = (ps_ref.shape[1] - 2 * SCR_PAD) // (SEL_BLOCK // CMP_STRIDE)

    @pl.when(i == 0)
    def _():
        sc_ref[...] = jnp.zeros_like(sc_ref)
        ps_ref[...] = jnp.zeros_like(ps_ref)

    def col_reduce(op, red, x):
        rows = x.shape[0]
        while rows <= 256 and rows % 16 == 0:
            rows //= 2
            x = op(x[:rows], x[rows:])
        return red(x, axis=0, keepdims=True)

    def colmax(x):
        return col_reduce(jnp.maximum, jnp.max, x)

    def tile4(x):
        return jnp.concatenate([x] * NSA_HPG, axis=1)

    gates = ga_ref[...]
    tl = i // QB_PER_CMP_TILE
    sub = i % QB_PER_CMP_TILE
    tp = tl - 1
    groups = range(NSA_KV_GROUPS)
    q_t_g = [jnp.concatenate([q_ref[NSA_HPG * g + j] for j in range(NSA_HPG)], axis=1) for g in groups]
    zeros = jnp.zeros((dh, QW), BF16)
    wq_g = [jnp.concatenate([q_t_g[g] if k == g else zeros for k in groups], axis=0) for g in groups]

    def cmp_scores(g, t):
        return _dot(kc_ref[pl.ds(pl.multiple_of(t * CMP_TILE, CMP_TILE), CMP_TILE), :], wq_g[g])

    def sc_rows(t):
        return pl.ds(pl.multiple_of(SCR_PAD + t * CMP_TILE, 8), CMP_TILE)

    def cmp_far(t, m):
        out = []
        for g in groups:
            s = cmp_scores(g, t)
            sc_ref[g, sc_rows(t), :] = s
            out.append(jnp.maximum(m[g], colmax(s)))
        return tuple(out)

    m_c = lax.fori_loop(0, jnp.maximum(tp, 0), cmp_far, tuple(jnp.full((1, QW), NEG, F32) for _ in groups))
    tpc = jnp.maximum(tp, 0)
    off = CMP_TILE - CMP_TOK_PER_QB - CMP_TOK_PER_QB * sub
    m_c = list(m_c)
    for g in groups:
        s = (cmp_scores(g, tpc) + tct_ref[g, pl.ds(pl.multiple_of(off, 8), CMP_TILE), :]
             + jnp.where(tp < 0, NEG, 0.0))
        sc_ref[g, sc_rows(tpc), :] = s
        m_c[g] = jnp.maximum(m_c[g], colmax(s))
    for g in groups:
        s = cmp_scores(g, tl) + tct_ref[g, pl.ds(pl.multiple_of(off + CMP_TILE, 8), CMP_TILE), :]
        sc_ref[g, sc_rows(tl), :] = s
        m_c[g] = jnp.maximum(m_c[g], colmax(s))

    def cmp_exp(t, carry):
        out = []
        for g in groups:
            l, acc = carry[g]
            e = jnp.exp(sc_ref[g, sc_rows(t), :] - m_c[g])
            sc_ref[g, sc_rows(t), :] = e
            vt = vct_ref[g, :, pl.ds(pl.multiple_of(t * CMP_TILE, CMP_TILE), CMP_TILE)]
            out.append((l + col_reduce(jnp.add, jnp.sum, e), acc + _dot(vt, e.astype(BF16))))
        return tuple(out)

    cmp_out = lax.fori_loop(0, tl + 1, cmp_exp,
                            tuple((jnp.zeros((1, QW), F32), jnp.zeros((dh, QW), F32)) for _ in groups))
    inv_c = [jnp.where(m_c[g] > 0.5 * NEG, 1.0 / jnp.maximum(cmp_out[g][0], 1e-30), 0.0) for g in groups]
    o_c_g = [cmp_out[g][1] * inv_c[g] for g in groups]

    def cmp_psum(t, _):
        for g in groups:
            p = sc_ref[g, sc_rows(t), :] * inv_c[g]
            ps = p[:, 0:Q_BLOCK]
            for j in range(1, NSA_HPG):
                ps = ps + p[:, j * Q_BLOCK:(j + 1) * Q_BLOCK]
            ps_ref[g, sc_rows(t), :] = ps.astype(BF16).astype(F32)
        return 0

    lax.fori_loop(0, tl + 1, cmp_psum, 0)
    per_blk = SEL_BLOCK // CMP_STRIDE
    imp_g = []
    for g in groups:
        imp = ps_ref[g, pl.ds(SCR_PAD - 1, nb, stride=per_blk), :]
        for k in range(1, CMP_LEN // CMP_STRIDE + per_blk - 1):
            imp = imp + ps_ref[g, pl.ds(SCR_PAD - 1 + k, nb, stride=per_blk), :]
        imp_g.append(imp)

    kw0 = pl.multiple_of(jnp.maximum(i * Q_BLOCK - WINDOW, 0), Q_BLOCK)
    tw0 = pl.multiple_of(jnp.maximum(WINDOW - i * Q_BLOCK, 0), Q_BLOCK)
    o_w_g = []
    for g in groups:
        s = _dot(kwin_ref[pl.ds(kw0, WIN_KEYS), :], wq_g[g]) + twt_ref[g, pl.ds(tw0, WIN_KEYS), :]
        e = jnp.exp(s - colmax(s)).astype(BF16)
        acc_w = _dot(vwint_ref[g, :, pl.ds(kw0, WIN_KEYS)], e)
        o_w_g.append(acc_w[0:dh] * (1.0 / jnp.maximum(acc_w[dh:dh + 1], 1e-30)))

    n_io = lax.broadcasted_iota(jnp.int32, (nb, Q_BLOCK), 0)
    r_io = lax.broadcasted_iota(jnp.int32, (nb, Q_BLOCK), 1)
    back = (i * Q_BLOCK + r_io) // SEL_BLOCK - n_io
    valid = back >= 0
    forced = (n_io == 0) | (valid & (back < N_LOCAL_SEL))
    work_g = [jnp.where(forced, -jnp.inf, jnp.where(valid, imp_g[g], -1e9)) for g in groups]
    n_f = n_io.astype(F32)
    def select_rounds(rows):
        def run(*work):
            work = [w[:rows] for w in work]
            for _ in range(SEL_TOP_N - 1 - N_LOCAL_SEL):
                for g in groups:
                    mx = colmax(work[g])
                    first = col_reduce(jnp.minimum, jnp.min, jnp.where(work[g] == mx, n_f[:rows], float(nb)))
                    work[g] = jnp.where(n_f[:rows] == first, -jnp.inf, work[g])
            rest = [jnp.zeros((nb - rows, Q_BLOCK), F32)] if rows < nb else []
            return tuple(jnp.concatenate([jnp.where(w == -jnp.inf, 1.0, 0.0)] + rest, axis=0) for w in work)
        return run

    levels = [SEL_ROWS_STEP * (k + 1) for k in range(nb // SEL_ROWS_STEP)]
    level = jnp.minimum((i * Q_BLOCK + Q_BLOCK - 1) // (SEL_BLOCK * SEL_ROWS_STEP), len(levels) - 1)
    sel_g = list(lax.switch(level, [select_rounds(rows) for rows in levels], *work_g))

    far_blocks = (i - 1) * (Q_BLOCK // SEL_BLOCK)
    selm_g = [(jnp.where(n_io < far_blocks, sel_g[g], 0.0) - 1.0).astype(BF16) for g in groups]

    far_end = (i - 1) * Q_BLOCK
    n_far = (jnp.maximum(far_end, 0) + SEL_TK - 1) // SEL_TK

    def flash_step(carry, s, vt):
        m_i, acc = carry
        m_new = jnp.maximum(m_i, colmax(s))
        alpha = jnp.exp(m_i - m_new)
        p = jnp.exp(s - m_new).astype(BF16)
        return m_new, alpha * acc + _dot(vt, p)

    carry = tuple((jnp.full((1, QW), NEG, F32), jnp.zeros((PV_ROWS, QW), F32)) for _ in groups)
    for w in range(nb // SEL_WIN):
        w_far = [jnp.concatenate([tile4(selm_g[g][w * SEL_WIN:(w + 1) * SEL_WIN]), wq_g[g]], axis=0)
                 for g in groups]

        lo = w * SEL_WIN_TILES
        last = min((w + 1) * SEL_WIN_TILES, n_keys // SEL_TK) - 1
        trips = (jnp.minimum(n_far, last + 1) - lo + 1) // 2

        def far_scores(kt, s_ref, g, w=w, w_far=w_far):
            k0 = pl.multiple_of(kt * SEL_TK, SEL_TK)
            e0 = pl.multiple_of((kt - w * SEL_WIN_TILES) * SEL_TK, SEL_TK)
            lhs = jnp.concatenate([et_ref[pl.ds(e0, SEL_TK), :],
                                   ksel_ref[pl.ds(k0, SEL_TK), :]], axis=1)
            s_ref[g] = _dot(lhs, w_far[g])

        def far_update(kt, s_ref, g, carry_g):
            k0 = pl.multiple_of(kt * SEL_TK, SEL_TK)
            return flash_step(carry_g, s_ref[g], vselt_ref[g, :, pl.ds(k0, SEL_TK)])

        def far_body(u, carry, lo=lo, last=last, far_scores=far_scores):
            kt = lo + 2 * u
            carry = list(carry)
            for g in groups:
                far_scores(kt + 1, sb_ref, g)
                carry[g] = far_update(kt, sa_ref, g, carry[g])
            for g in groups:
                far_scores(jnp.minimum(kt + 2, last), sa_ref, g)
                carry[g] = far_update(kt + 1, sb_ref, g, carry[g])
            return tuple(carry)

        @pl.when(trips > 0)
        def _(lo=lo, far_scores=far_scores):
            for g in groups:
                far_scores(jnp.int32(lo), sa_ref, g)

        carry = lax.fori_loop(0, trips, far_body, carry)

    heads_out = []
    for g in groups:
        wq, o_c, sel = wq_g[g], o_c_g[g], sel_g[g]
        k0n = pl.multiple_of(jnp.maximum(far_end, 0), Q_BLOCK)
        t0n = pl.multiple_of(jnp.maximum(-far_end, 0), Q_BLOCK)
        near_blocks = 2 * Q_BLOCK // SEL_BLOCK
        pb = lax.broadcasted_iota(jnp.int32, (16, nb), 0)
        pn = lax.broadcasted_iota(jnp.int32, (16, nb), 1)
        pick_rows = jnp.where((pn == jnp.maximum(far_blocks, 0) + pb) & (pb < near_blocks), 1.0, 0.0).astype(BF16)
        sel_rows = _dot(pick_rows, sel.astype(BF16))
        add_near = jnp.concatenate(
            [jnp.broadcast_to((sel_rows[b:b + 1, :] - 1.0) * (-NEG), (SEL_BLOCK, Q_BLOCK)) for b in range(near_blocks)],
            axis=0)
        s = (_dot(ksel_ref[pl.ds(k0n, 2 * Q_BLOCK), :], wq) + tnt_ref[g, pl.ds(t0n, 2 * Q_BLOCK), :]
             + tile4(add_near))
        m_s, acc_s = flash_step(carry[g], s, vselt_ref[g, :, pl.ds(k0n, 2 * Q_BLOCK)])
        o_s = acc_s[0:dh] * (1.0 / jnp.maximum(acc_s[dh:dh + 1], 1e-30))

        o_w = o_w_g[g]

        for j in range(NSA_HPG):
            hd = NSA_HPG * g + j
            cols = slice(j * Q_BLOCK, (j + 1) * Q_BLOCK)
            heads_out.append(gates[hd:hd + 1] * o_c[:, cols]
                             + gates[NSA_HEADS + hd:NSA_HEADS + hd + 1] * o_s[:, cols]
                             + gates[2 * NSA_HEADS + hd:2 * NSA_HEADS + hd + 1] * o_w[:, cols])

    out_ref[...] = jnp.concatenate(heads_out, axis=0).T.astype(out_ref.dtype)


def _nsa_attention(q_t, gates_t, kc, vct, tct, ksel, vselt, et, tnt, kwin, vwint, twt):
    H, dh, S = q_t.shape
    n_qb = S // Q_BLOCK
    n_tok = S // CMP_STRIDE
    nb = -(-(S // SEL_BLOCK) // SEL_WIN) * SEL_WIN
    ps_rows = nb * (SEL_BLOCK // CMP_STRIDE) + 2 * SCR_PAD
    full = lambda a: _const_spec(a.shape)
    return pl.pallas_call(
        functools.partial(_nsa_kernel, S),
        grid=(n_qb,),
        in_specs=[pl.BlockSpec((H, dh, Q_BLOCK), lambda i: (0, 0, i)),
                  pl.BlockSpec((gates_t.shape[0], Q_BLOCK), lambda i: (0, i)),
                  full(kc), full(vct), full(tct), full(ksel), full(vselt), full(et), full(tnt),
                  full(kwin), full(vwint), full(twt)],
        out_specs=pl.BlockSpec((Q_BLOCK, H * dh), lambda i: (i, 0)),
        out_shape=jax.ShapeDtypeStruct((S, H * dh), BF16),
        scratch_shapes=[pltpu.VMEM((NSA_KV_GROUPS, SCR_PAD + n_tok, QW), F32),
                        pltpu.VMEM((NSA_KV_GROUPS, ps_rows, Q_BLOCK), F32),
                        pltpu.VMEM((NSA_KV_GROUPS, SEL_TK, QW), F32),
                        pltpu.VMEM((NSA_KV_GROUPS, SEL_TK, QW), F32)],
        compiler_params=_params("arbitrary"),
        name="nsa_attention",
    )(q_t, gates_t, kc, vct, tct, ksel, vselt, et, tnt, kwin, vwint, twt)


def _retention_kernel(q_ref, k_ref, v_ref, g_ref, nw_ref, dmask_ref, qd_ref, kd_ref, cd_ref, out_ref, state_ref):
    @pl.when(pl.program_id(0) == 0)
    def _():
        state_ref[...] = jnp.zeros_like(state_ref)

    states = [state_ref[hd] for hd in range(RET_HEADS)]
    for c in range(q_ref.shape[0] // RET_CHUNK):
        rows = slice(c * RET_CHUNK, (c + 1) * RET_CHUNK)
        for hd in range(RET_HEADS):
            ks = slice(hd * RET_QK_DIM, (hd + 1) * RET_QK_DIM)
            vs = slice(hd * RET_V_DIM, (hd + 1) * RET_V_DIM)
            q = q_ref[rows, ks]
            k = k_ref[rows, ks]
            v = v_ref[rows, vs]
            state = states[hd]
            inner = _dot_nt(q, k.astype(BF16)) * dmask_ref[hd]
            y = _dot(inner.astype(BF16), v) + _dot(q, state.astype(BF16)) * qd_ref[hd]
            kd = k * kd_ref[hd]
            states[hd] = state * cd_ref[hd] + _dot(kd.T.astype(BF16), v)
            mu = jnp.mean(y, axis=-1, keepdims=True)
            yc = y - mu
            var = jnp.mean(yc * yc, axis=-1, keepdims=True)
            yn = yc * lax.rsqrt(var + EPS) * nw_ref[:, vs]
            out_ref[rows, vs] = (jax.nn.silu(g_ref[rows, vs]) * yn).astype(out_ref.dtype)
    for hd in range(RET_HEADS):
        state_ref[hd] = states[hd]


RET_STEP_CHUNKS = 8


def _retention(q, k, v, g, norm_w, dmask, qd, kd, cd):
    S = q.shape[0]
    C = RET_CHUNK * RET_STEP_CHUNKS
    row = lambda w: pl.BlockSpec((C, w), lambda i: (i, 0))
    full = lambda a: _const_spec(a.shape)
    return pl.pallas_call(
        _retention_kernel,
        grid=(S // C,),
        in_specs=[row(RET_QK_W), row(RET_QK_W), row(RET_V_W), row(RET_V_W),
                  full(norm_w), full(dmask), full(qd), full(kd), full(cd)],
        out_specs=row(RET_V_W),
        out_shape=jax.ShapeDtypeStruct((S, RET_V_W), BF16),
        scratch_shapes=[pltpu.VMEM((RET_HEADS, RET_QK_DIM, RET_V_DIM), F32)],
        compiler_params=_params("arbitrary"),
        name="retention",
    )(q, k, v, g, norm_w, dmask, qd, kd, cd)


MERGE_TM = 512
MERGE_SUB_ROWS = 256
GRP_LANE = 40
RANK_LANE = 41


def _merge_kernel(x_ref, ya_ref, yr_ref, gta_ref, gtr_ref, wa_ref, wr_ref, wo_ref, nw_ref, wrt_ref, brt_ref,
                  tri_ref, x1_ref, h_ref, comb_ref, cnt_ref, carry_ref):
    @pl.when(pl.program_id(0) == 0)
    def _():
        carry_ref[...] = jnp.zeros_like(carry_ref)

    sub = tri_ref.shape[0]
    carry = carry_ref[...]
    for r in range(x_ref.shape[0] // sub):
        carry = _merge_rows(slice(r * sub, (r + 1) * sub), carry, x_ref, ya_ref, yr_ref, gta_ref, gtr_ref,
                            wa_ref, wr_ref, wo_ref, nw_ref, wrt_ref, brt_ref, tri_ref, x1_ref, h_ref, comb_ref)
    carry_ref[...] = carry
    cnt_ref[...] = jnp.broadcast_to(carry, cnt_ref.shape)


def _merge_rows(rows, carry, x_ref, ya_ref, yr_ref, gta_ref, gtr_ref, wa_ref, wr_ref, wo_ref, nw_ref, wrt_ref,
                brt_ref, tri_ref, x1_ref, h_ref, comb_ref):
    merged = gta_ref[rows] * _dot(ya_ref[rows], wa_ref[...]) + gtr_ref[rows] * _dot(yr_ref[rows], wr_ref[...])
    x1 = x_ref[rows] + _dot(merged.astype(BF16), wo_ref[...])
    x1_ref[rows] = x1
    h = (x1 * lax.rsqrt(jnp.mean(x1 * x1, axis=-1, keepdims=True) + EPS) * nw_ref[...]).astype(BF16)
    h_ref[rows] = _pack_bf16_pairs(h.astype(F32))
    logits = _dot(h, wrt_ref[...]) + brt_ref[...]
    lane = lax.broadcasted_iota(jnp.int32, logits.shape, 1)
    lane_f = lane.astype(F32)
    big = float(LANE)
    is_grp = (lane >= N_EXPERTS) & (lane < N_EXPERTS + N_GROUPS)
    lg = jnp.where(is_grp, logits, -jnp.inf)
    eg = jnp.where(is_grp, jnp.exp(lg - jnp.max(lg, axis=-1, keepdims=True)), 0.0)
    pg = eg / jnp.sum(eg, axis=-1, keepdims=True)
    p_grp = jnp.max(pg, axis=-1, keepdims=True)
    grp = jnp.min(jnp.where(is_grp & (pg == p_grp), lane_f, big), axis=-1, keepdims=True) - N_EXPERTS
    in_grp = (lane_f >= grp * EXPERTS_PER_GROUP) & (lane_f < (grp + 1.0) * EXPERTS_PER_GROUP)
    le = jnp.where(in_grp, logits, -jnp.inf)
    m1 = jnp.max(le, axis=-1, keepdims=True)
    i1 = jnp.min(jnp.where(le == m1, lane_f, big), axis=-1, keepdims=True)
    le2 = jnp.where(lane_f == i1, -jnp.inf, le)
    m2 = jnp.max(le2, axis=-1, keepdims=True)
    i2 = jnp.min(jnp.where(le2 == m2, lane_f, big), axis=-1, keepdims=True)
    e2 = jnp.exp(m2 - m1)
    den = 1.0 + e2
    onehot = jnp.where(lane_f == grp, 1.0, 0.0)
    before = _dot(tri_ref[...], onehot.astype(BF16)) + carry
    rank = jnp.sum(onehot * before, axis=-1, keepdims=True)
    comb_ref[rows] = (jnp.where(lane_f == i1, (1.0 / den) * p_grp, 0.0)
                      + jnp.where(lane_f == i2, (e2 / den) * p_grp, 0.0)
                      + jnp.where(lane == GRP_LANE, grp, 0.0)
                      + jnp.where(lane == RANK_LANE, rank, 0.0))
    return carry + jnp.sum(onehot, axis=0, keepdims=True)


def _pack_bf16_pairs(x):
    n = x.shape[1] // 2
    return pltpu.pack_elementwise([x[:, :n], x[:, n:]], packed_dtype=BF16)


def _unpack_bf16_pairs(w):
    return tuple(pltpu.unpack_elementwise(w, index=k, packed_dtype=BF16, unpacked_dtype=F32) for k in range(2))


def _merge(x2d, ya, yr, gta, gtr, wa, wr, wo, nw, wrt, brt):
    S = x2d.shape[0]
    tm = MERGE_TM
    row = lambda w: pl.BlockSpec((tm, w), lambda i: (i, 0))
    full = lambda a: _const_spec(a.shape)
    sub = MERGE_SUB_ROWS
    tri = jnp.asarray(np.tril(np.ones((sub, sub), np.float32), -1), BF16)
    return pl.pallas_call(
        _merge_kernel,
        grid=(S // tm,),
        in_specs=[row(D_MODEL), row(NSA_Q_W), row(RET_V_W), row(D_MODEL), row(D_MODEL),
                  full(wa), full(wr), full(wo), full(nw), full(wrt), full(brt), full(tri)],
        out_specs=[row(D_MODEL), row(D_MODEL // 2), row(LANE), pl.BlockSpec((8, LANE), lambda i: (0, 0))],
        out_shape=[jax.ShapeDtypeStruct((S, D_MODEL), F32), jax.ShapeDtypeStruct((S, D_MODEL // 2), jnp.uint32),
                   jax.ShapeDtypeStruct((S, LANE), F32), jax.ShapeDtypeStruct((8, LANE), F32)],
        scratch_shapes=[pltpu.VMEM((1, LANE), F32)],
        compiler_params=_params("arbitrary"),
        name="merge_router",
    )(x2d, ya, yr, gta, gtr, wa, wr, wo, nw, wrt, brt, tri)


MOE_TM = 1024
MOE_SUB_TILES = 2
RANK_TM = 512
HALF = D_MODEL // 2


def _permute_kernel(pos_ref, h_ref, comb_ref, hs_ref, cs_ref):
    i = pl.program_id(0)

    @pl.when(i == 0)
    def _():
        hs_ref[...] = jnp.zeros_like(hs_ref)
        cs_ref[...] = jnp.zeros_like(cs_ref)

    tm = h_ref.shape[0]

    def body(r, _):
        d = pos_ref[i * tm + r]
        hs_ref[pl.ds(d, 1), :] = h_ref[pl.ds(r, 1), :]
        cs_ref[pl.ds(d, 1), :] = comb_ref[pl.ds(r, 1), :]
        return 0

    lax.fori_loop(0, tm, body, 0, unroll=8)


def _permute(pos, hu, comb, n_sorted):
    S = hu.shape[0]
    tm = RANK_TM
    resident = lambda w: pl.BlockSpec((n_sorted, w), lambda i, p: (0, 0), pipeline_mode=pl.Buffered(1))
    return pl.pallas_call(
        _permute_kernel,
        grid_spec=pltpu.PrefetchScalarGridSpec(
            num_scalar_prefetch=1, grid=(S // tm,),
            in_specs=[pl.BlockSpec((tm, HALF), lambda i, p: (i, 0)), pl.BlockSpec((tm, LANE), lambda i, p: (i, 0))],
            out_specs=[resident(HALF), resident(LANE)]),
        out_shape=[jax.ShapeDtypeStruct((n_sorted, HALF), jnp.uint32), jax.ShapeDtypeStruct((n_sorted, LANE), F32)],
        compiler_params=_params("arbitrary"),
        name="moe_permute",
    )(pos, hu, comb)


def _moe_kernel(tg_ref, tv_ref, hs_ref, cs_ref, wg_ref, wu_ref, wd_ref, ys_ref, lo_ref, hi_ref, acc_ref):
    i = pl.program_id(0)
    e = pl.program_id(1)

    @pl.when(e == 0)
    def _():
        lo, hi = _unpack_bf16_pairs(hs_ref[...])
        lo_ref[...] = lo.astype(BF16)
        hi_ref[...] = hi.astype(BF16)
        acc_ref[...] = jnp.zeros_like(acc_ref)

    @pl.when(tv_ref[i] > 0)
    def _():
        wg = wg_ref[0].astype(BF16)
        wu = wu_ref[0].astype(BF16)
        wd = wd_ref[0].astype(BF16)
        sub = lo_ref.shape[0] // MOE_SUB_TILES
        for r in range(MOE_SUB_TILES):
            rows = slice(r * sub, (r + 1) * sub)
            lo, hi = lo_ref[rows], hi_ref[rows]
            gate = _dot(lo, wg[:HALF]) + _dot(hi, wg[HALF:])
            up = _dot(lo, wu[:HALF]) + _dot(hi, wu[HALF:])
            comb = cs_ref[rows]
            lane = lax.broadcasted_iota(jnp.int32, comb.shape, 1)
            c = jnp.sum(jnp.where(lane == tg_ref[i] * EXPERTS_PER_GROUP + e, comb, 0.0), axis=-1, keepdims=True)
            acc_ref[rows] += _dot((jax.nn.silu(gate) * up).astype(BF16), wd) * c

    @pl.when(e == pl.num_programs(1) - 1)
    def _():
        ys_ref[...] = _pack_bf16_pairs(acc_ref[...].astype(BF16).astype(F32))


def _moe(tile_group, tile_valid, hs, cs, wg, wu, wd):
    n_sorted = hs.shape[0]
    tm = MOE_TM
    row = lambda w: pl.BlockSpec((tm, w), lambda i, e, tg, tv: (i, 0))
    wspec = lambda a, b: pl.BlockSpec((1, a, b), lambda i, e, tg, tv: (tg[i] * EXPERTS_PER_GROUP + e, 0, 0))
    return pl.pallas_call(
        _moe_kernel,
        grid_spec=pltpu.PrefetchScalarGridSpec(
            num_scalar_prefetch=2, grid=(n_sorted // tm, EXPERTS_PER_GROUP),
            in_specs=[row(HALF), row(LANE), wspec(D_MODEL, EXPERT_FF), wspec(D_MODEL, EXPERT_FF),
                      wspec(EXPERT_FF, D_MODEL)],
            out_specs=row(HALF),
            scratch_shapes=[pltpu.VMEM((tm, HALF), BF16), pltpu.VMEM((tm, HALF), BF16),
                            pltpu.VMEM((tm, D_MODEL), F32)]),
        out_shape=jax.ShapeDtypeStruct((n_sorted, HALF), jnp.uint32),
        compiler_params=_params("parallel", "arbitrary"),
        name="moe_experts",
    )(tile_group, tile_valid, hs, cs, wg, wu, wd)


def _unpermute_kernel(pos_ref, ys_ref, x1_ref, nw_ref, out_ref, stage_ref):
    i = pl.program_id(0)
    tm = x1_ref.shape[0]

    def body(r, _):
        stage_ref[pl.ds(r, 1), :] = ys_ref[pl.ds(pos_ref[i * tm + r], 1), :]
        return 0

    lax.fori_loop(0, tm, body, 0, unroll=8)
    lo, hi = _unpack_bf16_pairs(stage_ref[...])
    x2 = x1_ref[...] + jnp.concatenate([lo, hi], axis=1)
    out_ref[...] = x2 * lax.rsqrt(jnp.mean(x2 * x2, axis=-1, keepdims=True) + EPS) * nw_ref[...]


def _unpermute_residual_norm(pos, ys, x1, nw):
    S = x1.shape[0]
    tm = RANK_TM
    return pl.pallas_call(
        _unpermute_kernel,
        grid_spec=pltpu.PrefetchScalarGridSpec(
            num_scalar_prefetch=1, grid=(S // tm,),
            in_specs=[pl.BlockSpec(ys.shape, lambda i, p: (0, 0), pipeline_mode=pl.Buffered(1)),
                      pl.BlockSpec((tm, D_MODEL), lambda i, p: (i, 0)),
                      pl.BlockSpec((1, D_MODEL), lambda i, p: (0, 0))],
            out_specs=pl.BlockSpec((tm, D_MODEL), lambda i, p: (i, 0)),
            scratch_shapes=[pltpu.VMEM((tm, HALF), jnp.uint32)]),
        out_shape=jax.ShapeDtypeStruct((S, D_MODEL), F32),
        compiler_params=_params("parallel"),
        name="moe_unpermute_norm",
    )(pos, ys, x1, nw)


def _hierarchical_moe(hu, comb, cnt, x1, wg, wu, wd, final_norm_w):
    S = hu.shape[0]
    tm = MOE_TM
    n_sorted = S + N_GROUPS * tm
    rank = comb[:, RANK_LANE:RANK_LANE + 1]
    grp = comb[:, GRP_LANE].astype(jnp.int32)
    counts = cnt[0, :N_GROUPS].astype(jnp.int32)
    padded = (counts + tm - 1) // tm * tm
    ends = jnp.cumsum(padded)
    starts = ends - padded
    pos = rank[:, 0].astype(jnp.int32) + jnp.sum(
        jnp.where(grp[:, None] == jnp.arange(N_GROUPS)[None, :], starts[None, :], 0), axis=1)
    tile_start = jnp.arange(n_sorted // tm, dtype=jnp.int32) * tm
    tile_group = jnp.minimum(jnp.sum((tile_start[:, None] >= ends[None, :]).astype(jnp.int32), axis=1), N_GROUPS - 1)
    tile_valid = (tile_start < ends[N_GROUPS - 1]).astype(jnp.int32)
    hs, cs = _permute(pos, hu, comb, n_sorted)
    ys = _moe(tile_group, tile_valid, hs, cs, wg, wu, wd)
    return _unpermute_residual_norm(pos, ys, x1, final_norm_w)


def _positional_tables():
    half = RET_QK_DIM // 2
    inv = ROPE_BASE ** (-jnp.arange(half, dtype=F32) / half)
    inv2 = jnp.concatenate([inv, inv]).reshape(1, RET_QK_DIM)
    C, H = RET_CHUNK, RET_HEADS
    log_gamma = jnp.log(1.0 - 2.0 ** (-5.0 - jnp.arange(H, dtype=F32)))
    idx = jnp.arange(C, dtype=F32)
    diff = idx[:, None] - idx[None, :]
    dmask = jnp.where(diff >= 0, jnp.exp(jnp.maximum(diff, 0.0)[None] * log_gamma[:, None, None]), 0.0)
    qd = jnp.exp((idx + 1.0)[None, :] * log_gamma[:, None])
    kd = jnp.exp((C - 1.0 - idx)[None, :] * log_gamma[:, None])
    cd = jnp.exp(C * log_gamma)
    qd = jnp.broadcast_to(qd[:, :, None], (H, C, RET_V_DIM))
    kd = jnp.broadcast_to(kd[:, :, None], (H, C, RET_QK_DIM))
    cd = jnp.broadcast_to(cd[:, None, None], (H, 1, RET_V_DIM))
    return inv2, dmask, qd, kd, cd


def _attention_tables(rel_bias):
    r = np.arange(Q_BLOCK)[None, :]
    cc = np.arange(3 * CMP_TILE)[:, None] - CMP_TILE
    d = r - CMP_STRIDE * (cc - (CMP_TILE - CMP_TOK_PER_QB)) - (CMP_LEN - 1)
    d = np.where(cc < 0, -2, np.where((cc >= CMP_TILE) | (d < 0), -1, d))
    tct = _bias_table_t(rel_bias, d, shift=True)
    d = r + Q_BLOCK - np.arange(3 * Q_BLOCK)[:, None]
    tnt = _bias_table_t(rel_bias, np.where(d < 0, -1, d), shift=True)
    d = r + WINDOW - np.arange(WIN_KEYS + WINDOW)[:, None]
    twt = _bias_table_t(rel_bias, np.where((d < 0) | (d >= WINDOW), -1, d), shift=False)
    kk = np.arange(SEL_WIN * SEL_BLOCK)[:, None] // SEL_BLOCK
    et = jnp.where(jnp.asarray(kk == np.arange(SEL_WIN)[None, :]), -NEG, 0.0).astype(BF16)
    return tct, tnt, twt, et


def _layer(x2d, rel_bias, norm_mix, w_in, cmp_pos_k, cmp_k_w1, cmp_k_w2, cmp_pos_v, cmp_v_w1, cmp_v_w2,
           ret_norm, w_nsa_up, w_ret_up, w_out, norm_ffn, w_rg, b_rg, w_re, b_re, moe_wg, moe_wu, moe_wd,
           final_norm_w):
    S = x2d.shape[0]
    G, dh = NSA_KV_GROUPS, NSA_HEAD_DIM
    assert S % (CMP_TILE * CMP_STRIDE) == 0 and S // SEL_BLOCK >= SEL_TOP_N

    o = np.cumsum((0,) + (NSA_Q_W, 6 * NSA_KV_W, 3 * NSA_HEADS, RET_QK_W, RET_QK_W, RET_V_W, RET_V_W,
                          D_MODEL, D_MODEL))
    seg = lambda n: w_in[:, o[n]:o[n + 1]]
    w_cat = jnp.concatenate([seg(0), seg(1), seg(3), seg(4), seg(5), seg(6), seg(7), seg(8), seg(2),
                             jnp.zeros((D_MODEL, LANE - 3 * NSA_HEADS), F32)], axis=1).astype(BF16)
    inv2, dmask, qd, kd, cd = _positional_tables()

    (qa_t, kcmp, vcmp, ksel, vsel_t, kwin, vwin_t, ga_t, qr, kr, vr, gr, gta, gtr) = _input_projection(
        x2d, norm_mix.reshape(1, D_MODEL), w_cat, inv2)

    nch = S // CMP_STRIDE
    kc = _compress(kcmp, cmp_pos_k, cmp_k_w1, cmp_k_w2).astype(BF16)
    vct = _compress(vcmp, cmp_pos_v, cmp_v_w1, cmp_v_w2).T.reshape(G, dh, nch).astype(BF16)

    tct, tnt, twt, et = _attention_tables(rel_bias)
    ya = _nsa_attention(qa_t.reshape(NSA_HEADS, dh, S), ga_t, kc, vct, tct,
                        ksel, vsel_t.reshape(G, PV_ROWS, S), et, tnt,
                        kwin, vwin_t.reshape(G, PV_ROWS, S), twt)

    yr = _retention(qr, kr, vr, gr, ret_norm.reshape(1, RET_V_W), dmask, qd, kd, cd)

    wrt = jnp.concatenate([w_re, w_rg, jnp.zeros((D_MODEL, LANE - N_EXPERTS - N_GROUPS), F32)], axis=1).astype(BF16)
    brt = jnp.concatenate([b_re, b_rg, jnp.zeros((LANE - N_EXPERTS - N_GROUPS,), F32)]).reshape(1, LANE)
    x1, h2, comb, cnt = _merge(x2d, ya, yr, gta, gtr, w_nsa_up.astype(BF16), w_ret_up.astype(BF16),
                               w_out.astype(BF16), norm_ffn.reshape(1, D_MODEL), wrt, brt)
    return _hierarchical_moe(h2, comb, cnt, x1, moe_wg, moe_wu, moe_wd, final_norm_w)


def kernel(x, rel_bias, norm_mix, w_in, cmp_pos_k, cmp_k_w1, cmp_k_w2, cmp_pos_v, cmp_v_w1, cmp_v_w2, ret_norm, w_nsa_up, w_ret_up, w_out, norm_ffn, w_router_group, b_router_group, w_router_expert, b_router_expert, moe_w_gate, moe_w_up, moe_w_down, norm_final):
    B, S, D = x.shape
    assert B == 1 and norm_mix.shape[0] == 1, "single sequence, depth 1"
    out = _layer(x[0], rel_bias, norm_mix[0], w_in[0], cmp_pos_k[0], cmp_k_w1[0], cmp_k_w2[0],
                 cmp_pos_v[0], cmp_v_w1[0], cmp_v_w2[0], ret_norm[0], w_nsa_up[0], w_ret_up[0], w_out[0],
                 norm_ffn[0], w_router_group[0], b_router_group[0], w_router_expert[0], b_router_expert[0],
                 moe_w_gate[0], moe_w_up[0], moe_w_down[0], norm_final.reshape(1, D))
    return out[None]
```

```python
import functools
import math

import numpy as np
import jax
import jax.numpy as jnp
from jax import lax
from jax.experimental import pallas as pl
from jax.experimental.pallas import tpu as pltpu

F32 = jnp.float32
BF16 = jnp.bfloat16

D_MODEL = 1024
NSA_HEADS = 8
NSA_KV_GROUPS = 2
NSA_HPG = NSA_HEADS // NSA_KV_GROUPS
NSA_HEAD_DIM = 64
CMP_LEN = 32
CMP_STRIDE = 16
CMP_HIDDEN = 256
SEL_BLOCK = 64
SEL_TOP_N = 16
N_LOCAL_SEL = 2
WINDOW = 512
Q_BLOCK = 128
RET_HEADS = 4
RET_QK_DIM = 128
RET_V_DIM = 256
RET_CHUNK = 128
ROPE_BASE = 10000.0
REL_BUCKETS = 32
REL_MAX_DIST = 128
N_GROUPS = 4
EXPERTS_PER_GROUP = 8
N_EXPERTS = N_GROUPS * EXPERTS_PER_GROUP
EXPERT_FF = 256
EPS = 1e-6

NSA_Q_W = NSA_HEADS * NSA_HEAD_DIM
NSA_KV_W = NSA_KV_GROUPS * NSA_HEAD_DIM
RET_QK_W = RET_HEADS * RET_QK_DIM
RET_V_W = RET_HEADS * RET_V_DIM

LANE = 128
NEG = -1e30
VMEM_LIMIT = 56 * 1024 * 1024

CMP_TILE = 128
QB_PER_CMP_TILE = CMP_TILE * CMP_STRIDE // Q_BLOCK
CMP_TOK_PER_QB = Q_BLOCK // CMP_STRIDE
SCR_PAD = 8
SEL_TK = 512
SEL_WIN = 128
SEL_WIN_TILES = SEL_WIN * SEL_BLOCK // SEL_TK
FAR_LONG_TILES = 4
SEL_ROWS_STEP = 64
PV_ROWS = 80
WIN_KEYS = WINDOW + Q_BLOCK
QW = NSA_HPG * Q_BLOCK


def _dot(a, b):
    return jnp.dot(a, b, preferred_element_type=F32)


def _dot_nt(a, b):
    return lax.dot_general(a, b, (((1,), (1,)), ((), ())), preferred_element_type=F32)


def _const_spec(shape):
    nd = len(shape)
    return pl.BlockSpec(shape, lambda *_: (0,) * nd, pipeline_mode=pl.Buffered(1))


def _params(*sem):
    return pltpu.CompilerParams(dimension_semantics=sem, vmem_limit_bytes=VMEM_LIMIT)


_C_QA = 0
_C_KV = _C_QA + NSA_Q_W
_C_QR = _C_KV + 6 * NSA_KV_W
_C_KR = _C_QR + RET_QK_W
_C_VR = _C_KR + RET_QK_W
_C_GR = _C_VR + RET_V_W
_C_GTA = _C_GR + RET_V_W
_C_GTR = _C_GTA + D_MODEL
_C_GA = _C_GTR + D_MODEL
_C_END = _C_GA + LANE
IN_TM = 512


def _inproj_kernel(x_ref, nw_ref, w_ref, inv_ref,
                   qat_ref, kcmp_ref, vcmp_ref, ksel_ref, vselt_ref, kwin_ref, vwint_ref, gat_ref,
                   qr_ref, kr_ref, vr_ref, gr_ref, gta_ref, gtr_ref):
    tm = x_ref.shape[0]
    x = x_ref[...]
    h = x * lax.rsqrt(jnp.mean(x * x, axis=-1, keepdims=True) + EPS) * nw_ref[...]
    hb = h.astype(BF16)

    def proj(a, width):
        return _dot(hb, w_ref[:, a:a + width])

    def values_t(v):
        vt = v.T.astype(BF16)
        aux = jnp.where(lax.broadcasted_iota(jnp.int32, (PV_ROWS - NSA_HEAD_DIM, tm), 0) == 0, 1.0, 0.0).astype(BF16)
        parts = []
        for g in range(NSA_KV_GROUPS):
            parts += [vt[g * NSA_HEAD_DIM:(g + 1) * NSA_HEAD_DIM], aux]
        return jnp.concatenate(parts, axis=0)

    qat_ref[...] = (proj(_C_QA, NSA_Q_W) * (NSA_HEAD_DIM ** -0.5)).T.astype(BF16)
    kv = proj(_C_KV, 6 * NSA_KV_W)
    kcmp_ref[...] = kv[:, 0 * LANE:1 * LANE]
    vcmp_ref[...] = kv[:, 1 * LANE:2 * LANE]
    ksel_ref[...] = kv[:, 2 * LANE:3 * LANE].astype(BF16)
    vselt_ref[...] = values_t(kv[:, 3 * LANE:4 * LANE])
    kwin_ref[...] = kv[:, 4 * LANE:5 * LANE].astype(BF16)
    vwint_ref[...] = values_t(kv[:, 5 * LANE:6 * LANE])
    gat_ref[...] = jax.nn.sigmoid(proj(_C_GA, LANE)).T

    hm = tm // 2
    half = RET_QK_DIM // 2
    low = lax.broadcasted_iota(jnp.int32, (hm, RET_QK_DIM), 1) < half
    row = pl.program_id(0) * tm + lax.broadcasted_iota(jnp.int32, (hm, RET_QK_DIM), 0)
    ang = jnp.where(low, row, row + hm).astype(F32) * inv_ref[...]
    c, s = jnp.cos(ang), jnp.sin(ang)
    c_sw, s_sw = pltpu.roll(c, half, 1), pltpu.roll(s, half, 1)
    cos2 = jnp.concatenate([jnp.where(low, c, c_sw), jnp.where(low, c_sw, c)], axis=0)
    sin2 = jnp.concatenate([jnp.where(low, -s, s_sw), jnp.where(low, -s_sw, s)], axis=0)
    qr = proj(_C_QR, RET_QK_W)
    kr = proj(_C_KR, RET_QK_W)
    for hd in range(RET_HEADS):
        sl = slice(hd * RET_QK_DIM, (hd + 1) * RET_QK_DIM)
        qh = qr[:, sl]
        kh = kr[:, sl]
        qr_ref[:, sl] = (qh * cos2 + pltpu.roll(qh, RET_QK_DIM // 2, 1) * sin2).astype(BF16)
        kr_ref[:, sl] = (kh * cos2 + pltpu.roll(kh, RET_QK_DIM // 2, 1) * sin2) * (RET_QK_DIM ** -0.5)
    vr_ref[...] = proj(_C_VR, RET_V_W).astype(BF16)
    gr_ref[...] = proj(_C_GR, RET_V_W)
    gta_ref[...] = jax.nn.sigmoid(proj(_C_GTA, D_MODEL))
    gtr_ref[...] = jax.nn.sigmoid(proj(_C_GTR, D_MODEL))


def _input_projection(x2d, norm_w, w_cat, inv2):
    S = x2d.shape[0]
    tm = IN_TM
    vt_rows = NSA_KV_GROUPS * PV_ROWS
    outs = [
        (NSA_Q_W, BF16, True),
        (LANE, F32, False), (LANE, F32, False),
        (LANE, BF16, False), (vt_rows, BF16, True),
        (LANE, BF16, False), (vt_rows, BF16, True),
        (LANE, F32, True),
        (RET_QK_W, BF16, False), (RET_QK_W, F32, False),
        (RET_V_W, BF16, False), (RET_V_W, F32, False),
        (D_MODEL, F32, False), (D_MODEL, F32, False),
    ]
    row = lambda w: pl.BlockSpec((tm, w), lambda i: (i, 0))
    col = lambda w: pl.BlockSpec((w, tm), lambda i: (0, i))
    return pl.pallas_call(
        _inproj_kernel,
        grid=(S // tm,),
        in_specs=[row(D_MODEL), _const_spec((1, D_MODEL)), _const_spec((D_MODEL, _C_END)),
                  _const_spec((1, RET_QK_DIM))],
        out_specs=[col(w) if t else row(w) for w, _, t in outs],
        out_shape=[jax.ShapeDtypeStruct((w, S) if t else (S, w), dt) for w, dt, t in outs],
        compiler_params=_params("parallel"),
        name="in_proj",
    )(x2d, norm_w, w_cat, inv2)


def _compress_kernel(x_ref, pos_ref, w1_ref, w2_ref, out_ref, hid_ref):
    n_tok = out_ref.shape[0]
    hid_ref[...] = jnp.zeros_like(hid_ref)

    def body(l, _):
        rows = x_ref[pl.ds(l, n_tok, stride=CMP_STRIDE), :] + pos_ref[pl.ds(l, 1), :]
        hid_ref[...] += _dot(rows.astype(BF16), w1_ref[l])
        return 0

    lax.fori_loop(0, CMP_LEN, body, 0)
    out_ref[...] = _dot(jax.nn.gelu(hid_ref[...]).astype(BF16), w2_ref[...])


def _compress(x, pos, w1, w2):
    S, width = x.shape
    n_tok = S // CMP_STRIDE
    G = NSA_KV_GROUPS
    def block_diag(w):
        w = w.astype(BF16)
        z = jnp.zeros_like(w)
        return jnp.concatenate([jnp.concatenate([w if k == g else z for k in range(G)], axis=-1)
                                for g in range(G)], axis=-2)

    w1_bd = block_diag(w1.reshape(CMP_LEN, NSA_HEAD_DIM, CMP_HIDDEN))
    w2_bd = block_diag(w2)
    x_pad = jnp.pad(x, ((0, CMP_LEN - CMP_STRIDE), (0, 0)))
    full = lambda a: _const_spec(a.shape)
    pos2 = jnp.concatenate([pos] * G, axis=1)
    return pl.pallas_call(
        _compress_kernel,
        grid=(1,),
        in_specs=[full(x_pad), full(pos2), full(w1_bd), full(w2_bd)],
        out_specs=pl.BlockSpec((n_tok, width), lambda i: (0, 0)),
        out_shape=jax.ShapeDtypeStruct((n_tok, width), F32),
        scratch_shapes=[pltpu.VMEM((n_tok, G * CMP_HIDDEN), F32)],
        compiler_params=_params("arbitrary"),
        name="nsa_compress",
    )(x_pad, pos2, w1_bd, w2_bd)


def _t5_bucket_np(dist):
    dist = np.maximum(dist, 0)
    max_exact = REL_BUCKETS // 2
    d_f = np.maximum(dist, 1).astype(np.float32)
    large = max_exact + (np.log(d_f / np.float32(max_exact)) / np.float32(math.log(REL_MAX_DIST / max_exact))
                         * np.float32(REL_BUCKETS - max_exact)).astype(np.int32)
    large = np.minimum(large, REL_BUCKETS - 1)
    return np.where(dist < max_exact, dist, large)


def _bias_table_t(rel_bias, dist, shift):
    bucket = jnp.asarray(np.where(dist >= 0, _t5_bucket_np(dist), -1), jnp.int32)
    rb = rel_bias - rel_bias[REL_BUCKETS - 1][None, :] if shift else rel_bias
    onehot = (bucket[None] == jnp.arange(REL_BUCKETS, dtype=jnp.int32)[:, None, None]).astype(F32)
    tab = jnp.einsum('bh,bkq->hkq', rb, onehot, precision=lax.Precision.HIGHEST)
    tab = jnp.where(jnp.asarray(dist == -1)[None], NEG, tab)
    K, Q = dist.shape
    return tab.reshape(NSA_KV_GROUPS, NSA_HPG, K, Q).transpose(0, 2, 1, 3).reshape(NSA_KV_GROUPS, K, NSA_HPG * Q)


def _nsa_kernel(n_keys, q_ref, ga_ref, kc_ref, vct_ref, tct_ref, ksel_ref, vselt_ref, et_ref, tnt_ref,
                kwin_ref, vwint_ref, twt_ref, out_ref, sc_ref, ps_ref, sa_ref, sb_ref, ma_ref, mb_ref):
    i = pl.program_id(0)
    dh = NSA_HEAD_DIM
    nb = (ps_ref.shape[1] - 2 * SCR_PAD) // (SEL_BLOCK // CMP_STRIDE)

    @pl.when(i == 0)
    def _():
        sc_ref[...] = jnp.zeros_like(sc_ref)
        ps_ref[...] = jnp.zeros_like(ps_ref)

    def col_reduce(op, red, x):
        rows = x.shape[0]
        while rows <= 256 and rows % 16 == 0:
            rows //= 2
            x = op(x[:rows], x[rows:])
        return red(x, axis=0, keepdims=True)

    def colmax(x):
        return col_reduce(jnp.maximum, jnp.max, x)

    def tile4(x):
        return jnp.concatenate([x] * NSA_HPG, axis=1)

    gates = ga_ref[...]
    tl = i // QB_PER_CMP_TILE
    sub = i % QB_PER_CMP_TILE
    tp = tl - 1
    groups = range(NSA_KV_GROUPS)
    q_t_g = [jnp.concatenate([q_ref[NSA_HPG * g + j] for j in range(NSA_HPG)], axis=1) for g in groups]
    zeros = jnp.zeros((dh, QW), BF16)
    wq_g = [jnp.concatenate([q_t_g[g] if k == g else zeros for k in groups], axis=0) for g in groups]

    def cmp_scores(g, t):
        return _dot(kc_ref[pl.ds(pl.multiple_of(t * CMP_TILE, CMP_TILE), CMP_TILE), :], wq_g[g])

    def sc_rows(t):
        return pl.ds(pl.multiple_of(SCR_PAD + t * CMP_TILE, 8), CMP_TILE)

    def cmp_far(t, m):
        out = []
        for g in groups:
            s = cmp_scores(g, t)
            sc_ref[g, sc_rows(t), :] = s
            out.append(jnp.maximum(m[g], colmax(s)))
        return tuple(out)

    m_c = lax.fori_loop(0, jnp.maximum(tp, 0), cmp_far, tuple(jnp.full((1, QW), NEG, F32) for _ in groups))
    tpc = jnp.maximum(tp, 0)
    off = CMP_TILE - CMP_TOK_PER_QB - CMP_TOK_PER_QB * sub
    m_c = list(m_c)
    for g in groups:
        s = (cmp_scores(g, tpc) + tct_ref[g, pl.ds(pl.multiple_of(off, 8), CMP_TILE), :]
             + jnp.where(tp < 0, NEG, 0.0))
        sc_ref[g, sc_rows(tpc), :] = s
        m_c[g] = jnp.maximum(m_c[g], colmax(s))
    for g in groups:
        s = cmp_scores(g, tl) + tct_ref[g, pl.ds(pl.multiple_of(off + CMP_TILE, 8), CMP_TILE), :]
        sc_ref[g, sc_rows(tl), :] = s
        m_c[g] = jnp.maximum(m_c[g], colmax(s))

    def cmp_exp(t, carry):
        out = []
        for g in groups:
            l, acc = carry[g]
            e = jnp.exp(sc_ref[g, sc_rows(t), :] - m_c[g])
            sc_ref[g, sc_rows(t), :] = e
            vt = vct_ref[g, :, pl.ds(pl.multiple_of(t * CMP_TILE, CMP_TILE), CMP_TILE)]
            out.append((l + col_reduce(jnp.add, jnp.sum, e), acc + _dot(vt, e.astype(BF16))))
        return tuple(out)

    cmp_out = lax.fori_loop(0, tl + 1, cmp_exp,
                            tuple((jnp.zeros((1, QW), F32), jnp.zeros((dh, QW), F32)) for _ in groups))
    inv_c = [jnp.where(m_c[g] > 0.5 * NEG, 1.0 / jnp.maximum(cmp_out[g][0], 1e-30), 0.0) for g in groups]
    o_c_g = [cmp_out[g][1] * inv_c[g] for g in groups]

    def cmp_psum(t, _):
        for g in groups:
            p = sc_ref[g, sc_rows(t), :] * inv_c[g]
            ps = p[:, 0:Q_BLOCK]
            for j in range(1, NSA_HPG):
                ps = ps + p[:, j * Q_BLOCK:(j + 1) * Q_BLOCK]
            ps_ref[g, sc_rows(t), :] = ps.astype(BF16).astype(F32)
        return 0

    lax.fori_loop(0, tl + 1, cmp_psum, 0)
    per_blk = SEL_BLOCK // CMP_STRIDE
    imp_g = []
    for g in groups:
        imp = ps_ref[g, pl.ds(SCR_PAD - 1, nb, stride=per_blk), :]
        for k in range(1, CMP_LEN // CMP_STRIDE + per_blk - 1):
            imp = imp + ps_ref[g, pl.ds(SCR_PAD - 1 + k, nb, stride=per_blk), :]
        imp_g.append(imp)

    kw0 = pl.multiple_of(jnp.maximum(i * Q_BLOCK - WINDOW, 0), Q_BLOCK)
    tw0 = pl.multiple_of(jnp.maximum(WINDOW - i * Q_BLOCK, 0), Q_BLOCK)
    o_w_g = []
    for g in groups:
        s = _dot(kwin_ref[pl.ds(kw0, WIN_KEYS), :], wq_g[g]) + twt_ref[g, pl.ds(tw0, WIN_KEYS), :]
        e = jnp.exp(s - colmax(s)).astype(BF16)
        acc_w = _dot(vwint_ref[g, :, pl.ds(kw0, WIN_KEYS)], e)
        o_w_g.append(acc_w[0:dh] * (1.0 / jnp.maximum(acc_w[dh:dh + 1], 1e-30)))

    n_io = lax.broadcasted_iota(jnp.int32, (nb, Q_BLOCK), 0)
    r_io = lax.broadcasted_iota(jnp.int32, (nb, Q_BLOCK), 1)
    back = (i * Q_BLOCK + r_io) // SEL_BLOCK - n_io
    valid = back >= 0
    forced = (n_io == 0) | (valid & (back < N_LOCAL_SEL))
    work_g = [jnp.where(forced, -jnp.inf, jnp.where(valid, imp_g[g], -1e9)) for g in groups]
    n_f = n_io.astype(F32)
    def select_rounds(rows):
        def run(*work):
            work = [w[:rows] for w in work]
            for _ in range(SEL_TOP_N - 1 - N_LOCAL_SEL):
                for g in groups:
                    mx = colmax(work[g])
                    first = col_reduce(jnp.minimum, jnp.min, jnp.where(work[g] == mx, n_f[:rows], float(nb)))
                    work[g] = jnp.where(n_f[:rows] == first, -jnp.inf, work[g])
            rest = [jnp.zeros((nb - rows, Q_BLOCK), F32)] if rows < nb else []
            return tuple(jnp.concatenate([jnp.where(w == -jnp.inf, 1.0, 0.0)] + rest, axis=0) for w in work)
        return run

    levels = [SEL_ROWS_STEP * (k + 1) for k in range(nb // SEL_ROWS_STEP)]
    level = jnp.minimum((i * Q_BLOCK + Q_BLOCK - 1) // (SEL_BLOCK * SEL_ROWS_STEP), len(levels) - 1)
    sel_g = list(lax.switch(level, [select_rounds(rows) for rows in levels], *work_g))

    far_blocks = (i - 1) * (Q_BLOCK // SEL_BLOCK)
    selm_g = [(jnp.where(n_io < far_blocks, sel_g[g], 0.0) - 1.0).astype(BF16) for g in groups]

    far_end = (i - 1) * Q_BLOCK
    n_far = (jnp.maximum(far_end, 0) + SEL_TK - 1) // SEL_TK

    def flash_step(carry, s, vt):
        m_i, acc = carry
        m_new = jnp.maximum(m_i, colmax(s))
        alpha = jnp.exp(m_i - m_new)
        p = jnp.exp(s - m_new).astype(BF16)
        return m_new, alpha * acc + _dot(vt, p)

    carry = tuple((jnp.full((1, QW), NEG, F32), jnp.zeros((PV_ROWS, QW), F32)) for _ in groups)
    for w in range(nb // SEL_WIN):
        w_far = [jnp.concatenate([tile4(selm_g[g][w * SEL_WIN:(w + 1) * SEL_WIN]), wq_g[g]], axis=0)
                 for g in groups]

        lo = w * SEL_WIN_TILES
        last = min((w + 1) * SEL_WIN_TILES, n_keys // SEL_TK) - 1
        trips = jnp.maximum(jnp.minimum(n_far, last + 1) - lo + 1, 0) // 2

        def far_scores(kt, s_ref, mx_ref, g, w=w, w_far=w_far):
            k0 = pl.multiple_of(kt * SEL_TK, SEL_TK)
            e0 = pl.multiple_of((kt - w * SEL_WIN_TILES) * SEL_TK, SEL_TK)
            lhs = jnp.concatenate([et_ref[pl.ds(e0, SEL_TK), :],
                                   ksel_ref[pl.ds(k0, SEL_TK), :]], axis=1)
            s = _dot(lhs, w_far[g])
            s_ref[g] = s
            mx_ref[g] = colmax(s)

        def far_update(kt, s_ref, mx_ref, g, carry_g):
            k0 = pl.multiple_of(kt * SEL_TK, SEL_TK)
            m_i, acc = carry_g
            m_new = jnp.maximum(m_i, mx_ref[g])
            p = jnp.exp(s_ref[g] - m_new).astype(BF16)
            return m_new, jnp.exp(m_i - m_new) * acc + _dot(vselt_ref[g, :, pl.ds(k0, SEL_TK)], p)

        def far_trip(tiles, first, last=last, far_scores=far_scores):
            def body(u, carry):
                kt = first + tiles * u
                carry = list(carry)
                bufs = [(sa_ref, ma_ref), (sb_ref, mb_ref)]
                for t in range(tiles):
                    for g in groups:
                        far_scores(jnp.minimum(kt + t + 1, last), *bufs[(t + 1) % 2], g)
                        carry[g] = far_update(kt + t, *bufs[t % 2], g, carry[g])
                return tuple(carry)
            return body

        @pl.when(trips > 0)
        def _(lo=lo, far_scores=far_scores):
            for g in groups:
                far_scores(jnp.int32(lo), sa_ref, ma_ref, g)

        long_trips = trips // (FAR_LONG_TILES // 2)
        carry = lax.fori_loop(0, long_trips, far_trip(FAR_LONG_TILES, lo), carry)
        carry = lax.fori_loop(0, trips - long_trips * (FAR_LONG_TILES // 2),
                              far_trip(2, lo + FAR_LONG_TILES * long_trips), carry)

    heads_out = []
    for g in groups:
        wq, o_c, sel = wq_g[g], o_c_g[g], sel_g[g]
        k0n = pl.multiple_of(jnp.maximum(far_end, 0), Q_BLOCK)
        t0n = pl.multiple_of(jnp.maximum(-far_end, 0), Q_BLOCK)
        near_blocks = 2 * Q_BLOCK // SEL_BLOCK
        pb = lax.broadcasted_iota(jnp.int32, (16, nb), 0)
        pn = lax.broadcasted_iota(jnp.int32, (16, nb), 1)
        pick_rows = jnp.where((pn == jnp.maximum(far_blocks, 0) + pb) & (pb < near_blocks), 1.0, 0.0).astype(BF16)
        sel_rows = _dot(pick_rows, sel.astype(BF16))
        add_near = jnp.concatenate(
            [jnp.broadcast_to((sel_rows[b:b + 1, :] - 1.0) * (-NEG), (SEL_BLOCK, Q_BLOCK)) for b in range(near_blocks)],
            axis=0)
        s = (_dot(ksel_ref[pl.ds(k0n, 2 * Q_BLOCK), :], wq) + tnt_ref[g, pl.ds(t0n, 2 * Q_BLOCK), :]
             + tile4(add_near))
        m_s, acc_s = flash_step(carry[g], s, vselt_ref[g, :, pl.ds(k0n, 2 * Q_BLOCK)])
        o_s = acc_s[0:dh] * (1.0 / jnp.maximum(acc_s[dh:dh + 1], 1e-30))

        o_w = o_w_g[g]

        for j in range(NSA_HPG):
            hd = NSA_HPG * g + j
            cols = slice(j * Q_BLOCK, (j + 1) * Q_BLOCK)
            heads_out.append(gates[hd:hd + 1] * o_c[:, cols]
                             + gates[NSA_HEADS + hd:NSA_HEADS + hd + 1] * o_s[:, cols]
                             + gates[2 * NSA_HEADS + hd:2 * NSA_HEADS + hd + 1] * o_w[:, cols])

    out_ref[...] = jnp.concatenate(heads_out, axis=0).T.astype(out_ref.dtype)


def _nsa_attention(q_t, gates_t, kc, vct, tct, ksel, vselt, et, tnt, kwin, vwint, twt):
    H, dh, S = q_t.shape
    n_qb = S // Q_BLOCK
    n_tok = S // CMP_STRIDE
    nb = -(-(S // SEL_BLOCK) // SEL_WIN) * SEL_WIN
    ps_rows = nb * (SEL_BLOCK // CMP_STRIDE) + 2 * SCR_PAD
    full = lambda a: _const_spec(a.shape)
    return pl.pallas_call(
        functools.partial(_nsa_kernel, S),
        grid=(n_qb,),
        in_specs=[pl.BlockSpec((H, dh, Q_BLOCK), lambda i: (0, 0, i)),
                  pl.BlockSpec((gates_t.shape[0], Q_BLOCK), lambda i: (0, i)),
                  full(kc), full(vct), full(tct), full(ksel), full(vselt), full(et), full(tnt),
                  full(kwin), full(vwint), full(twt)],
        out_specs=pl.BlockSpec((Q_BLOCK, H * dh), lambda i: (i, 0)),
        out_shape=jax.ShapeDtypeStruct((S, H * dh), BF16),
        scratch_shapes=[pltpu.VMEM((NSA_KV_GROUPS, SCR_PAD + n_tok, QW), F32),
                        pltpu.VMEM((NSA_KV_GROUPS, ps_rows, Q_BLOCK), F32),
                        pltpu.VMEM((NSA_KV_GROUPS, SEL_TK, QW), F32),
                        pltpu.VMEM((NSA_KV_GROUPS, SEL_TK, QW), F32),
                        pltpu.VMEM((NSA_KV_GROUPS, 1, QW), F32),
                        pltpu.VMEM((NSA_KV_GROUPS, 1, QW), F32)],
        compiler_params=_params("arbitrary"),
        name="nsa_attention",
    )(q_t, gates_t, kc, vct, tct, ksel, vselt, et, tnt, kwin, vwint, twt)


def _retention_kernel(q_ref, k_ref, v_ref, g_ref, nw_ref, dmask_ref, qd_ref, kd_ref, cd_ref, out_ref, state_ref):
    @pl.when(pl.program_id(0) == 0)
    def _():
        state_ref[...] = jnp.zeros_like(state_ref)

    states = [state_ref[hd] for hd in range(RET_HEADS)]
    for c in range(q_ref.shape[0] // RET_CHUNK):
        rows = slice(c * RET_CHUNK, (c + 1) * RET_CHUNK)
        for hd in range(RET_HEADS):
            ks = slice(hd * RET_QK_DIM, (hd + 1) * RET_QK_DIM)
            vs = slice(hd * RET_V_DIM, (hd + 1) * RET_V_DIM)
            q = q_ref[rows, ks]
            k = k_ref[rows, ks]
            v = v_ref[rows, vs]
            state = states[hd]
            inner = _dot_nt(q, k.astype(BF16)) * dmask_ref[hd]
            y = _dot(inner.astype(BF16), v) + _dot(q, state.astype(BF16)) * qd_ref[hd]
            kd = k * kd_ref[hd]
            states[hd] = state * cd_ref[hd] + _dot(kd.T.astype(BF16), v)
            mu = jnp.mean(y, axis=-1, keepdims=True)
            yc = y - mu
            var = jnp.mean(yc * yc, axis=-1, keepdims=True)
            yn = yc * lax.rsqrt(var + EPS) * nw_ref[:, vs]
            out_ref[rows, vs] = (jax.nn.silu(g_ref[rows, vs]) * yn).astype(out_ref.dtype)
    for hd in range(RET_HEADS):
        state_ref[hd] = states[hd]


RET_STEP_CHUNKS = 8


def _retention(q, k, v, g, norm_w, dmask, qd, kd, cd):
    S = q.shape[0]
    C = RET_CHUNK * RET_STEP_CHUNKS
    row = lambda w: pl.BlockSpec((C, w), lambda i: (i, 0))
    full = lambda a: _const_spec(a.shape)
    return pl.pallas_call(
        _retention_kernel,
        grid=(S // C,),
        in_specs=[row(RET_QK_W), row(RET_QK_W), row(RET_V_W), row(RET_V_W),
                  full(norm_w), full(dmask), full(qd), full(kd), full(cd)],
        out_specs=row(RET_V_W),
        out_shape=jax.ShapeDtypeStruct((S, RET_V_W), BF16),
        scratch_shapes=[pltpu.VMEM((RET_HEADS, RET_QK_DIM, RET_V_DIM), F32)],
        compiler_params=_params("arbitrary"),
        name="retention",
    )(q, k, v, g, norm_w, dmask, qd, kd, cd)


MERGE_TM = 512
MERGE_SUB_ROWS = 256
GRP_LANE = 40
RANK_LANE = 41


def _merge_kernel(x_ref, ya_ref, yr_ref, gta_ref, gtr_ref, wa_ref, wr_ref, wo_ref, nw_ref, wrt_ref, brt_ref,
                  tri_ref, x1_ref, h_ref, comb_ref, cnt_ref, carry_ref):
    @pl.when(pl.program_id(0) == 0)
    def _():
        carry_ref[...] = jnp.zeros_like(carry_ref)

    sub = tri_ref.shape[0]
    carry = carry_ref[...]
    for r in range(x_ref.shape[0] // sub):
        carry = _merge_rows(slice(r * sub, (r + 1) * sub), carry, x_ref, ya_ref, yr_ref, gta_ref, gtr_ref,
                            wa_ref, wr_ref, wo_ref, nw_ref, wrt_ref, brt_ref, tri_ref, x1_ref, h_ref, comb_ref)
    carry_ref[...] = carry
    cnt_ref[...] = jnp.broadcast_to(carry, cnt_ref.shape)


def _merge_rows(rows, carry, x_ref, ya_ref, yr_ref, gta_ref, gtr_ref, wa_ref, wr_ref, wo_ref, nw_ref, wrt_ref,
                brt_ref, tri_ref, x1_ref, h_ref, comb_ref):
    merged = gta_ref[rows] * _dot(ya_ref[rows], wa_ref[...]) + gtr_ref[rows] * _dot(yr_ref[rows], wr_ref[...])
    x1 = x_ref[rows] + _dot(merged.astype(BF16), wo_ref[...])
    x1_ref[rows] = x1
    h = (x1 * lax.rsqrt(jnp.mean(x1 * x1, axis=-1, keepdims=True) + EPS) * nw_ref[...]).astype(BF16)
    h_ref[rows] = _pack_bf16_pairs(h.astype(F32))
    logits = _dot(h, wrt_ref[...]) + brt_ref[...]
    lane = lax.broadcasted_iota(jnp.int32, logits.shape, 1)
    lane_f = lane.astype(F32)
    big = float(LANE)
    is_grp = (lane >= N_EXPERTS) & (lane < N_EXPERTS + N_GROUPS)
    lg = jnp.where(is_grp, logits, -jnp.inf)
    eg = jnp.where(is_grp, jnp.exp(lg - jnp.max(lg, axis=-1, keepdims=True)), 0.0)
    pg = eg / jnp.sum(eg, axis=-1, keepdims=True)
    p_grp = jnp.max(pg, axis=-1, keepdims=True)
    grp = jnp.min(jnp.where(is_grp & (pg == p_grp), lane_f, big), axis=-1, keepdims=True) - N_EXPERTS
    in_grp = (lane_f >= grp * EXPERTS_PER_GROUP) & (lane_f < (grp + 1.0) * EXPERTS_PER_GROUP)
    le = jnp.where(in_grp, logits, -jnp.inf)
    m1 = jnp.max(le, axis=-1, keepdims=True)
    i1 = jnp.min(jnp.where(le == m1, lane_f, big), axis=-1, keepdims=True)
    le2 = jnp.where(lane_f == i1, -jnp.inf, le)
    m2 = jnp.max(le2, axis=-1, keepdims=True)
    i2 = jnp.min(jnp.where(le2 == m2, lane_f, big), axis=-1, keepdims=True)
    e2 = jnp.exp(m2 - m1)
    den = 1.0 + e2
    onehot = jnp.where(lane_f == grp, 1.0, 0.0)
    before = _dot(tri_ref[...], onehot.astype(BF16)) + carry
    rank = jnp.sum(onehot * before, axis=-1, keepdims=True)
    comb_ref[rows] = (jnp.where(lane_f == i1, (1.0 / den) * p_grp, 0.0)
                      + jnp.where(lane_f == i2, (e2 / den) * p_grp, 0.0)
                      + jnp.where(lane == GRP_LANE, grp, 0.0)
                      + jnp.where(lane == RANK_LANE, rank, 0.0))
    return carry + jnp.sum(onehot, axis=0, keepdims=True)


def _pack_bf16_pairs(x):
    n = x.shape[1] // 2
    return pltpu.pack_elementwise([x[:, :n], x[:, n:]], packed_dtype=BF16)


def _unpack_bf16_pairs(w):
    return tuple(pltpu.unpack_elementwise(w, index=k, packed_dtype=BF16, unpacked_dtype=F32) for k in range(2))


def _merge(x2d, ya, yr, gta, gtr, wa, wr, wo, nw, wrt, brt):
    S = x2d.shape[0]
    tm = MERGE_TM
    row = lambda w: pl.BlockSpec((tm, w), lambda i: (i, 0))
    full = lambda a: _const_spec(a.shape)
    sub = MERGE_SUB_ROWS
    tri = jnp.asarray(np.tril(np.ones((sub, sub), np.float32), -1), BF16)
    return pl.pallas_call(
        _merge_kernel,
        grid=(S // tm,),
        in_specs=[row(D_MODEL), row(NSA_Q_W), row(RET_V_W), row(D_MODEL), row(D_MODEL),
                  full(wa), full(wr), full(wo), full(nw), full(wrt), full(brt), full(tri)],
        out_specs=[row(D_MODEL), row(D_MODEL // 2), row(LANE), pl.BlockSpec((8, LANE), lambda i: (0, 0))],
        out_shape=[jax.ShapeDtypeStruct((S, D_MODEL), F32), jax.ShapeDtypeStruct((S, D_MODEL // 2), jnp.uint32),
                   jax.ShapeDtypeStruct((S, LANE), F32), jax.ShapeDtypeStruct((8, LANE), F32)],
        scratch_shapes=[pltpu.VMEM((1, LANE), F32)],
        compiler_params=_params("arbitrary"),
        name="merge_router",
    )(x2d, ya, yr, gta, gtr, wa, wr, wo, nw, wrt, brt, tri)


MOE_TM = 1024
MOE_SUB_TILES = 2
RANK_TM = 512
HALF = D_MODEL // 2


def _permute_kernel(pos_ref, h_ref, comb_ref, hs_ref, cs_ref):
    i = pl.program_id(0)

    @pl.when(i == 0)
    def _():
        hs_ref[...] = jnp.zeros_like(hs_ref)
        cs_ref[...] = jnp.zeros_like(cs_ref)

    tm = h_ref.shape[0]

    def body(r, _):
        d = pos_ref[i * tm + r]
        hs_ref[pl.ds(d, 1), :] = h_ref[pl.ds(r, 1), :]
        cs_ref[pl.ds(d, 1), :] = comb_ref[pl.ds(r, 1), :]
        return 0

    lax.fori_loop(0, tm, body, 0, unroll=8)


def _permute(pos, hu, comb, n_sorted):
    S = hu.shape[0]
    tm = RANK_TM
    resident = lambda w: pl.BlockSpec((n_sorted, w), lambda i, p: (0, 0), pipeline_mode=pl.Buffered(1))
    return pl.pallas_call(
        _permute_kernel,
        grid_spec=pltpu.PrefetchScalarGridSpec(
            num_scalar_prefetch=1, grid=(S // tm,),
            in_specs=[pl.BlockSpec((tm, HALF), lambda i, p: (i, 0)), pl.BlockSpec((tm, LANE), lambda i, p: (i, 0))],
            out_specs=[resident(HALF), resident(LANE)]),
        out_shape=[jax.ShapeDtypeStruct((n_sorted, HALF), jnp.uint32), jax.ShapeDtypeStruct((n_sorted, LANE), F32)],
        compiler_params=_params("arbitrary"),
        name="moe_permute",
    )(pos, hu, comb)


def _moe_kernel(tg_ref, tv_ref, hs_ref, cs_ref, wg_ref, wu_ref, wd_ref, ys_ref, lo_ref, hi_ref, acc_ref):
    i = pl.program_id(0)
    e = pl.program_id(1)

    @pl.when(e == 0)
    def _():
        lo, hi = _unpack_bf16_pairs(hs_ref[...])
        lo_ref[...] = lo.astype(BF16)
        hi_ref[...] = hi.astype(BF16)
        acc_ref[...] = jnp.zeros_like(acc_ref)

    @pl.when(tv_ref[i] > 0)
    def _():
        wg = wg_ref[0].astype(BF16)
        wu = wu_ref[0].astype(BF16)
        wd = wd_ref[0].astype(BF16)
        sub = lo_ref.shape[0] // MOE_SUB_TILES
        for r in range(MOE_SUB_TILES):
            rows = slice(r * sub, (r + 1) * sub)
            lo, hi = lo_ref[rows], hi_ref[rows]
            gate = _dot(lo, wg[:HALF]) + _dot(hi, wg[HALF:])
            up = _dot(lo, wu[:HALF]) + _dot(hi, wu[HALF:])
            comb = cs_ref[rows]
            lane = lax.broadcasted_iota(jnp.int32, comb.shape, 1)
            c = jnp.sum(jnp.where(lane == tg_ref[i] * EXPERTS_PER_GROUP + e, comb, 0.0), axis=-1, keepdims=True)
            acc_ref[rows] += _dot((jax.nn.silu(gate) * up).astype(BF16), wd) * c

    @pl.when(e == pl.num_programs(1) - 1)
    def _():
        ys_ref[...] = _pack_bf16_pairs(acc_ref[...].astype(BF16).astype(F32))


def _moe(tile_group, tile_valid, hs, cs, wg, wu, wd):
    n_sorted = hs.shape[0]
    tm = MOE_TM
    row = lambda w: pl.BlockSpec((tm, w), lambda i, e, tg, tv: (i, 0))
    wspec = lambda a, b: pl.BlockSpec((1, a, b), lambda i, e, tg, tv: (tg[i] * EXPERTS_PER_GROUP + e, 0, 0))
    return pl.pallas_call(
        _moe_kernel,
        grid_spec=pltpu.PrefetchScalarGridSpec(
            num_scalar_prefetch=2, grid=(n_sorted // tm, EXPERTS_PER_GROUP),
            in_specs=[row(HALF), row(LANE), wspec(D_MODEL, EXPERT_FF), wspec(D_MODEL, EXPERT_FF),
                      wspec(EXPERT_FF, D_MODEL)],
            out_specs=row(HALF),
            scratch_shapes=[pltpu.VMEM((tm, HALF), BF16), pltpu.VMEM((tm, HALF), BF16),
                            pltpu.VMEM((tm, D_MODEL), F32)]),
        out_shape=jax.ShapeDtypeStruct((n_sorted, HALF), jnp.uint32),
        compiler_params=_params("parallel", "arbitrary"),
        name="moe_experts",
    )(tile_group, tile_valid, hs, cs, wg, wu, wd)


def _unpermute_kernel(pos_ref, ys_ref, x1_ref, nw_ref, out_ref, stage_ref):
    i = pl.program_id(0)
    tm = x1_ref.shape[0]

    def body(r, _):
        stage_ref[pl.ds(r, 1), :] = ys_ref[pl.ds(pos_ref[i * tm + r], 1), :]
        return 0

    lax.fori_loop(0, tm, body, 0, unroll=8)
    lo, hi = _unpack_bf16_pairs(stage_ref[...])
    x2 = x1_ref[...] + jnp.concatenate([lo, hi], axis=1)
    out_ref[...] = x2 * lax.rsqrt(jnp.mean(x2 * x2, axis=-1, keepdims=True) + EPS) * nw_ref[...]


def _unpermute_residual_norm(pos, ys, x1, nw):
    S = x1.shape[0]
    tm = RANK_TM
    return pl.pallas_call(
        _unpermute_kernel,
        grid_spec=pltpu.PrefetchScalarGridSpec(
            num_scalar_prefetch=1, grid=(S // tm,),
            in_specs=[pl.BlockSpec(ys.shape, lambda i, p: (0, 0), pipeline_mode=pl.Buffered(1)),
                      pl.BlockSpec((tm, D_MODEL), lambda i, p: (i, 0)),
                      pl.BlockSpec((1, D_MODEL), lambda i, p: (0, 0))],
            out_specs=pl.BlockSpec((tm, D_MODEL), lambda i, p: (i, 0)),
            scratch_shapes=[pltpu.VMEM((tm, HALF), jnp.uint32)]),
        out_shape=jax.ShapeDtypeStruct((S, D_MODEL), F32),
        compiler_params=_params("parallel"),
        name="moe_unpermute_norm",
    )(pos, ys, x1, nw)


def _hierarchical_moe(hu, comb, cnt, x1, wg, wu, wd, final_norm_w):
    S = hu.shape[0]
    tm = MOE_TM
    n_sorted = S + N_GROUPS * tm
    rank = comb[:, RANK_LANE:RANK_LANE + 1]
    grp = comb[:, GRP_LANE].astype(jnp.int32)
    counts = cnt[0, :N_GROUPS].astype(jnp.int32)
    padded = (counts + tm - 1) // tm * tm
    ends = jnp.cumsum(padded)
    starts = ends - padded
    pos = rank[:, 0].astype(jnp.int32) + jnp.sum(
        jnp.where(grp[:, None] == jnp.arange(N_GROUPS)[None, :], starts[None, :], 0), axis=1)
    tile_start = jnp.arange(n_sorted // tm, dtype=jnp.int32) * tm
    tile_group = jnp.minimum(jnp.sum((tile_start[:, None] >= ends[None, :]).astype(jnp.int32), axis=1), N_GROUPS - 1)
    tile_valid = (tile_start < ends[N_GROUPS - 1]).astype(jnp.int32)
    hs, cs = _permute(pos, hu, comb, n_sorted)
    ys = _moe(tile_group, tile_valid, hs, cs, wg, wu, wd)
    return _unpermute_residual_norm(pos, ys, x1, final_norm_w)


def _positional_tables():
    half = RET_QK_DIM // 2
    inv = ROPE_BASE ** (-jnp.arange(half, dtype=F32) / half)
    inv2 = jnp.concatenate([inv, inv]).reshape(1, RET_QK_DIM)
    C, H = RET_CHUNK, RET_HEADS
    log_gamma = jnp.log(1.0 - 2.0 ** (-5.0 - jnp.arange(H, dtype=F32)))
    idx = jnp.arange(C, dtype=F32)
    diff = idx[:, None] - idx[None, :]
    dmask = jnp.where(diff >= 0, jnp.exp(jnp.maximum(diff, 0.0)[None] * log_gamma[:, None, None]), 0.0)
    qd = jnp.exp((idx + 1.0)[None, :] * log_gamma[:, None])
    kd = jnp.exp((C - 1.0 - idx)[None, :] * log_gamma[:, None])
    cd = jnp.exp(C * log_gamma)
    qd = jnp.broadcast_to(qd[:, :, None], (H, C, RET_V_DIM))
    kd = jnp.broadcast_to(kd[:, :, None], (H, C, RET_QK_DIM))
    cd = jnp.broadcast_to(cd[:, None, None], (H, 1, RET_V_DIM))
    return inv2, dmask, qd, kd, cd


def _attention_tables(rel_bias):
    r = np.arange(Q_BLOCK)[None, :]
    cc = np.arange(3 * CMP_TILE)[:, None] - CMP_TILE
    d = r - CMP_STRIDE * (cc - (CMP_TILE - CMP_TOK_PER_QB)) - (CMP_LEN - 1)
    d = np.where(cc < 0, -2, np.where((cc >= CMP_TILE) | (d < 0), -1, d))
    tct = _bias_table_t(rel_bias, d, shift=True)
    d = r + Q_BLOCK - np.arange(3 * Q_BLOCK)[:, None]
    tnt = _bias_table_t(rel_bias, np.where(d < 0, -1, d), shift=True)
    d = r + WINDOW - np.arange(WIN_KEYS + WINDOW)[:, None]
    twt = _bias_table_t(rel_bias, np.where((d < 0) | (d >= WINDOW), -1, d), shift=False)
    kk = np.arange(SEL_WIN * SEL_BLOCK)[:, None] // SEL_BLOCK
    et = jnp.where(jnp.asarray(kk == np.arange(SEL_WIN)[None, :]), -NEG, 0.0).astype(BF16)
    return tct, tnt, twt, et


def _layer(x2d, rel_bias, norm_mix, w_in, cmp_pos_k, cmp_k_w1, cmp_k_w2, cmp_pos_v, cmp_v_w1, cmp_v_w2,
           ret_norm, w_nsa_up, w_ret_up, w_out, norm_ffn, w_rg, b_rg, w_re, b_re, moe_wg, moe_wu, moe_wd,
           final_norm_w):
    S = x2d.shape[0]
    G, dh = NSA_KV_GROUPS, NSA_HEAD_DIM
    assert S % (CMP_TILE * CMP_STRIDE) == 0 and S // SEL_BLOCK >= SEL_TOP_N

    o = np.cumsum((0,) + (NSA_Q_W, 6 * NSA_KV_W, 3 * NSA_HEADS, RET_QK_W, RET_QK_W, RET_V_W, RET_V_W,
                          D_MODEL, D_MODEL))
    seg = lambda n: w_in[:, o[n]:o[n + 1]]
    w_cat = jnp.concatenate([seg(0), seg(1), seg(3), seg(4), seg(5), seg(6), seg(7), seg(8), seg(2),
                             jnp.zeros((D_MODEL, LANE - 3 * NSA_HEADS), F32)], axis=1).astype(BF16)
    inv2, dmask, qd, kd, cd = _positional_tables()

    (qa_t, kcmp, vcmp, ksel, vsel_t, kwin, vwin_t, ga_t, qr, kr, vr, gr, gta, gtr) = _input_projection(
        x2d, norm_mix.reshape(1, D_MODEL), w_cat, inv2)

    nch = S // CMP_STRIDE
    kc = _compress(kcmp, cmp_pos_k, cmp_k_w1, cmp_k_w2).astype(BF16)
    vct = _compress(vcmp, cmp_pos_v, cmp_v_w1, cmp_v_w2).T.reshape(G, dh, nch).astype(BF16)

    tct, tnt, twt, et = _attention_tables(rel_bias)
    ya = _nsa_attention(qa_t.reshape(NSA_HEADS, dh, S), ga_t, kc, vct, tct,
                        ksel, vsel_t.reshape(G, PV_ROWS, S), et, tnt,
                        kwin, vwin_t.reshape(G, PV_ROWS, S), twt)

    yr = _retention(qr, kr, vr, gr, ret_norm.reshape(1, RET_V_W), dmask, qd, kd, cd)

    wrt = jnp.concatenate([w_re, w_rg, jnp.zeros((D_MODEL, LANE - N_EXPERTS - N_GROUPS), F32)], axis=1).astype(BF16)
    brt = jnp.concatenate([b_re, b_rg, jnp.zeros((LANE - N_EXPERTS - N_GROUPS,), F32)]).reshape(1, LANE)
    x1, h2, comb, cnt = _merge(x2d, ya, yr, gta, gtr, w_nsa_up.astype(BF16), w_ret_up.astype(BF16),
                               w_out.astype(BF16), norm_ffn.reshape(1, D_MODEL), wrt, brt)
    return _hierarchical_moe(h2, comb, cnt, x1, moe_wg, moe_wu, moe_wd, final_norm_w)


def kernel(x, rel_bias, norm_mix, w_in, cmp_pos_k, cmp_k_w1, cmp_k_w2, cmp_pos_v, cmp_v_w1, cmp_v_w2, ret_norm, w_nsa_up, w_ret_up, w_out, norm_ffn, w_router_group, b_router_group, w_router_expert, b_router_expert, moe_w_gate, moe_w_up, moe_w_down, norm_final):
    B, S, D = x.shape
    assert B == 1 and norm_mix.shape[0] == 1, "single sequence, depth 1"
    out = _layer(x[0], rel_bias, norm_mix[0], w_in[0], cmp_pos_k[0], cmp_k_w1[0], cmp_k_w2[0],
                 cmp_pos_v[0], cmp_v_w1[0], cmp_v_w2[0], ret_norm[0], w_nsa_up[0], w_ret_up[0], w_out[0],
                 norm_ffn[0], w_router_group[0], b_router_group[0], w_router_expert[0], b_router_expert[0],
                 moe_w_gate[0], moe_w_up[0], moe_w_down[0], norm_final.reshape(1, D))
    return out[None]
```

```python
import functools
import math

import numpy as np
import jax
import jax.numpy as jnp
from jax import lax
from jax.experimental import pallas as pl
from jax.experimental.pallas import tpu as pltpu

F32 = jnp.float32
BF16 = jnp.bfloat16

D_MODEL = 1024
NSA_HEADS = 8
NSA_KV_GROUPS = 2
NSA_HPG = NSA_HEADS // NSA_KV_GROUPS
NSA_HEAD_DIM = 64
CMP_LEN = 32
CMP_STRIDE = 16
CMP_HIDDEN = 256
SEL_BLOCK = 64
SEL_TOP_N = 16
N_LOCAL_SEL = 2
WINDOW = 512
Q_BLOCK = 128
RET_HEADS = 4
RET_QK_DIM = 128
RET_V_DIM = 256
RET_CHUNK = 128
ROPE_BASE = 10000.0
REL_BUCKETS = 32
REL_MAX_DIST = 128
N_GROUPS = 4
EXPERTS_PER_GROUP = 8
N_EXPERTS = N_GROUPS * EXPERTS_PER_GROUP
EXPERT_FF = 256
EPS = 1e-6

NSA_Q_W = NSA_HEADS * NSA_HEAD_DIM
NSA_KV_W = NSA_KV_GROUPS * NSA_HEAD_DIM
RET_QK_W = RET_HEADS * RET_QK_DIM
RET_V_W = RET_HEADS * RET_V_DIM

LANE = 128
NEG = -1e30
VMEM_LIMIT = 56 * 1024 * 1024

CMP_TILE = 128
QB_PER_CMP_TILE = CMP_TILE * CMP_STRIDE // Q_BLOCK
CMP_TOK_PER_QB = Q_BLOCK // CMP_STRIDE
SCR_PAD = 8
SEL_TK = 512
SEL_WIN = 128
SEL_WIN_TILES = SEL_WIN * SEL_BLOCK // SEL_TK
FAR_TRIP_TILES = (8, 4, 2)
SEL_ROWS_STEP = 64
PV_ROWS = 80
WIN_KEYS = WINDOW + Q_BLOCK
QW = NSA_HPG * Q_BLOCK


def _dot(a, b):
    return jnp.dot(a, b, preferred_element_type=F32)


def _dot_nt(a, b):
    return lax.dot_general(a, b, (((1,), (1,)), ((), ())), preferred_element_type=F32)


def _const_spec(shape):
    nd = len(shape)
    return pl.BlockSpec(shape, lambda *_: (0,) * nd, pipeline_mode=pl.Buffered(1))


def _params(*sem):
    return pltpu.CompilerParams(dimension_semantics=sem, vmem_limit_bytes=VMEM_LIMIT)


_C_QA = 0
_C_KV = _C_QA + NSA_Q_W
_C_QR = _C_KV + 6 * NSA_KV_W
_C_KR = _C_QR + RET_QK_W
_C_VR = _C_KR + RET_QK_W
_C_GR = _C_VR + RET_V_W
_C_GTA = _C_GR + RET_V_W
_C_GTR = _C_GTA + D_MODEL
_C_GA = _C_GTR + D_MODEL
_C_END = _C_GA + LANE
IN_TM = 512


def _inproj_kernel(x_ref, nw_ref, w_ref, inv_ref,
                   qat_ref, kcmp_ref, vcmp_ref, ksel_ref, vselt_ref, kwin_ref, vwint_ref, gat_ref,
                   qr_ref, kr_ref, vr_ref, gr_ref, gta_ref, gtr_ref):
    tm = x_ref.shape[0]
    x = x_ref[...]
    h = x * lax.rsqrt(jnp.mean(x * x, axis=-1, keepdims=True) + EPS) * nw_ref[...]
    hb = h.astype(BF16)

    def proj(a, width):
        return _dot(hb, w_ref[:, a:a + width])

    def values_t(v):
        vt = v.T.astype(BF16)
        aux = jnp.where(lax.broadcasted_iota(jnp.int32, (PV_ROWS - NSA_HEAD_DIM, tm), 0) == 0, 1.0, 0.0).astype(BF16)
        parts = []
        for g in range(NSA_KV_GROUPS):
            parts += [vt[g * NSA_HEAD_DIM:(g + 1) * NSA_HEAD_DIM], aux]
        return jnp.concatenate(parts, axis=0)

    qat_ref[...] = (proj(_C_QA, NSA_Q_W) * (NSA_HEAD_DIM ** -0.5)).T.astype(BF16)
    kv = proj(_C_KV, 6 * NSA_KV_W)
    kcmp_ref[...] = kv[:, 0 * LANE:1 * LANE]
    vcmp_ref[...] = kv[:, 1 * LANE:2 * LANE]
    ksel_ref[...] = kv[:, 2 * LANE:3 * LANE].astype(BF16)
    vselt_ref[...] = values_t(kv[:, 3 * LANE:4 * LANE])
    kwin_ref[...] = kv[:, 4 * LANE:5 * LANE].astype(BF16)
    vwint_ref[...] = values_t(kv[:, 5 * LANE:6 * LANE])
    gat_ref[...] = jax.nn.sigmoid(proj(_C_GA, LANE)).T

    hm = tm // 2
    half = RET_QK_DIM // 2
    low = lax.broadcasted_iota(jnp.int32, (hm, RET_QK_DIM), 1) < half
    row = pl.program_id(0) * tm + lax.broadcasted_iota(jnp.int32, (hm, RET_QK_DIM), 0)
    ang = jnp.where(low, row, row + hm).astype(F32) * inv_ref[...]
    c, s = jnp.cos(ang), jnp.sin(ang)
    c_sw, s_sw = pltpu.roll(c, half, 1), pltpu.roll(s, half, 1)
    cos2 = jnp.concatenate([jnp.where(low, c, c_sw), jnp.where(low, c_sw, c)], axis=0)
    sin2 = jnp.concatenate([jnp.where(low, -s, s_sw), jnp.where(low, -s_sw, s)], axis=0)
    qr = proj(_C_QR, RET_QK_W)
    kr = proj(_C_KR, RET_QK_W)
    for hd in range(RET_HEADS):
        sl = slice(hd * RET_QK_DIM, (hd + 1) * RET_QK_DIM)
        qh = qr[:, sl]
        kh = kr[:, sl]
        qr_ref[:, sl] = (qh * cos2 + pltpu.roll(qh, RET_QK_DIM // 2, 1) * sin2).astype(BF16)
        kr_ref[:, sl] = (kh * cos2 + pltpu.roll(kh, RET_QK_DIM // 2, 1) * sin2) * (RET_QK_DIM ** -0.5)
    vr_ref[...] = proj(_C_VR, RET_V_W).astype(BF16)
    gr_ref[...] = proj(_C_GR, RET_V_W)
    gta_ref[...] = jax.nn.sigmoid(proj(_C_GTA, D_MODEL))
    gtr_ref[...] = jax.nn.sigmoid(proj(_C_GTR, D_MODEL))


def _input_projection(x2d, norm_w, w_cat, inv2):
    S = x2d.shape[0]
    tm = IN_TM
    vt_rows = NSA_KV_GROUPS * PV_ROWS
    outs = [
        (NSA_Q_W, BF16, True),
        (LANE, F32, False), (LANE, F32, False),
        (LANE, BF16, False), (vt_rows, BF16, True),
        (LANE, BF16, False), (vt_rows, BF16, True),
        (LANE, F32, True),
        (RET_QK_W, BF16, False), (RET_QK_W, F32, False),
        (RET_V_W, BF16, False), (RET_V_W, F32, False),
        (D_MODEL, F32, False), (D_MODEL, F32, False),
    ]
    row = lambda w: pl.BlockSpec((tm, w), lambda i: (i, 0))
    col = lambda w: pl.BlockSpec((w, tm), lambda i: (0, i))
    return pl.pallas_call(
        _inproj_kernel,
        grid=(S // tm,),
        in_specs=[row(D_MODEL), _const_spec((1, D_MODEL)), _const_spec((D_MODEL, _C_END)),
                  _const_spec((1, RET_QK_DIM))],
        out_specs=[col(w) if t else row(w) for w, _, t in outs],
        out_shape=[jax.ShapeDtypeStruct((w, S) if t else (S, w), dt) for w, dt, t in outs],
        compiler_params=_params("parallel"),
        name="in_proj",
    )(x2d, norm_w, w_cat, inv2)


def _compress_kernel(x_ref, pos_ref, w1_ref, w2_ref, out_ref, hid_ref):
    n_tok = out_ref.shape[0]
    hid_ref[...] = jnp.zeros_like(hid_ref)

    def body(l, _):
        rows = x_ref[pl.ds(l, n_tok, stride=CMP_STRIDE), :] + pos_ref[pl.ds(l, 1), :]
        hid_ref[...] += _dot(rows.astype(BF16), w1_ref[l])
        return 0

    lax.fori_loop(0, CMP_LEN, body, 0)
    out_ref[...] = _dot(jax.nn.gelu(hid_ref[...]).astype(BF16), w2_ref[...])


def _compress(x, pos, w1, w2):
    S, width = x.shape
    n_tok = S // CMP_STRIDE
    G = NSA_KV_GROUPS
    def block_diag(w):
        w = w.astype(BF16)
        z = jnp.zeros_like(w)
        return jnp.concatenate([jnp.concatenate([w if k == g else z for k in range(G)], axis=-1)
                                for g in range(G)], axis=-2)

    w1_bd = block_diag(w1.reshape(CMP_LEN, NSA_HEAD_DIM, CMP_HIDDEN))
    w2_bd = block_diag(w2)
    x_pad = jnp.pad(x, ((0, CMP_LEN - CMP_STRIDE), (0, 0)))
    full = lambda a: _const_spec(a.shape)
    pos2 = jnp.concatenate([pos] * G, axis=1)
    return pl.pallas_call(
        _compress_kernel,
        grid=(1,),
        in_specs=[full(x_pad), full(pos2), full(w1_bd), full(w2_bd)],
        out_specs=pl.BlockSpec((n_tok, width), lambda i: (0, 0)),
        out_shape=jax.ShapeDtypeStruct((n_tok, width), F32),
        scratch_shapes=[pltpu.VMEM((n_tok, G * CMP_HIDDEN), F32)],
        compiler_params=_params("arbitrary"),
        name="nsa_compress",
    )(x_pad, pos2, w1_bd, w2_bd)


def _t5_bucket_np(dist):
    dist = np.maximum(dist, 0)
    max_exact = REL_BUCKETS // 2
    d_f = np.maximum(dist, 1).astype(np.float32)
    large = max_exact + (np.log(d_f / np.float32(max_exact)) / np.float32(math.log(REL_MAX_DIST / max_exact))
                         * np.float32(REL_BUCKETS - max_exact)).astype(np.int32)
    large = np.minimum(large, REL_BUCKETS - 1)
    return np.where(dist < max_exact, dist, large)


def _bias_table_t(rel_bias, dist, shift):
    bucket = jnp.asarray(np.where(dist >= 0, _t5_bucket_np(dist), -1), jnp.int32)
    rb = rel_bias - rel_bias[REL_BUCKETS - 1][None, :] if shift else rel_bias
    onehot = (bucket[None] == jnp.arange(REL_BUCKETS, dtype=jnp.int32)[:, None, None]).astype(F32)
    tab = jnp.einsum('bh,bkq->hkq', rb, onehot, precision=lax.Precision.HIGHEST)
    tab = jnp.where(jnp.asarray(dist == -1)[None], NEG, tab)
    K, Q = dist.shape
    return tab.reshape(NSA_KV_GROUPS, NSA_HPG, K, Q).transpose(0, 2, 1, 3).reshape(NSA_KV_GROUPS, K, NSA_HPG * Q)


def _nsa_kernel(n_keys, q_ref, ga_ref, kc_ref, vct_ref, tct_ref, ksel_ref, vselt_ref, et_ref, tnt_ref,
                kwin_ref, vwint_ref, twt_ref, out_ref, sc_ref, ps_ref, sa_ref, sb_ref, ma_ref, mb_ref):
    i = pl.program_id(0)
    dh = NSA_HEAD_DIM
    nb = (ps_ref.shape[1] - 2 * SCR_PAD) // (SEL_BLOCK // CMP_STRIDE)

    @pl.when(i == 0)
    def _():
        sc_ref[...] = jnp.zeros_like(sc_ref)
        ps_ref[...] = jnp.zeros_like(ps_ref)

    def col_reduce(op, red, x):
        rows = x.shape[0]
        while rows <= 256 and rows % 16 == 0:
            rows //= 2
            x = op(x[:rows], x[rows:])
        return red(x, axis=0, keepdims=True)

    def colmax(x):
        return col_reduce(jnp.maximum, jnp.max, x)

    def tile4(x):
        return jnp.concatenate([x] * NSA_HPG, axis=1)

    gates = ga_ref[...]
    tl = i // QB_PER_CMP_TILE
    sub = i % QB_PER_CMP_TILE
    tp = tl - 1
    groups = range(NSA_KV_GROUPS)
    q_t_g = [jnp.concatenate([q_ref[NSA_HPG * g + j] for j in range(NSA_HPG)], axis=1) for g in groups]
    zeros = jnp.zeros((dh, QW), BF16)
    wq_g = [jnp.concatenate([q_t_g[g] if k == g else zeros for k in groups], axis=0) for g in groups]

    def cmp_scores(g, t):
        return _dot(kc_ref[pl.ds(pl.multiple_of(t * CMP_TILE, CMP_TILE), CMP_TILE), :], wq_g[g])

    def sc_rows(t):
        return pl.ds(pl.multiple_of(SCR_PAD + t * CMP_TILE, 8), CMP_TILE)

    def cmp_far(t, m):
        out = []
        for g in groups:
            s = cmp_scores(g, t)
            sc_ref[g, sc_rows(t), :] = s
            out.append(jnp.maximum(m[g], colmax(s)))
        return tuple(out)

    m_c = lax.fori_loop(0, jnp.maximum(tp, 0), cmp_far, tuple(jnp.full((1, QW), NEG, F32) for _ in groups))
    tpc = jnp.maximum(tp, 0)
    off = CMP_TILE - CMP_TOK_PER_QB - CMP_TOK_PER_QB * sub
    m_c = list(m_c)
    for g in groups:
        s = (cmp_scores(g, tpc) + tct_ref[g, pl.ds(pl.multiple_of(off, 8), CMP_TILE), :]
             + jnp.where(tp < 0, NEG, 0.0))
        sc_ref[g, sc_rows(tpc), :] = s
        m_c[g] = jnp.maximum(m_c[g], colmax(s))
    for g in groups:
        s = cmp_scores(g, tl) + tct_ref[g, pl.ds(pl.multiple_of(off + CMP_TILE, 8), CMP_TILE), :]
        sc_ref[g, sc_rows(tl), :] = s
        m_c[g] = jnp.maximum(m_c[g], colmax(s))

    def cmp_exp(t, carry):
        out = []
        for g in groups:
            l, acc = carry[g]
            e = jnp.exp(sc_ref[g, sc_rows(t), :] - m_c[g])
            sc_ref[g, sc_rows(t), :] = e
            vt = vct_ref[g, :, pl.ds(pl.multiple_of(t * CMP_TILE, CMP_TILE), CMP_TILE)]
            out.append((l + col_reduce(jnp.add, jnp.sum, e), acc + _dot(vt, e.astype(BF16))))
        return tuple(out)

    cmp_out = lax.fori_loop(0, tl + 1, cmp_exp,
                            tuple((jnp.zeros((1, QW), F32), jnp.zeros((dh, QW), F32)) for _ in groups))
    inv_c = [jnp.where(m_c[g] > 0.5 * NEG, 1.0 / jnp.maximum(cmp_out[g][0], 1e-30), 0.0) for g in groups]
    o_c_g = [cmp_out[g][1] * inv_c[g] for g in groups]

    def cmp_psum(t, _):
        for g in groups:
            p = sc_ref[g, sc_rows(t), :] * inv_c[g]
            ps = p[:, 0:Q_BLOCK]
            for j in range(1, NSA_HPG):
                ps = ps + p[:, j * Q_BLOCK:(j + 1) * Q_BLOCK]
            ps_ref[g, sc_rows(t), :] = ps.astype(BF16).astype(F32)
        return 0

    lax.fori_loop(0, tl + 1, cmp_psum, 0)
    per_blk = SEL_BLOCK // CMP_STRIDE
    imp_g = []
    for g in groups:
        imp = ps_ref[g, pl.ds(SCR_PAD - 1, nb, stride=per_blk), :]
        for k in range(1, CMP_LEN // CMP_STRIDE + per_blk - 1):
            imp = imp + ps_ref[g, pl.ds(SCR_PAD - 1 + k, nb, stride=per_blk), :]
        imp_g.append(imp)

    kw0 = pl.multiple_of(jnp.maximum(i * Q_BLOCK - WINDOW, 0), Q_BLOCK)
    tw0 = pl.multiple_of(jnp.maximum(WINDOW - i * Q_BLOCK, 0), Q_BLOCK)
    o_w_g = []
    for g in groups:
        s = _dot(kwin_ref[pl.ds(kw0, WIN_KEYS), :], wq_g[g]) + twt_ref[g, pl.ds(tw0, WIN_KEYS), :]
        e = jnp.exp(s - colmax(s)).astype(BF16)
        acc_w = _dot(vwint_ref[g, :, pl.ds(kw0, WIN_KEYS)], e)
        o_w_g.append(acc_w[0:dh] * (1.0 / jnp.maximum(acc_w[dh:dh + 1], 1e-30)))

    n_io = lax.broadcasted_iota(jnp.int32, (nb, Q_BLOCK), 0)
    r_io = lax.broadcasted_iota(jnp.int32, (nb, Q_BLOCK), 1)
    back = (i * Q_BLOCK + r_io) // SEL_BLOCK - n_io
    valid = back >= 0
    forced = (n_io == 0) | (valid & (back < N_LOCAL_SEL))
    work_g = [jnp.where(forced, -jnp.inf, jnp.where(valid, imp_g[g], -1e9)) for g in groups]
    n_f = n_io.astype(F32)
    def select_rounds(rows):
        def run(*work):
            work = [w[:rows] for w in work]
            for _ in range(SEL_TOP_N - 1 - N_LOCAL_SEL):
                for g in groups:
                    mx = colmax(work[g])
                    first = col_reduce(jnp.minimum, jnp.min, jnp.where(work[g] == mx, n_f[:rows], float(nb)))
                    work[g] = jnp.where(n_f[:rows] == first, -jnp.inf, work[g])
            rest = [jnp.zeros((nb - rows, Q_BLOCK), F32)] if rows < nb else []
            return tuple(jnp.concatenate([jnp.where(w == -jnp.inf, 1.0, 0.0)] + rest, axis=0) for w in work)
        return run

    levels = [SEL_ROWS_STEP * (k + 1) for k in range(nb // SEL_ROWS_STEP)]
    level = jnp.minimum((i * Q_BLOCK + Q_BLOCK - 1) // (SEL_BLOCK * SEL_ROWS_STEP), len(levels) - 1)
    sel_g = list(lax.switch(level, [select_rounds(rows) for rows in levels], *work_g))

    far_blocks = (i - 1) * (Q_BLOCK // SEL_BLOCK)
    selm_g = [(jnp.where(n_io < far_blocks, sel_g[g], 0.0) - 1.0).astype(BF16) for g in groups]

    far_end = (i - 1) * Q_BLOCK
    n_far = (jnp.maximum(far_end, 0) + SEL_TK - 1) // SEL_TK

    def flash_step(carry, s, vt):
        m_i, acc = carry
        m_new = jnp.maximum(m_i, colmax(s))
        alpha = jnp.exp(m_i - m_new)
        p = jnp.exp(s - m_new).astype(BF16)
        return m_new, alpha * acc + _dot(vt, p)

    carry = tuple((jnp.full((1, QW), NEG, F32), jnp.zeros((PV_ROWS, QW), F32)) for _ in groups)
    for w in range(nb // SEL_WIN):
        w_far = [jnp.concatenate([tile4(selm_g[g][w * SEL_WIN:(w + 1) * SEL_WIN]), wq_g[g]], axis=0)
                 for g in groups]

        lo = w * SEL_WIN_TILES
        last = min((w + 1) * SEL_WIN_TILES, n_keys // SEL_TK) - 1
        trips = jnp.maximum(jnp.minimum(n_far, last + 1) - lo + 1, 0) // 2

        def far_scores(kt, s_ref, mx_ref, g, w=w, w_far=w_far):
            k0 = pl.multiple_of(kt * SEL_TK, SEL_TK)
            e0 = pl.multiple_of((kt - w * SEL_WIN_TILES) * SEL_TK, SEL_TK)
            lhs = jnp.concatenate([et_ref[pl.ds(e0, SEL_TK), :],
                                   ksel_ref[pl.ds(k0, SEL_TK), :]], axis=1)
            s = _dot(lhs, w_far[g])
            s_ref[g] = s
            mx_ref[g] = colmax(s)

        def far_update(kt, s_ref, mx_ref, g, carry_g):
            k0 = pl.multiple_of(kt * SEL_TK, SEL_TK)
            m_i, acc = carry_g
            m_new = jnp.maximum(m_i, mx_ref[g])
            p = jnp.exp(s_ref[g] - m_new).astype(BF16)
            return m_new, jnp.exp(m_i - m_new) * acc + _dot(vselt_ref[g, :, pl.ds(k0, SEL_TK)], p)

        def far_trip(tiles, first, last=last, far_scores=far_scores):
            def body(u, carry):
                kt = first + tiles * u
                carry = list(carry)
                bufs = [(sa_ref, ma_ref), (sb_ref, mb_ref)]
                for t in range(tiles):
                    for g in groups:
                        far_scores(jnp.minimum(kt + t + 1, last), *bufs[(t + 1) % 2], g)
                        carry[g] = far_update(kt + t, *bufs[t % 2], g, carry[g])
                return tuple(carry)
            return body

        @pl.when(trips > 0)
        def _(lo=lo, far_scores=far_scores):
            for g in groups:
                far_scores(jnp.int32(lo), sa_ref, ma_ref, g)

        first, left = lo, trips
        for tiles in FAR_TRIP_TILES:
            count = left // (tiles // 2)
            carry = lax.fori_loop(0, count, far_trip(tiles, first), carry)
            first, left = first + tiles * count, left - count * (tiles // 2)

    heads_out = []
    for g in groups:
        wq, o_c, sel = wq_g[g], o_c_g[g], sel_g[g]
        k0n = pl.multiple_of(jnp.maximum(far_end, 0), Q_BLOCK)
        t0n = pl.multiple_of(jnp.maximum(-far_end, 0), Q_BLOCK)
        near_blocks = 2 * Q_BLOCK // SEL_BLOCK
        pb = lax.broadcasted_iota(jnp.int32, (16, nb), 0)
        pn = lax.broadcasted_iota(jnp.int32, (16, nb), 1)
        pick_rows = jnp.where((pn == jnp.maximum(far_blocks, 0) + pb) & (pb < near_blocks), 1.0, 0.0).astype(BF16)
        sel_rows = _dot(pick_rows, sel.astype(BF16))
        add_near = jnp.concatenate(
            [jnp.broadcast_to((sel_rows[b:b + 1, :] - 1.0) * (-NEG), (SEL_BLOCK, Q_BLOCK)) for b in range(near_blocks)],
            axis=0)
        s = (_dot(ksel_ref[pl.ds(k0n, 2 * Q_BLOCK), :], wq) + tnt_ref[g, pl.ds(t0n, 2 * Q_BLOCK), :]
             + tile4(add_near))
        m_s, acc_s = flash_step(carry[g], s, vselt_ref[g, :, pl.ds(k0n, 2 * Q_BLOCK)])
        o_s = acc_s[0:dh] * (1.0 / jnp.maximum(acc_s[dh:dh + 1], 1e-30))

        o_w = o_w_g[g]

        for j in range(NSA_HPG):
            hd = NSA_HPG * g + j
            cols = slice(j * Q_BLOCK, (j + 1) * Q_BLOCK)
            heads_out.append(gates[hd:hd + 1] * o_c[:, cols]
                             + gates[NSA_HEADS + hd:NSA_HEADS + hd + 1] * o_s[:, cols]
                             + gates[2 * NSA_HEADS + hd:2 * NSA_HEADS + hd + 1] * o_w[:, cols])

    out_ref[...] = jnp.concatenate(heads_out, axis=0).T.astype(out_ref.dtype)


def _nsa_attention(q_t, gates_t, kc, vct, tct, ksel, vselt, et, tnt, kwin, vwint, twt):
    H, dh, S = q_t.shape
    n_qb = S // Q_BLOCK
    n_tok = S // CMP_STRIDE
    nb = -(-(S // SEL_BLOCK) // SEL_WIN) * SEL_WIN
    ps_rows = nb * (SEL_BLOCK // CMP_STRIDE) + 2 * SCR_PAD
    full = lambda a: _const_spec(a.shape)
    return pl.pallas_call(
        functools.partial(_nsa_kernel, S),
        grid=(n_qb,),
        in_specs=[pl.BlockSpec((H, dh, Q_BLOCK), lambda i: (0, 0, i)),
                  pl.BlockSpec((gates_t.shape[0], Q_BLOCK), lambda i: (0, i)),
                  full(kc), full(vct), full(tct), full(ksel), full(vselt), full(et), full(tnt),
                  full(kwin), full(vwint), full(twt)],
        out_specs=pl.BlockSpec((Q_BLOCK, H * dh), lambda i: (i, 0)),
        out_shape=jax.ShapeDtypeStruct((S, H * dh), BF16),
        scratch_shapes=[pltpu.VMEM((NSA_KV_GROUPS, SCR_PAD + n_tok, QW), F32),
                        pltpu.VMEM((NSA_KV_GROUPS, ps_rows, Q_BLOCK), F32),
                        pltpu.VMEM((NSA_KV_GROUPS, SEL_TK, QW), F32),
                        pltpu.VMEM((NSA_KV_GROUPS, SEL_TK, QW), F32),
                        pltpu.VMEM((NSA_KV_GROUPS, 1, QW), F32),
                        pltpu.VMEM((NSA_KV_GROUPS, 1, QW), F32)],
        compiler_params=_params("arbitrary"),
        name="nsa_attention",
    )(q_t, gates_t, kc, vct, tct, ksel, vselt, et, tnt, kwin, vwint, twt)


def _retention_kernel(q_ref, k_ref, v_ref, g_ref, nw_ref, dmask_ref, qd_ref, kd_ref, cd_ref, out_ref, state_ref):
    @pl.when(pl.program_id(0) == 0)
    def _():
        state_ref[...] = jnp.zeros_like(state_ref)

    states = [state_ref[hd] for hd in range(RET_HEADS)]
    for c in range(q_ref.shape[0] // RET_CHUNK):
        rows = slice(c * RET_CHUNK, (c + 1) * RET_CHUNK)
        for hd in range(RET_HEADS):
            ks = slice(hd * RET_QK_DIM, (hd + 1) * RET_QK_DIM)
            vs = slice(hd * RET_V_DIM, (hd + 1) * RET_V_DIM)
            q = q_ref[rows, ks]
            k = k_ref[rows, ks]
            v = v_ref[rows, vs]
            state = states[hd]
            inner = _dot_nt(q, k.astype(BF16)) * dmask_ref[hd]
            y = _dot(inner.astype(BF16), v) + _dot(q, state.astype(BF16)) * qd_ref[hd]
            kd = k * kd_ref[hd]
            states[hd] = state * cd_ref[hd] + _dot(kd.T.astype(BF16), v)
            mu = jnp.mean(y, axis=-1, keepdims=True)
            yc = y - mu
            var = jnp.mean(yc * yc, axis=-1, keepdims=True)
            yn = yc * lax.rsqrt(var + EPS) * nw_ref[:, vs]
            out_ref[rows, vs] = (jax.nn.silu(g_ref[rows, vs]) * yn).astype(out_ref.dtype)
    for hd in range(RET_HEADS):
        state_ref[hd] = states[hd]


RET_STEP_CHUNKS = 8


def _retention(q, k, v, g, norm_w, dmask, qd, kd, cd):
    S = q.shape[0]
    C = RET_CHUNK * RET_STEP_CHUNKS
    row = lambda w: pl.BlockSpec((C, w), lambda i: (i, 0))
    full = lambda a: _const_spec(a.shape)
    return pl.pallas_call(
        _retention_kernel,
        grid=(S // C,),
        in_specs=[row(RET_QK_W), row(RET_QK_W), row(RET_V_W), row(RET_V_W),
                  full(norm_w), full(dmask), full(qd), full(kd), full(cd)],
        out_specs=row(RET_V_W),
        out_shape=jax.ShapeDtypeStruct((S, RET_V_W), BF16),
        scratch_shapes=[pltpu.VMEM((RET_HEADS, RET_QK_DIM, RET_V_DIM), F32)],
        compiler_params=_params("arbitrary"),
        name="retention",
    )(q, k, v, g, norm_w, dmask, qd, kd, cd)


MERGE_TM = 512
MERGE_SUB_ROWS = 256
GRP_LANE = 40
RANK_LANE = 41


def _merge_kernel(x_ref, ya_ref, yr_ref, gta_ref, gtr_ref, wa_ref, wr_ref, wo_ref, nw_ref, wrt_ref, brt_ref,
                  tri_ref, x1_ref, h_ref, comb_ref, cnt_ref, carry_ref):
    @pl.when(pl.program_id(0) == 0)
    def _():
        carry_ref[...] = jnp.zeros_like(carry_ref)

    sub = tri_ref.shape[0]
    carry = carry_ref[...]
    for r in range(x_ref.shape[0] // sub):
        carry = _merge_rows(slice(r * sub, (r + 1) * sub), carry, x_ref, ya_ref, yr_ref, gta_ref, gtr_ref,
                            wa_ref, wr_ref, wo_ref, nw_ref, wrt_ref, brt_ref, tri_ref, x1_ref, h_ref, comb_ref)
    carry_ref[...] = carry
    cnt_ref[...] = jnp.broadcast_to(carry, cnt_ref.shape)


def _merge_rows(rows, carry, x_ref, ya_ref, yr_ref, gta_ref, gtr_ref, wa_ref, wr_ref, wo_ref, nw_ref, wrt_ref,
                brt_ref, tri_ref, x1_ref, h_ref, comb_ref):
    merged = gta_ref[rows] * _dot(ya_ref[rows], wa_ref[...]) + gtr_ref[rows] * _dot(yr_ref[rows], wr_ref[...])
    x1 = x_ref[rows] + _dot(merged.astype(BF16), wo_ref[...])
    x1_ref[rows] = x1
    h = (x1 * lax.rsqrt(jnp.mean(x1 * x1, axis=-1, keepdims=True) + EPS) * nw_ref[...]).astype(BF16)
    h_ref[rows] = _pack_bf16_pairs(h.astype(F32))
    logits = _dot(h, wrt_ref[...]) + brt_ref[...]
    lane = lax.broadcasted_iota(jnp.int32, logits.shape, 1)
    lane_f = lane.astype(F32)
    big = float(LANE)
    is_grp = (lane >= N_EXPERTS) & (lane < N_EXPERTS + N_GROUPS)
    lg = jnp.where(is_grp, logits, -jnp.inf)
    eg = jnp.where(is_grp, jnp.exp(lg - jnp.max(lg, axis=-1, keepdims=True)), 0.0)
    pg = eg / jnp.sum(eg, axis=-1, keepdims=True)
    p_grp = jnp.max(pg, axis=-1, keepdims=True)
    grp = jnp.min(jnp.where(is_grp & (pg == p_grp), lane_f, big), axis=-1, keepdims=True) - N_EXPERTS
    in_grp = (lane_f >= grp * EXPERTS_PER_GROUP) & (lane_f < (grp + 1.0) * EXPERTS_PER_GROUP)
    le = jnp.where(in_grp, logits, -jnp.inf)
    m1 = jnp.max(le, axis=-1, keepdims=True)
    i1 = jnp.min(jnp.where(le == m1, lane_f, big), axis=-1, keepdims=True)
    le2 = jnp.where(lane_f == i1, -jnp.inf, le)
    m2 = jnp.max(le2, axis=-1, keepdims=True)
    i2 = jnp.min(jnp.where(le2 == m2, lane_f, big), axis=-1, keepdims=True)
    e2 = jnp.exp(m2 - m1)
    den = 1.0 + e2
    onehot = jnp.where(lane_f == grp, 1.0, 0.0)
    before = _dot(tri_ref[...], onehot.astype(BF16)) + carry
    rank = jnp.sum(onehot * before, axis=-1, keepdims=True)
    comb_ref[rows] = (jnp.where(lane_f == i1, (1.0 / den) * p_grp, 0.0)
                      + jnp.where(lane_f == i2, (e2 / den) * p_grp, 0.0)
                      + jnp.where(lane == GRP_LANE, grp, 0.0)
                      + jnp.where(lane == RANK_LANE, rank, 0.0))
    return carry + jnp.sum(onehot, axis=0, keepdims=True)


def _pack_bf16_pairs(x):
    n = x.shape[1] // 2
    return pltpu.pack_elementwise([x[:, :n], x[:, n:]], packed_dtype=BF16)


def _unpack_bf16_pairs(w):
    return tuple(pltpu.unpack_elementwise(w, index=k, packed_dtype=BF16, unpacked_dtype=F32) for k in range(2))


def _merge(x2d, ya, yr, gta, gtr, wa, wr, wo, nw, wrt, brt):
    S = x2d.shape[0]
    tm = MERGE_TM
    row = lambda w: pl.BlockSpec((tm, w), lambda i: (i, 0))
    full = lambda a: _const_spec(a.shape)
    sub = MERGE_SUB_ROWS
    tri = jnp.asarray(np.tril(np.ones((sub, sub), np.float32), -1), BF16)
    return pl.pallas_call(
        _merge_kernel,
        grid=(S // tm,),
        in_specs=[row(D_MODEL), row(NSA_Q_W), row(RET_V_W), row(D_MODEL), row(D_MODEL),
                  full(wa), full(wr), full(wo), full(nw), full(wrt), full(brt), full(tri)],
        out_specs=[row(D_MODEL), row(D_MODEL // 2), row(LANE), pl.BlockSpec((8, LANE), lambda i: (0, 0))],
        out_shape=[jax.ShapeDtypeStruct((S, D_MODEL), F32), jax.ShapeDtypeStruct((S, D_MODEL // 2), jnp.uint32),
                   jax.ShapeDtypeStruct((S, LANE), F32), jax.ShapeDtypeStruct((8, LANE), F32)],
        scratch_shapes=[pltpu.VMEM((1, LANE), F32)],
        compiler_params=_params("arbitrary"),
        name="merge_router",
    )(x2d, ya, yr, gta, gtr, wa, wr, wo, nw, wrt, brt, tri)


MOE_TM = 1024
MOE_SUB_TILES = 2
RANK_TM = 512
HALF = D_MODEL // 2


def _permute_kernel(pos_ref, h_ref, comb_ref, hs_ref, cs_ref):
    i = pl.program_id(0)

    @pl.when(i == 0)
    def _():
        hs_ref[...] = jnp.zeros_like(hs_ref)
        cs_ref[...] = jnp.zeros_like(cs_ref)

    tm = h_ref.shape[0]

    def body(r, _):
        d = pos_ref[i * tm + r]
        hs_ref[pl.ds(d, 1), :] = h_ref[pl.ds(r, 1), :]
        cs_ref[pl.ds(d, 1), :] = comb_ref[pl.ds(r, 1), :]
        return 0

    lax.fori_loop(0, tm, body, 0, unroll=8)


def _permute(pos, hu, comb, n_sorted):
    S = hu.shape[0]
    tm = RANK_TM
    resident = lambda w: pl.BlockSpec((n_sorted, w), lambda i, p: (0, 0), pipeline_mode=pl.Buffered(1))
    return pl.pallas_call(
        _permute_kernel,
        grid_spec=pltpu.PrefetchScalarGridSpec(
            num_scalar_prefetch=1, grid=(S // tm,),
            in_specs=[pl.BlockSpec((tm, HALF), lambda i, p: (i, 0)), pl.BlockSpec((tm, LANE), lambda i, p: (i, 0))],
            out_specs=[resident(HALF), resident(LANE)]),
        out_shape=[jax.ShapeDtypeStruct((n_sorted, HALF), jnp.uint32), jax.ShapeDtypeStruct((n_sorted, LANE), F32)],
        compiler_params=_params("arbitrary"),
        name="moe_permute",
    )(pos, hu, comb)


def _moe_kernel(tg_ref, tv_ref, hs_ref, cs_ref, wg_ref, wu_ref, wd_ref, ys_ref, lo_ref, hi_ref, acc_ref):
    i = pl.program_id(0)
    e = pl.program_id(1)

    @pl.when(e == 0)
    def _():
        lo, hi = _unpack_bf16_pairs(hs_ref[...])
        lo_ref[...] = lo.astype(BF16)
        hi_ref[...] = hi.astype(BF16)
        acc_ref[...] = jnp.zeros_like(acc_ref)

    @pl.when(tv_ref[i] > 0)
    def _():
        wg = wg_ref[0].astype(BF16)
        wu = wu_ref[0].astype(BF16)
        wd = wd_ref[0].astype(BF16)
        sub = lo_ref.shape[0] // MOE_SUB_TILES
        for r in range(MOE_SUB_TILES):
            rows = slice(r * sub, (r + 1) * sub)
            lo, hi = lo_ref[rows], hi_ref[rows]
            gate = _dot(lo, wg[:HALF]) + _dot(hi, wg[HALF:])
            up = _dot(lo, wu[:HALF]) + _dot(hi, wu[HALF:])
            comb = cs_ref[rows]
            lane = lax.broadcasted_iota(jnp.int32, comb.shape, 1)
            c = jnp.sum(jnp.where(lane == tg_ref[i] * EXPERTS_PER_GROUP + e, comb, 0.0), axis=-1, keepdims=True)
            acc_ref[rows] += _dot((jax.nn.silu(gate) * up).astype(BF16), wd) * c

    @pl.when(e == pl.num_programs(1) - 1)
    def _():
        ys_ref[...] = _pack_bf16_pairs(acc_ref[...].astype(BF16).astype(F32))


def _moe(tile_group, tile_valid, hs, cs, wg, wu, wd):
    n_sorted = hs.shape[0]
    tm = MOE_TM
    row = lambda w: pl.BlockSpec((tm, w), lambda i, e, tg, tv: (i, 0))
    wspec = lambda a, b: pl.BlockSpec((1, a, b), lambda i, e, tg, tv: (tg[i] * EXPERTS_PER_GROUP + e, 0, 0))
    return pl.pallas_call(
        _moe_kernel,
        grid_spec=pltpu.PrefetchScalarGridSpec(
            num_scalar_prefetch=2, grid=(n_sorted // tm, EXPERTS_PER_GROUP),
            in_specs=[row(HALF), row(LANE), wspec(D_MODEL, EXPERT_FF), wspec(D_MODEL, EXPERT_FF),
                      wspec(EXPERT_FF, D_MODEL)],
            out_specs=row(HALF),
            scratch_shapes=[pltpu.VMEM((tm, HALF), BF16), pltpu.VMEM((tm, HALF), BF16),
                            pltpu.VMEM((tm, D_MODEL), F32)]),
        out_shape=jax.ShapeDtypeStruct((n_sorted, HALF), jnp.uint32),
        compiler_params=_params("parallel", "arbitrary"),
        name="moe_experts",
    )(tile_group, tile_valid, hs, cs, wg, wu, wd)


def _unpermute_kernel(pos_ref, ys_ref, x1_ref, nw_ref, out_ref, stage_ref):
    i = pl.program_id(0)
    tm = x1_ref.shape[0]

    def body(r, _):
        stage_ref[pl.ds(r, 1), :] = ys_ref[pl.ds(pos_ref[i * tm + r], 1), :]
        return 0

    lax.fori_loop(0, tm, body, 0, unroll=8)
    lo, hi = _unpack_bf16_pairs(stage_ref[...])
    x2 = x1_ref[...] + jnp.concatenate([lo, hi], axis=1)
    out_ref[...] = x2 * lax.rsqrt(jnp.mean(x2 * x2, axis=-1, keepdims=True) + EPS) * nw_ref[...]


def _unpermute_residual_norm(pos, ys, x1, nw):
    S = x1.shape[0]
    tm = RANK_TM
    return pl.pallas_call(
        _unpermute_kernel,
        grid_spec=pltpu.PrefetchScalarGridSpec(
            num_scalar_prefetch=1, grid=(S // tm,),
            in_specs=[pl.BlockSpec(ys.shape, lambda i, p: (0, 0), pipeline_mode=pl.Buffered(1)),
                      pl.BlockSpec((tm, D_MODEL), lambda i, p: (i, 0)),
                      pl.BlockSpec((1, D_MODEL), lambda i, p: (0, 0))],
            out_specs=pl.BlockSpec((tm, D_MODEL), lambda i, p: (i, 0)),
            scratch_shapes=[pltpu.VMEM((tm, HALF), jnp.uint32)]),
        out_shape=jax.ShapeDtypeStruct((S, D_MODEL), F32),
        compiler_params=_params("parallel"),
        name="moe_unpermute_norm",
    )(pos, ys, x1, nw)


def _hierarchical_moe(hu, comb, cnt, x1, wg, wu, wd, final_norm_w):
    S = hu.shape[0]
    tm = MOE_TM
    n_sorted = S + N_GROUPS * tm
    assert RANK_LANE == GRP_LANE + 1
    grp_rank = comb[:, GRP_LANE:RANK_LANE + 1].astype(jnp.int32)
    grp, rank = grp_rank[:, 0], grp_rank[:, 1]
    counts = cnt[0, :N_GROUPS].astype(jnp.int32)
    padded = (counts + tm - 1) // tm * tm
    ends = jnp.cumsum(padded)
    starts = ends - padded
    pos = rank + jnp.sum(jnp.where(grp[:, None] == jnp.arange(N_GROUPS)[None, :], starts[None, :], 0), axis=1)
    tile_start = jnp.arange(n_sorted // tm, dtype=jnp.int32) * tm
    tile_group = jnp.minimum(jnp.sum((tile_start[:, None] >= ends[None, :]).astype(jnp.int32), axis=1), N_GROUPS - 1)
    tile_valid = (tile_start < ends[N_GROUPS - 1]).astype(jnp.int32)
    hs, cs = _permute(pos, hu, comb, n_sorted)
    ys = _moe(tile_group, tile_valid, hs, cs, wg, wu, wd)
    return _unpermute_residual_norm(pos, ys, x1, final_norm_w)


def _positional_tables():
    half = RET_QK_DIM // 2
    inv = ROPE_BASE ** (-jnp.arange(half, dtype=F32) / half)
    inv2 = jnp.concatenate([inv, inv]).reshape(1, RET_QK_DIM)
    C, H = RET_CHUNK, RET_HEADS
    log_gamma = jnp.log(1.0 - 2.0 ** (-5.0 - jnp.arange(H, dtype=F32)))
    idx = jnp.arange(C, dtype=F32)
    diff = idx[:, None] - idx[None, :]
    dmask = jnp.where(diff >= 0, jnp.exp(jnp.maximum(diff, 0.0)[None] * log_gamma[:, None, None]), 0.0)
    qd = jnp.exp((idx + 1.0)[None, :] * log_gamma[:, None])
    kd = jnp.exp((C - 1.0 - idx)[None, :] * log_gamma[:, None])
    cd = jnp.exp(C * log_gamma)
    qd = jnp.broadcast_to(qd[:, :, None], (H, C, RET_V_DIM))
    kd = jnp.broadcast_to(kd[:, :, None], (H, C, RET_QK_DIM))
    cd = jnp.broadcast_to(cd[:, None, None], (H, 1, RET_V_DIM))
    return inv2, dmask, qd, kd, cd


def _attention_tables(rel_bias):
    r = np.arange(Q_BLOCK)[None, :]
    cc = np.arange(3 * CMP_TILE)[:, None] - CMP_TILE
    d = r - CMP_STRIDE * (cc - (CMP_TILE - CMP_TOK_PER_QB)) - (CMP_LEN - 1)
    d = np.where(cc < 0, -2, np.where((cc >= CMP_TILE) | (d < 0), -1, d))
    tct = _bias_table_t(rel_bias, d, shift=True)
    d = r + Q_BLOCK - np.arange(3 * Q_BLOCK)[:, None]
    tnt = _bias_table_t(rel_bias, np.where(d < 0, -1, d), shift=True)
    d = r + WINDOW - np.arange(WIN_KEYS + WINDOW)[:, None]
    twt = _bias_table_t(rel_bias, np.where((d < 0) | (d >= WINDOW), -1, d), shift=False)
    kk = np.arange(SEL_WIN * SEL_BLOCK)[:, None] // SEL_BLOCK
    et = jnp.where(jnp.asarray(kk == np.arange(SEL_WIN)[None, :]), -NEG, 0.0).astype(BF16)
    return tct, tnt, twt, et


def _layer(x2d, rel_bias, norm_mix, w_in, cmp_pos_k, cmp_k_w1, cmp_k_w2, cmp_pos_v, cmp_v_w1, cmp_v_w2,
           ret_norm, w_nsa_up, w_ret_up, w_out, norm_ffn, w_rg, b_rg, w_re, b_re, moe_wg, moe_wu, moe_wd,
           final_norm_w):
    S = x2d.shape[0]
    G, dh = NSA_KV_GROUPS, NSA_HEAD_DIM
    assert S % (CMP_TILE * CMP_STRIDE) == 0 and S // SEL_BLOCK >= SEL_TOP_N

    o = np.cumsum((0,) + (NSA_Q_W, 6 * NSA_KV_W, 3 * NSA_HEADS, RET_QK_W, RET_QK_W, RET_V_W, RET_V_W,
                          D_MODEL, D_MODEL))
    seg = lambda n: w_in[:, o[n]:o[n + 1]]
    w_cat = jnp.concatenate([seg(0), seg(1), seg(3), seg(4), seg(5), seg(6), seg(7), seg(8), seg(2),
                             jnp.zeros((D_MODEL, LANE - 3 * NSA_HEADS), F32)], axis=1).astype(BF16)
    inv2, dmask, qd, kd, cd = _positional_tables()

    (qa_t, kcmp, vcmp, ksel, vsel_t, kwin, vwin_t, ga_t, qr, kr, vr, gr, gta, gtr) = _input_projection(
        x2d, norm_mix.reshape(1, D_MODEL), w_cat, inv2)

    nch = S // CMP_STRIDE
    kc = _compress(kcmp, cmp_pos_k, cmp_k_w1, cmp_k_w2).astype(BF16)
    vct = _compress(vcmp, cmp_pos_v, cmp_v_w1, cmp_v_w2).T.reshape(G, dh, nch).astype(BF16)

    tct, tnt, twt, et = _attention_tables(rel_bias)
    ya = _nsa_attention(qa_t.reshape(NSA_HEADS, dh, S), ga_t, kc, vct, tct,
                        ksel, vsel_t.reshape(G, PV_ROWS, S), et, tnt,
                        kwin, vwin_t.reshape(G, PV_ROWS, S), twt)

    yr = _retention(qr, kr, vr, gr, ret_norm.reshape(1, RET_V_W), dmask, qd, kd, cd)

    wrt = jnp.concatenate([w_re, w_rg, jnp.zeros((D_MODEL, LANE - N_EXPERTS - N_GROUPS), F32)], axis=1).astype(BF16)
    brt = jnp.concatenate([b_re, b_rg, jnp.zeros((LANE - N_EXPERTS - N_GROUPS,), F32)]).reshape(1, LANE)
    x1, h2, comb, cnt = _merge(x2d, ya, yr, gta, gtr, w_nsa_up.astype(BF16), w_ret_up.astype(BF16),
                               w_out.astype(BF16), norm_ffn.reshape(1, D_MODEL), wrt, brt)
    return _hierarchical_moe(h2, comb, cnt, x1, moe_wg, moe_wu, moe_wd, final_norm_w)


def kernel(x, rel_bias, norm_mix, w_in, cmp_pos_k, cmp_k_w1, cmp_k_w2, cmp_pos_v, cmp_v_w1, cmp_v_w2, ret_norm, w_nsa_up, w_ret_up, w_out, norm_ffn, w_router_group, b_router_group, w_router_expert, b_router_expert, moe_w_gate, moe_w_up, moe_w_down, norm_final):
    B, S, D = x.shape
    assert B == 1 and norm_mix.shape[0] == 1, "single sequence, depth 1"
    out = _layer(x[0], rel_bias, norm_mix[0], w_in[0], cmp_pos_k[0], cmp_k_w1[0], cmp_k_w2[0],
                 cmp_pos_v[0], cmp_v_w1[0], cmp_v_w2[0], ret_norm[0], w_nsa_up[0], w_ret_up[0], w_out[0],
                 norm_ffn[0], w_router_group[0], b_router_group[0], w_router_expert[0], b_router_expert[0],
                 moe_w_gate[0], moe_w_up[0], moe_w_down[0], norm_final.reshape(1, D))
    return out[None]
```

```python
import functools
import math

import numpy as np
import jax
import jax.numpy as jnp
from jax import lax
from jax.experimental import pallas as pl
from jax.experimental.pallas import tpu as pltpu

F32 = jnp.float32
BF16 = jnp.bfloat16

D_MODEL = 1024
NSA_HEADS = 8
NSA_KV_GROUPS = 2
NSA_HPG = NSA_HEADS // NSA_KV_GROUPS
NSA_HEAD_DIM = 64
CMP_LEN = 32
CMP_STRIDE = 16
CMP_HIDDEN = 256
SEL_BLOCK = 64
SEL_TOP_N = 16
N_LOCAL_SEL = 2
WINDOW = 512
Q_BLOCK = 128
RET_HEADS = 4
RET_QK_DIM = 128
RET_V_DIM = 256
RET_CHUNK = 128
ROPE_BASE = 10000.0
REL_BUCKETS = 32
REL_MAX_DIST = 128
N_GROUPS = 4
EXPERTS_PER_GROUP = 8
N_EXPERTS = N_GROUPS * EXPERTS_PER_GROUP
EXPERT_FF = 256
EPS = 1e-6

NSA_Q_W = NSA_HEADS * NSA_HEAD_DIM
NSA_KV_W = NSA_KV_GROUPS * NSA_HEAD_DIM
RET_QK_W = RET_HEADS * RET_QK_DIM
RET_V_W = RET_HEADS * RET_V_DIM

LANE = 128
NEG = -1e30
VMEM_LIMIT = 56 * 1024 * 1024

CMP_TILE = 128
QB_PER_CMP_TILE = CMP_TILE * CMP_STRIDE // Q_BLOCK
CMP_TOK_PER_QB = Q_BLOCK // CMP_STRIDE
SCR_PAD = 8
SEL_TK = 512
SEL_WIN = 128
SEL_WIN_TILES = SEL_WIN * SEL_BLOCK // SEL_TK
CMP_TRIP_TILES = (4, 2, 1)
FAR_TRIP_TILES = (8, 4, 2)
SEL_ROWS_STEP = 64
PV_ROWS = 80
WIN_KEYS = WINDOW + Q_BLOCK
QW = NSA_HPG * Q_BLOCK


def _dot(a, b):
    return jnp.dot(a, b, preferred_element_type=F32)


def _dot_nt(a, b):
    return lax.dot_general(a, b, (((1,), (1,)), ((), ())), preferred_element_type=F32)


def _const_spec(shape):
    nd = len(shape)
    return pl.BlockSpec(shape, lambda *_: (0,) * nd, pipeline_mode=pl.Buffered(1))


def _params(*sem):
    return pltpu.CompilerParams(dimension_semantics=sem, vmem_limit_bytes=VMEM_LIMIT)


_C_QA = 0
_C_KV = _C_QA + NSA_Q_W
_C_QR = _C_KV + 6 * NSA_KV_W
_C_KR = _C_QR + RET_QK_W
_C_VR = _C_KR + RET_QK_W
_C_GR = _C_VR + RET_V_W
_C_GTA = _C_GR + RET_V_W
_C_GTR = _C_GTA + D_MODEL
_C_GA = _C_GTR + D_MODEL
_C_END = _C_GA + LANE
IN_TM = 512


def _inproj_kernel(x_ref, nw_ref, w_ref, inv_ref,
                   qat_ref, kcmp_ref, vcmp_ref, ksel_ref, vselt_ref, kwin_ref, vwint_ref, gat_ref,
                   qr_ref, kr_ref, vr_ref, gr_ref, gta_ref, gtr_ref):
    tm = x_ref.shape[0]
    x = x_ref[...]
    h = x * lax.rsqrt(jnp.mean(x * x, axis=-1, keepdims=True) + EPS) * nw_ref[...]
    hb = h.astype(BF16)

    def proj(a, width):
        return _dot(hb, w_ref[:, a:a + width])

    def values_t(v):
        vt = v.T.astype(BF16)
        aux = jnp.where(lax.broadcasted_iota(jnp.int32, (PV_ROWS - NSA_HEAD_DIM, tm), 0) == 0, 1.0, 0.0).astype(BF16)
        parts = []
        for g in range(NSA_KV_GROUPS):
            parts += [vt[g * NSA_HEAD_DIM:(g + 1) * NSA_HEAD_DIM], aux]
        return jnp.concatenate(parts, axis=0)

    qat_ref[...] = (proj(_C_QA, NSA_Q_W) * (NSA_HEAD_DIM ** -0.5)).T.astype(BF16)
    kv = proj(_C_KV, 6 * NSA_KV_W)
    kcmp_ref[...] = kv[:, 0 * LANE:1 * LANE]
    vcmp_ref[...] = kv[:, 1 * LANE:2 * LANE]
    ksel_ref[...] = kv[:, 2 * LANE:3 * LANE].astype(BF16)
    vselt_ref[...] = values_t(kv[:, 3 * LANE:4 * LANE])
    kwin_ref[...] = kv[:, 4 * LANE:5 * LANE].astype(BF16)
    vwint_ref[...] = values_t(kv[:, 5 * LANE:6 * LANE])
    gat_ref[...] = jax.nn.sigmoid(proj(_C_GA, LANE)).T

    hm = tm // 2
    half = RET_QK_DIM // 2
    low = lax.broadcasted_iota(jnp.int32, (hm, RET_QK_DIM), 1) < half
    row = pl.program_id(0) * tm + lax.broadcasted_iota(jnp.int32, (hm, RET_QK_DIM), 0)
    ang = jnp.where(low, row, row + hm).astype(F32) * inv_ref[...]
    c, s = jnp.cos(ang), jnp.sin(ang)
    c_sw, s_sw = pltpu.roll(c, half, 1), pltpu.roll(s, half, 1)
    cos2 = jnp.concatenate([jnp.where(low, c, c_sw), jnp.where(low, c_sw, c)], axis=0)
    sin2 = jnp.concatenate([jnp.where(low, -s, s_sw), jnp.where(low, -s_sw, s)], axis=0)
    qr = proj(_C_QR, RET_QK_W)
    kr = proj(_C_KR, RET_QK_W)
    for hd in range(RET_HEADS):
        sl = slice(hd * RET_QK_DIM, (hd + 1) * RET_QK_DIM)
        qh = qr[:, sl]
        kh = kr[:, sl]
        qr_ref[:, sl] = (qh * cos2 + pltpu.roll(qh, RET_QK_DIM // 2, 1) * sin2).astype(BF16)
        kr_ref[:, sl] = (kh * cos2 + pltpu.roll(kh, RET_QK_DIM // 2, 1) * sin2) * (RET_QK_DIM ** -0.5)
    vr_ref[...] = proj(_C_VR, RET_V_W).astype(BF16)
    gr_ref[...] = proj(_C_GR, RET_V_W)
    gta_ref[...] = jax.nn.sigmoid(proj(_C_GTA, D_MODEL))
    gtr_ref[...] = jax.nn.sigmoid(proj(_C_GTR, D_MODEL))


def _input_projection(x2d, norm_w, w_cat, inv2):
    S = x2d.shape[0]
    tm = IN_TM
    vt_rows = NSA_KV_GROUPS * PV_ROWS
    outs = [
        (NSA_Q_W, BF16, True),
        (LANE, F32, False), (LANE, F32, False),
        (LANE, BF16, False), (vt_rows, BF16, True),
        (LANE, BF16, False), (vt_rows, BF16, True),
        (LANE, F32, True),
        (RET_QK_W, BF16, False), (RET_QK_W, F32, False),
        (RET_V_W, BF16, False), (RET_V_W, F32, False),
        (D_MODEL, F32, False), (D_MODEL, F32, False),
    ]
    row = lambda w: pl.BlockSpec((tm, w), lambda i: (i, 0))
    col = lambda w: pl.BlockSpec((w, tm), lambda i: (0, i))
    return pl.pallas_call(
        _inproj_kernel,
        grid=(S // tm,),
        in_specs=[row(D_MODEL), _const_spec((1, D_MODEL)), _const_spec((D_MODEL, _C_END)),
                  _const_spec((1, RET_QK_DIM))],
        out_specs=[col(w) if t else row(w) for w, _, t in outs],
        out_shape=[jax.ShapeDtypeStruct((w, S) if t else (S, w), dt) for w, dt, t in outs],
        compiler_params=_params("parallel"),
        name="in_proj",
    )(x2d, norm_w, w_cat, inv2)


def _compress_kernel(x_ref, pos_ref, w1_ref, w2_ref, out_ref, hid_ref):
    n_tok = out_ref.shape[0]
    hid_ref[...] = jnp.zeros_like(hid_ref)

    def body(l, _):
        rows = x_ref[pl.ds(l, n_tok, stride=CMP_STRIDE), :] + pos_ref[pl.ds(l, 1), :]
        hid_ref[...] += _dot(rows.astype(BF16), w1_ref[l])
        return 0

    lax.fori_loop(0, CMP_LEN, body, 0)
    out_ref[...] = _dot(jax.nn.gelu(hid_ref[...]).astype(BF16), w2_ref[...])


def _compress(x, pos, w1, w2):
    S, width = x.shape
    n_tok = S // CMP_STRIDE
    G = NSA_KV_GROUPS
    def block_diag(w):
        w = w.astype(BF16)
        z = jnp.zeros_like(w)
        return jnp.concatenate([jnp.concatenate([w if k == g else z for k in range(G)], axis=-1)
                                for g in range(G)], axis=-2)

    w1_bd = block_diag(w1.reshape(CMP_LEN, NSA_HEAD_DIM, CMP_HIDDEN))
    w2_bd = block_diag(w2)
    x_pad = jnp.pad(x, ((0, CMP_LEN - CMP_STRIDE), (0, 0)))
    full = lambda a: _const_spec(a.shape)
    pos2 = jnp.concatenate([pos] * G, axis=1)
    return pl.pallas_call(
        _compress_kernel,
        grid=(1,),
        in_specs=[full(x_pad), full(pos2), full(w1_bd), full(w2_bd)],
        out_specs=pl.BlockSpec((n_tok, width), lambda i: (0, 0)),
        out_shape=jax.ShapeDtypeStruct((n_tok, width), F32),
        scratch_shapes=[pltpu.VMEM((n_tok, G * CMP_HIDDEN), F32)],
        compiler_params=_params("arbitrary"),
        name="nsa_compress",
    )(x_pad, pos2, w1_bd, w2_bd)


def _t5_bucket_np(dist):
    dist = np.maximum(dist, 0)
    max_exact = REL_BUCKETS // 2
    d_f = np.maximum(dist, 1).astype(np.float32)
    large = max_exact + (np.log(d_f / np.float32(max_exact)) / np.float32(math.log(REL_MAX_DIST / max_exact))
                         * np.float32(REL_BUCKETS - max_exact)).astype(np.int32)
    large = np.minimum(large, REL_BUCKETS - 1)
    return np.where(dist < max_exact, dist, large)


def _bias_table_t(rel_bias, dist, shift):
    bucket = jnp.asarray(np.where(dist >= 0, _t5_bucket_np(dist), -1), jnp.int32)
    rb = rel_bias - rel_bias[REL_BUCKETS - 1][None, :] if shift else rel_bias
    onehot = (bucket[None] == jnp.arange(REL_BUCKETS, dtype=jnp.int32)[:, None, None]).astype(F32)
    tab = jnp.einsum('bh,bkq->hkq', rb, onehot, precision=lax.Precision.HIGHEST)
    tab = jnp.where(jnp.asarray(dist == -1)[None], NEG, tab)
    K, Q = dist.shape
    return tab.reshape(NSA_KV_GROUPS, NSA_HPG, K, Q).transpose(0, 2, 1, 3).reshape(NSA_KV_GROUPS, K, NSA_HPG * Q)


def _nsa_kernel(n_keys, q_ref, ga_ref, kc_ref, vct_ref, tct_ref, ksel_ref, vselt_ref, et_ref, tnt_ref,
                kwin_ref, vwint_ref, twt_ref, out_ref, sc_ref, ps_ref, sa_ref, sb_ref, ma_ref, mb_ref):
    i = pl.program_id(0)
    dh = NSA_HEAD_DIM
    nb = (ps_ref.shape[1] - 2 * SCR_PAD) // (SEL_BLOCK // CMP_STRIDE)

    @pl.when(i == 0)
    def _():
        sc_ref[...] = jnp.zeros_like(sc_ref)
        ps_ref[...] = jnp.zeros_like(ps_ref)

    def col_reduce(op, red, x):
        rows = x.shape[0]
        while rows <= 256 and rows % 16 == 0:
            rows //= 2
            x = op(x[:rows], x[rows:])
        return red(x, axis=0, keepdims=True)

    def colmax(x):
        return col_reduce(jnp.maximum, jnp.max, x)

    def tile4(x):
        return jnp.concatenate([x] * NSA_HPG, axis=1)

    gates = ga_ref[...]
    tl = i // QB_PER_CMP_TILE
    sub = i % QB_PER_CMP_TILE
    tp = tl - 1
    groups = range(NSA_KV_GROUPS)
    q_t_g = [jnp.concatenate([q_ref[NSA_HPG * g + j] for j in range(NSA_HPG)], axis=1) for g in groups]
    zeros = jnp.zeros((dh, QW), BF16)
    wq_g = [jnp.concatenate([q_t_g[g] if k == g else zeros for k in groups], axis=0) for g in groups]

    def cmp_scores(g, t):
        return _dot(kc_ref[pl.ds(pl.multiple_of(t * CMP_TILE, CMP_TILE), CMP_TILE), :], wq_g[g])

    def sc_rows(t):
        return pl.ds(pl.multiple_of(SCR_PAD + t * CMP_TILE, 8), CMP_TILE)

    def cmp_far(t, m):
        out = []
        for g in groups:
            s = cmp_scores(g, t)
            sc_ref[g, sc_rows(t), :] = s
            out.append(jnp.maximum(m[g], colmax(s)))
        return tuple(out)

    def tile_loop(count, step, carry):
        first, left = 0, count
        for per_trip in CMP_TRIP_TILES:
            trips = left // per_trip

            def body(u, carry, first=first, per_trip=per_trip):
                for k in range(per_trip):
                    carry = step(first + per_trip * u + k, carry)
                return carry

            carry = lax.fori_loop(0, trips, body, carry)
            first, left = first + per_trip * trips, left - per_trip * trips
        return carry

    m_c = tile_loop(jnp.maximum(tp, 0), cmp_far, tuple(jnp.full((1, QW), NEG, F32) for _ in groups))
    tpc = jnp.maximum(tp, 0)
    off = CMP_TILE - CMP_TOK_PER_QB - CMP_TOK_PER_QB * sub
    m_c = list(m_c)
    for g in groups:
        s = (cmp_scores(g, tpc) + tct_ref[g, pl.ds(pl.multiple_of(off, 8), CMP_TILE), :]
             + jnp.where(tp < 0, NEG, 0.0))
        sc_ref[g, sc_rows(tpc), :] = s
        m_c[g] = jnp.maximum(m_c[g], colmax(s))
    for g in groups:
        s = cmp_scores(g, tl) + tct_ref[g, pl.ds(pl.multiple_of(off + CMP_TILE, 8), CMP_TILE), :]
        sc_ref[g, sc_rows(tl), :] = s
        m_c[g] = jnp.maximum(m_c[g], colmax(s))

    def cmp_exp(t, carry):
        out = []
        for g in groups:
            l, acc = carry[g]
            e = jnp.exp(sc_ref[g, sc_rows(t), :] - m_c[g])
            sc_ref[g, sc_rows(t), :] = e
            vt = vct_ref[g, :, pl.ds(pl.multiple_of(t * CMP_TILE, CMP_TILE), CMP_TILE)]
            out.append((l + col_reduce(jnp.add, jnp.sum, e), acc + _dot(vt, e.astype(BF16))))
        return tuple(out)

    cmp_out = tile_loop(tl + 1, cmp_exp,
                        tuple((jnp.zeros((1, QW), F32), jnp.zeros((dh, QW), F32)) for _ in groups))
    inv_c = [jnp.where(m_c[g] > 0.5 * NEG, 1.0 / jnp.maximum(cmp_out[g][0], 1e-30), 0.0) for g in groups]
    o_c_g = [cmp_out[g][1] * inv_c[g] for g in groups]

    def cmp_psum(t, _):
        for g in groups:
            p = sc_ref[g, sc_rows(t), :] * inv_c[g]
            ps = p[:, 0:Q_BLOCK]
            for j in range(1, NSA_HPG):
                ps = ps + p[:, j * Q_BLOCK:(j + 1) * Q_BLOCK]
            ps_ref[g, sc_rows(t), :] = ps.astype(BF16).astype(F32)
        return 0

    tile_loop(tl + 1, cmp_psum, 0)
    per_blk = SEL_BLOCK // CMP_STRIDE
    imp_g = []
    for g in groups:
        imp = ps_ref[g, pl.ds(SCR_PAD - 1, nb, stride=per_blk), :]
        for k in range(1, CMP_LEN // CMP_STRIDE + per_blk - 1):
            imp = imp + ps_ref[g, pl.ds(SCR_PAD - 1 + k, nb, stride=per_blk), :]
        imp_g.append(imp)

    kw0 = pl.multiple_of(jnp.maximum(i * Q_BLOCK - WINDOW, 0), Q_BLOCK)
    tw0 = pl.multiple_of(jnp.maximum(WINDOW - i * Q_BLOCK, 0), Q_BLOCK)
    o_w_g = []
    for g in groups:
        s = _dot(kwin_ref[pl.ds(kw0, WIN_KEYS), :], wq_g[g]) + twt_ref[g, pl.ds(tw0, WIN_KEYS), :]
        e = jnp.exp(s - colmax(s)).astype(BF16)
        acc_w = _dot(vwint_ref[g, :, pl.ds(kw0, WIN_KEYS)], e)
        o_w_g.append(acc_w[0:dh] * (1.0 / jnp.maximum(acc_w[dh:dh + 1], 1e-30)))

    n_io = lax.broadcasted_iota(jnp.int32, (nb, Q_BLOCK), 0)
    r_io = lax.broadcasted_iota(jnp.int32, (nb, Q_BLOCK), 1)
    back = (i * Q_BLOCK + r_io) // SEL_BLOCK - n_io
    valid = back >= 0
    forced = (n_io == 0) | (valid & (back < N_LOCAL_SEL))
    work_g = [jnp.where(forced, -jnp.inf, jnp.where(valid, imp_g[g], -1e9)) for g in groups]
    n_f = n_io.astype(F32)
    def select_rounds(rows):
        def run(*work):
            work = [w[:rows] for w in work]
            for _ in range(SEL_TOP_N - 1 - N_LOCAL_SEL):
                for g in groups:
                    mx = colmax(work[g])
                    first = col_reduce(jnp.minimum, jnp.min, jnp.where(work[g] == mx, n_f[:rows], float(nb)))
                    work[g] = jnp.where(n_f[:rows] == first, -jnp.inf, work[g])
            rest = [jnp.zeros((nb - rows, Q_BLOCK), F32)] if rows < nb else []
            return tuple(jnp.concatenate([jnp.where(w == -jnp.inf, 1.0, 0.0)] + rest, axis=0) for w in work)
        return run

    levels = [SEL_ROWS_STEP * (k + 1) for k in range(nb // SEL_ROWS_STEP)]
    level = jnp.minimum((i * Q_BLOCK + Q_BLOCK - 1) // (SEL_BLOCK * SEL_ROWS_STEP), len(levels) - 1)
    sel_g = list(lax.switch(level, [select_rounds(rows) for rows in levels], *work_g))

    far_blocks = (i - 1) * (Q_BLOCK // SEL_BLOCK)
    selm_g = [(jnp.where(n_io < far_blocks, sel_g[g], 0.0) - 1.0).astype(BF16) for g in groups]

    far_end = (i - 1) * Q_BLOCK
    n_far = (jnp.maximum(far_end, 0) + SEL_TK - 1) // SEL_TK

    def flash_step(carry, s, vt):
        m_i, acc = carry
        m_new = jnp.maximum(m_i, colmax(s))
        alpha = jnp.exp(m_i - m_new)
        p = jnp.exp(s - m_new).astype(BF16)
        return m_new, alpha * acc + _dot(vt, p)

    carry = tuple((jnp.full((1, QW), NEG, F32), jnp.zeros((PV_ROWS, QW), F32)) for _ in groups)
    for w in range(nb // SEL_WIN):
        w_far = [jnp.concatenate([tile4(selm_g[g][w * SEL_WIN:(w + 1) * SEL_WIN]), wq_g[g]], axis=0)
                 for g in groups]

        lo = w * SEL_WIN_TILES
        last = min((w + 1) * SEL_WIN_TILES, n_keys // SEL_TK) - 1
        trips = jnp.maximum(jnp.minimum(n_far, last + 1) - lo + 1, 0) // 2

        def far_scores(kt, s_ref, mx_ref, g, w=w, w_far=w_far):
            k0 = pl.multiple_of(kt * SEL_TK, SEL_TK)
            e0 = pl.multiple_of((kt - w * SEL_WIN_TILES) * SEL_TK, SEL_TK)
            lhs = jnp.concatenate([et_ref[pl.ds(e0, SEL_TK), :],
                                   ksel_ref[pl.ds(k0, SEL_TK), :]], axis=1)
            s = _dot(lhs, w_far[g])
            s_ref[g] = s
            mx_ref[g] = colmax(s)

        def far_update(kt, s_ref, mx_ref, g, carry_g):
            k0 = pl.multiple_of(kt * SEL_TK, SEL_TK)
            m_i, acc = carry_g
            m_new = jnp.maximum(m_i, mx_ref[g])
            p = jnp.exp(s_ref[g] - m_new).astype(BF16)
            return m_new, jnp.exp(m_i - m_new) * acc + _dot(vselt_ref[g, :, pl.ds(k0, SEL_TK)], p)

        def far_trip(tiles, first, last=last, far_scores=far_scores):
            def body(u, carry):
                kt = first + tiles * u
                carry = list(carry)
                bufs = [(sa_ref, ma_ref), (sb_ref, mb_ref)]
                for t in range(tiles):
                    for g in groups:
                        far_scores(jnp.minimum(kt + t + 1, last), *bufs[(t + 1) % 2], g)
                        carry[g] = far_update(kt + t, *bufs[t % 2], g, carry[g])
                return tuple(carry)
            return body

        @pl.when(trips > 0)
        def _(lo=lo, far_scores=far_scores):
            for g in groups:
                far_scores(jnp.int32(lo), sa_ref, ma_ref, g)

        first, left = lo, trips
        for tiles in FAR_TRIP_TILES:
            count = left // (tiles // 2)
            carry = lax.fori_loop(0, count, far_trip(tiles, first), carry)
            first, left = first + tiles * count, left - count * (tiles // 2)

    heads_out = []
    for g in groups:
        wq, o_c, sel = wq_g[g], o_c_g[g], sel_g[g]
        k0n = pl.multiple_of(jnp.maximum(far_end, 0), Q_BLOCK)
        t0n = pl.multiple_of(jnp.maximum(-far_end, 0), Q_BLOCK)
        near_blocks = 2 * Q_BLOCK // SEL_BLOCK
        pb = lax.broadcasted_iota(jnp.int32, (16, nb), 0)
        pn = lax.broadcasted_iota(jnp.int32, (16, nb), 1)
        pick_rows = jnp.where((pn == jnp.maximum(far_blocks, 0) + pb) & (pb < near_blocks), 1.0, 0.0).astype(BF16)
        sel_rows = _dot(pick_rows, sel.astype(BF16))
        add_near = jnp.concatenate(
            [jnp.broadcast_to((sel_rows[b:b + 1, :] - 1.0) * (-NEG), (SEL_BLOCK, Q_BLOCK)) for b in range(near_blocks)],
            axis=0)
        s = (_dot(ksel_ref[pl.ds(k0n, 2 * Q_BLOCK), :], wq) + tnt_ref[g, pl.ds(t0n, 2 * Q_BLOCK), :]
             + tile4(add_near))
        m_s, acc_s = flash_step(carry[g], s, vselt_ref[g, :, pl.ds(k0n, 2 * Q_BLOCK)])
        o_s = acc_s[0:dh] * (1.0 / jnp.maximum(acc_s[dh:dh + 1], 1e-30))

        o_w = o_w_g[g]

        for j in range(NSA_HPG):
            hd = NSA_HPG * g + j
            cols = slice(j * Q_BLOCK, (j + 1) * Q_BLOCK)
            heads_out.append(gates[hd:hd + 1] * o_c[:, cols]
                             + gates[NSA_HEADS + hd:NSA_HEADS + hd + 1] * o_s[:, cols]
                             + gates[2 * NSA_HEADS + hd:2 * NSA_HEADS + hd + 1] * o_w[:, cols])

    out_ref[...] = jnp.concatenate(heads_out, axis=0).T.astype(out_ref.dtype)


def _nsa_attention(q_t, gates_t, kc, vct, tct, ksel, vselt, et, tnt, kwin, vwint, twt):
    H, dh, S = q_t.shape
    n_qb = S // Q_BLOCK
    n_tok = S // CMP_STRIDE
    nb = -(-(S // SEL_BLOCK) // SEL_WIN) * SEL_WIN
    ps_rows = nb * (SEL_BLOCK // CMP_STRIDE) + 2 * SCR_PAD
    full = lambda a: _const_spec(a.shape)
    return pl.pallas_call(
        functools.partial(_nsa_kernel, S),
        grid=(n_qb,),
        in_specs=[pl.BlockSpec((H, dh, Q_BLOCK), lambda i: (0, 0, i)),
                  pl.BlockSpec((gates_t.shape[0], Q_BLOCK), lambda i: (0, i)),
                  full(kc), full(vct), full(tct), full(ksel), full(vselt), full(et), full(tnt),
                  full(kwin), full(vwint), full(twt)],
        out_specs=pl.BlockSpec((Q_BLOCK, H * dh), lambda i: (i, 0)),
        out_shape=jax.ShapeDtypeStruct((S, H * dh), BF16),
        scratch_shapes=[pltpu.VMEM((NSA_KV_GROUPS, SCR_PAD + n_tok, QW), F32),
                        pltpu.VMEM((NSA_KV_GROUPS, ps_rows, Q_BLOCK), F32),
                        pltpu.VMEM((NSA_KV_GROUPS, SEL_TK, QW), F32),
                        pltpu.VMEM((NSA_KV_GROUPS, SEL_TK, QW), F32),
                        pltpu.VMEM((NSA_KV_GROUPS, 1, QW), F32),
                        pltpu.VMEM((NSA_KV_GROUPS, 1, QW), F32)],
        compiler_params=_params("arbitrary"),
        name="nsa_attention",
    )(q_t, gates_t, kc, vct, tct, ksel, vselt, et, tnt, kwin, vwint, twt)


def _retention_kernel(q_ref, k_ref, v_ref, g_ref, nw_ref, dmask_ref, qd_ref, kd_ref, cd_ref, out_ref, state_ref):
    @pl.when(pl.program_id(0) == 0)
    def _():
        state_ref[...] = jnp.zeros_like(state_ref)

    states = [state_ref[hd] for hd in range(RET_HEADS)]
    for c in range(q_ref.shape[0] // RET_CHUNK):
        rows = slice(c * RET_CHUNK, (c + 1) * RET_CHUNK)
        for hd in range(RET_HEADS):
            ks = slice(hd * RET_QK_DIM, (hd + 1) * RET_QK_DIM)
            vs = slice(hd * RET_V_DIM, (hd + 1) * RET_V_DIM)
            q = q_ref[rows, ks]
            k = k_ref[rows, ks]
            v = v_ref[rows, vs]
            state = states[hd]
            inner = _dot_nt(q, k.astype(BF16)) * dmask_ref[hd]
            y = _dot(inner.astype(BF16), v) + _dot(q, state.astype(BF16)) * qd_ref[hd]
            kd = k * kd_ref[hd]
            states[hd] = state * cd_ref[hd] + _dot(kd.T.astype(BF16), v)
            mu = jnp.mean(y, axis=-1, keepdims=True)
            yc = y - mu
            var = jnp.mean(yc * yc, axis=-1, keepdims=True)
            yn = yc * lax.rsqrt(var + EPS) * nw_ref[:, vs]
            out_ref[rows, vs] = (jax.nn.silu(g_ref[rows, vs]) * yn).astype(out_ref.dtype)
    for hd in range(RET_HEADS):
        state_ref[hd] = states[hd]


RET_STEP_CHUNKS = 8


def _retention(q, k, v, g, norm_w, dmask, qd, kd, cd):
    S = q.shape[0]
    C = RET_CHUNK * RET_STEP_CHUNKS
    row = lambda w: pl.BlockSpec((C, w), lambda i: (i, 0))
    full = lambda a: _const_spec(a.shape)
    return pl.pallas_call(
        _retention_kernel,
        grid=(S // C,),
        in_specs=[row(RET_QK_W), row(RET_QK_W), row(RET_V_W), row(RET_V_W),
                  full(norm_w), full(dmask), full(qd), full(kd), full(cd)],
        out_specs=row(RET_V_W),
        out_shape=jax.ShapeDtypeStruct((S, RET_V_W), BF16),
        scratch_shapes=[pltpu.VMEM((RET_HEADS, RET_QK_DIM, RET_V_DIM), F32)],
        compiler_params=_params("arbitrary"),
        name="retention",
    )(q, k, v, g, norm_w, dmask, qd, kd, cd)


MERGE_TM = 512
MERGE_SUB_ROWS = 256
GRP_LANE = 40
RANK_LANE = 41


def _merge_kernel(x_ref, ya_ref, yr_ref, gta_ref, gtr_ref, wa_ref, wr_ref, wo_ref, nw_ref, wrt_ref, brt_ref,
                  tri_ref, x1_ref, h_ref, comb_ref, cnt_ref, carry_ref):
    @pl.when(pl.program_id(0) == 0)
    def _():
        carry_ref[...] = jnp.zeros_like(carry_ref)

    sub = tri_ref.shape[0]
    carry = carry_ref[...]
    for r in range(x_ref.shape[0] // sub):
        carry = _merge_rows(slice(r * sub, (r + 1) * sub), carry, x_ref, ya_ref, yr_ref, gta_ref, gtr_ref,
                            wa_ref, wr_ref, wo_ref, nw_ref, wrt_ref, brt_ref, tri_ref, x1_ref, h_ref, comb_ref)
    carry_ref[...] = carry
    cnt_ref[...] = jnp.broadcast_to(carry, cnt_ref.shape)


def _merge_rows(rows, carry, x_ref, ya_ref, yr_ref, gta_ref, gtr_ref, wa_ref, wr_ref, wo_ref, nw_ref, wrt_ref,
                brt_ref, tri_ref, x1_ref, h_ref, comb_ref):
    merged = gta_ref[rows] * _dot(ya_ref[rows], wa_ref[...]) + gtr_ref[rows] * _dot(yr_ref[rows], wr_ref[...])
    x1 = x_ref[rows] + _dot(merged.astype(BF16), wo_ref[...])
    x1_ref[rows] = x1
    h = (x1 * lax.rsqrt(jnp.mean(x1 * x1, axis=-1, keepdims=True) + EPS) * nw_ref[...]).astype(BF16)
    h_ref[rows] = _pack_bf16_pairs(h.astype(F32))
    logits = _dot(h, wrt_ref[...]) + brt_ref[...]
    lane = lax.broadcasted_iota(jnp.int32, logits.shape, 1)
    lane_f = lane.astype(F32)
    big = float(LANE)
    is_grp = (lane >= N_EXPERTS) & (lane < N_EXPERTS + N_GROUPS)
    lg = jnp.where(is_grp, logits, -jnp.inf)
    eg = jnp.where(is_grp, jnp.exp(lg - jnp.max(lg, axis=-1, keepdims=True)), 0.0)
    pg = eg / jnp.sum(eg, axis=-1, keepdims=True)
    p_grp = jnp.max(pg, axis=-1, keepdims=True)
    grp = jnp.min(jnp.where(is_grp & (pg == p_grp), lane_f, big), axis=-1, keepdims=True) - N_EXPERTS
    in_grp = (lane_f >= grp * EXPERTS_PER_GROUP) & (lane_f < (grp + 1.0) * EXPERTS_PER_GROUP)
    le = jnp.where(in_grp, logits, -jnp.inf)
    m1 = jnp.max(le, axis=-1, keepdims=True)
    i1 = jnp.min(jnp.where(le == m1, lane_f, big), axis=-1, keepdims=True)
    le2 = jnp.where(lane_f == i1, -jnp.inf, le)
    m2 = jnp.max(le2, axis=-1, keepdims=True)
    i2 = jnp.min(jnp.where(le2 == m2, lane_f, big), axis=-1, keepdims=True)
    e2 = jnp.exp(m2 - m1)
    den = 1.0 + e2
    onehot = jnp.where(lane_f == grp, 1.0, 0.0)
    before = _dot(tri_ref[...], onehot.astype(BF16)) + carry
    rank = jnp.sum(onehot * before, axis=-1, keepdims=True)
    comb_ref[rows] = (jnp.where(lane_f == i1, (1.0 / den) * p_grp, 0.0)
                      + jnp.where(lane_f == i2, (e2 / den) * p_grp, 0.0)
                      + jnp.where(lane == GRP_LANE, grp, 0.0)
                      + jnp.where(lane == RANK_LANE, rank, 0.0))
    return carry + jnp.sum(onehot, axis=0, keepdims=True)


def _pack_bf16_pairs(x):
    n = x.shape[1] // 2
    return pltpu.pack_elementwise([x[:, :n], x[:, n:]], packed_dtype=BF16)


def _unpack_bf16_pairs(w):
    return tuple(pltpu.unpack_elementwise(w, index=k, packed_dtype=BF16, unpacked_dtype=F32) for k in range(2))


def _merge(x2d, ya, yr, gta, gtr, wa, wr, wo, nw, wrt, brt):
    S = x2d.shape[0]
    tm = MERGE_TM
    row = lambda w: pl.BlockSpec((tm, w), lambda i: (i, 0))
    full = lambda a: _const_spec(a.shape)
    sub = MERGE_SUB_ROWS
    tri = jnp.asarray(np.tril(np.ones((sub, sub), np.float32), -1), BF16)
    return pl.pallas_call(
        _merge_kernel,
        grid=(S // tm,),
        in_specs=[row(D_MODEL), row(NSA_Q_W), row(RET_V_W), row(D_MODEL), row(D_MODEL),
                  full(wa), full(wr), full(wo), full(nw), full(wrt), full(brt), full(tri)],
        out_specs=[row(D_MODEL), row(D_MODEL // 2), row(LANE), pl.BlockSpec((8, LANE), lambda i: (0, 0))],
        out_shape=[jax.ShapeDtypeStruct((S, D_MODEL), F32), jax.ShapeDtypeStruct((S, D_MODEL // 2), jnp.uint32),
                   jax.ShapeDtypeStruct((S, LANE), F32), jax.ShapeDtypeStruct((8, LANE), F32)],
        scratch_shapes=[pltpu.VMEM((1, LANE), F32)],
        compiler_params=_params("arbitrary"),
        name="merge_router",
    )(x2d, ya, yr, gta, gtr, wa, wr, wo, nw, wrt, brt, tri)


MOE_TM = 1024
MOE_SUB_TILES = 2
RANK_TM = 512
HALF = D_MODEL // 2


def _permute_kernel(pos_ref, h_ref, comb_ref, hs_ref, cs_ref):
    i = pl.program_id(0)

    @pl.when(i == 0)
    def _():
        hs_ref[...] = jnp.zeros_like(hs_ref)
        cs_ref[...] = jnp.zeros_like(cs_ref)

    tm = h_ref.shape[0]

    def body(r, _):
        d = pos_ref[i * tm + r]
        hs_ref[pl.ds(d, 1), :] = h_ref[pl.ds(r, 1), :]
        cs_ref[pl.ds(d, 1), :] = comb_ref[pl.ds(r, 1), :]
        return 0

    lax.fori_loop(0, tm, body, 0, unroll=8)


def _permute(pos, hu, comb, n_sorted):
    S = hu.shape[0]
    tm = RANK_TM
    resident = lambda w: pl.BlockSpec((n_sorted, w), lambda i, p: (0, 0), pipeline_mode=pl.Buffered(1))
    return pl.pallas_call(
        _permute_kernel,
        grid_spec=pltpu.PrefetchScalarGridSpec(
            num_scalar_prefetch=1, grid=(S // tm,),
            in_specs=[pl.BlockSpec((tm, HALF), lambda i, p: (i, 0)), pl.BlockSpec((tm, LANE), lambda i, p: (i, 0))],
            out_specs=[resident(HALF), resident(LANE)]),
        out_shape=[jax.ShapeDtypeStruct((n_sorted, HALF), jnp.uint32), jax.ShapeDtypeStruct((n_sorted, LANE), F32)],
        compiler_params=_params("arbitrary"),
        name="moe_permute",
    )(pos, hu, comb)


def _moe_kernel(tg_ref, tv_ref, hs_ref, cs_ref, wg_ref, wu_ref, wd_ref, ys_ref, lo_ref, hi_ref, acc_ref):
    i = pl.program_id(0)
    e = pl.program_id(1)

    @pl.when(e == 0)
    def _():
        lo, hi = _unpack_bf16_pairs(hs_ref[...])
        lo_ref[...] = lo.astype(BF16)
        hi_ref[...] = hi.astype(BF16)
        acc_ref[...] = jnp.zeros_like(acc_ref)

    @pl.when(tv_ref[i] > 0)
    def _():
        wg = wg_ref[0].astype(BF16)
        wu = wu_ref[0].astype(BF16)
        wd = wd_ref[0].astype(BF16)
        sub = lo_ref.shape[0] // MOE_SUB_TILES
        for r in range(MOE_SUB_TILES):
            rows = slice(r * sub, (r + 1) * sub)
            lo, hi = lo_ref[rows], hi_ref[rows]
            gate = _dot(lo, wg[:HALF]) + _dot(hi, wg[HALF:])
            up = _dot(lo, wu[:HALF]) + _dot(hi, wu[HALF:])
            comb = cs_ref[rows]
            lane = lax.broadcasted_iota(jnp.int32, comb.shape, 1)
            c = jnp.sum(jnp.where(lane == tg_ref[i] * EXPERTS_PER_GROUP + e, comb, 0.0), axis=-1, keepdims=True)
            acc_ref[rows] += _dot((jax.nn.silu(gate) * up).astype(BF16), wd) * c

    @pl.when(e == pl.num_programs(1) - 1)
    def _():
        ys_ref[...] = _pack_bf16_pairs(acc_ref[...].astype(BF16).astype(F32))


def _moe(tile_group, tile_valid, hs, cs, wg, wu, wd):
    n_sorted = hs.shape[0]
    tm = MOE_TM
    row = lambda w: pl.BlockSpec((tm, w), lambda i, e, tg, tv: (i, 0))
    wspec = lambda a, b: pl.BlockSpec((1, a, b), lambda i, e, tg, tv: (tg[i] * EXPERTS_PER_GROUP + e, 0, 0))
    return pl.pallas_call(
        _moe_kernel,
        grid_spec=pltpu.PrefetchScalarGridSpec(
            num_scalar_prefetch=2, grid=(n_sorted // tm, EXPERTS_PER_GROUP),
            in_specs=[row(HALF), row(LANE), wspec(D_MODEL, EXPERT_FF), wspec(D_MODEL, EXPERT_FF),
                      wspec(EXPERT_FF, D_MODEL)],
            out_specs=row(HALF),
            scratch_shapes=[pltpu.VMEM((tm, HALF), BF16), pltpu.VMEM((tm, HALF), BF16),
                            pltpu.VMEM((tm, D_MODEL), F32)]),
        out_shape=jax.ShapeDtypeStruct((n_sorted, HALF), jnp.uint32),
        compiler_params=_params("parallel", "arbitrary"),
        name="moe_experts",
    )(tile_group, tile_valid, hs, cs, wg, wu, wd)


def _unpermute_kernel(pos_ref, ys_ref, x1_ref, nw_ref, out_ref, stage_ref):
    i = pl.program_id(0)
    tm = x1_ref.shape[0]

    def body(r, _):
        stage_ref[pl.ds(r, 1), :] = ys_ref[pl.ds(pos_ref[i * tm + r], 1), :]
        return 0

    lax.fori_loop(0, tm, body, 0, unroll=8)
    lo, hi = _unpack_bf16_pairs(stage_ref[...])
    x2 = x1_ref[...] + jnp.concatenate([lo, hi], axis=1)
    out_ref[...] = x2 * lax.rsqrt(jnp.mean(x2 * x2, axis=-1, keepdims=True) + EPS) * nw_ref[...]


def _unpermute_residual_norm(pos, ys, x1, nw):
    S = x1.shape[0]
    tm = RANK_TM
    return pl.pallas_call(
        _unpermute_kernel,
        grid_spec=pltpu.PrefetchScalarGridSpec(
            num_scalar_prefetch=1, grid=(S // tm,),
            in_specs=[pl.BlockSpec(ys.shape, lambda i, p: (0, 0), pipeline_mode=pl.Buffered(1)),
                      pl.BlockSpec((tm, D_MODEL), lambda i, p: (i, 0)),
                      pl.BlockSpec((1, D_MODEL), lambda i, p: (0, 0))],
            out_specs=pl.BlockSpec((tm, D_MODEL), lambda i, p: (i, 0)),
            scratch_shapes=[pltpu.VMEM((tm, HALF), jnp.uint32)]),
        out_shape=jax.ShapeDtypeStruct((S, D_MODEL), F32),
        compiler_params=_params("parallel"),
        name="moe_unpermute_norm",
    )(pos, ys, x1, nw)


def _hierarchical_moe(hu, comb, cnt, x1, wg, wu, wd, final_norm_w):
    S = hu.shape[0]
    tm = MOE_TM
    n_sorted = S + N_GROUPS * tm
    assert RANK_LANE == GRP_LANE + 1
    grp_rank = comb[:, GRP_LANE:RANK_LANE + 1].astype(jnp.int32)
    grp, rank = grp_rank[:, 0], grp_rank[:, 1]
    counts = cnt[0, :N_GROUPS].astype(jnp.int32)
    padded = (counts + tm - 1) // tm * tm
    ends = jnp.cumsum(padded)
    starts = ends - padded
    pos = rank + jnp.sum(jnp.where(grp[:, None] == jnp.arange(N_GROUPS)[None, :], starts[None, :], 0), axis=1)
    tile_start = jnp.arange(n_sorted // tm, dtype=jnp.int32) * tm
    tile_group = jnp.minimum(jnp.sum((tile_start[:, None] >= ends[None, :]).astype(jnp.int32), axis=1), N_GROUPS - 1)
    tile_valid = (tile_start < ends[N_GROUPS - 1]).astype(jnp.int32)
    hs, cs = _permute(pos, hu, comb, n_sorted)
    ys = _moe(tile_group, tile_valid, hs, cs, wg, wu, wd)
    return _unpermute_residual_norm(pos, ys, x1, final_norm_w)


def _positional_tables():
    half = RET_QK_DIM // 2
    inv = ROPE_BASE ** (-jnp.arange(half, dtype=F32) / half)
    inv2 = jnp.concatenate([inv, inv]).reshape(1, RET_QK_DIM)
    C, H = RET_CHUNK, RET_HEADS
    log_gamma = jnp.log(1.0 - 2.0 ** (-5.0 - jnp.arange(H, dtype=F32)))
    idx = jnp.arange(C, dtype=F32)
    diff = idx[:, None] - idx[None, :]
    dmask = jnp.where(diff >= 0, jnp.exp(jnp.maximum(diff, 0.0)[None] * log_gamma[:, None, None]), 0.0)
    qd = jnp.exp((idx + 1.0)[None, :] * log_gamma[:, None])
    kd = jnp.exp((C - 1.0 - idx)[None, :] * log_gamma[:, None])
    cd = jnp.exp(C * log_gamma)
    qd = jnp.broadcast_to(qd[:, :, None], (H, C, RET_V_DIM))
    kd = jnp.broadcast_to(kd[:, :, None], (H, C, RET_QK_DIM))
    cd = jnp.broadcast_to(cd[:, None, None], (H, 1, RET_V_DIM))
    return inv2, dmask, qd, kd, cd


def _attention_tables(rel_bias):
    r = np.arange(Q_BLOCK)[None, :]
    cc = np.arange(3 * CMP_TILE)[:, None] - CMP_TILE
    d = r - CMP_STRIDE * (cc - (CMP_TILE - CMP_TOK_PER_QB)) - (CMP_LEN - 1)
    d = np.where(cc < 0, -2, np.where((cc >= CMP_TILE) | (d < 0), -1, d))
    tct = _bias_table_t(rel_bias, d, shift=True)
    d = r + Q_BLOCK - np.arange(3 * Q_BLOCK)[:, None]
    tnt = _bias_table_t(rel_bias, np.where(d < 0, -1, d), shift=True)
    d = r + WINDOW - np.arange(WIN_KEYS + WINDOW)[:, None]
    twt = _bias_table_t(rel_bias, np.where((d < 0) | (d >= WINDOW), -1, d), shift=False)
    kk = np.arange(SEL_WIN * SEL_BLOCK)[:, None] // SEL_BLOCK
    et = jnp.where(jnp.asarray(kk == np.arange(SEL_WIN)[None, :]), -NEG, 0.0).astype(BF16)
    return tct, tnt, twt, et


def _layer(x2d, rel_bias, norm_mix, w_in, cmp_pos_k, cmp_k_w1, cmp_k_w2, cmp_pos_v, cmp_v_w1, cmp_v_w2,
           ret_norm, w_nsa_up, w_ret_up, w_out, norm_ffn, w_rg, b_rg, w_re, b_re, moe_wg, moe_wu, moe_wd,
           final_norm_w):
    S = x2d.shape[0]
    G, dh = NSA_KV_GROUPS, NSA_HEAD_DIM
    assert S % (CMP_TILE * CMP_STRIDE) == 0 and S // SEL_BLOCK >= SEL_TOP_N

    o = np.cumsum((0,) + (NSA_Q_W, 6 * NSA_KV_W, 3 * NSA_HEADS, RET_QK_W, RET_QK_W, RET_V_W, RET_V_W,
                          D_MODEL, D_MODEL))
    seg = lambda n: w_in[:, o[n]:o[n + 1]]
    w_cat = jnp.concatenate([seg(0), seg(1), seg(3), seg(4), seg(5), seg(6), seg(7), seg(8), seg(2),
                             jnp.zeros((D_MODEL, LANE - 3 * NSA_HEADS), F32)], axis=1).astype(BF16)
    inv2, dmask, qd, kd, cd = _positional_tables()

    (qa_t, kcmp, vcmp, ksel, vsel_t, kwin, vwin_t, ga_t, qr, kr, vr, gr, gta, gtr) = _input_projection(
        x2d, norm_mix.reshape(1, D_MODEL), w_cat, inv2)

    nch = S // CMP_STRIDE
    kc = _compress(kcmp, cmp_pos_k, cmp_k_w1, cmp_k_w2).astype(BF16)
    vct = _compress(vcmp, cmp_pos_v, cmp_v_w1, cmp_v_w2).T.reshape(G, dh, nch).astype(BF16)

    tct, tnt, twt, et = _attention_tables(rel_bias)
    ya = _nsa_attention(qa_t.reshape(NSA_HEADS, dh, S), ga_t, kc, vct, tct,
                        ksel, vsel_t.reshape(G, PV_ROWS, S), et, tnt,
                        kwin, vwin_t.reshape(G, PV_ROWS, S), twt)

    yr = _retention(qr, kr, vr, gr, ret_norm.reshape(1, RET_V_W), dmask, qd, kd, cd)

    wrt = jnp.concatenate([w_re, w_rg, jnp.zeros((D_MODEL, LANE - N_EXPERTS - N_GROUPS), F32)], axis=1).astype(BF16)
    brt = jnp.concatenate([b_re, b_rg, jnp.zeros((LANE - N_EXPERTS - N_GROUPS,), F32)]).reshape(1, LANE)
    x1, h2, comb, cnt = _merge(x2d, ya, yr, gta, gtr, w_nsa_up.astype(BF16), w_ret_up.astype(BF16),
                               w_out.astype(BF16), norm_ffn.reshape(1, D_MODEL), wrt, brt)
    return _hierarchical_moe(h2, comb, cnt, x1, moe_wg, moe_wu, moe_wd, final_norm_w)


def kernel(x, rel_bias, norm_mix, w_in, cmp_pos_k, cmp_k_w1, cmp_k_w2, cmp_pos_v, cmp_v_w1, cmp_v_w2, ret_norm, w_nsa_up, w_ret_up, w_out, norm_ffn, w_router_group, b_router_group, w_router_expert, b_router_expert, moe_w_gate, moe_w_up, moe_w_down, norm_final):
    B, S, D = x.shape
    assert B == 1 and norm_mix.shape[0] == 1, "single sequence, depth 1"
    out = _layer(x[0], rel_bias, norm_mix[0], w_in[0], cmp_pos_k[0], cmp_k_w1[0], cmp_k_w2[0],
                 cmp_pos_v[0], cmp_v_w1[0], cmp_v_w2[0], ret_norm[0], w_nsa_up[0], w_ret_up[0], w_out[0],
                 norm_ffn[0], w_router_group[0], b_router_group[0], w_router_expert[0], b_router_expert[0],
                 moe_w_gate[0], moe_w_up[0], moe_w_down[0], norm_final.reshape(1, D))
    return out[None]
```

```python
import functools
import math

import numpy as np
import jax
import jax.numpy as jnp
from jax import lax
from jax.experimental import pallas as pl
from jax.experimental.pallas import tpu as pltpu

F32 = jnp.float32
BF16 = jnp.bfloat16

D_MODEL = 1024
NSA_HEADS = 8
NSA_KV_GROUPS = 2
NSA_HPG = NSA_HEADS // NSA_KV_GROUPS
NSA_HEAD_DIM = 64
CMP_LEN = 32
CMP_STRIDE = 16
CMP_HIDDEN = 256
SEL_BLOCK = 64
SEL_TOP_N = 16
N_LOCAL_SEL = 2
WINDOW = 512
Q_BLOCK = 128
RET_HEADS = 4
RET_QK_DIM = 128
RET_V_DIM = 256
RET_CHUNK = 128
ROPE_BASE = 10000.0
REL_BUCKETS = 32
REL_MAX_DIST = 128
N_GROUPS = 4
EXPERTS_PER_GROUP = 8
N_EXPERTS = N_GROUPS * EXPERTS_PER_GROUP
EXPERT_FF = 256
EPS = 1e-6

NSA_Q_W = NSA_HEADS * NSA_HEAD_DIM
NSA_KV_W = NSA_KV_GROUPS * NSA_HEAD_DIM
RET_QK_W = RET_HEADS * RET_QK_DIM
RET_V_W = RET_HEADS * RET_V_DIM

LANE = 128
NEG = -1e30
VMEM_LIMIT = 56 * 1024 * 1024

CMP_TILE = 128
QB_PER_CMP_TILE = CMP_TILE * CMP_STRIDE // Q_BLOCK
CMP_TOK_PER_QB = Q_BLOCK // CMP_STRIDE
SCR_PAD = 8
SEL_TK = 512
SEL_WIN = 128
SEL_WIN_TILES = SEL_WIN * SEL_BLOCK // SEL_TK
CMP_TRIP_TILES = (4, 2, 1)
FAR_TRIP_TILES = (8, 4, 2)
SEL_ROWS_STEP = 64
PV_ROWS = 80
WIN_KEYS = WINDOW + Q_BLOCK
QW = NSA_HPG * Q_BLOCK


def _dot(a, b):
    return jnp.dot(a, b, preferred_element_type=F32)


def _dot_nt(a, b):
    return lax.dot_general(a, b, (((1,), (1,)), ((), ())), preferred_element_type=F32)


def _const_spec(shape):
    nd = len(shape)
    return pl.BlockSpec(shape, lambda *_: (0,) * nd, pipeline_mode=pl.Buffered(1))


def _params(*sem):
    return pltpu.CompilerParams(dimension_semantics=sem, vmem_limit_bytes=VMEM_LIMIT)


_C_QA = 0
_C_KV = _C_QA + NSA_Q_W
_C_QR = _C_KV + 6 * NSA_KV_W
_C_KR = _C_QR + RET_QK_W
_C_VR = _C_KR + RET_QK_W
_C_GR = _C_VR + RET_V_W
_C_GTA = _C_GR + RET_V_W
_C_GTR = _C_GTA + D_MODEL
_C_GA = _C_GTR + D_MODEL
_C_END = _C_GA + LANE
IN_TM = 512


def _inproj_kernel(x_ref, nw_ref, w_ref, inv_ref,
                   qat_ref, kcmp_ref, vcmp_ref, ksel_ref, vselt_ref, kwin_ref, vwint_ref, gat_ref,
                   qr_ref, kr_ref, vr_ref, gr_ref, gta_ref, gtr_ref):
    tm = x_ref.shape[0]
    x = x_ref[...]
    h = x * lax.rsqrt(jnp.mean(x * x, axis=-1, keepdims=True) + EPS) * nw_ref[...]
    hb = h.astype(BF16)

    def proj(a, width):
        return _dot(hb, w_ref[:, a:a + width])

    def values_t(v):
        vt = v.T.astype(BF16)
        aux = jnp.where(lax.broadcasted_iota(jnp.int32, (PV_ROWS - NSA_HEAD_DIM, tm), 0) == 0, 1.0, 0.0).astype(BF16)
        parts = []
        for g in range(NSA_KV_GROUPS):
            parts += [vt[g * NSA_HEAD_DIM:(g + 1) * NSA_HEAD_DIM], aux]
        return jnp.concatenate(parts, axis=0)

    qat_ref[...] = (proj(_C_QA, NSA_Q_W) * (NSA_HEAD_DIM ** -0.5)).T.astype(BF16)
    kv = proj(_C_KV, 6 * NSA_KV_W)
    kcmp_ref[...] = kv[:, 0 * LANE:1 * LANE]
    vcmp_ref[...] = kv[:, 1 * LANE:2 * LANE]
    ksel_ref[...] = kv[:, 2 * LANE:3 * LANE].astype(BF16)
    vselt_ref[...] = values_t(kv[:, 3 * LANE:4 * LANE])
    kwin_ref[...] = kv[:, 4 * LANE:5 * LANE].astype(BF16)
    vwint_ref[...] = values_t(kv[:, 5 * LANE:6 * LANE])
    gat_ref[...] = jax.nn.sigmoid(proj(_C_GA, LANE)).T

    hm = tm // 2
    half = RET_QK_DIM // 2
    low = lax.broadcasted_iota(jnp.int32, (hm, RET_QK_DIM), 1) < half
    row = pl.program_id(0) * tm + lax.broadcasted_iota(jnp.int32, (hm, RET_QK_DIM), 0)
    ang = jnp.where(low, row, row + hm).astype(F32) * inv_ref[...]
    c, s = jnp.cos(ang), jnp.sin(ang)
    c_sw, s_sw = pltpu.roll(c, half, 1), pltpu.roll(s, half, 1)
    cos2 = jnp.concatenate([jnp.where(low, c, c_sw), jnp.where(low, c_sw, c)], axis=0)
    sin2 = jnp.concatenate([jnp.where(low, -s, s_sw), jnp.where(low, -s_sw, s)], axis=0)
    qr = proj(_C_QR, RET_QK_W)
    kr = proj(_C_KR, RET_QK_W)
    for hd in range(RET_HEADS):
        sl = slice(hd * RET_QK_DIM, (hd + 1) * RET_QK_DIM)
        qh = qr[:, sl]
        kh = kr[:, sl]
        qr_ref[:, sl] = (qh * cos2 + pltpu.roll(qh, RET_QK_DIM // 2, 1) * sin2).astype(BF16)
        kr_ref[:, sl] = (kh * cos2 + pltpu.roll(kh, RET_QK_DIM // 2, 1) * sin2) * (RET_QK_DIM ** -0.5)
    vr_ref[...] = proj(_C_VR, RET_V_W).astype(BF16)
    gr_ref[...] = proj(_C_GR, RET_V_W)
    gta_ref[...] = jax.nn.sigmoid(proj(_C_GTA, D_MODEL))
    gtr_ref[...] = jax.nn.sigmoid(proj(_C_GTR, D_MODEL))


def _input_projection(x2d, norm_w, w_cat, inv2):
    S = x2d.shape[0]
    tm = IN_TM
    vt_rows = NSA_KV_GROUPS * PV_ROWS
    outs = [
        (NSA_Q_W, BF16, True),
        (LANE, F32, False), (LANE, F32, False),
        (LANE, BF16, False), (vt_rows, BF16, True),
        (LANE, BF16, False), (vt_rows, BF16, True),
        (LANE, F32, True),
        (RET_QK_W, BF16, False), (RET_QK_W, F32, False),
        (RET_V_W, BF16, False), (RET_V_W, F32, False),
        (D_MODEL, F32, False), (D_MODEL, F32, False),
    ]
    row = lambda w: pl.BlockSpec((tm, w), lambda i: (i, 0))
    col = lambda w: pl.BlockSpec((w, tm), lambda i: (0, i))
    return pl.pallas_call(
        _inproj_kernel,
        grid=(S // tm,),
        in_specs=[row(D_MODEL), _const_spec((1, D_MODEL)), _const_spec((D_MODEL, _C_END)),
                  _const_spec((1, RET_QK_DIM))],
        out_specs=[col(w) if t else row(w) for w, _, t in outs],
        out_shape=[jax.ShapeDtypeStruct((w, S) if t else (S, w), dt) for w, dt, t in outs],
        compiler_params=_params("parallel"),
        name="in_proj",
    )(x2d, norm_w, w_cat, inv2)


def _compress_kernel(x_ref, pos_ref, w1_ref, w2_ref, out_ref, hid_ref):
    n_tok = out_ref.shape[0]
    hid_ref[...] = jnp.zeros_like(hid_ref)

    def body(l, _):
        rows = x_ref[pl.ds(l, n_tok, stride=CMP_STRIDE), :] + pos_ref[pl.ds(l, 1), :]
        hid_ref[...] += _dot(rows.astype(BF16), w1_ref[l])
        return 0

    lax.fori_loop(0, CMP_LEN, body, 0)
    out_ref[...] = _dot(jax.nn.gelu(hid_ref[...]).astype(BF16), w2_ref[...])


def _compress(x, pos, w1, w2):
    S, width = x.shape
    n_tok = S // CMP_STRIDE
    G = NSA_KV_GROUPS
    def block_diag(w):
        w = w.astype(BF16)
        z = jnp.zeros_like(w)
        return jnp.concatenate([jnp.concatenate([w if k == g else z for k in range(G)], axis=-1)
                                for g in range(G)], axis=-2)

    w1_bd = block_diag(w1.reshape(CMP_LEN, NSA_HEAD_DIM, CMP_HIDDEN))
    w2_bd = block_diag(w2)
    x_pad = jnp.pad(x, ((0, CMP_LEN - CMP_STRIDE), (0, 0)))
    full = lambda a: _const_spec(a.shape)
    pos2 = jnp.concatenate([pos] * G, axis=1)
    return pl.pallas_call(
        _compress_kernel,
        grid=(1,),
        in_specs=[full(x_pad), full(pos2), full(w1_bd), full(w2_bd)],
        out_specs=pl.BlockSpec((n_tok, width), lambda i: (0, 0)),
        out_shape=jax.ShapeDtypeStruct((n_tok, width), F32),
        scratch_shapes=[pltpu.VMEM((n_tok, G * CMP_HIDDEN), F32)],
        compiler_params=_params("arbitrary"),
        name="nsa_compress",
    )(x_pad, pos2, w1_bd, w2_bd)


def _t5_bucket_np(dist):
    dist = np.maximum(dist, 0)
    max_exact = REL_BUCKETS // 2
    d_f = np.maximum(dist, 1).astype(np.float32)
    large = max_exact + (np.log(d_f / np.float32(max_exact)) / np.float32(math.log(REL_MAX_DIST / max_exact))
                         * np.float32(REL_BUCKETS - max_exact)).astype(np.int32)
    large = np.minimum(large, REL_BUCKETS - 1)
    return np.where(dist < max_exact, dist, large)


def _bias_table_t(rel_bias, dist, shift):
    bucket = jnp.asarray(np.where(dist >= 0, _t5_bucket_np(dist), -1), jnp.int32)
    rb = rel_bias - rel_bias[REL_BUCKETS - 1][None, :] if shift else rel_bias
    onehot = (bucket[None] == jnp.arange(REL_BUCKETS, dtype=jnp.int32)[:, None, None]).astype(F32)
    tab = jnp.einsum('bh,bkq->hkq', rb, onehot, precision=lax.Precision.HIGHEST)
    tab = jnp.where(jnp.asarray(dist == -1)[None], NEG, tab)
    K, Q = dist.shape
    return tab.reshape(NSA_KV_GROUPS, NSA_HPG, K, Q).transpose(0, 2, 1, 3).reshape(NSA_KV_GROUPS, K, NSA_HPG * Q)


def _nsa_kernel(n_keys, q_ref, ga_ref, kc_ref, vct_ref, tct_ref, ksel_ref, vselt_ref, et_ref, tnt_ref,
                kwin_ref, vwint_ref, twt_ref, out_ref, sc_ref, ps_ref, sa_ref, sb_ref, ma_ref, mb_ref):
    i = pl.program_id(0)
    dh = NSA_HEAD_DIM
    nb = (ps_ref.shape[1] - 2 * SCR_PAD) // (SEL_BLOCK // CMP_STRIDE)

    @pl.when(i == 0)
    def _():
        sc_ref[...] = jnp.zeros_like(sc_ref)
        ps_ref[...] = jnp.zeros_like(ps_ref)

    def col_reduce(op, red, x):
        rows = x.shape[0]
        while rows <= 256 and rows % 16 == 0:
            rows //= 2
            x = op(x[:rows], x[rows:])
        return red(x, axis=0, keepdims=True)

    def colmax(x):
        return col_reduce(jnp.maximum, jnp.max, x)

    def tile4(x):
        return jnp.concatenate([x] * NSA_HPG, axis=1)

    gates = ga_ref[...]
    tl = i // QB_PER_CMP_TILE
    sub = i % QB_PER_CMP_TILE
    tp = tl - 1
    groups = range(NSA_KV_GROUPS)
    q_t_g = [jnp.concatenate([q_ref[NSA_HPG * g + j] for j in range(NSA_HPG)], axis=1) for g in groups]
    zeros = jnp.zeros((dh, QW), BF16)
    wq_g = [jnp.concatenate([q_t_g[g] if k == g else zeros for k in groups], axis=0) for g in groups]

    def cmp_scores(g, t):
        return _dot(kc_ref[pl.ds(pl.multiple_of(t * CMP_TILE, CMP_TILE), CMP_TILE), :], wq_g[g])

    def sc_rows(t):
        return pl.ds(pl.multiple_of(SCR_PAD + t * CMP_TILE, 8), CMP_TILE)

    def cmp_far(t, m):
        out = []
        for g in groups:
            s = cmp_scores(g, t)
            sc_ref[g, sc_rows(t), :] = s
            out.append(jnp.maximum(m[g], colmax(s)))
        return tuple(out)

    def tile_loop(count, step, carry):
        first, left = 0, count
        for per_trip in CMP_TRIP_TILES:
            trips = left // per_trip

            def body(u, carry, first=first, per_trip=per_trip):
                for k in range(per_trip):
                    carry = step(first + per_trip * u + k, carry)
                return carry

            carry = lax.fori_loop(0, trips, body, carry)
            first, left = first + per_trip * trips, left - per_trip * trips
        return carry

    m_c = tile_loop(jnp.maximum(tp, 0), cmp_far, tuple(jnp.full((1, QW), NEG, F32) for _ in groups))
    tpc = jnp.maximum(tp, 0)
    off = CMP_TILE - CMP_TOK_PER_QB - CMP_TOK_PER_QB * sub
    m_c = list(m_c)
    for g in groups:
        s = (cmp_scores(g, tpc) + tct_ref[g, pl.ds(pl.multiple_of(off, 8), CMP_TILE), :]
             + jnp.where(tp < 0, NEG, 0.0))
        sc_ref[g, sc_rows(tpc), :] = s
        m_c[g] = jnp.maximum(m_c[g], colmax(s))
    for g in groups:
        s = cmp_scores(g, tl) + tct_ref[g, pl.ds(pl.multiple_of(off + CMP_TILE, 8), CMP_TILE), :]
        sc_ref[g, sc_rows(tl), :] = s
        m_c[g] = jnp.maximum(m_c[g], colmax(s))

    def cmp_exp(t, carry):
        out = []
        for g in groups:
            l, acc = carry[g]
            e = jnp.exp(sc_ref[g, sc_rows(t), :] - m_c[g])
            sc_ref[g, sc_rows(t), :] = e
            vt = vct_ref[g, :, pl.ds(pl.multiple_of(t * CMP_TILE, CMP_TILE), CMP_TILE)]
            out.append((l + col_reduce(jnp.add, jnp.sum, e), acc + _dot(vt, e.astype(BF16))))
        return tuple(out)

    cmp_out = tile_loop(tl + 1, cmp_exp,
                        tuple((jnp.zeros((1, QW), F32), jnp.zeros((dh, QW), F32)) for _ in groups))
    inv_c = [jnp.where(m_c[g] > 0.5 * NEG, 1.0 / jnp.maximum(cmp_out[g][0], 1e-30), 0.0) for g in groups]
    o_c_g = [cmp_out[g][1] * inv_c[g] for g in groups]

    def cmp_psum(t, _):
        for g in groups:
            p = sc_ref[g, sc_rows(t), :] * inv_c[g]
            ps = p[:, 0:Q_BLOCK]
            for j in range(1, NSA_HPG):
                ps = ps + p[:, j * Q_BLOCK:(j + 1) * Q_BLOCK]
            ps_ref[g, sc_rows(t), :] = ps.astype(BF16).astype(F32)
        return 0

    tile_loop(tl + 1, cmp_psum, 0)
    per_blk = SEL_BLOCK // CMP_STRIDE
    imp_g = []
    for g in groups:
        imp = ps_ref[g, pl.ds(SCR_PAD - 1, nb, stride=per_blk), :]
        for k in range(1, CMP_LEN // CMP_STRIDE + per_blk - 1):
            imp = imp + ps_ref[g, pl.ds(SCR_PAD - 1 + k, nb, stride=per_blk), :]
        imp_g.append(imp)

    kw0 = pl.multiple_of(jnp.maximum(i * Q_BLOCK - WINDOW, 0), Q_BLOCK)
    tw0 = pl.multiple_of(jnp.maximum(WINDOW - i * Q_BLOCK, 0), Q_BLOCK)
    o_w_g = []
    for g in groups:
        s = _dot(kwin_ref[pl.ds(kw0, WIN_KEYS), :], wq_g[g]) + twt_ref[g, pl.ds(tw0, WIN_KEYS), :]
        e = jnp.exp(s - colmax(s)).astype(BF16)
        acc_w = _dot(vwint_ref[g, :, pl.ds(kw0, WIN_KEYS)], e)
        o_w_g.append(acc_w[0:dh] * (1.0 / jnp.maximum(acc_w[dh:dh + 1], 1e-30)))

    n_io = lax.broadcasted_iota(jnp.int32, (nb, Q_BLOCK), 0)
    r_io = lax.broadcasted_iota(jnp.int32, (nb, Q_BLOCK), 1)
    back = (i * Q_BLOCK + r_io) // SEL_BLOCK - n_io
    valid = back >= 0
    forced = (n_io == 0) | (valid & (back < N_LOCAL_SEL))
    work_g = [jnp.where(forced, -jnp.inf, jnp.where(valid, imp_g[g], -1e9)) for g in groups]
    n_f = n_io.astype(F32)
    def select_rounds(rows):
        def run(*work):
            work = [w[:rows] for w in work]
            for _ in range(SEL_TOP_N - 1 - N_LOCAL_SEL):
                for g in groups:
                    mx = colmax(work[g])
                    first = col_reduce(jnp.minimum, jnp.min, jnp.where(work[g] == mx, n_f[:rows], float(nb)))
                    work[g] = jnp.where(n_f[:rows] == first, -jnp.inf, work[g])
            rest = [jnp.zeros((nb - rows, Q_BLOCK), F32)] if rows < nb else []
            return tuple(jnp.concatenate([jnp.where(w == -jnp.inf, 1.0, 0.0)] + rest, axis=0) for w in work)
        return run

    levels = [SEL_ROWS_STEP * (k + 1) for k in range(nb // SEL_ROWS_STEP)]
    level = jnp.minimum((i * Q_BLOCK + Q_BLOCK - 1) // (SEL_BLOCK * SEL_ROWS_STEP), len(levels) - 1)
    sel_g = list(lax.switch(level, [select_rounds(rows) for rows in levels], *work_g))

    far_blocks = (i - 1) * (Q_BLOCK // SEL_BLOCK)
    selm_g = [(jnp.where(n_io < far_blocks, sel_g[g], 0.0) - 1.0).astype(BF16) for g in groups]

    far_end = (i - 1) * Q_BLOCK
    n_far = (jnp.maximum(far_end, 0) + SEL_TK - 1) // SEL_TK

    def flash_step(carry, s, vt):
        m_i, acc = carry
        m_new = jnp.maximum(m_i, colmax(s))
        alpha = jnp.exp(m_i - m_new)
        p = jnp.exp(s - m_new).astype(BF16)
        return m_new, alpha * acc + _dot(vt, p)

    carry = tuple((jnp.full((1, QW), NEG, F32), jnp.zeros((PV_ROWS, QW), F32)) for _ in groups)
    for w in range(nb // SEL_WIN):
        w_far = [jnp.concatenate([tile4(selm_g[g][w * SEL_WIN:(w + 1) * SEL_WIN]), wq_g[g]], axis=0)
                 for g in groups]

        lo = w * SEL_WIN_TILES
        last = min((w + 1) * SEL_WIN_TILES, n_keys // SEL_TK) - 1
        trips = jnp.maximum(jnp.minimum(n_far, last + 1) - lo + 1, 0) // 2

        def far_scores(kt, s_ref, mx_ref, g, w=w, w_far=w_far):
            k0 = pl.multiple_of(kt * SEL_TK, SEL_TK)
            e0 = pl.multiple_of((kt - w * SEL_WIN_TILES) * SEL_TK, SEL_TK)
            lhs = jnp.concatenate([et_ref[pl.ds(e0, SEL_TK), :],
                                   ksel_ref[pl.ds(k0, SEL_TK), :]], axis=1)
            s = _dot(lhs, w_far[g])
            s_ref[g] = s
            mx_ref[g] = colmax(s)

        def far_update(kt, s_ref, mx_ref, g, carry_g):
            k0 = pl.multiple_of(kt * SEL_TK, SEL_TK)
            m_i, acc = carry_g
            m_new = jnp.maximum(m_i, mx_ref[g])
            p = jnp.exp(s_ref[g] - m_new).astype(BF16)
            return m_new, jnp.exp(m_i - m_new) * acc + _dot(vselt_ref[g, :, pl.ds(k0, SEL_TK)], p)

        def far_trip(tiles, first, last=last, far_scores=far_scores):
            def body(u, carry):
                kt = first + tiles * u
                carry = list(carry)
                bufs = [(sa_ref, ma_ref), (sb_ref, mb_ref)]
                for t in range(tiles):
                    for g in groups:
                        far_scores(jnp.minimum(kt + t + 1, last), *bufs[(t + 1) % 2], g)
                        carry[g] = far_update(kt + t, *bufs[t % 2], g, carry[g])
                return tuple(carry)
            return body

        @pl.when(trips > 0)
        def _(lo=lo, far_scores=far_scores):
            for g in groups:
                far_scores(jnp.int32(lo), sa_ref, ma_ref, g)

        first, left = lo, trips
        for tiles in FAR_TRIP_TILES:
            count = left // (tiles // 2)
            carry = lax.fori_loop(0, count, far_trip(tiles, first), carry)
            first, left = first + tiles * count, left - count * (tiles // 2)

    heads_out = []
    for g in groups:
        wq, o_c, sel = wq_g[g], o_c_g[g], sel_g[g]
        k0n = pl.multiple_of(jnp.maximum(far_end, 0), Q_BLOCK)
        t0n = pl.multiple_of(jnp.maximum(-far_end, 0), Q_BLOCK)
        near_blocks = 2 * Q_BLOCK // SEL_BLOCK
        pb = lax.broadcasted_iota(jnp.int32, (16, nb), 0)
        pn = lax.broadcasted_iota(jnp.int32, (16, nb), 1)
        pick_rows = jnp.where((pn == jnp.maximum(far_blocks, 0) + pb) & (pb < near_blocks), 1.0, 0.0).astype(BF16)
        sel_rows = _dot(pick_rows, sel.astype(BF16))
        add_near = jnp.concatenate(
            [jnp.broadcast_to((sel_rows[b:b + 1, :] - 1.0) * (-NEG), (SEL_BLOCK, Q_BLOCK)) for b in range(near_blocks)],
            axis=0)
        s = (_dot(ksel_ref[pl.ds(k0n, 2 * Q_BLOCK), :], wq) + tnt_ref[g, pl.ds(t0n, 2 * Q_BLOCK), :]
             + tile4(add_near))
        m_s, acc_s = flash_step(carry[g], s, vselt_ref[g, :, pl.ds(k0n, 2 * Q_BLOCK)])
        o_s = acc_s[0:dh] * (1.0 / jnp.maximum(acc_s[dh:dh + 1], 1e-30))

        o_w = o_w_g[g]

        for j in range(NSA_HPG):
            hd = NSA_HPG * g + j
            cols = slice(j * Q_BLOCK, (j + 1) * Q_BLOCK)
            heads_out.append(gates[hd:hd + 1] * o_c[:, cols]
                             + gates[NSA_HEADS + hd:NSA_HEADS + hd + 1] * o_s[:, cols]
                             + gates[2 * NSA_HEADS + hd:2 * NSA_HEADS + hd + 1] * o_w[:, cols])

    out_ref[...] = jnp.concatenate(heads_out, axis=0).T.astype(out_ref.dtype)


def _nsa_attention(q_t, gates_t, kc, vct, tct, ksel, vselt, et, tnt, kwin, vwint, twt):
    H, dh, S = q_t.shape
    n_qb = S // Q_BLOCK
    n_tok = S // CMP_STRIDE
    nb = -(-(S // SEL_BLOCK) // SEL_WIN) * SEL_WIN
    ps_rows = nb * (SEL_BLOCK // CMP_STRIDE) + 2 * SCR_PAD
    full = lambda a: _const_spec(a.shape)
    return pl.pallas_call(
        functools.partial(_nsa_kernel, S),
        grid=(n_qb,),
        in_specs=[pl.BlockSpec((H, dh, Q_BLOCK), lambda i: (0, 0, i)),
                  pl.BlockSpec((gates_t.shape[0], Q_BLOCK), lambda i: (0, i)),
                  full(kc), full(vct), full(tct), full(ksel), full(vselt), full(et), full(tnt),
                  full(kwin), full(vwint), full(twt)],
        out_specs=pl.BlockSpec((Q_BLOCK, H * dh), lambda i: (i, 0)),
        out_shape=jax.ShapeDtypeStruct((S, H * dh), BF16),
        scratch_shapes=[pltpu.VMEM((NSA_KV_GROUPS, SCR_PAD + n_tok, QW), F32),
                        pltpu.VMEM((NSA_KV_GROUPS, ps_rows, Q_BLOCK), F32),
                        pltpu.VMEM((NSA_KV_GROUPS, SEL_TK, QW), F32),
                        pltpu.VMEM((NSA_KV_GROUPS, SEL_TK, QW), F32),
                        pltpu.VMEM((NSA_KV_GROUPS, 1, QW), F32),
                        pltpu.VMEM((NSA_KV_GROUPS, 1, QW), F32)],
        compiler_params=_params("arbitrary"),
        name="nsa_attention",
    )(q_t, gates_t, kc, vct, tct, ksel, vselt, et, tnt, kwin, vwint, twt)


def _retention_kernel(q_ref, k_ref, v_ref, g_ref, nw_ref, dmask_ref, qd_ref, kd_ref, cd_ref, out_ref, state_ref):
    @pl.when(pl.program_id(0) == 0)
    def _():
        state_ref[...] = jnp.zeros_like(state_ref)

    states = [state_ref[hd] for hd in range(RET_HEADS)]
    for c in range(q_ref.shape[0] // RET_CHUNK):
        rows = slice(c * RET_CHUNK, (c + 1) * RET_CHUNK)
        for hd in range(RET_HEADS):
            ks = slice(hd * RET_QK_DIM, (hd + 1) * RET_QK_DIM)
            vs = slice(hd * RET_V_DIM, (hd + 1) * RET_V_DIM)
            q = q_ref[rows, ks]
            k = k_ref[rows, ks]
            v = v_ref[rows, vs]
            state = states[hd]
            inner = _dot_nt(q, k.astype(BF16)) * dmask_ref[hd]
            y = _dot(inner.astype(BF16), v) + _dot(q, state.astype(BF16)) * qd_ref[hd]
            kd = k * kd_ref[hd]
            states[hd] = state * cd_ref[hd] + _dot(kd.T.astype(BF16), v)
            mu = jnp.mean(y, axis=-1, keepdims=True)
            yc = y - mu
            var = jnp.mean(yc * yc, axis=-1, keepdims=True)
            yn = yc * lax.rsqrt(var + EPS) * nw_ref[:, vs]
            out_ref[rows, vs] = (jax.nn.silu(g_ref[rows, vs]) * yn).astype(out_ref.dtype)
    for hd in range(RET_HEADS):
        state_ref[hd] = states[hd]


RET_STEP_CHUNKS = 8


def _retention(q, k, v, g, norm_w, dmask, qd, kd, cd):
    S = q.shape[0]
    C = RET_CHUNK * RET_STEP_CHUNKS
    row = lambda w: pl.BlockSpec((C, w), lambda i: (i, 0))
    full = lambda a: _const_spec(a.shape)
    return pl.pallas_call(
        _retention_kernel,
        grid=(S // C,),
        in_specs=[row(RET_QK_W), row(RET_QK_W), row(RET_V_W), row(RET_V_W),
                  full(norm_w), full(dmask), full(qd), full(kd), full(cd)],
        out_specs=row(RET_V_W),
        out_shape=jax.ShapeDtypeStruct((S, RET_V_W), BF16),
        scratch_shapes=[pltpu.VMEM((RET_HEADS, RET_QK_DIM, RET_V_DIM), F32)],
        compiler_params=_params("arbitrary"),
        name="retention",
    )(q, k, v, g, norm_w, dmask, qd, kd, cd)


MERGE_TM = 512
MERGE_SUB_ROWS = 256
GRP_LANE = 40
RANK_LANE = 41


def _merge_kernel(x_ref, ya_ref, yr_ref, gta_ref, gtr_ref, wa_ref, wr_ref, wo_ref, nw_ref, wrt_ref, brt_ref,
                  tri_ref, x1_ref, h_ref, comb_ref, cnt_ref, carry_ref):
    @pl.when(pl.program_id(0) == 0)
    def _():
        carry_ref[...] = jnp.zeros_like(carry_ref)

    sub = tri_ref.shape[0]
    carry = carry_ref[...]
    for r in range(x_ref.shape[0] // sub):
        carry = _merge_rows(slice(r * sub, (r + 1) * sub), carry, x_ref, ya_ref, yr_ref, gta_ref, gtr_ref,
                            wa_ref, wr_ref, wo_ref, nw_ref, wrt_ref, brt_ref, tri_ref, x1_ref, h_ref, comb_ref)
    carry_ref[...] = carry
    cnt_ref[...] = jnp.broadcast_to(carry, cnt_ref.shape)


def _merge_rows(rows, carry, x_ref, ya_ref, yr_ref, gta_ref, gtr_ref, wa_ref, wr_ref, wo_ref, nw_ref, wrt_ref,
                brt_ref, tri_ref, x1_ref, h_ref, comb_ref):
    merged = gta_ref[rows] * _dot(ya_ref[rows], wa_ref[...]) + gtr_ref[rows] * _dot(yr_ref[rows], wr_ref[...])
    x1 = x_ref[rows] + _dot(merged.astype(BF16), wo_ref[...])
    x1_ref[rows] = x1
    h = (x1 * lax.rsqrt(jnp.mean(x1 * x1, axis=-1, keepdims=True) + EPS) * nw_ref[...]).astype(BF16)
    h_ref[rows] = _pack_bf16_pairs(h.astype(F32))
    logits = _dot(h, wrt_ref[...]) + brt_ref[...]
    lane = lax.broadcasted_iota(jnp.int32, logits.shape, 1)
    lane_f = lane.astype(F32)
    big = float(LANE)
    is_grp = (lane >= N_EXPERTS) & (lane < N_EXPERTS + N_GROUPS)
    lg = jnp.where(is_grp, logits, -jnp.inf)
    eg = jnp.where(is_grp, jnp.exp(lg - jnp.max(lg, axis=-1, keepdims=True)), 0.0)
    pg = eg / jnp.sum(eg, axis=-1, keepdims=True)
    p_grp = jnp.max(pg, axis=-1, keepdims=True)
    grp = jnp.min(jnp.where(is_grp & (pg == p_grp), lane_f, big), axis=-1, keepdims=True) - N_EXPERTS
    in_grp = (lane_f >= grp * EXPERTS_PER_GROUP) & (lane_f < (grp + 1.0) * EXPERTS_PER_GROUP)
    le = jnp.where(in_grp, logits, -jnp.inf)
    m1 = jnp.max(le, axis=-1, keepdims=True)
    i1 = jnp.min(jnp.where(le == m1, lane_f, big), axis=-1, keepdims=True)
    le2 = jnp.where(lane_f == i1, -jnp.inf, le)
    m2 = jnp.max(le2, axis=-1, keepdims=True)
    i2 = jnp.min(jnp.where(le2 == m2, lane_f, big), axis=-1, keepdims=True)
    e2 = jnp.exp(m2 - m1)
    den = 1.0 + e2
    onehot = jnp.where(lane_f == grp, 1.0, 0.0)
    before = _dot(tri_ref[...], onehot.astype(BF16)) + carry
    rank = jnp.sum(onehot * before, axis=-1, keepdims=True)
    comb_ref[rows] = (jnp.where(lane_f == i1, (1.0 / den) * p_grp, 0.0)
                      + jnp.where(lane_f == i2, (e2 / den) * p_grp, 0.0)
                      + jnp.where(lane == GRP_LANE, grp, 0.0)
                      + jnp.where(lane == RANK_LANE, rank, 0.0))
    return carry + jnp.sum(onehot, axis=0, keepdims=True)


def _pack_bf16_pairs(x):
    n = x.shape[1] // 2
    return pltpu.pack_elementwise([x[:, :n], x[:, n:]], packed_dtype=BF16)


def _unpack_bf16_pairs(w):
    return tuple(pltpu.unpack_elementwise(w, index=k, packed_dtype=BF16, unpacked_dtype=F32) for k in range(2))


def _merge(x2d, ya, yr, gta, gtr, wa, wr, wo, nw, wrt, brt):
    S = x2d.shape[0]
    tm = MERGE_TM
    row = lambda w: pl.BlockSpec((tm, w), lambda i: (i, 0))
    full = lambda a: _const_spec(a.shape)
    sub = MERGE_SUB_ROWS
    tri = jnp.asarray(np.tril(np.ones((sub, sub), np.float32), -1), BF16)
    return pl.pallas_call(
        _merge_kernel,
        grid=(S // tm,),
        in_specs=[row(D_MODEL), row(NSA_Q_W), row(RET_V_W), row(D_MODEL), row(D_MODEL),
                  full(wa), full(wr), full(wo), full(nw), full(wrt), full(brt), full(tri)],
        out_specs=[row(D_MODEL), row(D_MODEL // 2), row(LANE), pl.BlockSpec((8, LANE), lambda i: (0, 0))],
        out_shape=[jax.ShapeDtypeStruct((S, D_MODEL), F32), jax.ShapeDtypeStruct((S, D_MODEL // 2), jnp.uint32),
                   jax.ShapeDtypeStruct((S, LANE), F32), jax.ShapeDtypeStruct((8, LANE), F32)],
        scratch_shapes=[pltpu.VMEM((1, LANE), F32)],
        compiler_params=_params("arbitrary"),
        name="merge_router",
    )(x2d, ya, yr, gta, gtr, wa, wr, wo, nw, wrt, brt, tri)


MOE_TM = 1024
MOE_SUB_TILES = 2
PERMUTE_TM = 512
HALF = D_MODEL // 2


def _permute_kernel(pos_ref, h_ref, comb_ref, hs_ref, cs_ref):
    i = pl.program_id(0)

    @pl.when(i == 0)
    def _():
        hs_ref[...] = jnp.zeros_like(hs_ref)
        cs_ref[...] = jnp.zeros_like(cs_ref)

    tm = h_ref.shape[0]

    def body(r, _):
        d = pos_ref[i * tm + r]
        hs_ref[pl.ds(d, 1), :] = h_ref[pl.ds(r, 1), :]
        cs_ref[pl.ds(d, 1), :] = comb_ref[pl.ds(r, 1), :]
        return 0

    lax.fori_loop(0, tm, body, 0, unroll=8)


def _permute(pos, hu, comb, n_sorted):
    S = hu.shape[0]
    tm = PERMUTE_TM
    resident = lambda w: pl.BlockSpec((n_sorted, w), lambda i, p: (0, 0), pipeline_mode=pl.Buffered(1))
    return pl.pallas_call(
        _permute_kernel,
        grid_spec=pltpu.PrefetchScalarGridSpec(
            num_scalar_prefetch=1, grid=(S // tm,),
            in_specs=[pl.BlockSpec((tm, HALF), lambda i, p: (i, 0)), pl.BlockSpec((tm, LANE), lambda i, p: (i, 0))],
            out_specs=[resident(HALF), resident(LANE)]),
        out_shape=[jax.ShapeDtypeStruct((n_sorted, HALF), jnp.uint32), jax.ShapeDtypeStruct((n_sorted, LANE), F32)],
        compiler_params=_params("arbitrary"),
        name="moe_permute",
    )(pos, hu, comb)


def _moe_kernel(tg_ref, tv_ref, hs_ref, cs_ref, wg_ref, wu_ref, wd_ref, ys_ref, lo_ref, hi_ref, acc_ref):
    i = pl.program_id(0)
    e = pl.program_id(1)

    @pl.when(e == 0)
    def _():
        lo, hi = _unpack_bf16_pairs(hs_ref[...])
        lo_ref[...] = lo.astype(BF16)
        hi_ref[...] = hi.astype(BF16)
        acc_ref[...] = jnp.zeros_like(acc_ref)

    @pl.when(tv_ref[i] > 0)
    def _():
        wg = wg_ref[0].astype(BF16)
        wu = wu_ref[0].astype(BF16)
        wd = wd_ref[0].astype(BF16)
        sub = lo_ref.shape[0] // MOE_SUB_TILES
        for r in range(MOE_SUB_TILES):
            rows = slice(r * sub, (r + 1) * sub)
            lo, hi = lo_ref[rows], hi_ref[rows]
            gate = _dot(lo, wg[:HALF]) + _dot(hi, wg[HALF:])
            up = _dot(lo, wu[:HALF]) + _dot(hi, wu[HALF:])
            comb = cs_ref[rows]
            lane = lax.broadcasted_iota(jnp.int32, comb.shape, 1)
            c = jnp.sum(jnp.where(lane == tg_ref[i] * EXPERTS_PER_GROUP + e, comb, 0.0), axis=-1, keepdims=True)
            acc_ref[rows] += _dot((jax.nn.silu(gate) * up).astype(BF16), wd) * c

    @pl.when(e == pl.num_programs(1) - 1)
    def _():
        ys_ref[...] = _pack_bf16_pairs(acc_ref[...].astype(BF16).astype(F32))


def _moe(tile_group, tile_valid, hs, cs, wg, wu, wd):
    n_sorted = hs.shape[0]
    tm = MOE_TM
    row = lambda w: pl.BlockSpec((tm, w), lambda i, e, tg, tv: (i, 0))
    wspec = lambda a, b: pl.BlockSpec((1, a, b), lambda i, e, tg, tv: (tg[i] * EXPERTS_PER_GROUP + e, 0, 0))
    return pl.pallas_call(
        _moe_kernel,
        grid_spec=pltpu.PrefetchScalarGridSpec(
            num_scalar_prefetch=2, grid=(n_sorted // tm, EXPERTS_PER_GROUP),
            in_specs=[row(HALF), row(LANE), wspec(D_MODEL, EXPERT_FF), wspec(D_MODEL, EXPERT_FF),
                      wspec(EXPERT_FF, D_MODEL)],
            out_specs=row(HALF),
            scratch_shapes=[pltpu.VMEM((tm, HALF), BF16), pltpu.VMEM((tm, HALF), BF16),
                            pltpu.VMEM((tm, D_MODEL), F32)]),
        out_shape=jax.ShapeDtypeStruct((n_sorted, HALF), jnp.uint32),
        compiler_params=_params("parallel", "arbitrary"),
        name="moe_experts",
    )(tile_group, tile_valid, hs, cs, wg, wu, wd)


def _unpermute_kernel(pos_ref, ys_ref, x1_ref, nw_ref, out_ref, stage_ref):
    i = pl.program_id(0)
    tm = x1_ref.shape[0]

    def body(r, _):
        stage_ref[pl.ds(r, 1), :] = ys_ref[pl.ds(pos_ref[i * tm + r], 1), :]
        return 0

    lax.fori_loop(0, tm, body, 0, unroll=8)
    lo, hi = _unpack_bf16_pairs(stage_ref[...])
    x2 = x1_ref[...] + jnp.concatenate([lo, hi], axis=1)
    out_ref[...] = x2 * lax.rsqrt(jnp.mean(x2 * x2, axis=-1, keepdims=True) + EPS) * nw_ref[...]


def _unpermute_residual_norm(pos, ys, x1, nw):
    S = x1.shape[0]
    tm = PERMUTE_TM
    return pl.pallas_call(
        _unpermute_kernel,
        grid_spec=pltpu.PrefetchScalarGridSpec(
            num_scalar_prefetch=1, grid=(S // tm,),
            in_specs=[pl.BlockSpec(ys.shape, lambda i, p: (0, 0), pipeline_mode=pl.Buffered(1)),
                      pl.BlockSpec((tm, D_MODEL), lambda i, p: (i, 0)),
                      pl.BlockSpec((1, D_MODEL), lambda i, p: (0, 0))],
            out_specs=pl.BlockSpec((tm, D_MODEL), lambda i, p: (i, 0)),
            scratch_shapes=[pltpu.VMEM((tm, HALF), jnp.uint32)]),
        out_shape=jax.ShapeDtypeStruct((S, D_MODEL), F32),
        compiler_params=_params("parallel"),
        name="moe_unpermute_norm",
    )(pos, ys, x1, nw)


def _hierarchical_moe(hu, comb, cnt, x1, wg, wu, wd, final_norm_w):
    S = hu.shape[0]
    tm = MOE_TM
    n_sorted = S + N_GROUPS * tm
    assert RANK_LANE == GRP_LANE + 1
    grp_rank = comb[:, GRP_LANE:RANK_LANE + 1].astype(jnp.int32)
    grp, rank = grp_rank[:, 0], grp_rank[:, 1]
    counts = cnt[0, :N_GROUPS].astype(jnp.int32)
    padded = (counts + tm - 1) // tm * tm
    ends = jnp.cumsum(padded)
    starts = ends - padded
    pos = rank + jnp.sum(jnp.where(grp[:, None] == jnp.arange(N_GROUPS)[None, :], starts[None, :], 0), axis=1)
    tile_start = jnp.arange(n_sorted // tm, dtype=jnp.int32) * tm
    tile_group = jnp.minimum(jnp.sum((tile_start[:, None] >= ends[None, :]).astype(jnp.int32), axis=1), N_GROUPS - 1)
    tile_valid = (tile_start < ends[N_GROUPS - 1]).astype(jnp.int32)
    hs, cs = _permute(pos, hu, comb, n_sorted)
    ys = _moe(tile_group, tile_valid, hs, cs, wg, wu, wd)
    return _unpermute_residual_norm(pos, ys, x1, final_norm_w)


def _positional_tables():
    half = RET_QK_DIM // 2
    inv = ROPE_BASE ** (-jnp.arange(half, dtype=F32) / half)
    inv2 = jnp.concatenate([inv, inv]).reshape(1, RET_QK_DIM)
    C, H = RET_CHUNK, RET_HEADS
    log_gamma = jnp.log(1.0 - 2.0 ** (-5.0 - jnp.arange(H, dtype=F32)))
    idx = jnp.arange(C, dtype=F32)
    diff = idx[:, None] - idx[None, :]
    dmask = jnp.where(diff >= 0, jnp.exp(jnp.maximum(diff, 0.0)[None] * log_gamma[:, None, None]), 0.0)
    qd = jnp.exp((idx + 1.0)[None, :] * log_gamma[:, None])
    kd = jnp.exp((C - 1.0 - idx)[None, :] * log_gamma[:, None])
    cd = jnp.exp(C * log_gamma)
    qd = jnp.broadcast_to(qd[:, :, None], (H, C, RET_V_DIM))
    kd = jnp.broadcast_to(kd[:, :, None], (H, C, RET_QK_DIM))
    cd = jnp.broadcast_to(cd[:, None, None], (H, 1, RET_V_DIM))
    return inv2, dmask, qd, kd, cd


def _attention_tables(rel_bias):
    r = np.arange(Q_BLOCK)[None, :]
    cc = np.arange(3 * CMP_TILE)[:, None] - CMP_TILE
    d = r - CMP_STRIDE * (cc - (CMP_TILE - CMP_TOK_PER_QB)) - (CMP_LEN - 1)
    d = np.where(cc < 0, -2, np.where((cc >= CMP_TILE) | (d < 0), -1, d))
    tct = _bias_table_t(rel_bias, d, shift=True)
    d = r + Q_BLOCK - np.arange(3 * Q_BLOCK)[:, None]
    tnt = _bias_table_t(rel_bias, np.where(d < 0, -1, d), shift=True)
    d = r + WINDOW - np.arange(WIN_KEYS + WINDOW)[:, None]
    twt = _bias_table_t(rel_bias, np.where((d < 0) | (d >= WINDOW), -1, d), shift=False)
    kk = np.arange(SEL_WIN * SEL_BLOCK)[:, None] // SEL_BLOCK
    et = jnp.where(jnp.asarray(kk == np.arange(SEL_WIN)[None, :]), -NEG, 0.0).astype(BF16)
    return tct, tnt, twt, et


def _layer(x2d, rel_bias, norm_mix, w_in, cmp_pos_k, cmp_k_w1, cmp_k_w2, cmp_pos_v, cmp_v_w1, cmp_v_w2,
           ret_norm, w_nsa_up, w_ret_up, w_out, norm_ffn, w_rg, b_rg, w_re, b_re, moe_wg, moe_wu, moe_wd,
           final_norm_w):
    S = x2d.shape[0]
    G, dh = NSA_KV_GROUPS, NSA_HEAD_DIM
    assert S % (CMP_TILE * CMP_STRIDE) == 0 and S // SEL_BLOCK >= SEL_TOP_N

    o = np.cumsum((0,) + (NSA_Q_W, 6 * NSA_KV_W, 3 * NSA_HEADS, RET_QK_W, RET_QK_W, RET_V_W, RET_V_W,
                          D_MODEL, D_MODEL))
    seg = lambda n: w_in[:, o[n]:o[n + 1]]
    w_cat = jnp.concatenate([seg(0), seg(1), seg(3), seg(4), seg(5), seg(6), seg(7), seg(8), seg(2),
                             jnp.zeros((D_MODEL, LANE - 3 * NSA_HEADS), F32)], axis=1).astype(BF16)
    inv2, dmask, qd, kd, cd = _positional_tables()

    (qa_t, kcmp, vcmp, ksel, vsel_t, kwin, vwin_t, ga_t, qr, kr, vr, gr, gta, gtr) = _input_projection(
        x2d, norm_mix.reshape(1, D_MODEL), w_cat, inv2)

    nch = S // CMP_STRIDE
    kc = _compress(kcmp, cmp_pos_k, cmp_k_w1, cmp_k_w2).astype(BF16)
    vct = _compress(vcmp, cmp_pos_v, cmp_v_w1, cmp_v_w2).T.reshape(G, dh, nch).astype(BF16)

    tct, tnt, twt, et = _attention_tables(rel_bias)
    ya = _nsa_attention(qa_t.reshape(NSA_HEADS, dh, S), ga_t, kc, vct, tct,
                        ksel, vsel_t.reshape(G, PV_ROWS, S), et, tnt,
                        kwin, vwin_t.reshape(G, PV_ROWS, S), twt)

    yr = _retention(qr, kr, vr, gr, ret_norm.reshape(1, RET_V_W), dmask, qd, kd, cd)

    wrt = jnp.concatenate([w_re, w_rg, jnp.zeros((D_MODEL, LANE - N_EXPERTS - N_GROUPS), F32)], axis=1).astype(BF16)
    brt = jnp.concatenate([b_re, b_rg, jnp.zeros((LANE - N_EXPERTS - N_GROUPS,), F32)]).reshape(1, LANE)
    x1, h2, comb, cnt = _merge(x2d, ya, yr, gta, gtr, w_nsa_up.astype(BF16), w_ret_up.astype(BF16),
                               w_out.astype(BF16), norm_ffn.reshape(1, D_MODEL), wrt, brt)
    return _hierarchical_moe(h2, comb, cnt, x1, moe_wg, moe_wu, moe_wd, final_norm_w)


def kernel(x, rel_bias, norm_mix, w_in, cmp_pos_k, cmp_k_w1, cmp_k_w2, cmp_pos_v, cmp_v_w1, cmp_v_w2, ret_norm, w_nsa_up, w_ret_up, w_out, norm_ffn, w_router_group, b_router_group, w_router_expert, b_router_expert, moe_w_gate, moe_w_up, moe_w_down, norm_final):
    B, S, D = x.shape
    assert B == 1 and norm_mix.shape[0] == 1, "single sequence, depth 1"
    out = _layer(x[0], rel_bias, norm_mix[0], w_in[0], cmp_pos_k[0], cmp_k_w1[0], cmp_k_w2[0],
                 cmp_pos_v[0], cmp_v_w1[0], cmp_v_w2[0], ret_norm[0], w_nsa_up[0], w_ret_up[0], w_out[0],
                 norm_ffn[0], w_router_group[0], b_router_group[0], w_router_expert[0], b_router_expert[0],
                 moe_w_gate[0], moe_w_up[0], moe_w_down[0], norm_final.reshape(1, D))
    return out[None]
```

```python
import functools
import math

import numpy as np
import jax
import jax.numpy as jnp
from jax import lax
from jax.experimental import pallas as pl
from jax.experimental.pallas import tpu as pltpu

F32 = jnp.float32
BF16 = jnp.bfloat16

D_MODEL = 1024
NSA_HEADS = 8
NSA_KV_GROUPS = 2
NSA_HPG = NSA_HEADS // NSA_KV_GROUPS
NSA_HEAD_DIM = 64
CMP_LEN = 32
CMP_STRIDE = 16
CMP_HIDDEN = 256
SEL_BLOCK = 64
SEL_TOP_N = 16
N_LOCAL_SEL = 2
WINDOW = 512
Q_BLOCK = 128
RET_HEADS = 4
RET_QK_DIM = 128
RET_V_DIM = 256
RET_CHUNK = 128
ROPE_BASE = 10000.0
REL_BUCKETS = 32
REL_MAX_DIST = 128
N_GROUPS = 4
EXPERTS_PER_GROUP = 8
N_EXPERTS = N_GROUPS * EXPERTS_PER_GROUP
EXPERT_FF = 256
EPS = 1e-6

NSA_Q_W = NSA_HEADS * NSA_HEAD_DIM
NSA_KV_W = NSA_KV_GROUPS * NSA_HEAD_DIM
RET_QK_W = RET_HEADS * RET_QK_DIM
RET_V_W = RET_HEADS * RET_V_DIM

LANE = 128
NEG = -1e30
VMEM_LIMIT = 56 * 1024 * 1024

CMP_TILE = 128
QB_PER_CMP_TILE = CMP_TILE * CMP_STRIDE // Q_BLOCK
CMP_TOK_PER_QB = Q_BLOCK // CMP_STRIDE
SCR_PAD = 8
SEL_TK = 512
SEL_WIN = 128
SEL_WIN_TILES = SEL_WIN * SEL_BLOCK // SEL_TK
CMP_TRIP_TILES = (4, 2, 1)
FAR_TRIP_TILES = (8, 4, 2)
SEL_ROWS_STEP = 64
PV_ROWS = 80
WIN_KEYS = WINDOW + Q_BLOCK
QW = NSA_HPG * Q_BLOCK


def _dot(a, b):
    return jnp.dot(a, b, preferred_element_type=F32)


def _dot_nt(a, b):
    return lax.dot_general(a, b, (((1,), (1,)), ((), ())), preferred_element_type=F32)


def _const_spec(shape):
    nd = len(shape)
    return pl.BlockSpec(shape, lambda *_: (0,) * nd, pipeline_mode=pl.Buffered(1))


def _params(*sem):
    return pltpu.CompilerParams(dimension_semantics=sem, vmem_limit_bytes=VMEM_LIMIT)


_C_QA = 0
_C_KV = _C_QA + NSA_Q_W
_C_QR = _C_KV + 6 * NSA_KV_W
_C_KR = _C_QR + RET_QK_W
_C_VR = _C_KR + RET_QK_W
_C_GR = _C_VR + RET_V_W
_C_GTA = _C_GR + RET_V_W
_C_GTR = _C_GTA + D_MODEL
_C_GA = _C_GTR + D_MODEL
_C_END = _C_GA + LANE
IN_TM = 512


def _inproj_kernel(x_ref, nw_ref, w_ref, inv_ref,
                   qat_ref, kcmp_ref, vcmp_ref, ksel_ref, vselt_ref, kwin_ref, vwint_ref, gat_ref,
                   qr_ref, kr_ref, vr_ref, gr_ref, gta_ref, gtr_ref):
    tm = x_ref.shape[0]
    x = x_ref[...]
    h = x * lax.rsqrt(jnp.mean(x * x, axis=-1, keepdims=True) + EPS) * nw_ref[...]
    hb = h.astype(BF16)

    def proj(a, width):
        return _dot(hb, w_ref[:, a:a + width])

    def values_t(v):
        vt = v.T.astype(BF16)
        aux = jnp.where(lax.broadcasted_iota(jnp.int32, (PV_ROWS - NSA_HEAD_DIM, tm), 0) == 0, 1.0, 0.0).astype(BF16)
        parts = []
        for g in range(NSA_KV_GROUPS):
            parts += [vt[g * NSA_HEAD_DIM:(g + 1) * NSA_HEAD_DIM], aux]
        return jnp.concatenate(parts, axis=0)

    qat_ref[...] = (proj(_C_QA, NSA_Q_W) * (NSA_HEAD_DIM ** -0.5)).T.astype(BF16)
    kv = proj(_C_KV, 6 * NSA_KV_W)
    kcmp_ref[...] = kv[:, 0 * LANE:1 * LANE]
    vcmp_ref[...] = kv[:, 1 * LANE:2 * LANE]
    ksel_ref[...] = kv[:, 2 * LANE:3 * LANE].astype(BF16)
    vselt_ref[...] = values_t(kv[:, 3 * LANE:4 * LANE])
    kwin_ref[...] = kv[:, 4 * LANE:5 * LANE].astype(BF16)
    vwint_ref[...] = values_t(kv[:, 5 * LANE:6 * LANE])
    gat_ref[...] = jax.nn.sigmoid(proj(_C_GA, LANE)).T

    hm = tm // 2
    half = RET_QK_DIM // 2
    low = lax.broadcasted_iota(jnp.int32, (hm, RET_QK_DIM), 1) < half
    row = pl.program_id(0) * tm + lax.broadcasted_iota(jnp.int32, (hm, RET_QK_DIM), 0)
    ang = jnp.where(low, row, row + hm).astype(F32) * inv_ref[...]
    c, s = jnp.cos(ang), jnp.sin(ang)
    c_sw, s_sw = pltpu.roll(c, half, 1), pltpu.roll(s, half, 1)
    cos2 = jnp.concatenate([jnp.where(low, c, c_sw), jnp.where(low, c_sw, c)], axis=0)
    sin2 = jnp.concatenate([jnp.where(low, -s, s_sw), jnp.where(low, -s_sw, s)], axis=0)
    qr = proj(_C_QR, RET_QK_W)
    kr = proj(_C_KR, RET_QK_W)
    for hd in range(RET_HEADS):
        sl = slice(hd * RET_QK_DIM, (hd + 1) * RET_QK_DIM)
        qh = qr[:, sl]
        kh = kr[:, sl]
        qr_ref[:, sl] = (qh * cos2 + pltpu.roll(qh, RET_QK_DIM // 2, 1) * sin2).astype(BF16)
        kr_ref[:, sl] = (kh * cos2 + pltpu.roll(kh, RET_QK_DIM // 2, 1) * sin2) * (RET_QK_DIM ** -0.5)
    vr_ref[...] = proj(_C_VR, RET_V_W).astype(BF16)
    gr_ref[...] = proj(_C_GR, RET_V_W)
    gta_ref[...] = jax.nn.sigmoid(proj(_C_GTA, D_MODEL))
    gtr_ref[...] = jax.nn.sigmoid(proj(_C_GTR, D_MODEL))


def _input_projection(x2d, norm_w, w_cat, inv2):
    S = x2d.shape[0]
    tm = IN_TM
    vt_rows = NSA_KV_GROUPS * PV_ROWS
    outs = [
        (NSA_Q_W, BF16, True),
        (LANE, F32, False), (LANE, F32, False),
        (LANE, BF16, False), (vt_rows, BF16, True),
        (LANE, BF16, False), (vt_rows, BF16, True),
        (LANE, F32, True),
        (RET_QK_W, BF16, False), (RET_QK_W, F32, False),
        (RET_V_W, BF16, False), (RET_V_W, F32, False),
        (D_MODEL, F32, False), (D_MODEL, F32, False),
    ]
    row = lambda w: pl.BlockSpec((tm, w), lambda i: (i, 0))
    col = lambda w: pl.BlockSpec((w, tm), lambda i: (0, i))
    return pl.pallas_call(
        _inproj_kernel,
        grid=(S // tm,),
        in_specs=[row(D_MODEL), _const_spec((1, D_MODEL)), _const_spec((D_MODEL, _C_END)),
                  _const_spec((1, RET_QK_DIM))],
        out_specs=[col(w) if t else row(w) for w, _, t in outs],
        out_shape=[jax.ShapeDtypeStruct((w, S) if t else (S, w), dt) for w, dt, t in outs],
        compiler_params=_params("parallel"),
        name="in_proj",
    )(x2d, norm_w, w_cat, inv2)


def _compress_kernel(x_ref, pos_ref, w1_ref, w2_ref, out_ref, hid_ref):
    n_tok = out_ref.shape[0]
    hid_ref[...] = jnp.zeros_like(hid_ref)

    def body(l, _):
        rows = x_ref[pl.ds(l, n_tok, stride=CMP_STRIDE), :] + pos_ref[pl.ds(l, 1), :]
        hid_ref[...] += _dot(rows.astype(BF16), w1_ref[l])
        return 0

    lax.fori_loop(0, CMP_LEN, body, 0)
    out_ref[...] = _dot(jax.nn.gelu(hid_ref[...]).astype(BF16), w2_ref[...])


def _compress(x, pos, w1, w2):
    S, width = x.shape
    n_tok = S // CMP_STRIDE
    G = NSA_KV_GROUPS
    def block_diag(w):
        w = w.astype(BF16)
        z = jnp.zeros_like(w)
        return jnp.concatenate([jnp.concatenate([w if k == g else z for k in range(G)], axis=-1)
                                for g in range(G)], axis=-2)

    w1_bd = block_diag(w1.reshape(CMP_LEN, NSA_HEAD_DIM, CMP_HIDDEN))
    w2_bd = block_diag(w2)
    x_pad = jnp.pad(x, ((0, CMP_LEN - CMP_STRIDE), (0, 0)))
    full = lambda a: _const_spec(a.shape)
    pos2 = jnp.concatenate([pos] * G, axis=1)
    return pl.pallas_call(
        _compress_kernel,
        grid=(1,),
        in_specs=[full(x_pad), full(pos2), full(w1_bd), full(w2_bd)],
        out_specs=pl.BlockSpec((n_tok, width), lambda i: (0, 0)),
        out_shape=jax.ShapeDtypeStruct((n_tok, width), F32),
        scratch_shapes=[pltpu.VMEM((n_tok, G * CMP_HIDDEN), F32)],
        compiler_params=_params("arbitrary"),
        name="nsa_compress",
    )(x_pad, pos2, w1_bd, w2_bd)


def _t5_bucket_np(dist):
    dist = np.maximum(dist, 0)
    max_exact = REL_BUCKETS // 2
    d_f = np.maximum(dist, 1).astype(np.float32)
    large = max_exact + (np.log(d_f / np.float32(max_exact)) / np.float32(math.log(REL_MAX_DIST / max_exact))
                         * np.float32(REL_BUCKETS - max_exact)).astype(np.int32)
    large = np.minimum(large, REL_BUCKETS - 1)
    return np.where(dist < max_exact, dist, large)


def _bias_table_t(rel_bias, dist, shift):
    bucket = jnp.asarray(np.where(dist >= 0, _t5_bucket_np(dist), -1), jnp.int32)
    rb = rel_bias - rel_bias[REL_BUCKETS - 1][None, :] if shift else rel_bias
    onehot = (bucket[None] == jnp.arange(REL_BUCKETS, dtype=jnp.int32)[:, None, None]).astype(F32)
    tab = jnp.einsum('bh,bkq->hkq', rb, onehot, precision=lax.Precision.HIGHEST)
    tab = jnp.where(jnp.asarray(dist == -1)[None], NEG, tab)
    K, Q = dist.shape
    return tab.reshape(NSA_KV_GROUPS, NSA_HPG, K, Q).transpose(0, 2, 1, 3).reshape(NSA_KV_GROUPS, K, NSA_HPG * Q)


def _nsa_kernel(n_keys, q_ref, ga_ref, kc_ref, vct_ref, tct_ref, ksel_ref, vselt_ref, et_ref, tnt_ref,
                kwin_ref, vwint_ref, twt_ref, out_ref, sc_ref, ps_ref, sa_ref, sb_ref, ma_ref, mb_ref):
    i = pl.program_id(0)
    dh = NSA_HEAD_DIM
    nb = (ps_ref.shape[1] - 2 * SCR_PAD) // (SEL_BLOCK // CMP_STRIDE)

    @pl.when(i == 0)
    def _():
        sc_ref[...] = jnp.zeros_like(sc_ref)
        ps_ref[...] = jnp.zeros_like(ps_ref)

    def col_reduce(op, red, x):
        rows = x.shape[0]
        while rows <= 256 and rows % 16 == 0:
            rows //= 2
            x = op(x[:rows], x[rows:])
        return red(x, axis=0, keepdims=True)

    def colmax(x):
        return col_reduce(jnp.maximum, jnp.max, x)

    def tile4(x):
        return jnp.concatenate([x] * NSA_HPG, axis=1)

    gates = ga_ref[...]
    tl = i // QB_PER_CMP_TILE
    sub = i % QB_PER_CMP_TILE
    tp = tl - 1
    groups = range(NSA_KV_GROUPS)
    q_t_g = [jnp.concatenate([q_ref[NSA_HPG * g + j] for j in range(NSA_HPG)], axis=1) for g in groups]
    zeros = jnp.zeros((dh, QW), BF16)
    wq_g = [jnp.concatenate([q_t_g[g] if k == g else zeros for k in groups], axis=0) for g in groups]

    def cmp_scores(g, t):
        return _dot(kc_ref[pl.ds(pl.multiple_of(t * CMP_TILE, CMP_TILE), CMP_TILE), :], wq_g[g])

    def sc_rows(t):
        return pl.ds(pl.multiple_of(SCR_PAD + t * CMP_TILE, 8), CMP_TILE)

    def cmp_far(t, m):
        out = []
        for g in groups:
            s = cmp_scores(g, t)
            sc_ref[g, sc_rows(t), :] = s
            out.append(jnp.maximum(m[g], colmax(s)))
        return tuple(out)

    def tile_loop(count, step, carry):
        first, left = 0, count
        for per_trip in CMP_TRIP_TILES:
            trips = left // per_trip

            def body(u, carry, first=first, per_trip=per_trip):
                for k in range(per_trip):
                    carry = step(first + per_trip * u + k, carry)
                return carry

            carry = lax.fori_loop(0, trips, body, carry)
            first, left = first + per_trip * trips, left - per_trip * trips
        return carry

    m_c = tile_loop(jnp.maximum(tp, 0), cmp_far, tuple(jnp.full((1, QW), NEG, F32) for _ in groups))
    tpc = jnp.maximum(tp, 0)
    off = CMP_TILE - CMP_TOK_PER_QB - CMP_TOK_PER_QB * sub
    m_c = list(m_c)
    for g in groups:
        s = (cmp_scores(g, tpc) + tct_ref[g, pl.ds(pl.multiple_of(off, 8), CMP_TILE), :]
             + jnp.where(tp < 0, NEG, 0.0))
        sc_ref[g, sc_rows(tpc), :] = s
        m_c[g] = jnp.maximum(m_c[g], colmax(s))
    for g in groups:
        s = cmp_scores(g, tl) + tct_ref[g, pl.ds(pl.multiple_of(off + CMP_TILE, 8), CMP_TILE), :]
        sc_ref[g, sc_rows(tl), :] = s
        m_c[g] = jnp.maximum(m_c[g], colmax(s))

    def cmp_exp(t, carry):
        out = []
        for g in groups:
            l, acc = carry[g]
            e = jnp.exp(sc_ref[g, sc_rows(t), :] - m_c[g])
            sc_ref[g, sc_rows(t), :] = e
            vt = vct_ref[g, :, pl.ds(pl.multiple_of(t * CMP_TILE, CMP_TILE), CMP_TILE)]
            out.append((l + col_reduce(jnp.add, jnp.sum, e), acc + _dot(vt, e.astype(BF16))))
        return tuple(out)

    cmp_out = tile_loop(tl + 1, cmp_exp,
                        tuple((jnp.zeros((1, QW), F32), jnp.zeros((dh, QW), F32)) for _ in groups))
    inv_c = [jnp.where(m_c[g] > 0.5 * NEG, 1.0 / jnp.maximum(cmp_out[g][0], 1e-30), 0.0) for g in groups]
    o_c_g = [cmp_out[g][1] * inv_c[g] for g in groups]

    def cmp_psum(t, _):
        for g in groups:
            p = sc_ref[g, sc_rows(t), :] * inv_c[g]
            ps = p[:, 0:Q_BLOCK]
            for j in range(1, NSA_HPG):
                ps = ps + p[:, j * Q_BLOCK:(j + 1) * Q_BLOCK]
            ps_ref[g, sc_rows(t), :] = ps.astype(BF16).astype(F32)
        return 0

    tile_loop(tl + 1, cmp_psum, 0)
    per_blk = SEL_BLOCK // CMP_STRIDE
    imp_g = []
    for g in groups:
        imp = ps_ref[g, pl.ds(SCR_PAD - 1, nb, stride=per_blk), :]
        for k in range(1, CMP_LEN // CMP_STRIDE + per_blk - 1):
            imp = imp + ps_ref[g, pl.ds(SCR_PAD - 1 + k, nb, stride=per_blk), :]
        imp_g.append(imp)

    kw0 = pl.multiple_of(jnp.maximum(i * Q_BLOCK - WINDOW, 0), Q_BLOCK)
    tw0 = pl.multiple_of(jnp.maximum(WINDOW - i * Q_BLOCK, 0), Q_BLOCK)
    o_w_g = []
    for g in groups:
        s = _dot(kwin_ref[pl.ds(kw0, WIN_KEYS), :], wq_g[g]) + twt_ref[g, pl.ds(tw0, WIN_KEYS), :]
        e = jnp.exp(s - colmax(s)).astype(BF16)
        acc_w = _dot(vwint_ref[g, :, pl.ds(kw0, WIN_KEYS)], e)
        o_w_g.append(acc_w[0:dh] * (1.0 / jnp.maximum(acc_w[dh:dh + 1], 1e-30)))

    n_io = lax.broadcasted_iota(jnp.int32, (nb, Q_BLOCK), 0)
    r_io = lax.broadcasted_iota(jnp.int32, (nb, Q_BLOCK), 1)
    back = (i * Q_BLOCK + r_io) // SEL_BLOCK - n_io
    valid = back >= 0
    forced = (n_io == 0) | (valid & (back < N_LOCAL_SEL))
    work_g = [jnp.where(forced, -jnp.inf, jnp.where(valid, imp_g[g], -1e9)) for g in groups]
    n_f = n_io.astype(F32)
    def select_rounds(rows):
        def run(*work):
            work = [w[:rows] for w in work]
            for _ in range(SEL_TOP_N - 1 - N_LOCAL_SEL):
                for g in groups:
                    mx = colmax(work[g])
                    first = col_reduce(jnp.minimum, jnp.min, jnp.where(work[g] == mx, n_f[:rows], float(nb)))
                    work[g] = jnp.where(n_f[:rows] == first, -jnp.inf, work[g])
            rest = [jnp.zeros((nb - rows, Q_BLOCK), F32)] if rows < nb else []
            return tuple(jnp.concatenate([jnp.where(w == -jnp.inf, 1.0, 0.0)] + rest, axis=0) for w in work)
        return run

    levels = [SEL_ROWS_STEP * (k + 1) for k in range(nb // SEL_ROWS_STEP)]
    level = jnp.minimum((i * Q_BLOCK + Q_BLOCK - 1) // (SEL_BLOCK * SEL_ROWS_STEP), len(levels) - 1)
    sel_g = list(lax.switch(level, [select_rounds(rows) for rows in levels], *work_g))

    far_blocks = (i - 1) * (Q_BLOCK // SEL_BLOCK)
    selm_g = [(jnp.where(n_io < far_blocks, sel_g[g], 0.0) - 1.0).astype(BF16) for g in groups]

    far_end = (i - 1) * Q_BLOCK
    n_far = (jnp.maximum(far_end, 0) + SEL_TK - 1) // SEL_TK

    def flash_step(carry, s, vt):
        m_i, acc = carry
        m_new = jnp.maximum(m_i, colmax(s))
        alpha = jnp.exp(m_i - m_new)
        p = jnp.exp(s - m_new).astype(BF16)
        return m_new, alpha * acc + _dot(vt, p)

    carry = tuple((jnp.full((1, QW), NEG, F32), jnp.zeros((PV_ROWS, QW), F32)) for _ in groups)
    for w in range(nb // SEL_WIN):
        w_far = [jnp.concatenate([tile4(selm_g[g][w * SEL_WIN:(w + 1) * SEL_WIN]), wq_g[g]], axis=0)
                 for g in groups]

        lo = w * SEL_WIN_TILES
        last = min((w + 1) * SEL_WIN_TILES, n_keys // SEL_TK) - 1
        trips = jnp.maximum(jnp.minimum(n_far, last + 1) - lo + 1, 0) // 2

        def far_scores(kt, s_ref, mx_ref, g, w=w, w_far=w_far):
            k0 = pl.multiple_of(kt * SEL_TK, SEL_TK)
            e0 = pl.multiple_of((kt - w * SEL_WIN_TILES) * SEL_TK, SEL_TK)
            lhs = jnp.concatenate([et_ref[pl.ds(e0, SEL_TK), :],
                                   ksel_ref[pl.ds(k0, SEL_TK), :]], axis=1)
            s = _dot(lhs, w_far[g])
            s_ref[g] = s
            mx_ref[g] = colmax(s)

        def far_update(kt, s_ref, mx_ref, g, carry_g):
            k0 = pl.multiple_of(kt * SEL_TK, SEL_TK)
            m_i, acc = carry_g
            m_new = jnp.maximum(m_i, mx_ref[g])
            p = jnp.exp(s_ref[g] - m_new).astype(BF16)
            return m_new, jnp.exp(m_i - m_new) * acc + _dot(vselt_ref[g, :, pl.ds(k0, SEL_TK)], p)

        def far_trip(tiles, first, last=last, far_scores=far_scores):
            def body(u, carry):
                kt = first + tiles * u
                carry = list(carry)
                bufs = [(sa_ref, ma_ref), (sb_ref, mb_ref)]
                for t in range(tiles):
                    for g in groups:
                        far_scores(jnp.minimum(kt + t + 1, last), *bufs[(t + 1) % 2], g)
                        carry[g] = far_update(kt + t, *bufs[t % 2], g, carry[g])
                return tuple(carry)
            return body

        @pl.when(trips > 0)
        def _(lo=lo, far_scores=far_scores):
            for g in groups:
                far_scores(jnp.int32(lo), sa_ref, ma_ref, g)

        first, left = lo, trips
        for tiles in FAR_TRIP_TILES:
            count = left // (tiles // 2)
            carry = lax.fori_loop(0, count, far_trip(tiles, first), carry)
            first, left = first + tiles * count, left - count * (tiles // 2)

    heads_out = []
    for g in groups:
        wq, o_c, sel = wq_g[g], o_c_g[g], sel_g[g]
        k0n = pl.multiple_of(jnp.maximum(far_end, 0), Q_BLOCK)
        t0n = pl.multiple_of(jnp.maximum(-far_end, 0), Q_BLOCK)
        near_blocks = 2 * Q_BLOCK // SEL_BLOCK
        pb = lax.broadcasted_iota(jnp.int32, (16, nb), 0)
        pn = lax.broadcasted_iota(jnp.int32, (16, nb), 1)
        pick_rows = jnp.where((pn == jnp.maximum(far_blocks, 0) + pb) & (pb < near_blocks), 1.0, 0.0).astype(BF16)
        sel_rows = _dot(pick_rows, sel.astype(BF16))
        add_near = jnp.concatenate(
            [jnp.broadcast_to((sel_rows[b:b + 1, :] - 1.0) * (-NEG), (SEL_BLOCK, Q_BLOCK)) for b in range(near_blocks)],
            axis=0)
        s = (_dot(ksel_ref[pl.ds(k0n, 2 * Q_BLOCK), :], wq) + tnt_ref[g, pl.ds(t0n, 2 * Q_BLOCK), :]
             + tile4(add_near))
        m_s, acc_s = flash_step(carry[g], s, vselt_ref[g, :, pl.ds(k0n, 2 * Q_BLOCK)])
        o_s = acc_s[0:dh] * (1.0 / jnp.maximum(acc_s[dh:dh + 1], 1e-30))

        o_w = o_w_g[g]

        for j in range(NSA_HPG):
            hd = NSA_HPG * g + j
            cols = slice(j * Q_BLOCK, (j + 1) * Q_BLOCK)
            heads_out.append(gates[hd:hd + 1] * o_c[:, cols]
                             + gates[NSA_HEADS + hd:NSA_HEADS + hd + 1] * o_s[:, cols]
                             + gates[2 * NSA_HEADS + hd:2 * NSA_HEADS + hd + 1] * o_w[:, cols])

    out_ref[...] = jnp.concatenate(heads_out, axis=0).T.astype(out_ref.dtype)


def _nsa_attention(q_t, gates_t, kc, vct, tct, ksel, vselt, et, tnt, kwin, vwint, twt):
    H, dh, S = q_t.shape
    n_qb = S // Q_BLOCK
    n_tok = S // CMP_STRIDE
    nb = -(-(S // SEL_BLOCK) // SEL_WIN) * SEL_WIN
    ps_rows = nb * (SEL_BLOCK // CMP_STRIDE) + 2 * SCR_PAD
    full = lambda a: _const_spec(a.shape)
    return pl.pallas_call(
        functools.partial(_nsa_kernel, S),
        grid=(n_qb,),
        in_specs=[pl.BlockSpec((H, dh, Q_BLOCK), lambda i: (0, 0, i)),
                  pl.BlockSpec((gates_t.shape[0], Q_BLOCK), lambda i: (0, i)),
                  full(kc), full(vct), full(tct), full(ksel), full(vselt), full(et), full(tnt),
                  full(kwin), full(vwint), full(twt)],
        out_specs=pl.BlockSpec((Q_BLOCK, H * dh), lambda i: (i, 0)),
        out_shape=jax.ShapeDtypeStruct((S, H * dh), BF16),
        scratch_shapes=[pltpu.VMEM((NSA_KV_GROUPS, SCR_PAD + n_tok, QW), F32),
                        pltpu.VMEM((NSA_KV_GROUPS, ps_rows, Q_BLOCK), F32),
                        pltpu.VMEM((NSA_KV_GROUPS, SEL_TK, QW), F32),
                        pltpu.VMEM((NSA_KV_GROUPS, SEL_TK, QW), F32),
                        pltpu.VMEM((NSA_KV_GROUPS, 1, QW), F32),
                        pltpu.VMEM((NSA_KV_GROUPS, 1, QW), F32)],
        compiler_params=_params("arbitrary"),
        name="nsa_attention",
    )(q_t, gates_t, kc, vct, tct, ksel, vselt, et, tnt, kwin, vwint, twt)


def _retention_kernel(q_ref, k_ref, v_ref, g_ref, nw_ref, dmask_ref, qd_ref, kd_ref, cd_ref, out_ref, state_ref):
    @pl.when(pl.program_id(0) == 0)
    def _():
        state_ref[...] = jnp.zeros_like(state_ref)

    states = [state_ref[hd] for hd in range(RET_HEADS)]
    for c in range(q_ref.shape[0] // RET_CHUNK):
        rows = slice(c * RET_CHUNK, (c + 1) * RET_CHUNK)
        for hd in range(RET_HEADS):
            ks = slice(hd * RET_QK_DIM, (hd + 1) * RET_QK_DIM)
            vs = slice(hd * RET_V_DIM, (hd + 1) * RET_V_DIM)
            q = q_ref[rows, ks]
            k = k_ref[rows, ks]
            v = v_ref[rows, vs]
            state = states[hd]
            inner = _dot_nt(q, k.astype(BF16)) * dmask_ref[hd]
            y = _dot(inner.astype(BF16), v) + _dot(q, state.astype(BF16)) * qd_ref[hd]
            kd = k * kd_ref[hd]
            states[hd] = state * cd_ref[hd] + _dot(kd.T.astype(BF16), v)
            mu = jnp.mean(y, axis=-1, keepdims=True)
            yc = y - mu
            var = jnp.mean(yc * yc, axis=-1, keepdims=True)
            yn = yc * lax.rsqrt(var + EPS) * nw_ref[:, vs]
            out_ref[rows, vs] = (jax.nn.silu(g_ref[rows, vs]) * yn).astype(out_ref.dtype)
    for hd in range(RET_HEADS):
        state_ref[hd] = states[hd]


RET_STEP_CHUNKS = 8


def _retention(q, k, v, g, norm_w, dmask, qd, kd, cd):
    S = q.shape[0]
    C = RET_CHUNK * RET_STEP_CHUNKS
    row = lambda w: pl.BlockSpec((C, w), lambda i: (i, 0))
    full = lambda a: _const_spec(a.shape)
    return pl.pallas_call(
        _retention_kernel,
        grid=(S // C,),
        in_specs=[row(RET_QK_W), row(RET_QK_W), row(RET_V_W), row(RET_V_W),
                  full(norm_w), full(dmask), full(qd), full(kd), full(cd)],
        out_specs=row(RET_V_W),
        out_shape=jax.ShapeDtypeStruct((S, RET_V_W), BF16),
        scratch_shapes=[pltpu.VMEM((RET_HEADS, RET_QK_DIM, RET_V_DIM), F32)],
        compiler_params=_params("arbitrary"),
        name="retention",
    )(q, k, v, g, norm_w, dmask, qd, kd, cd)


MERGE_TM = 512
MERGE_SUB_ROWS = 256
GRP_LANE = 40
RANK_LANE = 41


def _merge_kernel(x_ref, ya_ref, yr_ref, gta_ref, gtr_ref, wa_ref, wr_ref, wo_ref, nw_ref, wrt_ref, brt_ref,
                  tri_ref, x1_ref, h_ref, comb_ref, cnt_ref, carry_ref):
    @pl.when(pl.program_id(0) == 0)
    def _():
        carry_ref[...] = jnp.zeros_like(carry_ref)

    sub = tri_ref.shape[0]
    carry = carry_ref[...]
    for r in range(x_ref.shape[0] // sub):
        carry = _merge_rows(slice(r * sub, (r + 1) * sub), carry, x_ref, ya_ref, yr_ref, gta_ref, gtr_ref,
                            wa_ref, wr_ref, wo_ref, nw_ref, wrt_ref, brt_ref, tri_ref, x1_ref, h_ref, comb_ref)
    carry_ref[...] = carry
    cnt_ref[...] = jnp.broadcast_to(carry, cnt_ref.shape)


def _merge_rows(rows, carry, x_ref, ya_ref, yr_ref, gta_ref, gtr_ref, wa_ref, wr_ref, wo_ref, nw_ref, wrt_ref,
                brt_ref, tri_ref, x1_ref, h_ref, comb_ref):
    merged = gta_ref[rows] * _dot(ya_ref[rows], wa_ref[...]) + gtr_ref[rows] * _dot(yr_ref[rows], wr_ref[...])
    x1 = x_ref[rows] + _dot(merged.astype(BF16), wo_ref[...])
    x1_ref[rows] = x1
    h = (x1 * lax.rsqrt(jnp.mean(x1 * x1, axis=-1, keepdims=True) + EPS) * nw_ref[...]).astype(BF16)
    h_ref[rows] = _pack_bf16_pairs(h.astype(F32))
    logits = _dot(h, wrt_ref[...]) + brt_ref[...]
    lane = lax.broadcasted_iota(jnp.int32, logits.shape, 1)
    lane_f = lane.astype(F32)
    big = float(LANE)
    is_grp = (lane >= N_EXPERTS) & (lane < N_EXPERTS + N_GROUPS)
    lg = jnp.where(is_grp, logits, -jnp.inf)
    eg = jnp.where(is_grp, jnp.exp(lg - jnp.max(lg, axis=-1, keepdims=True)), 0.0)
    pg = eg / jnp.sum(eg, axis=-1, keepdims=True)
    p_grp = jnp.max(pg, axis=-1, keepdims=True)
    grp = jnp.min(jnp.where(is_grp & (pg == p_grp), lane_f, big), axis=-1, keepdims=True) - N_EXPERTS
    in_grp = (lane_f >= grp * EXPERTS_PER_GROUP) & (lane_f < (grp + 1.0) * EXPERTS_PER_GROUP)
    le = jnp.where(in_grp, logits, -jnp.inf)
    m1 = jnp.max(le, axis=-1, keepdims=True)
    i1 = jnp.min(jnp.where(le == m1, lane_f, big), axis=-1, keepdims=True)
    le2 = jnp.where(lane_f == i1, -jnp.inf, le)
    m2 = jnp.max(le2, axis=-1, keepdims=True)
    i2 = jnp.min(jnp.where(le2 == m2, lane_f, big), axis=-1, keepdims=True)
    e2 = jnp.exp(m2 - m1)
    den = 1.0 + e2
    onehot = jnp.where(lane_f == grp, 1.0, 0.0)
    before = _dot(tri_ref[...], onehot.astype(BF16)) + carry
    rank = jnp.sum(onehot * before, axis=-1, keepdims=True)
    comb_ref[rows] = (jnp.where(lane_f == i1, (1.0 / den) * p_grp, 0.0)
                      + jnp.where(lane_f == i2, (e2 / den) * p_grp, 0.0)
                      + jnp.where(lane == GRP_LANE, grp, 0.0)
                      + jnp.where(lane == RANK_LANE, rank, 0.0))
    return carry + jnp.sum(onehot, axis=0, keepdims=True)


def _pack_bf16_pairs(x):
    n = x.shape[1] // 2
    return pltpu.pack_elementwise([x[:, :n], x[:, n:]], packed_dtype=BF16)


def _unpack_bf16_pairs(w):
    return tuple(pltpu.unpack_elementwise(w, index=k, packed_dtype=BF16, unpacked_dtype=F32) for k in range(2))


def _merge(x2d, ya, yr, gta, gtr, wa, wr, wo, nw, wrt, brt):
    S = x2d.shape[0]
    tm = MERGE_TM
    row = lambda w: pl.BlockSpec((tm, w), lambda i: (i, 0))
    full = lambda a: _const_spec(a.shape)
    sub = MERGE_SUB_ROWS
    tri = jnp.asarray(np.tril(np.ones((sub, sub), np.float32), -1), BF16)
    return pl.pallas_call(
        _merge_kernel,
        grid=(S // tm,),
        in_specs=[row(D_MODEL), row(NSA_Q_W), row(RET_V_W), row(D_MODEL), row(D_MODEL),
                  full(wa), full(wr), full(wo), full(nw), full(wrt), full(brt), full(tri)],
        out_specs=[row(D_MODEL), row(D_MODEL // 2), row(LANE), pl.BlockSpec((8, LANE), lambda i: (0, 0))],
        out_shape=[jax.ShapeDtypeStruct((S, D_MODEL), F32), jax.ShapeDtypeStruct((S, D_MODEL // 2), jnp.uint32),
                   jax.ShapeDtypeStruct((S, LANE), F32), jax.ShapeDtypeStruct((8, LANE), F32)],
        scratch_shapes=[pltpu.VMEM((1, LANE), F32)],
        compiler_params=_params("arbitrary"),
        name="merge_router",
    )(x2d, ya, yr, gta, gtr, wa, wr, wo, nw, wrt, brt, tri)


MOE_TM = 1024
MOE_SUB_TILES = 2
PERMUTE_TM = 512
HALF = D_MODEL // 2


def _permute_kernel(pos_ref, h_ref, comb_ref, hs_ref, cs_ref):
    i = pl.program_id(0)

    @pl.when(i == 0)
    def _():
        hs_ref[...] = jnp.zeros_like(hs_ref)
        cs_ref[...] = jnp.zeros_like(cs_ref)

    tm = h_ref.shape[0]

    def body(r, _):
        d = pos_ref[i * tm + r]
        hs_ref[pl.ds(d, 1), :] = h_ref[pl.ds(r, 1), :]
        cs_ref[pl.ds(d, 1), :] = comb_ref[pl.ds(r, 1), :]
        return 0

    lax.fori_loop(0, tm, body, 0, unroll=8)


def _permute(pos, hu, comb, n_sorted):
    S = hu.shape[0]
    tm = PERMUTE_TM
    resident = lambda w: pl.BlockSpec((n_sorted, w), lambda i, p: (0, 0), pipeline_mode=pl.Buffered(1))
    return pl.pallas_call(
        _permute_kernel,
        grid_spec=pltpu.PrefetchScalarGridSpec(
            num_scalar_prefetch=1, grid=(S // tm,),
            in_specs=[pl.BlockSpec((tm, HALF), lambda i, p: (i, 0)), pl.BlockSpec((tm, LANE), lambda i, p: (i, 0))],
            out_specs=[resident(HALF), resident(LANE)]),
        out_shape=[jax.ShapeDtypeStruct((n_sorted, HALF), jnp.uint32), jax.ShapeDtypeStruct((n_sorted, LANE), F32)],
        compiler_params=_params("arbitrary"),
        name="moe_permute",
    )(pos, hu, comb)


def _moe_kernel(tg_ref, tv_ref, tf_ref, hs_ref, cs_ref, wg_ref, wu_ref, wd_ref, ys_ref,
                lo_ref, hi_ref, acc_ref, wgb_ref, wub_ref, wdb_ref):
    i = pl.program_id(0)
    e = pl.program_id(1)

    @pl.when(e == 0)
    def _():
        lo, hi = _unpack_bf16_pairs(hs_ref[...])
        lo_ref[...] = lo.astype(BF16)
        hi_ref[...] = hi.astype(BF16)
        acc_ref[...] = jnp.zeros_like(acc_ref)

    @pl.when(tf_ref[i] > 0)
    def _():
        wgb_ref[e] = wg_ref[0].astype(BF16)
        wub_ref[e] = wu_ref[0].astype(BF16)
        wdb_ref[e] = wd_ref[0].astype(BF16)

    @pl.when(tv_ref[i] > 0)
    def _():
        wg = wgb_ref[e]
        wu = wub_ref[e]
        wd = wdb_ref[e]
        sub = lo_ref.shape[0] // MOE_SUB_TILES
        for r in range(MOE_SUB_TILES):
            rows = slice(r * sub, (r + 1) * sub)
            lo, hi = lo_ref[rows], hi_ref[rows]
            gate = _dot(lo, wg[:HALF]) + _dot(hi, wg[HALF:])
            up = _dot(lo, wu[:HALF]) + _dot(hi, wu[HALF:])
            comb = cs_ref[rows]
            lane = lax.broadcasted_iota(jnp.int32, comb.shape, 1)
            c = jnp.sum(jnp.where(lane == tg_ref[i] * EXPERTS_PER_GROUP + e, comb, 0.0), axis=-1, keepdims=True)
            acc_ref[rows] += _dot((jax.nn.silu(gate) * up).astype(BF16), wd) * c

    @pl.when(e == pl.num_programs(1) - 1)
    def _():
        ys_ref[...] = _pack_bf16_pairs(acc_ref[...].astype(BF16).astype(F32))


def _moe(tile_group, tile_valid, tile_first, hs, cs, wg, wu, wd):
    n_sorted = hs.shape[0]
    tm = MOE_TM
    row = lambda w: pl.BlockSpec((tm, w), lambda i, e, tg, tv, tf: (i, 0))
    wspec = lambda a, b: pl.BlockSpec((1, a, b), lambda i, e, tg, tv, tf: (tg[i] * EXPERTS_PER_GROUP + e, 0, 0))
    epg = EXPERTS_PER_GROUP
    return pl.pallas_call(
        _moe_kernel,
        grid_spec=pltpu.PrefetchScalarGridSpec(
            num_scalar_prefetch=3, grid=(n_sorted // tm, epg),
            in_specs=[row(HALF), row(LANE), wspec(D_MODEL, EXPERT_FF), wspec(D_MODEL, EXPERT_FF),
                      wspec(EXPERT_FF, D_MODEL)],
            out_specs=row(HALF),
            scratch_shapes=[pltpu.VMEM((tm, HALF), BF16), pltpu.VMEM((tm, HALF), BF16),
                            pltpu.VMEM((tm, D_MODEL), F32),
                            pltpu.VMEM((epg, D_MODEL, EXPERT_FF), BF16),
                            pltpu.VMEM((epg, D_MODEL, EXPERT_FF), BF16),
                            pltpu.VMEM((epg, EXPERT_FF, D_MODEL), BF16)]),
        out_shape=jax.ShapeDtypeStruct((n_sorted, HALF), jnp.uint32),
        compiler_params=_params("arbitrary", "arbitrary"),
        name="moe_experts",
    )(tile_group, tile_valid, tile_first, hs, cs, wg, wu, wd)


def _unpermute_kernel(pos_ref, ys_ref, x1_ref, nw_ref, out_ref, stage_ref):
    i = pl.program_id(0)
    tm = x1_ref.shape[0]

    def body(r, _):
        stage_ref[pl.ds(r, 1), :] = ys_ref[pl.ds(pos_ref[i * tm + r], 1), :]
        return 0

    lax.fori_loop(0, tm, body, 0, unroll=8)
    lo, hi = _unpack_bf16_pairs(stage_ref[...])
    x2 = x1_ref[...] + jnp.concatenate([lo, hi], axis=1)
    out_ref[...] = x2 * lax.rsqrt(jnp.mean(x2 * x2, axis=-1, keepdims=True) + EPS) * nw_ref[...]


def _unpermute_residual_norm(pos, ys, x1, nw):
    S = x1.shape[0]
    tm = PERMUTE_TM
    return pl.pallas_call(
        _unpermute_kernel,
        grid_spec=pltpu.PrefetchScalarGridSpec(
            num_scalar_prefetch=1, grid=(S // tm,),
            in_specs=[pl.BlockSpec(ys.shape, lambda i, p: (0, 0), pipeline_mode=pl.Buffered(1)),
                      pl.BlockSpec((tm, D_MODEL), lambda i, p: (i, 0)),
                      pl.BlockSpec((1, D_MODEL), lambda i, p: (0, 0))],
            out_specs=pl.BlockSpec((tm, D_MODEL), lambda i, p: (i, 0)),
            scratch_shapes=[pltpu.VMEM((tm, HALF), jnp.uint32)]),
        out_shape=jax.ShapeDtypeStruct((S, D_MODEL), F32),
        compiler_params=_params("parallel"),
        name="moe_unpermute_norm",
    )(pos, ys, x1, nw)


def _hierarchical_moe(hu, comb, cnt, x1, wg, wu, wd, final_norm_w):
    S = hu.shape[0]
    tm = MOE_TM
    n_sorted = S + N_GROUPS * tm
    assert RANK_LANE == GRP_LANE + 1
    grp_rank = comb[:, GRP_LANE:RANK_LANE + 1].astype(jnp.int32)
    grp, rank = grp_rank[:, 0], grp_rank[:, 1]
    counts = cnt[0, :N_GROUPS].astype(jnp.int32)
    padded = (counts + tm - 1) // tm * tm
    ends = jnp.cumsum(padded)
    starts = ends - padded
    pos = rank + jnp.sum(jnp.where(grp[:, None] == jnp.arange(N_GROUPS)[None, :], starts[None, :], 0), axis=1)
    tile_start = jnp.arange(n_sorted // tm, dtype=jnp.int32) * tm
    tile_group = jnp.minimum(jnp.sum((tile_start[:, None] >= ends[None, :]).astype(jnp.int32), axis=1), N_GROUPS - 1)
    tile_valid = (tile_start < ends[N_GROUPS - 1]).astype(jnp.int32)
    tile_first = jnp.any(tile_start[:, None] == starts[None, :], axis=1).astype(jnp.int32) * tile_valid
    hs, cs = _permute(pos, hu, comb, n_sorted)
    ys = _moe(tile_group, tile_valid, tile_first, hs, cs, wg, wu, wd)
    return _unpermute_residual_norm(pos, ys, x1, final_norm_w)


def _positional_tables():
    half = RET_QK_DIM // 2
    inv = ROPE_BASE ** (-jnp.arange(half, dtype=F32) / half)
    inv2 = jnp.concatenate([inv, inv]).reshape(1, RET_QK_DIM)
    C, H = RET_CHUNK, RET_HEADS
    log_gamma = jnp.log(1.0 - 2.0 ** (-5.0 - jnp.arange(H, dtype=F32)))
    idx = jnp.arange(C, dtype=F32)
    diff = idx[:, None] - idx[None, :]
    dmask = jnp.where(diff >= 0, jnp.exp(jnp.maximum(diff, 0.0)[None] * log_gamma[:, None, None]), 0.0)
    qd = jnp.exp((idx + 1.0)[None, :] * log_gamma[:, None])
    kd = jnp.exp((C - 1.0 - idx)[None, :] * log_gamma[:, None])
    cd = jnp.exp(C * log_gamma)
    qd = jnp.broadcast_to(qd[:, :, None], (H, C, RET_V_DIM))
    kd = jnp.broadcast_to(kd[:, :, None], (H, C, RET_QK_DIM))
    cd = jnp.broadcast_to(cd[:, None, None], (H, 1, RET_V_DIM))
    return inv2, dmask, qd, kd, cd


def _attention_tables(rel_bias):
    r = np.arange(Q_BLOCK)[None, :]
    cc = np.arange(3 * CMP_TILE)[:, None] - CMP_TILE
    d = r - CMP_STRIDE * (cc - (CMP_TILE - CMP_TOK_PER_QB)) - (CMP_LEN - 1)
    d = np.where(cc < 0, -2, np.where((cc >= CMP_TILE) | (d < 0), -1, d))
    tct = _bias_table_t(rel_bias, d, shift=True)
    d = r + Q_BLOCK - np.arange(3 * Q_BLOCK)[:, None]
    tnt = _bias_table_t(rel_bias, np.where(d < 0, -1, d), shift=True)
    d = r + WINDOW - np.arange(WIN_KEYS + WINDOW)[:, None]
    twt = _bias_table_t(rel_bias, np.where((d < 0) | (d >= WINDOW), -1, d), shift=False)
    kk = np.arange(SEL_WIN * SEL_BLOCK)[:, None] // SEL_BLOCK
    et = jnp.where(jnp.asarray(kk == np.arange(SEL_WIN)[None, :]), -NEG, 0.0).astype(BF16)
    return tct, tnt, twt, et


def _layer(x2d, rel_bias, norm_mix, w_in, cmp_pos_k, cmp_k_w1, cmp_k_w2, cmp_pos_v, cmp_v_w1, cmp_v_w2,
           ret_norm, w_nsa_up, w_ret_up, w_out, norm_ffn, w_rg, b_rg, w_re, b_re, moe_wg, moe_wu, moe_wd,
           final_norm_w):
    S = x2d.shape[0]
    G, dh = NSA_KV_GROUPS, NSA_HEAD_DIM
    assert S % (CMP_TILE * CMP_STRIDE) == 0 and S // SEL_BLOCK >= SEL_TOP_N

    o = np.cumsum((0,) + (NSA_Q_W, 6 * NSA_KV_W, 3 * NSA_HEADS, RET_QK_W, RET_QK_W, RET_V_W, RET_V_W,
                          D_MODEL, D_MODEL))
    seg = lambda n: w_in[:, o[n]:o[n + 1]]
    w_cat = jnp.concatenate([seg(0), seg(1), seg(3), seg(4), seg(5), seg(6), seg(7), seg(8), seg(2),
                             jnp.zeros((D_MODEL, LANE - 3 * NSA_HEADS), F32)], axis=1).astype(BF16)
    inv2, dmask, qd, kd, cd = _positional_tables()

    (qa_t, kcmp, vcmp, ksel, vsel_t, kwin, vwin_t, ga_t, qr, kr, vr, gr, gta, gtr) = _input_projection(
        x2d, norm_mix.reshape(1, D_MODEL), w_cat, inv2)

    nch = S // CMP_STRIDE
    kc = _compress(kcmp, cmp_pos_k, cmp_k_w1, cmp_k_w2).astype(BF16)
    vct = _compress(vcmp, cmp_pos_v, cmp_v_w1, cmp_v_w2).T.reshape(G, dh, nch).astype(BF16)

    tct, tnt, twt, et = _attention_tables(rel_bias)
    ya = _nsa_attention(qa_t.reshape(NSA_HEADS, dh, S), ga_t, kc, vct, tct,
                        ksel, vsel_t.reshape(G, PV_ROWS, S), et, tnt,
                        kwin, vwin_t.reshape(G, PV_ROWS, S), twt)

    yr = _retention(qr, kr, vr, gr, ret_norm.reshape(1, RET_V_W), dmask, qd, kd, cd)

    wrt = jnp.concatenate([w_re, w_rg, jnp.zeros((D_MODEL, LANE - N_EXPERTS - N_GROUPS), F32)], axis=1).astype(BF16)
    brt = jnp.concatenate([b_re, b_rg, jnp.zeros((LANE - N_EXPERTS - N_GROUPS,), F32)]).reshape(1, LANE)
    x1, h2, comb, cnt = _merge(x2d, ya, yr, gta, gtr, w_nsa_up.astype(BF16), w_ret_up.astype(BF16),
                               w_out.astype(BF16), norm_ffn.reshape(1, D_MODEL), wrt, brt)
    return _hierarchical_moe(h2, comb, cnt, x1, moe_wg, moe_wu, moe_wd, final_norm_w)


def kernel(x, rel_bias, norm_mix, w_in, cmp_pos_k, cmp_k_w1, cmp_k_w2, cmp_pos_v, cmp_v_w1, cmp_v_w2, ret_norm, w_nsa_up, w_ret_up, w_out, norm_ffn, w_router_group, b_router_group, w_router_expert, b_router_expert, moe_w_gate, moe_w_up, moe_w_down, norm_final):
    B, S, D = x.shape
    assert B == 1 and norm_mix.shape[0] == 1, "single sequence, depth 1"
    out = _layer(x[0], rel_bias, norm_mix[0], w_in[0], cmp_pos_k[0], cmp_k_w1[0], cmp_k_w2[0],
                 cmp_pos_v[0], cmp_v_w1[0], cmp_v_w2[0], ret_norm[0], w_nsa_up[0], w_ret_up[0], w_out[0],
                 norm_ffn[0], w_router_group[0], b_router_group[0], w_router_expert[0], b_router_expert[0],
                 moe_w_gate[0], moe_w_up[0], moe_w_down[0], norm_final.reshape(1, D))
    return out[None]
```

```python
import functools
import math

import numpy as np
import jax
import jax.numpy as jnp
from jax import lax
from jax.experimental import pallas as pl
from jax.experimental.pallas import tpu as pltpu

F32 = jnp.float32
BF16 = jnp.bfloat16

D_MODEL = 1024
NSA_HEADS = 8
NSA_KV_GROUPS = 2
NSA_HPG = NSA_HEADS // NSA_KV_GROUPS
NSA_HEAD_DIM = 64
CMP_LEN = 32
CMP_STRIDE = 16
CMP_HIDDEN = 256
SEL_BLOCK = 64
SEL_TOP_N = 16
N_LOCAL_SEL = 2
WINDOW = 512
Q_BLOCK = 128
RET_HEADS = 4
RET_QK_DIM = 128
RET_V_DIM = 256
RET_CHUNK = 128
ROPE_BASE = 10000.0
REL_BUCKETS = 32
REL_MAX_DIST = 128
N_GROUPS = 4
EXPERTS_PER_GROUP = 8
N_EXPERTS = N_GROUPS * EXPERTS_PER_GROUP
EXPERT_FF = 256
EPS = 1e-6

NSA_Q_W = NSA_HEADS * NSA_HEAD_DIM
NSA_KV_W = NSA_KV_GROUPS * NSA_HEAD_DIM
RET_QK_W = RET_HEADS * RET_QK_DIM
RET_V_W = RET_HEADS * RET_V_DIM

LANE = 128
NEG = -1e30
VMEM_LIMIT = 56 * 1024 * 1024

CMP_TILE = 128
QB_PER_CMP_TILE = CMP_TILE * CMP_STRIDE // Q_BLOCK
CMP_TOK_PER_QB = Q_BLOCK // CMP_STRIDE
SCR_PAD = 8
SEL_TK = 512
SEL_WIN = 128
SEL_WIN_TILES = SEL_WIN * SEL_BLOCK // SEL_TK
CMP_TRIP_TILES = (4, 2, 1)
FAR_TRIP_TILES = (8, 4, 2)
SEL_ROWS_STEP = 64
PV_ROWS = 80
WIN_KEYS = WINDOW + Q_BLOCK
QW = NSA_HPG * Q_BLOCK


def _dot(a, b):
    return jnp.dot(a, b, preferred_element_type=F32)


def _dot_nt(a, b):
    return lax.dot_general(a, b, (((1,), (1,)), ((), ())), preferred_element_type=F32)


def _const_spec(shape):
    nd = len(shape)
    return pl.BlockSpec(shape, lambda *_: (0,) * nd, pipeline_mode=pl.Buffered(1))


def _params(*sem):
    return pltpu.CompilerParams(dimension_semantics=sem, vmem_limit_bytes=VMEM_LIMIT)


_C_QA = 0
_C_KV = _C_QA + NSA_Q_W
_C_QR = _C_KV + 6 * NSA_KV_W
_C_KR = _C_QR + RET_QK_W
_C_VR = _C_KR + RET_QK_W
_C_GR = _C_VR + RET_V_W
_C_GTA = _C_GR + RET_V_W
_C_GTR = _C_GTA + D_MODEL
_C_GA = _C_GTR + D_MODEL
_C_END = _C_GA + LANE
IN_TM = 512


def _inproj_kernel(x_ref, nw_ref, w_ref, inv_ref,
                   qat_ref, kcmp_ref, vcmp_ref, ksel_ref, vselt_ref, kwin_ref, vwint_ref, gat_ref,
                   qr_ref, kr_ref, vr_ref, gr_ref, gta_ref, gtr_ref):
    tm = x_ref.shape[0]
    x = x_ref[...]
    h = x * lax.rsqrt(jnp.mean(x * x, axis=-1, keepdims=True) + EPS) * nw_ref[...]
    hb = h.astype(BF16)

    def proj(a, width):
        return _dot(hb, w_ref[:, a:a + width])

    def values_t(v):
        vt = v.T.astype(BF16)
        aux = jnp.where(lax.broadcasted_iota(jnp.int32, (PV_ROWS - NSA_HEAD_DIM, tm), 0) == 0, 1.0, 0.0).astype(BF16)
        parts = []
        for g in range(NSA_KV_GROUPS):
            parts += [vt[g * NSA_HEAD_DIM:(g + 1) * NSA_HEAD_DIM], aux]
        return jnp.concatenate(parts, axis=0)

    qat_ref[...] = (proj(_C_QA, NSA_Q_W) * (NSA_HEAD_DIM ** -0.5)).T.astype(BF16)
    kv = proj(_C_KV, 6 * NSA_KV_W)
    kcmp_ref[...] = kv[:, 0 * LANE:1 * LANE]
    vcmp_ref[...] = kv[:, 1 * LANE:2 * LANE]
    ksel_ref[...] = kv[:, 2 * LANE:3 * LANE].astype(BF16)
    vselt_ref[...] = values_t(kv[:, 3 * LANE:4 * LANE])
    kwin_ref[...] = kv[:, 4 * LANE:5 * LANE].astype(BF16)
    vwint_ref[...] = values_t(kv[:, 5 * LANE:6 * LANE])
    gat_ref[...] = jax.nn.sigmoid(proj(_C_GA, LANE)).T

    hm = tm // 2
    half = RET_QK_DIM // 2
    low = lax.broadcasted_iota(jnp.int32, (hm, RET_QK_DIM), 1) < half
    row = pl.program_id(0) * tm + lax.broadcasted_iota(jnp.int32, (hm, RET_QK_DIM), 0)
    ang = jnp.where(low, row, row + hm).astype(F32) * inv_ref[...]
    c, s = jnp.cos(ang), jnp.sin(ang)
    c_sw, s_sw = pltpu.roll(c, half, 1), pltpu.roll(s, half, 1)
    cos2 = jnp.concatenate([jnp.where(low, c, c_sw), jnp.where(low, c_sw, c)], axis=0)
    sin2 = jnp.concatenate([jnp.where(low, -s, s_sw), jnp.where(low, -s_sw, s)], axis=0)
    qr = proj(_C_QR, RET_QK_W)
    kr = proj(_C_KR, RET_QK_W)
    for hd in range(RET_HEADS):
        sl = slice(hd * RET_QK_DIM, (hd + 1) * RET_QK_DIM)
        qh = qr[:, sl]
        kh = kr[:, sl]
        qr_ref[:, sl] = (qh * cos2 + pltpu.roll(qh, RET_QK_DIM // 2, 1) * sin2).astype(BF16)
        kr_ref[:, sl] = (kh * cos2 + pltpu.roll(kh, RET_QK_DIM // 2, 1) * sin2) * (RET_QK_DIM ** -0.5)
    vr_ref[...] = proj(_C_VR, RET_V_W).astype(BF16)
    gr_ref[...] = proj(_C_GR, RET_V_W)
    gta_ref[...] = jax.nn.sigmoid(proj(_C_GTA, D_MODEL))
    gtr_ref[...] = jax.nn.sigmoid(proj(_C_GTR, D_MODEL))


def _input_projection(x2d, norm_w, w_cat, inv2):
    S = x2d.shape[0]
    tm = IN_TM
    vt_rows = NSA_KV_GROUPS * PV_ROWS
    outs = [
        (NSA_Q_W, BF16, True),
        (LANE, F32, False), (LANE, F32, False),
        (LANE, BF16, False), (vt_rows, BF16, True),
        (LANE, BF16, False), (vt_rows, BF16, True),
        (LANE, F32, True),
        (RET_QK_W, BF16, False), (RET_QK_W, F32, False),
        (RET_V_W, BF16, False), (RET_V_W, F32, False),
        (D_MODEL, F32, False), (D_MODEL, F32, False),
    ]
    row = lambda w: pl.BlockSpec((tm, w), lambda i: (i, 0))
    col = lambda w: pl.BlockSpec((w, tm), lambda i: (0, i))
    return pl.pallas_call(
        _inproj_kernel,
        grid=(S // tm,),
        in_specs=[row(D_MODEL), _const_spec((1, D_MODEL)), _const_spec((D_MODEL, _C_END)),
                  _const_spec((1, RET_QK_DIM))],
        out_specs=[col(w) if t else row(w) for w, _, t in outs],
        out_shape=[jax.ShapeDtypeStruct((w, S) if t else (S, w), dt) for w, dt, t in outs],
        compiler_params=_params("parallel"),
        name="in_proj",
    )(x2d, norm_w, w_cat, inv2)


def _compress_kernel(x_ref, pos_ref, w1_ref, w2_ref, out_ref, hid_ref):
    n_tok = out_ref.shape[0]
    hid_ref[...] = jnp.zeros_like(hid_ref)

    def body(l, _):
        rows = x_ref[pl.ds(l, n_tok, stride=CMP_STRIDE), :] + pos_ref[pl.ds(l, 1), :]
        hid_ref[...] += _dot(rows.astype(BF16), w1_ref[l])
        return 0

    lax.fori_loop(0, CMP_LEN, body, 0)
    out_ref[...] = _dot(jax.nn.gelu(hid_ref[...]).astype(BF16), w2_ref[...])


def _compress(x, pos, w1, w2):
    S, width = x.shape
    n_tok = S // CMP_STRIDE
    G = NSA_KV_GROUPS
    def block_diag(w):
        w = w.astype(BF16)
        z = jnp.zeros_like(w)
        return jnp.concatenate([jnp.concatenate([w if k == g else z for k in range(G)], axis=-1)
                                for g in range(G)], axis=-2)

    w1_bd = block_diag(w1.reshape(CMP_LEN, NSA_HEAD_DIM, CMP_HIDDEN))
    w2_bd = block_diag(w2)
    x_pad = jnp.pad(x, ((0, CMP_LEN - CMP_STRIDE), (0, 0)))
    full = lambda a: _const_spec(a.shape)
    pos2 = jnp.concatenate([pos] * G, axis=1)
    return pl.pallas_call(
        _compress_kernel,
        grid=(1,),
        in_specs=[full(x_pad), full(pos2), full(w1_bd), full(w2_bd)],
        out_specs=pl.BlockSpec((n_tok, width), lambda i: (0, 0)),
        out_shape=jax.ShapeDtypeStruct((n_tok, width), F32),
        scratch_shapes=[pltpu.VMEM((n_tok, G * CMP_HIDDEN), F32)],
        compiler_params=_params("arbitrary"),
        name="nsa_compress",
    )(x_pad, pos2, w1_bd, w2_bd)


def _t5_bucket_np(dist):
    dist = np.maximum(dist, 0)
    max_exact = REL_BUCKETS // 2
    d_f = np.maximum(dist, 1).astype(np.float32)
    large = max_exact + (np.log(d_f / np.float32(max_exact)) / np.float32(math.log(REL_MAX_DIST / max_exact))
                         * np.float32(REL_BUCKETS - max_exact)).astype(np.int32)
    large = np.minimum(large, REL_BUCKETS - 1)
    return np.where(dist < max_exact, dist, large)


def _bias_table_t(rel_bias, dist, shift):
    bucket = jnp.asarray(np.where(dist >= 0, _t5_bucket_np(dist), -1), jnp.int32)
    rb = rel_bias - rel_bias[REL_BUCKETS - 1][None, :] if shift else rel_bias
    onehot = (bucket[None] == jnp.arange(REL_BUCKETS, dtype=jnp.int32)[:, None, None]).astype(F32)
    tab = jnp.einsum('bh,bkq->hkq', rb, onehot, precision=lax.Precision.HIGHEST)
    tab = jnp.where(jnp.asarray(dist == -1)[None], NEG, tab)
    K, Q = dist.shape
    return tab.reshape(NSA_KV_GROUPS, NSA_HPG, K, Q).transpose(0, 2, 1, 3).reshape(NSA_KV_GROUPS, K, NSA_HPG * Q)


def _nsa_kernel(n_keys, q_ref, ga_ref, kc_ref, vct_ref, tct_ref, ksel_ref, vselt_ref, et_ref, tnt_ref,
                kwin_ref, vwint_ref, twt_ref, out_ref, sc_ref, ps_ref, sa_ref, sb_ref, ma_ref, mb_ref):
    i = pl.program_id(0)
    dh = NSA_HEAD_DIM
    nb = (ps_ref.shape[1] - 2 * SCR_PAD) // (SEL_BLOCK // CMP_STRIDE)

    @pl.when(i == 0)
    def _():
        sc_ref[...] = jnp.zeros_like(sc_ref)
        ps_ref[...] = jnp.zeros_like(ps_ref)

    def col_reduce(op, red, x):
        rows = x.shape[0]
        while rows <= 256 and rows % 16 == 0:
            rows //= 2
            x = op(x[:rows], x[rows:])
        return red(x, axis=0, keepdims=True)

    def colmax(x):
        return col_reduce(jnp.maximum, jnp.max, x)

    def tile4(x):
        return jnp.concatenate([x] * NSA_HPG, axis=1)

    gates = ga_ref[...]
    tl = i // QB_PER_CMP_TILE
    sub = i % QB_PER_CMP_TILE
    tp = tl - 1
    groups = range(NSA_KV_GROUPS)
    q_t_g = [jnp.concatenate([q_ref[NSA_HPG * g + j] for j in range(NSA_HPG)], axis=1) for g in groups]
    zeros = jnp.zeros((dh, QW), BF16)
    wq_g = [jnp.concatenate([q_t_g[g] if k == g else zeros for k in groups], axis=0) for g in groups]

    def cmp_scores(g, t):
        return _dot(kc_ref[pl.ds(pl.multiple_of(t * CMP_TILE, CMP_TILE), CMP_TILE), :], wq_g[g])

    def sc_rows(t):
        return pl.ds(pl.multiple_of(SCR_PAD + t * CMP_TILE, 8), CMP_TILE)

    def cmp_far(t, m):
        out = []
        for g in groups:
            s = cmp_scores(g, t)
            sc_ref[g, sc_rows(t), :] = s
            out.append(jnp.maximum(m[g], colmax(s)))
        return tuple(out)

    def tile_loop(count, step, carry):
        first, left = 0, count
        for per_trip in CMP_TRIP_TILES:
            trips = left // per_trip

            def body(u, carry, first=first, per_trip=per_trip):
                for k in range(per_trip):
                    carry = step(first + per_trip * u + k, carry)
                return carry

            carry = lax.fori_loop(0, trips, body, carry)
            first, left = first + per_trip * trips, left - per_trip * trips
        return carry

    m_c = tile_loop(jnp.maximum(tp, 0), cmp_far, tuple(jnp.full((1, QW), NEG, F32) for _ in groups))
    tpc = jnp.maximum(tp, 0)
    off = CMP_TILE - CMP_TOK_PER_QB - CMP_TOK_PER_QB * sub
    m_c = list(m_c)
    for g in groups:
        s = (cmp_scores(g, tpc) + tct_ref[g, pl.ds(pl.multiple_of(off, 8), CMP_TILE), :]
             + jnp.where(tp < 0, NEG, 0.0))
        sc_ref[g, sc_rows(tpc), :] = s
        m_c[g] = jnp.maximum(m_c[g], colmax(s))
    for g in groups:
        s = cmp_scores(g, tl) + tct_ref[g, pl.ds(pl.multiple_of(off + CMP_TILE, 8), CMP_TILE), :]
        sc_ref[g, sc_rows(tl), :] = s
        m_c[g] = jnp.maximum(m_c[g], colmax(s))

    def cmp_exp(t, carry):
        out = []
        for g in groups:
            l, acc = carry[g]
            e = jnp.exp(sc_ref[g, sc_rows(t), :] - m_c[g])
            sc_ref[g, sc_rows(t), :] = e
            vt = vct_ref[g, :, pl.ds(pl.multiple_of(t * CMP_TILE, CMP_TILE), CMP_TILE)]
            out.append((l + col_reduce(jnp.add, jnp.sum, e), acc + _dot(vt, e.astype(BF16))))
        return tuple(out)

    cmp_out = tile_loop(tl + 1, cmp_exp,
                        tuple((jnp.zeros((1, QW), F32), jnp.zeros((dh, QW), F32)) for _ in groups))
    inv_c = [jnp.where(m_c[g] > 0.5 * NEG, 1.0 / jnp.maximum(cmp_out[g][0], 1e-30), 0.0) for g in groups]
    o_c_g = [cmp_out[g][1] * inv_c[g] for g in groups]

    def cmp_psum(t, _):
        for g in groups:
            p = sc_ref[g, sc_rows(t), :] * inv_c[g]
            ps = p[:, 0:Q_BLOCK]
            for j in range(1, NSA_HPG):
                ps = ps + p[:, j * Q_BLOCK:(j + 1) * Q_BLOCK]
            ps_ref[g, sc_rows(t), :] = ps.astype(BF16).astype(F32)
        return 0

    tile_loop(tl + 1, cmp_psum, 0)
    per_blk = SEL_BLOCK // CMP_STRIDE
    imp_g = []
    for g in groups:
        imp = ps_ref[g, pl.ds(SCR_PAD - 1, nb, stride=per_blk), :]
        for k in range(1, CMP_LEN // CMP_STRIDE + per_blk - 1):
            imp = imp + ps_ref[g, pl.ds(SCR_PAD - 1 + k, nb, stride=per_blk), :]
        imp_g.append(imp)

    kw0 = pl.multiple_of(jnp.maximum(i * Q_BLOCK - WINDOW, 0), Q_BLOCK)
    tw0 = pl.multiple_of(jnp.maximum(WINDOW - i * Q_BLOCK, 0), Q_BLOCK)
    o_w_g = []
    for g in groups:
        s = _dot(kwin_ref[pl.ds(kw0, WIN_KEYS), :], wq_g[g]) + twt_ref[g, pl.ds(tw0, WIN_KEYS), :]
        e = jnp.exp(s - colmax(s)).astype(BF16)
        acc_w = _dot(vwint_ref[g, :, pl.ds(kw0, WIN_KEYS)], e)
        o_w_g.append(acc_w[0:dh] * (1.0 / jnp.maximum(acc_w[dh:dh + 1], 1e-30)))

    n_io = lax.broadcasted_iota(jnp.int32, (nb, Q_BLOCK), 0)
    r_io = lax.broadcasted_iota(jnp.int32, (nb, Q_BLOCK), 1)
    back = (i * Q_BLOCK + r_io) // SEL_BLOCK - n_io
    valid = back >= 0
    forced = (n_io == 0) | (valid & (back < N_LOCAL_SEL))
    work_g = [jnp.where(forced, -jnp.inf, jnp.where(valid, imp_g[g], -1e9)) for g in groups]
    n_f = n_io.astype(F32)
    def select_rounds(rows):
        def run(*work):
            work = [w[:rows] for w in work]
            for _ in range(SEL_TOP_N - 1 - N_LOCAL_SEL):
                for g in groups:
                    mx = colmax(work[g])
                    first = col_reduce(jnp.minimum, jnp.min, jnp.where(work[g] == mx, n_f[:rows], float(nb)))
                    work[g] = jnp.where(n_f[:rows] == first, -jnp.inf, work[g])
            rest = [jnp.zeros((nb - rows, Q_BLOCK), F32)] if rows < nb else []
            return tuple(jnp.concatenate([jnp.where(w == -jnp.inf, 1.0, 0.0)] + rest, axis=0) for w in work)
        return run

    levels = [SEL_ROWS_STEP * (k + 1) for k in range(nb // SEL_ROWS_STEP)]
    level = jnp.minimum((i * Q_BLOCK + Q_BLOCK - 1) // (SEL_BLOCK * SEL_ROWS_STEP), len(levels) - 1)
    sel_g = list(lax.switch(level, [select_rounds(rows) for rows in levels], *work_g))

    far_blocks = (i - 1) * (Q_BLOCK // SEL_BLOCK)
    selm_g = [(jnp.where(n_io < far_blocks, sel_g[g], 0.0) - 1.0).astype(BF16) for g in groups]

    far_end = (i - 1) * Q_BLOCK
    n_far = (jnp.maximum(far_end, 0) + SEL_TK - 1) // SEL_TK

    k0n = pl.multiple_of(jnp.maximum(far_end, 0), Q_BLOCK)
    t0n = pl.multiple_of(jnp.maximum(-far_end, 0), Q_BLOCK)
    near_blocks = 2 * Q_BLOCK // SEL_BLOCK
    pb = lax.broadcasted_iota(jnp.int32, (16, nb), 0)
    pn = lax.broadcasted_iota(jnp.int32, (16, nb), 1)
    pick_rows = jnp.where((pn == jnp.maximum(far_blocks, 0) + pb) & (pb < near_blocks), 1.0, 0.0).astype(BF16)
    s_near_g = []
    for g in groups:
        sel_rows = _dot(pick_rows, sel_g[g].astype(BF16))
        add_near = jnp.concatenate(
            [jnp.broadcast_to((sel_rows[b:b + 1, :] - 1.0) * (-NEG), (SEL_BLOCK, Q_BLOCK)) for b in range(near_blocks)],
            axis=0)
        s_near_g.append(_dot(ksel_ref[pl.ds(k0n, 2 * Q_BLOCK), :], wq_g[g])
                        + tnt_ref[g, pl.ds(t0n, 2 * Q_BLOCK), :] + tile4(add_near))


    def flash_step(carry, s, vt):
        m_i, acc = carry
        m_new = jnp.maximum(m_i, colmax(s))
        alpha = jnp.exp(m_i - m_new)
        p = jnp.exp(s - m_new).astype(BF16)
        return m_new, alpha * acc + _dot(vt, p)

    carry = tuple((jnp.full((1, QW), NEG, F32), jnp.zeros((PV_ROWS, QW), F32)) for _ in groups)
    for w in range(nb // SEL_WIN):
        w_far = [jnp.concatenate([tile4(selm_g[g][w * SEL_WIN:(w + 1) * SEL_WIN]), wq_g[g]], axis=0)
                 for g in groups]

        lo = w * SEL_WIN_TILES
        last = min((w + 1) * SEL_WIN_TILES, n_keys // SEL_TK) - 1
        trips = jnp.maximum(jnp.minimum(n_far, last + 1) - lo + 1, 0) // 2

        def far_scores(kt, s_ref, mx_ref, g, w=w, w_far=w_far):
            k0 = pl.multiple_of(kt * SEL_TK, SEL_TK)
            e0 = pl.multiple_of((kt - w * SEL_WIN_TILES) * SEL_TK, SEL_TK)
            lhs = jnp.concatenate([et_ref[pl.ds(e0, SEL_TK), :],
                                   ksel_ref[pl.ds(k0, SEL_TK), :]], axis=1)
            s = _dot(lhs, w_far[g])
            s_ref[g] = s
            mx_ref[g] = colmax(s)

        def far_update(kt, s_ref, mx_ref, g, carry_g):
            k0 = pl.multiple_of(kt * SEL_TK, SEL_TK)
            m_i, acc = carry_g
            m_new = jnp.maximum(m_i, mx_ref[g])
            p = jnp.exp(s_ref[g] - m_new).astype(BF16)
            return m_new, jnp.exp(m_i - m_new) * acc + _dot(vselt_ref[g, :, pl.ds(k0, SEL_TK)], p)

        def far_trip(tiles, first, last=last, far_scores=far_scores):
            def body(u, carry):
                kt = first + tiles * u
                carry = list(carry)
                bufs = [(sa_ref, ma_ref), (sb_ref, mb_ref)]
                for t in range(tiles):
                    for g in groups:
                        far_scores(jnp.minimum(kt + t + 1, last), *bufs[(t + 1) % 2], g)
                        carry[g] = far_update(kt + t, *bufs[t % 2], g, carry[g])
                return tuple(carry)
            return body

        @pl.when(trips > 0)
        def _(lo=lo, far_scores=far_scores):
            for g in groups:
                far_scores(jnp.int32(lo), sa_ref, ma_ref, g)

        first, left = lo, trips
        for tiles in FAR_TRIP_TILES:
            count = left // (tiles // 2)
            carry = lax.fori_loop(0, count, far_trip(tiles, first), carry)
            first, left = first + tiles * count, left - count * (tiles // 2)

    heads_out = []
    for g in groups:
        o_c = o_c_g[g]
        m_s, acc_s = flash_step(carry[g], s_near_g[g], vselt_ref[g, :, pl.ds(k0n, 2 * Q_BLOCK)])
        o_s = acc_s[0:dh] * (1.0 / jnp.maximum(acc_s[dh:dh + 1], 1e-30))

        o_w = o_w_g[g]

        for j in range(NSA_HPG):
            hd = NSA_HPG * g + j
            cols = slice(j * Q_BLOCK, (j + 1) * Q_BLOCK)
            heads_out.append(gates[hd:hd + 1] * o_c[:, cols]
                             + gates[NSA_HEADS + hd:NSA_HEADS + hd + 1] * o_s[:, cols]
                             + gates[2 * NSA_HEADS + hd:2 * NSA_HEADS + hd + 1] * o_w[:, cols])

    out_ref[...] = jnp.concatenate(heads_out, axis=0).T.astype(out_ref.dtype)


def _nsa_attention(q_t, gates_t, kc, vct, tct, ksel, vselt, et, tnt, kwin, vwint, twt):
    H, dh, S = q_t.shape
    n_qb = S // Q_BLOCK
    n_tok = S // CMP_STRIDE
    nb = -(-(S // SEL_BLOCK) // SEL_WIN) * SEL_WIN
    ps_rows = nb * (SEL_BLOCK // CMP_STRIDE) + 2 * SCR_PAD
    full = lambda a: _const_spec(a.shape)
    return pl.pallas_call(
        functools.partial(_nsa_kernel, S),
        grid=(n_qb,),
        in_specs=[pl.BlockSpec((H, dh, Q_BLOCK), lambda i: (0, 0, i)),
                  pl.BlockSpec((gates_t.shape[0], Q_BLOCK), lambda i: (0, i)),
                  full(kc), full(vct), full(tct), full(ksel), full(vselt), full(et), full(tnt),
                  full(kwin), full(vwint), full(twt)],
        out_specs=pl.BlockSpec((Q_BLOCK, H * dh), lambda i: (i, 0)),
        out_shape=jax.ShapeDtypeStruct((S, H * dh), BF16),
        scratch_shapes=[pltpu.VMEM((NSA_KV_GROUPS, SCR_PAD + n_tok, QW), F32),
                        pltpu.VMEM((NSA_KV_GROUPS, ps_rows, Q_BLOCK), F32),
                        pltpu.VMEM((NSA_KV_GROUPS, SEL_TK, QW), F32),
                        pltpu.VMEM((NSA_KV_GROUPS, SEL_TK, QW), F32),
                        pltpu.VMEM((NSA_KV_GROUPS, 1, QW), F32),
                        pltpu.VMEM((NSA_KV_GROUPS, 1, QW), F32)],
        compiler_params=_params("arbitrary"),
        name="nsa_attention",
    )(q_t, gates_t, kc, vct, tct, ksel, vselt, et, tnt, kwin, vwint, twt)


def _retention_kernel(q_ref, k_ref, v_ref, g_ref, nw_ref, dmask_ref, qd_ref, kd_ref, cd_ref, out_ref, state_ref):
    @pl.when(pl.program_id(0) == 0)
    def _():
        state_ref[...] = jnp.zeros_like(state_ref)

    states = [state_ref[hd] for hd in range(RET_HEADS)]
    for c in range(q_ref.shape[0] // RET_CHUNK):
        rows = slice(c * RET_CHUNK, (c + 1) * RET_CHUNK)
        for hd in range(RET_HEADS):
            ks = slice(hd * RET_QK_DIM, (hd + 1) * RET_QK_DIM)
            vs = slice(hd * RET_V_DIM, (hd + 1) * RET_V_DIM)
            q = q_ref[rows, ks]
            k = k_ref[rows, ks]
            v = v_ref[rows, vs]
            state = states[hd]
            inner = _dot_nt(q, k.astype(BF16)) * dmask_ref[hd]
            y = _dot(inner.astype(BF16), v) + _dot(q, state.astype(BF16)) * qd_ref[hd]
            kd = k * kd_ref[hd]
            states[hd] = state * cd_ref[hd] + _dot(kd.T.astype(BF16), v)
            mu = jnp.mean(y, axis=-1, keepdims=True)
            yc = y - mu
            var = jnp.mean(yc * yc, axis=-1, keepdims=True)
            yn = yc * lax.rsqrt(var + EPS) * nw_ref[:, vs]
            out_ref[rows, vs] = (jax.nn.silu(g_ref[rows, vs]) * yn).astype(out_ref.dtype)
    for hd in range(RET_HEADS):
        state_ref[hd] = states[hd]


RET_STEP_CHUNKS = 8


def _retention(q, k, v, g, norm_w, dmask, qd, kd, cd):
    S = q.shape[0]
    C = RET_CHUNK * RET_STEP_CHUNKS
    row = lambda w: pl.BlockSpec((C, w), lambda i: (i, 0))
    full = lambda a: _const_spec(a.shape)
    return pl.pallas_call(
        _retention_kernel,
        grid=(S // C,),
        in_specs=[row(RET_QK_W), row(RET_QK_W), row(RET_V_W), row(RET_V_W),
                  full(norm_w), full(dmask), full(qd), full(kd), full(cd)],
        out_specs=row(RET_V_W),
        out_shape=jax.ShapeDtypeStruct((S, RET_V_W), BF16),
        scratch_shapes=[pltpu.VMEM((RET_HEADS, RET_QK_DIM, RET_V_DIM), F32)],
        compiler_params=_params("arbitrary"),
        name="retention",
    )(q, k, v, g, norm_w, dmask, qd, kd, cd)


MERGE_TM = 512
MERGE_SUB_ROWS = 256
GRP_LANE = 40
RANK_LANE = 41


def _merge_kernel(x_ref, ya_ref, yr_ref, gta_ref, gtr_ref, wa_ref, wr_ref, wo_ref, nw_ref, wrt_ref, brt_ref,
                  tri_ref, x1_ref, h_ref, comb_ref, cnt_ref, carry_ref):
    @pl.when(pl.program_id(0) == 0)
    def _():
        carry_ref[...] = jnp.zeros_like(carry_ref)

    sub = tri_ref.shape[0]
    carry = carry_ref[...]
    for r in range(x_ref.shape[0] // sub):
        carry = _merge_rows(slice(r * sub, (r + 1) * sub), carry, x_ref, ya_ref, yr_ref, gta_ref, gtr_ref,
                            wa_ref, wr_ref, wo_ref, nw_ref, wrt_ref, brt_ref, tri_ref, x1_ref, h_ref, comb_ref)
    carry_ref[...] = carry
    cnt_ref[...] = jnp.broadcast_to(carry, cnt_ref.shape)


def _merge_rows(rows, carry, x_ref, ya_ref, yr_ref, gta_ref, gtr_ref, wa_ref, wr_ref, wo_ref, nw_ref, wrt_ref,
                brt_ref, tri_ref, x1_ref, h_ref, comb_ref):
    merged = gta_ref[rows] * _dot(ya_ref[rows], wa_ref[...]) + gtr_ref[rows] * _dot(yr_ref[rows], wr_ref[...])
    x1 = x_ref[rows] + _dot(merged.astype(BF16), wo_ref[...])
    x1_ref[rows] = x1
    h = (x1 * lax.rsqrt(jnp.mean(x1 * x1, axis=-1, keepdims=True) + EPS) * nw_ref[...]).astype(BF16)
    h_ref[rows] = _pack_bf16_pairs(h.astype(F32))
    logits = _dot(h, wrt_ref[...]) + brt_ref[...]
    lane = lax.broadcasted_iota(jnp.int32, logits.shape, 1)
    lane_f = lane.astype(F32)
    big = float(LANE)
    is_grp = (lane >= N_EXPERTS) & (lane < N_EXPERTS + N_GROUPS)
    lg = jnp.where(is_grp, logits, -jnp.inf)
    eg = jnp.where(is_grp, jnp.exp(lg - jnp.max(lg, axis=-1, keepdims=True)), 0.0)
    pg = eg / jnp.sum(eg, axis=-1, keepdims=True)
    p_grp = jnp.max(pg, axis=-1, keepdims=True)
    grp = jnp.min(jnp.where(is_grp & (pg == p_grp), lane_f, big), axis=-1, keepdims=True) - N_EXPERTS
    in_grp = (lane_f >= grp * EXPERTS_PER_GROUP) & (lane_f < (grp + 1.0) * EXPERTS_PER_GROUP)
    le = jnp.where(in_grp, logits, -jnp.inf)
    m1 = jnp.max(le, axis=-1, keepdims=True)
    i1 = jnp.min(jnp.where(le == m1, lane_f, big), axis=-1, keepdims=True)
    le2 = jnp.where(lane_f == i1, -jnp.inf, le)
    m2 = jnp.max(le2, axis=-1, keepdims=True)
    i2 = jnp.min(jnp.where(le2 == m2, lane_f, big), axis=-1, keepdims=True)
    e2 = jnp.exp(m2 - m1)
    den = 1.0 + e2
    onehot = jnp.where(lane_f == grp, 1.0, 0.0)
    before = _dot(tri_ref[...], onehot.astype(BF16)) + carry
    rank = jnp.sum(onehot * before, axis=-1, keepdims=True)
    comb_ref[rows] = (jnp.where(lane_f == i1, (1.0 / den) * p_grp, 0.0)
                      + jnp.where(lane_f == i2, (e2 / den) * p_grp, 0.0)
                      + jnp.where(lane == GRP_LANE, grp, 0.0)
                      + jnp.where(lane == RANK_LANE, rank, 0.0))
    return carry + jnp.sum(onehot, axis=0, keepdims=True)


def _pack_bf16_pairs(x):
    n = x.shape[1] // 2
    return pltpu.pack_elementwise([x[:, :n], x[:, n:]], packed_dtype=BF16)


def _unpack_bf16_pairs(w):
    return tuple(pltpu.unpack_elementwise(w, index=k, packed_dtype=BF16, unpacked_dtype=F32) for k in range(2))


def _merge(x2d, ya, yr, gta, gtr, wa, wr, wo, nw, wrt, brt):
    S = x2d.shape[0]
    tm = MERGE_TM
    row = lambda w: pl.BlockSpec((tm, w), lambda i: (i, 0))
    full = lambda a: _const_spec(a.shape)
    sub = MERGE_SUB_ROWS
    tri = jnp.asarray(np.tril(np.ones((sub, sub), np.float32), -1), BF16)
    return pl.pallas_call(
        _merge_kernel,
        grid=(S // tm,),
        in_specs=[row(D_MODEL), row(NSA_Q_W), row(RET_V_W), row(D_MODEL), row(D_MODEL),
                  full(wa), full(wr), full(wo), full(nw), full(wrt), full(brt), full(tri)],
        out_specs=[row(D_MODEL), row(D_MODEL // 2), row(LANE), pl.BlockSpec((8, LANE), lambda i: (0, 0))],
        out_shape=[jax.ShapeDtypeStruct((S, D_MODEL), F32), jax.ShapeDtypeStruct((S, D_MODEL // 2), jnp.uint32),
                   jax.ShapeDtypeStruct((S, LANE), F32), jax.ShapeDtypeStruct((8, LANE), F32)],
        scratch_shapes=[pltpu.VMEM((1, LANE), F32)],
        compiler_params=_params("arbitrary"),
        name="merge_router",
    )(x2d, ya, yr, gta, gtr, wa, wr, wo, nw, wrt, brt, tri)


MOE_TM = 1024
MOE_SUB_TILES = 2
PERMUTE_TM = 512
HALF = D_MODEL // 2


def _permute_kernel(pos_ref, h_ref, comb_ref, hs_ref, cs_ref):
    i = pl.program_id(0)

    @pl.when(i == 0)
    def _():
        hs_ref[...] = jnp.zeros_like(hs_ref)
        cs_ref[...] = jnp.zeros_like(cs_ref)

    tm = h_ref.shape[0]

    def body(r, _):
        d = pos_ref[i * tm + r]
        hs_ref[pl.ds(d, 1), :] = h_ref[pl.ds(r, 1), :]
        cs_ref[pl.ds(d, 1), :] = comb_ref[pl.ds(r, 1), :]
        return 0

    lax.fori_loop(0, tm, body, 0, unroll=8)


def _permute(pos, hu, comb, n_sorted):
    S = hu.shape[0]
    tm = PERMUTE_TM
    resident = lambda w: pl.BlockSpec((n_sorted, w), lambda i, p: (0, 0), pipeline_mode=pl.Buffered(1))
    return pl.pallas_call(
        _permute_kernel,
        grid_spec=pltpu.PrefetchScalarGridSpec(
            num_scalar_prefetch=1, grid=(S // tm,),
            in_specs=[pl.BlockSpec((tm, HALF), lambda i, p: (i, 0)), pl.BlockSpec((tm, LANE), lambda i, p: (i, 0))],
            out_specs=[resident(HALF), resident(LANE)]),
        out_shape=[jax.ShapeDtypeStruct((n_sorted, HALF), jnp.uint32), jax.ShapeDtypeStruct((n_sorted, LANE), F32)],
        compiler_params=_params("arbitrary"),
        name="moe_permute",
    )(pos, hu, comb)


def _moe_kernel(tg_ref, tv_ref, hs_ref, cs_ref, wg_ref, wu_ref, wd_ref, ys_ref, lo_ref, hi_ref, acc_ref):
    i = pl.program_id(0)
    e = pl.program_id(1)

    @pl.when(e == 0)
    def _():
        lo, hi = _unpack_bf16_pairs(hs_ref[...])
        lo_ref[...] = lo.astype(BF16)
        hi_ref[...] = hi.astype(BF16)
        acc_ref[...] = jnp.zeros_like(acc_ref)

    @pl.when(tv_ref[i] > 0)
    def _():
        wg = wg_ref[0].astype(BF16)
        wu = wu_ref[0].astype(BF16)
        wd = wd_ref[0].astype(BF16)
        sub = lo_ref.shape[0] // MOE_SUB_TILES
        for r in range(MOE_SUB_TILES):
            rows = slice(r * sub, (r + 1) * sub)
            lo, hi = lo_ref[rows], hi_ref[rows]
            gate = _dot(lo, wg[:HALF]) + _dot(hi, wg[HALF:])
            up = _dot(lo, wu[:HALF]) + _dot(hi, wu[HALF:])
            comb = cs_ref[rows]
            lane = lax.broadcasted_iota(jnp.int32, comb.shape, 1)
            c = jnp.sum(jnp.where(lane == tg_ref[i] * EXPERTS_PER_GROUP + e, comb, 0.0), axis=-1, keepdims=True)
            acc_ref[rows] += _dot((jax.nn.silu(gate) * up).astype(BF16), wd) * c

    @pl.when(e == pl.num_programs(1) - 1)
    def _():
        ys_ref[...] = _pack_bf16_pairs(acc_ref[...].astype(BF16).astype(F32))


def _moe(tile_group, tile_valid, hs, cs, wg, wu, wd):
    n_sorted = hs.shape[0]
    tm = MOE_TM
    row = lambda w: pl.BlockSpec((tm, w), lambda i, e, tg, tv: (i, 0))
    wspec = lambda a, b: pl.BlockSpec((1, a, b), lambda i, e, tg, tv: (tg[i] * EXPERTS_PER_GROUP + e, 0, 0))
    return pl.pallas_call(
        _moe_kernel,
        grid_spec=pltpu.PrefetchScalarGridSpec(
            num_scalar_prefetch=2, grid=(n_sorted // tm, EXPERTS_PER_GROUP),
            in_specs=[row(HALF), row(LANE), wspec(D_MODEL, EXPERT_FF), wspec(D_MODEL, EXPERT_FF),
                      wspec(EXPERT_FF, D_MODEL)],
            out_specs=row(HALF),
            scratch_shapes=[pltpu.VMEM((tm, HALF), BF16), pltpu.VMEM((tm, HALF), BF16),
                            pltpu.VMEM((tm, D_MODEL), F32)]),
        out_shape=jax.ShapeDtypeStruct((n_sorted, HALF), jnp.uint32),
        compiler_params=_params("parallel", "arbitrary"),
        name="moe_experts",
    )(tile_group, tile_valid, hs, cs, wg, wu, wd)


def _unpermute_kernel(pos_ref, ys_ref, x1_ref, nw_ref, out_ref, stage_ref):
    i = pl.program_id(0)
    tm = x1_ref.shape[0]

    def body(r, _):
        stage_ref[pl.ds(r, 1), :] = ys_ref[pl.ds(pos_ref[i * tm + r], 1), :]
        return 0

    lax.fori_loop(0, tm, body, 0, unroll=8)
    lo, hi = _unpack_bf16_pairs(stage_ref[...])
    x2 = x1_ref[...] + jnp.concatenate([lo, hi], axis=1)
    out_ref[...] = x2 * lax.rsqrt(jnp.mean(x2 * x2, axis=-1, keepdims=True) + EPS) * nw_ref[...]


def _unpermute_residual_norm(pos, ys, x1, nw):
    S = x1.shape[0]
    tm = PERMUTE_TM
    return pl.pallas_call(
        _unpermute_kernel,
        grid_spec=pltpu.PrefetchScalarGridSpec(
            num_scalar_prefetch=1, grid=(S // tm,),
            in_specs=[pl.BlockSpec(ys.shape, lambda i, p: (0, 0), pipeline_mode=pl.Buffered(1)),
                      pl.BlockSpec((tm, D_MODEL), lambda i, p: (i, 0)),
                      pl.BlockSpec((1, D_MODEL), lambda i, p: (0, 0))],
            out_specs=pl.BlockSpec((tm, D_MODEL), lambda i, p: (i, 0)),
            scratch_shapes=[pltpu.VMEM((tm, HALF), jnp.uint32)]),
        out_shape=jax.ShapeDtypeStruct((S, D_MODEL), F32),
        compiler_params=_params("parallel"),
        name="moe_unpermute_norm",
    )(pos, ys, x1, nw)


def _hierarchical_moe(hu, comb, cnt, x1, wg, wu, wd, final_norm_w):
    S = hu.shape[0]
    tm = MOE_TM
    n_sorted = S + N_GROUPS * tm
    assert RANK_LANE == GRP_LANE + 1
    grp_rank = comb[:, GRP_LANE:RANK_LANE + 1].astype(jnp.int32)
    grp, rank = grp_rank[:, 0], grp_rank[:, 1]
    counts = cnt[0, :N_GROUPS].astype(jnp.int32)
    padded = (counts + tm - 1) // tm * tm
    ends = jnp.cumsum(padded)
    starts = ends - padded
    pos = rank + jnp.sum(jnp.where(grp[:, None] == jnp.arange(N_GROUPS)[None, :], starts[None, :], 0), axis=1)
    tile_start = jnp.arange(n_sorted // tm, dtype=jnp.int32) * tm
    tile_group = jnp.minimum(jnp.sum((tile_start[:, None] >= ends[None, :]).astype(jnp.int32), axis=1), N_GROUPS - 1)
    tile_valid = (tile_start < ends[N_GROUPS - 1]).astype(jnp.int32)
    hs, cs = _permute(pos, hu, comb, n_sorted)
    ys = _moe(tile_group, tile_valid, hs, cs, wg, wu, wd)
    return _unpermute_residual_norm(pos, ys, x1, final_norm_w)


def _positional_tables():
    half = RET_QK_DIM // 2
    inv = ROPE_BASE ** (-jnp.arange(half, dtype=F32) / half)
    inv2 = jnp.concatenate([inv, inv]).reshape(1, RET_QK_DIM)
    C, H = RET_CHUNK, RET_HEADS
    log_gamma = jnp.log(1.0 - 2.0 ** (-5.0 - jnp.arange(H, dtype=F32)))
    idx = jnp.arange(C, dtype=F32)
    diff = idx[:, None] - idx[None, :]
    dmask = jnp.where(diff >= 0, jnp.exp(jnp.maximum(diff, 0.0)[None] * log_gamma[:, None, None]), 0.0)
    qd = jnp.exp((idx + 1.0)[None, :] * log_gamma[:, None])
    kd = jnp.exp((C - 1.0 - idx)[None, :] * log_gamma[:, None])
    cd = jnp.exp(C * log_gamma)
    qd = jnp.broadcast_to(qd[:, :, None], (H, C, RET_V_DIM))
    kd = jnp.broadcast_to(kd[:, :, None], (H, C, RET_QK_DIM))
    cd = jnp.broadcast_to(cd[:, None, None], (H, 1, RET_V_DIM))
    return inv2, dmask, qd, kd, cd


def _attention_tables(rel_bias):
    r = np.arange(Q_BLOCK)[None, :]
    cc = np.arange(3 * CMP_TILE)[:, None] - CMP_TILE
    d = r - CMP_STRIDE * (cc - (CMP_TILE - CMP_TOK_PER_QB)) - (CMP_LEN - 1)
    d = np.where(cc < 0, -2, np.where((cc >= CMP_TILE) | (d < 0), -1, d))
    tct = _bias_table_t(rel_bias, d, shift=True)
    d = r + Q_BLOCK - np.arange(3 * Q_BLOCK)[:, None]
    tnt = _bias_table_t(rel_bias, np.where(d < 0, -1, d), shift=True)
    d = r + WINDOW - np.arange(WIN_KEYS + WINDOW)[:, None]
    twt = _bias_table_t(rel_bias, np.where((d < 0) | (d >= WINDOW), -1, d), shift=False)
    kk = np.arange(SEL_WIN * SEL_BLOCK)[:, None] // SEL_BLOCK
    et = jnp.where(jnp.asarray(kk == np.arange(SEL_WIN)[None, :]), -NEG, 0.0).astype(BF16)
    return tct, tnt, twt, et


def _layer(x2d, rel_bias, norm_mix, w_in, cmp_pos_k, cmp_k_w1, cmp_k_w2, cmp_pos_v, cmp_v_w1, cmp_v_w2,
           ret_norm, w_nsa_up, w_ret_up, w_out, norm_ffn, w_rg, b_rg, w_re, b_re, moe_wg, moe_wu, moe_wd,
           final_norm_w):
    S = x2d.shape[0]
    G, dh = NSA_KV_GROUPS, NSA_HEAD_DIM
    assert S % (CMP_TILE * CMP_STRIDE) == 0 and S // SEL_BLOCK >= SEL_TOP_N

    o = np.cumsum((0,) + (NSA_Q_W, 6 * NSA_KV_W, 3 * NSA_HEADS, RET_QK_W, RET_QK_W, RET_V_W, RET_V_W,
                          D_MODEL, D_MODEL))
    seg = lambda n: w_in[:, o[n]:o[n + 1]]
    w_cat = jnp.concatenate([seg(0), seg(1), seg(3), seg(4), seg(5), seg(6), seg(7), seg(8), seg(2),
                             jnp.zeros((D_MODEL, LANE - 3 * NSA_HEADS), F32)], axis=1).astype(BF16)
    inv2, dmask, qd, kd, cd = _positional_tables()

    (qa_t, kcmp, vcmp, ksel, vsel_t, kwin, vwin_t, ga_t, qr, kr, vr, gr, gta, gtr) = _input_projection(
        x2d, norm_mix.reshape(1, D_MODEL), w_cat, inv2)

    nch = S // CMP_STRIDE
    kc = _compress(kcmp, cmp_pos_k, cmp_k_w1, cmp_k_w2).astype(BF16)
    vct = _compress(vcmp, cmp_pos_v, cmp_v_w1, cmp_v_w2).T.reshape(G, dh, nch).astype(BF16)

    tct, tnt, twt, et = _attention_tables(rel_bias)
    ya = _nsa_attention(qa_t.reshape(NSA_HEADS, dh, S), ga_t, kc, vct, tct,
                        ksel, vsel_t.reshape(G, PV_ROWS, S), et, tnt,
                        kwin, vwin_t.reshape(G, PV_ROWS, S), twt)

    yr = _retention(qr, kr, vr, gr, ret_norm.reshape(1, RET_V_W), dmask, qd, kd, cd)

    wrt = jnp.concatenate([w_re, w_rg, jnp.zeros((D_MODEL, LANE - N_EXPERTS - N_GROUPS), F32)], axis=1).astype(BF16)
    brt = jnp.concatenate([b_re, b_rg, jnp.zeros((LANE - N_EXPERTS - N_GROUPS,), F32)]).reshape(1, LANE)
    x1, h2, comb, cnt = _merge(x2d, ya, yr, gta, gtr, w_nsa_up.astype(BF16), w_ret_up.astype(BF16),
                               w_out.astype(BF16), norm_ffn.reshape(1, D_MODEL), wrt, brt)
    return _hierarchical_moe(h2, comb, cnt, x1, moe_wg, moe_wu, moe_wd, final_norm_w)


def kernel(x, rel_bias, norm_mix, w_in, cmp_pos_k, cmp_k_w1, cmp_k_w2, cmp_pos_v, cmp_v_w1, cmp_v_w2, ret_norm, w_nsa_up, w_ret_up, w_out, norm_ffn, w_router_group, b_router_group, w_router_expert, b_router_expert, moe_w_gate, moe_w_up, moe_w_down, norm_final):
    B, S, D = x.shape
    assert B == 1 and norm_mix.shape[0] == 1, "single sequence, depth 1"
    out = _layer(x[0], rel_bias, norm_mix[0], w_in[0], cmp_pos_k[0], cmp_k_w1[0], cmp_k_w2[0],
                 cmp_pos_v[0], cmp_v_w1[0], cmp_v_w2[0], ret_norm[0], w_nsa_up[0], w_ret_up[0], w_out[0],
                 norm_ffn[0], w_router_group[0], b_router_group[0], w_router_expert[0], b_router_expert[0],
                 moe_w_gate[0], moe_w_up[0], moe_w_down[0], norm_final.reshape(1, D))
    return out[None]
```

```python
import functools
import math

import numpy as np
import jax
import jax.numpy as jnp
from jax import lax
from jax.experimental import pallas as pl
from jax.experimental.pallas import tpu as pltpu

F32 = jnp.float32
BF16 = jnp.bfloat16

D_MODEL = 1024
NSA_HEADS = 8
NSA_KV_GROUPS = 2
NSA_HPG = NSA_HEADS // NSA_KV_GROUPS
NSA_HEAD_DIM = 64
CMP_LEN = 32
CMP_STRIDE = 16
CMP_HIDDEN = 256
SEL_BLOCK = 64
SEL_TOP_N = 16
N_LOCAL_SEL = 2
WINDOW = 512
Q_BLOCK = 128
RET_HEADS = 4
RET_QK_DIM = 128
RET_V_DIM = 256
RET_CHUNK = 128
ROPE_BASE = 10000.0
REL_BUCKETS = 32
REL_MAX_DIST = 128
N_GROUPS = 4
EXPERTS_PER_GROUP = 8
N_EXPERTS = N_GROUPS * EXPERTS_PER_GROUP
EXPERT_FF = 256
EPS = 1e-6

NSA_Q_W = NSA_HEADS * NSA_HEAD_DIM
NSA_KV_W = NSA_KV_GROUPS * NSA_HEAD_DIM
RET_QK_W = RET_HEADS * RET_QK_DIM
RET_V_W = RET_HEADS * RET_V_DIM

LANE = 128
NEG = -1e30
VMEM_LIMIT = 56 * 1024 * 1024

CMP_TILE = 128
QB_PER_CMP_TILE = CMP_TILE * CMP_STRIDE // Q_BLOCK
CMP_TOK_PER_QB = Q_BLOCK // CMP_STRIDE
SCR_PAD = 8
SEL_TK = 512
SEL_WIN = 128
SEL_WIN_TILES = SEL_WIN * SEL_BLOCK // SEL_TK
CMP_TRIP_TILES = (4, 2, 1)
FAR_TRIP_TILES = (8, 4, 2)
SEL_ROWS_STEP = 64
PV_ROWS = 80
WIN_KEYS = WINDOW + Q_BLOCK
QW = NSA_HPG * Q_BLOCK


def _dot(a, b):
    return jnp.dot(a, b, preferred_element_type=F32)


def _dot_nt(a, b):
    return lax.dot_general(a, b, (((1,), (1,)), ((), ())), preferred_element_type=F32)


def _const_spec(shape):
    nd = len(shape)
    return pl.BlockSpec(shape, lambda *_: (0,) * nd, pipeline_mode=pl.Buffered(1))


def _params(*sem):
    return pltpu.CompilerParams(dimension_semantics=sem, vmem_limit_bytes=VMEM_LIMIT)


_C_QA = 0
_C_KV = _C_QA + NSA_Q_W
_C_QR = _C_KV + 6 * NSA_KV_W
_C_KR = _C_QR + RET_QK_W
_C_VR = _C_KR + RET_QK_W
_C_GR = _C_VR + RET_V_W
_C_GTA = _C_GR + RET_V_W
_C_GTR = _C_GTA + D_MODEL
_C_GA = _C_GTR + D_MODEL
_C_END = _C_GA + LANE
IN_TM = 512


def _inproj_kernel(x_ref, nw_ref, w_ref, inv_ref,
                   qat_ref, kcmp_ref, vcmp_ref, ksel_ref, vselt_ref, kwin_ref, vwint_ref, gat_ref,
                   qr_ref, kr_ref, vr_ref, gr_ref, gta_ref, gtr_ref):
    tm = x_ref.shape[0]
    x = x_ref[...]
    h = x * lax.rsqrt(jnp.mean(x * x, axis=-1, keepdims=True) + EPS) * nw_ref[...]
    hb = h.astype(BF16)

    def proj(a, width):
        return _dot(hb, w_ref[:, a:a + width])

    def values_t(v):
        vt = v.T.astype(BF16)
        aux = jnp.where(lax.broadcasted_iota(jnp.int32, (PV_ROWS - NSA_HEAD_DIM, tm), 0) == 0, 1.0, 0.0).astype(BF16)
        parts = []
        for g in range(NSA_KV_GROUPS):
            parts += [vt[g * NSA_HEAD_DIM:(g + 1) * NSA_HEAD_DIM], aux]
        return jnp.concatenate(parts, axis=0)

    qat_ref[...] = (proj(_C_QA, NSA_Q_W) * (NSA_HEAD_DIM ** -0.5)).T.astype(BF16)
    kv = proj(_C_KV, 6 * NSA_KV_W)
    kcmp_ref[...] = kv[:, 0 * LANE:1 * LANE]
    vcmp_ref[...] = kv[:, 1 * LANE:2 * LANE]
    ksel_ref[...] = kv[:, 2 * LANE:3 * LANE].astype(BF16)
    vselt_ref[...] = values_t(kv[:, 3 * LANE:4 * LANE])
    kwin_ref[...] = kv[:, 4 * LANE:5 * LANE].astype(BF16)
    vwint_ref[...] = values_t(kv[:, 5 * LANE:6 * LANE])
    gat_ref[...] = jax.nn.sigmoid(proj(_C_GA, LANE)).T

    hm = tm // 2
    half = RET_QK_DIM // 2
    low = lax.broadcasted_iota(jnp.int32, (hm, RET_QK_DIM), 1) < half
    row = pl.program_id(0) * tm + lax.broadcasted_iota(jnp.int32, (hm, RET_QK_DIM), 0)
    ang = jnp.where(low, row, row + hm).astype(F32) * inv_ref[...]
    c, s = jnp.cos(ang), jnp.sin(ang)
    c_sw, s_sw = pltpu.roll(c, half, 1), pltpu.roll(s, half, 1)
    cos2 = jnp.concatenate([jnp.where(low, c, c_sw), jnp.where(low, c_sw, c)], axis=0)
    sin2 = jnp.concatenate([jnp.where(low, -s, s_sw), jnp.where(low, -s_sw, s)], axis=0)
    qr = proj(_C_QR, RET_QK_W)
    kr = proj(_C_KR, RET_QK_W)
    for hd in range(RET_HEADS):
        sl = slice(hd * RET_QK_DIM, (hd + 1) * RET_QK_DIM)
        qh = qr[:, sl]
        kh = kr[:, sl]
        qr_ref[:, sl] = (qh * cos2 + pltpu.roll(qh, RET_QK_DIM // 2, 1) * sin2).astype(BF16)
        kr_ref[:, sl] = (kh * cos2 + pltpu.roll(kh, RET_QK_DIM // 2, 1) * sin2) * (RET_QK_DIM ** -0.5)
    vr_ref[...] = proj(_C_VR, RET_V_W).astype(BF16)
    gr_ref[...] = proj(_C_GR, RET_V_W)
    gta_ref[...] = jax.nn.sigmoid(proj(_C_GTA, D_MODEL))
    gtr_ref[...] = jax.nn.sigmoid(proj(_C_GTR, D_MODEL))


def _input_projection(x2d, norm_w, w_cat, inv2):
    S = x2d.shape[0]
    tm = IN_TM
    vt_rows = NSA_KV_GROUPS * PV_ROWS
    outs = [
        (NSA_Q_W, BF16, True),
        (LANE, F32, False), (LANE, F32, False),
        (LANE, BF16, False), (vt_rows, BF16, True),
        (LANE, BF16, False), (vt_rows, BF16, True),
        (LANE, F32, True),
        (RET_QK_W, BF16, False), (RET_QK_W, F32, False),
        (RET_V_W, BF16, False), (RET_V_W, F32, False),
        (D_MODEL, F32, False), (D_MODEL, F32, False),
    ]
    row = lambda w: pl.BlockSpec((tm, w), lambda i: (i, 0))
    col = lambda w: pl.BlockSpec((w, tm), lambda i: (0, i))
    return pl.pallas_call(
        _inproj_kernel,
        grid=(S // tm,),
        in_specs=[row(D_MODEL), _const_spec((1, D_MODEL)), _const_spec((D_MODEL, _C_END)),
                  _const_spec((1, RET_QK_DIM))],
        out_specs=[col(w) if t else row(w) for w, _, t in outs],
        out_shape=[jax.ShapeDtypeStruct((w, S) if t else (S, w), dt) for w, dt, t in outs],
        compiler_params=_params("parallel"),
        name="in_proj",
    )(x2d, norm_w, w_cat, inv2)


def _compress_kernel(x_ref, pos_ref, w1_ref, w2_ref, out_ref, hid_ref):
    n_tok = out_ref.shape[0]
    hid_ref[...] = jnp.zeros_like(hid_ref)

    def body(l, _):
        rows = x_ref[pl.ds(l, n_tok, stride=CMP_STRIDE), :] + pos_ref[pl.ds(l, 1), :]
        hid_ref[...] += _dot(rows.astype(BF16), w1_ref[l])
        return 0

    lax.fori_loop(0, CMP_LEN, body, 0)
    out_ref[...] = _dot(jax.nn.gelu(hid_ref[...]).astype(BF16), w2_ref[...])


def _compress(x, pos, w1, w2):
    S, width = x.shape
    n_tok = S // CMP_STRIDE
    G = NSA_KV_GROUPS
    def block_diag(w):
        w = w.astype(BF16)
        z = jnp.zeros_like(w)
        return jnp.concatenate([jnp.concatenate([w if k == g else z for k in range(G)], axis=-1)
                                for g in range(G)], axis=-2)

    w1_bd = block_diag(w1.reshape(CMP_LEN, NSA_HEAD_DIM, CMP_HIDDEN))
    w2_bd = block_diag(w2)
    x_pad = jnp.pad(x, ((0, CMP_LEN - CMP_STRIDE), (0, 0)))
    full = lambda a: _const_spec(a.shape)
    pos2 = jnp.concatenate([pos] * G, axis=1)
    return pl.pallas_call(
        _compress_kernel,
        grid=(1,),
        in_specs=[full(x_pad), full(pos2), full(w1_bd), full(w2_bd)],
        out_specs=pl.BlockSpec((n_tok, width), lambda i: (0, 0)),
        out_shape=jax.ShapeDtypeStruct((n_tok, width), F32),
        scratch_shapes=[pltpu.VMEM((n_tok, G * CMP_HIDDEN), F32)],
        compiler_params=_params("arbitrary"),
        name="nsa_compress",
    )(x_pad, pos2, w1_bd, w2_bd)


def _t5_bucket_np(dist):
    dist = np.maximum(dist, 0)
    max_exact = REL_BUCKETS // 2
    d_f = np.maximum(dist, 1).astype(np.float32)
    large = max_exact + (np.log(d_f / np.float32(max_exact)) / np.float32(math.log(REL_MAX_DIST / max_exact))
                         * np.float32(REL_BUCKETS - max_exact)).astype(np.int32)
    large = np.minimum(large, REL_BUCKETS - 1)
    return np.where(dist < max_exact, dist, large)


def _bias_table_t(rel_bias, dist, shift):
    bucket = jnp.asarray(np.where(dist >= 0, _t5_bucket_np(dist), -1), jnp.int32)
    rb = rel_bias - rel_bias[REL_BUCKETS - 1][None, :] if shift else rel_bias
    onehot = (bucket[None] == jnp.arange(REL_BUCKETS, dtype=jnp.int32)[:, None, None]).astype(F32)
    tab = jnp.einsum('bh,bkq->hkq', rb, onehot, precision=lax.Precision.HIGHEST)
    tab = jnp.where(jnp.asarray(dist == -1)[None], NEG, tab)
    K, Q = dist.shape
    return tab.reshape(NSA_KV_GROUPS, NSA_HPG, K, Q).transpose(0, 2, 1, 3).reshape(NSA_KV_GROUPS, K, NSA_HPG * Q)


def _nsa_kernel(n_keys, q_ref, ga_ref, kc_ref, vct_ref, tct_ref, ksel_ref, vselt_ref, et_ref, tnt_ref,
                kwin_ref, vwint_ref, twt_ref, out_ref, sc_ref, ps_ref, sa_ref, sb_ref, ma_ref, mb_ref):
    i = pl.program_id(0)
    dh = NSA_HEAD_DIM
    nb = (ps_ref.shape[1] - 2 * SCR_PAD) // (SEL_BLOCK // CMP_STRIDE)

    @pl.when(i == 0)
    def _():
        sc_ref[...] = jnp.zeros_like(sc_ref)
        ps_ref[...] = jnp.zeros_like(ps_ref)

    def col_reduce(op, red, x):
        rows = x.shape[0]
        while rows <= 256 and rows % 16 == 0:
            rows //= 2
            x = op(x[:rows], x[rows:])
        return red(x, axis=0, keepdims=True)

    def colmax(x):
        return col_reduce(jnp.maximum, jnp.max, x)

    def tile4(x):
        return jnp.concatenate([x] * NSA_HPG, axis=1)

    gates = ga_ref[...]
    tl = i // QB_PER_CMP_TILE
    sub = i % QB_PER_CMP_TILE
    tp = tl - 1
    groups = range(NSA_KV_GROUPS)
    q_t_g = [jnp.concatenate([q_ref[NSA_HPG * g + j] for j in range(NSA_HPG)], axis=1) for g in groups]
    zeros = jnp.zeros((dh, QW), BF16)
    wq_g = [jnp.concatenate([q_t_g[g] if k == g else zeros for k in groups], axis=0) for g in groups]

    def cmp_scores(g, t):
        return _dot(kc_ref[pl.ds(pl.multiple_of(t * CMP_TILE, CMP_TILE), CMP_TILE), :], wq_g[g])

    def sc_rows(t):
        return pl.ds(pl.multiple_of(SCR_PAD + t * CMP_TILE, 8), CMP_TILE)

    def cmp_far(t, m):
        out = []
        for g in groups:
            s = cmp_scores(g, t)
            sc_ref[g, sc_rows(t), :] = s
            out.append(jnp.maximum(m[g], colmax(s)))
        return tuple(out)

    def tile_loop(count, step, carry):
        first, left = 0, count
        for per_trip in CMP_TRIP_TILES:
            trips = left // per_trip

            def body(u, carry, first=first, per_trip=per_trip):
                for k in range(per_trip):
                    carry = step(first + per_trip * u + k, carry)
                return carry

            carry = lax.fori_loop(0, trips, body, carry)
            first, left = first + per_trip * trips, left - per_trip * trips
        return carry

    m_c = tile_loop(jnp.maximum(tp, 0), cmp_far, tuple(jnp.full((1, QW), NEG, F32) for _ in groups))
    tpc = jnp.maximum(tp, 0)
    off = CMP_TILE - CMP_TOK_PER_QB - CMP_TOK_PER_QB * sub
    m_c = list(m_c)
    for g in groups:
        s = (cmp_scores(g, tpc) + tct_ref[g, pl.ds(pl.multiple_of(off, 8), CMP_TILE), :]
             + jnp.where(tp < 0, NEG, 0.0))
        sc_ref[g, sc_rows(tpc), :] = s
        m_c[g] = jnp.maximum(m_c[g], colmax(s))
    for g in groups:
        s = cmp_scores(g, tl) + tct_ref[g, pl.ds(pl.multiple_of(off + CMP_TILE, 8), CMP_TILE), :]
        sc_ref[g, sc_rows(tl), :] = s
        m_c[g] = jnp.maximum(m_c[g], colmax(s))

    def cmp_exp(t, carry):
        out = []
        for g in groups:
            l, acc = carry[g]
            e = jnp.exp(sc_ref[g, sc_rows(t), :] - m_c[g])
            sc_ref[g, sc_rows(t), :] = e
            vt = vct_ref[g, :, pl.ds(pl.multiple_of(t * CMP_TILE, CMP_TILE), CMP_TILE)]
            out.append((l + col_reduce(jnp.add, jnp.sum, e), acc + _dot(vt, e.astype(BF16))))
        return tuple(out)

    cmp_out = tile_loop(tl + 1, cmp_exp,
                        tuple((jnp.zeros((1, QW), F32), jnp.zeros((dh, QW), F32)) for _ in groups))
    inv_c = [jnp.where(m_c[g] > 0.5 * NEG, 1.0 / jnp.maximum(cmp_out[g][0], 1e-30), 0.0) for g in groups]
    o_c_g = [cmp_out[g][1] * inv_c[g] for g in groups]

    def cmp_psum(t, _):
        for g in groups:
            p = sc_ref[g, sc_rows(t), :] * inv_c[g]
            ps = p[:, 0:Q_BLOCK]
            for j in range(1, NSA_HPG):
                ps = ps + p[:, j * Q_BLOCK:(j + 1) * Q_BLOCK]
            ps_ref[g, sc_rows(t), :] = ps.astype(BF16).astype(F32)
        return 0

    tile_loop(tl + 1, cmp_psum, 0)
    per_blk = SEL_BLOCK // CMP_STRIDE
    imp_g = []
    for g in groups:
        imp = ps_ref[g, pl.ds(SCR_PAD - 1, nb, stride=per_blk), :]
        for k in range(1, CMP_LEN // CMP_STRIDE + per_blk - 1):
            imp = imp + ps_ref[g, pl.ds(SCR_PAD - 1 + k, nb, stride=per_blk), :]
        imp_g.append(imp)

    kw0 = pl.multiple_of(jnp.maximum(i * Q_BLOCK - WINDOW, 0), Q_BLOCK)
    tw0 = pl.multiple_of(jnp.maximum(WINDOW - i * Q_BLOCK, 0), Q_BLOCK)
    o_w_g = []
    for g in groups:
        s = _dot(kwin_ref[pl.ds(kw0, WIN_KEYS), :], wq_g[g]) + twt_ref[g, pl.ds(tw0, WIN_KEYS), :]
        e = jnp.exp(s - colmax(s)).astype(BF16)
        acc_w = _dot(vwint_ref[g, :, pl.ds(kw0, WIN_KEYS)], e)
        o_w_g.append(acc_w[0:dh] * (1.0 / jnp.maximum(acc_w[dh:dh + 1], 1e-30)))

    gated_cw_g = [[gates[NSA_HPG * g + j:NSA_HPG * g + j + 1] * o_c_g[g][:, j * Q_BLOCK:(j + 1) * Q_BLOCK]
                   + gates[2 * NSA_HEADS + NSA_HPG * g + j:2 * NSA_HEADS + NSA_HPG * g + j + 1]
                   * o_w_g[g][:, j * Q_BLOCK:(j + 1) * Q_BLOCK] for j in range(NSA_HPG)] for g in groups]

    n_io = lax.broadcasted_iota(jnp.int32, (nb, Q_BLOCK), 0)
    r_io = lax.broadcasted_iota(jnp.int32, (nb, Q_BLOCK), 1)
    back = (i * Q_BLOCK + r_io) // SEL_BLOCK - n_io
    valid = back >= 0
    forced = (n_io == 0) | (valid & (back < N_LOCAL_SEL))
    work_g = [jnp.where(forced, -jnp.inf, jnp.where(valid, imp_g[g], -1e9)) for g in groups]
    n_f = n_io.astype(F32)
    def select_rounds(rows):
        def run(*work):
            work = [w[:rows] for w in work]
            for _ in range(SEL_TOP_N - 1 - N_LOCAL_SEL):
                for g in groups:
                    mx = colmax(work[g])
                    first = col_reduce(jnp.minimum, jnp.min, jnp.where(work[g] == mx, n_f[:rows], float(nb)))
                    work[g] = jnp.where(n_f[:rows] == first, -jnp.inf, work[g])
            rest = [jnp.zeros((nb - rows, Q_BLOCK), F32)] if rows < nb else []
            return tuple(jnp.concatenate([jnp.where(w == -jnp.inf, 1.0, 0.0)] + rest, axis=0) for w in work)
        return run

    levels = [SEL_ROWS_STEP * (k + 1) for k in range(nb // SEL_ROWS_STEP)]
    level = jnp.minimum((i * Q_BLOCK + Q_BLOCK - 1) // (SEL_BLOCK * SEL_ROWS_STEP), len(levels) - 1)
    sel_g = list(lax.switch(level, [select_rounds(rows) for rows in levels], *work_g))

    far_blocks = (i - 1) * (Q_BLOCK // SEL_BLOCK)
    selm_g = [(jnp.where(n_io < far_blocks, sel_g[g], 0.0) - 1.0).astype(BF16) for g in groups]

    far_end = (i - 1) * Q_BLOCK
    n_far = (jnp.maximum(far_end, 0) + SEL_TK - 1) // SEL_TK

    k0n = pl.multiple_of(jnp.maximum(far_end, 0), Q_BLOCK)
    t0n = pl.multiple_of(jnp.maximum(-far_end, 0), Q_BLOCK)
    near_blocks = 2 * Q_BLOCK // SEL_BLOCK
    pb = lax.broadcasted_iota(jnp.int32, (16, nb), 0)
    pn = lax.broadcasted_iota(jnp.int32, (16, nb), 1)
    pick_rows = jnp.where((pn == jnp.maximum(far_blocks, 0) + pb) & (pb < near_blocks), 1.0, 0.0).astype(BF16)
    s_near_g = []
    for g in groups:
        sel_rows = _dot(pick_rows, sel_g[g].astype(BF16))
        add_near = jnp.concatenate(
            [jnp.broadcast_to((sel_rows[b:b + 1, :] - 1.0) * (-NEG), (SEL_BLOCK, Q_BLOCK)) for b in range(near_blocks)],
            axis=0)
        s_near_g.append(_dot(ksel_ref[pl.ds(k0n, 2 * Q_BLOCK), :], wq_g[g])
                        + tnt_ref[g, pl.ds(t0n, 2 * Q_BLOCK), :] + tile4(add_near))


    def flash_step(carry, s, vt):
        m_i, acc = carry
        m_new = jnp.maximum(m_i, colmax(s))
        alpha = jnp.exp(m_i - m_new)
        p = jnp.exp(s - m_new).astype(BF16)
        return m_new, alpha * acc + _dot(vt, p)

    carry = tuple((jnp.full((1, QW), NEG, F32), jnp.zeros((PV_ROWS, QW), F32)) for _ in groups)
    for w in range(nb // SEL_WIN):
        w_far = [jnp.concatenate([tile4(selm_g[g][w * SEL_WIN:(w + 1) * SEL_WIN]), wq_g[g]], axis=0)
                 for g in groups]

        lo = w * SEL_WIN_TILES
        last = min((w + 1) * SEL_WIN_TILES, n_keys // SEL_TK) - 1
        trips = jnp.maximum(jnp.minimum(n_far, last + 1) - lo + 1, 0) // 2

        def far_scores(kt, s_ref, mx_ref, g, w=w, w_far=w_far):
            k0 = pl.multiple_of(kt * SEL_TK, SEL_TK)
            e0 = pl.multiple_of((kt - w * SEL_WIN_TILES) * SEL_TK, SEL_TK)
            lhs = jnp.concatenate([et_ref[pl.ds(e0, SEL_TK), :],
                                   ksel_ref[pl.ds(k0, SEL_TK), :]], axis=1)
            s = _dot(lhs, w_far[g])
            s_ref[g] = s
            mx_ref[g] = colmax(s)

        def far_update(kt, s_ref, mx_ref, g, carry_g):
            k0 = pl.multiple_of(kt * SEL_TK, SEL_TK)
            m_i, acc = carry_g
            m_new = jnp.maximum(m_i, mx_ref[g])
            p = jnp.exp(s_ref[g] - m_new).astype(BF16)
            return m_new, jnp.exp(m_i - m_new) * acc + _dot(vselt_ref[g, :, pl.ds(k0, SEL_TK)], p)

        def far_trip(tiles, first, last=last, far_scores=far_scores):
            def body(u, carry):
                kt = first + tiles * u
                carry = list(carry)
                bufs = [(sa_ref, ma_ref), (sb_ref, mb_ref)]
                for t in range(tiles):
                    for g in groups:
                        far_scores(jnp.minimum(kt + t + 1, last), *bufs[(t + 1) % 2], g)
                        carry[g] = far_update(kt + t, *bufs[t % 2], g, carry[g])
                return tuple(carry)
            return body

        @pl.when(trips > 0)
        def _(lo=lo, far_scores=far_scores):
            for g in groups:
                far_scores(jnp.int32(lo), sa_ref, ma_ref, g)

        first, left = lo, trips
        for tiles in FAR_TRIP_TILES:
            count = left // (tiles // 2)
            carry = lax.fori_loop(0, count, far_trip(tiles, first), carry)
            first, left = first + tiles * count, left - count * (tiles // 2)

    heads_out = []
    for g in groups:
        m_s, acc_s = flash_step(carry[g], s_near_g[g], vselt_ref[g, :, pl.ds(k0n, 2 * Q_BLOCK)])
        o_s = acc_s[0:dh] * (1.0 / jnp.maximum(acc_s[dh:dh + 1], 1e-30))
        for j in range(NSA_HPG):
            hd = NSA_HPG * g + j
            cols = slice(j * Q_BLOCK, (j + 1) * Q_BLOCK)
            heads_out.append(gated_cw_g[g][j] + gates[NSA_HEADS + hd:NSA_HEADS + hd + 1] * o_s[:, cols])

    out_ref[...] = jnp.concatenate(heads_out, axis=0).T.astype(out_ref.dtype)


def _nsa_attention(q_t, gates_t, kc, vct, tct, ksel, vselt, et, tnt, kwin, vwint, twt):
    H, dh, S = q_t.shape
    n_qb = S // Q_BLOCK
    n_tok = S // CMP_STRIDE
    nb = -(-(S // SEL_BLOCK) // SEL_WIN) * SEL_WIN
    ps_rows = nb * (SEL_BLOCK // CMP_STRIDE) + 2 * SCR_PAD
    full = lambda a: _const_spec(a.shape)
    return pl.pallas_call(
        functools.partial(_nsa_kernel, S),
        grid=(n_qb,),
        in_specs=[pl.BlockSpec((H, dh, Q_BLOCK), lambda i: (0, 0, i)),
                  pl.BlockSpec((gates_t.shape[0], Q_BLOCK), lambda i: (0, i)),
                  full(kc), full(vct), full(tct), full(ksel), full(vselt), full(et), full(tnt),
                  full(kwin), full(vwint), full(twt)],
        out_specs=pl.BlockSpec((Q_BLOCK, H * dh), lambda i: (i, 0)),
        out_shape=jax.ShapeDtypeStruct((S, H * dh), BF16),
        scratch_shapes=[pltpu.VMEM((NSA_KV_GROUPS, SCR_PAD + n_tok, QW), F32),
                        pltpu.VMEM((NSA_KV_GROUPS, ps_rows, Q_BLOCK), F32),
                        pltpu.VMEM((NSA_KV_GROUPS, SEL_TK, QW), F32),
                        pltpu.VMEM((NSA_KV_GROUPS, SEL_TK, QW), F32),
                        pltpu.VMEM((NSA_KV_GROUPS, 1, QW), F32),
                        pltpu.VMEM((NSA_KV_GROUPS, 1, QW), F32)],
        compiler_params=_params("arbitrary"),
        name="nsa_attention",
    )(q_t, gates_t, kc, vct, tct, ksel, vselt, et, tnt, kwin, vwint, twt)


def _retention_kernel(q_ref, k_ref, v_ref, g_ref, nw_ref, dmask_ref, qd_ref, kd_ref, cd_ref, out_ref, state_ref):
    @pl.when(pl.program_id(0) == 0)
    def _():
        state_ref[...] = jnp.zeros_like(state_ref)

    states = [state_ref[hd] for hd in range(RET_HEADS)]
    for c in range(q_ref.shape[0] // RET_CHUNK):
        rows = slice(c * RET_CHUNK, (c + 1) * RET_CHUNK)
        for hd in range(RET_HEADS):
            ks = slice(hd * RET_QK_DIM, (hd + 1) * RET_QK_DIM)
            vs = slice(hd * RET_V_DIM, (hd + 1) * RET_V_DIM)
            q = q_ref[rows, ks]
            k = k_ref[rows, ks]
            v = v_ref[rows, vs]
            state = states[hd]
            inner = _dot_nt(q, k.astype(BF16)) * dmask_ref[hd]
            y = _dot(inner.astype(BF16), v) + _dot(q, state.astype(BF16)) * qd_ref[hd]
            kd = k * kd_ref[hd]
            states[hd] = state * cd_ref[hd] + _dot(kd.T.astype(BF16), v)
            mu = jnp.mean(y, axis=-1, keepdims=True)
            yc = y - mu
            var = jnp.mean(yc * yc, axis=-1, keepdims=True)
            yn = yc * lax.rsqrt(var + EPS) * nw_ref[:, vs]
            out_ref[rows, vs] = (jax.nn.silu(g_ref[rows, vs]) * yn).astype(out_ref.dtype)
    for hd in range(RET_HEADS):
        state_ref[hd] = states[hd]


RET_STEP_CHUNKS = 8


def _retention(q, k, v, g, norm_w, dmask, qd, kd, cd):
    S = q.shape[0]
    C = RET_CHUNK * RET_STEP_CHUNKS
    row = lambda w: pl.BlockSpec((C, w), lambda i: (i, 0))
    full = lambda a: _const_spec(a.shape)
    return pl.pallas_call(
        _retention_kernel,
        grid=(S // C,),
        in_specs=[row(RET_QK_W), row(RET_QK_W), row(RET_V_W), row(RET_V_W),
                  full(norm_w), full(dmask), full(qd), full(kd), full(cd)],
        out_specs=row(RET_V_W),
        out_shape=jax.ShapeDtypeStruct((S, RET_V_W), BF16),
        scratch_shapes=[pltpu.VMEM((RET_HEADS, RET_QK_DIM, RET_V_DIM), F32)],
        compiler_params=_params("arbitrary"),
        name="retention",
    )(q, k, v, g, norm_w, dmask, qd, kd, cd)


MERGE_TM = 512
MERGE_SUB_ROWS = 256
GRP_LANE = 40
RANK_LANE = 41


def _merge_kernel(x_ref, ya_ref, yr_ref, gta_ref, gtr_ref, wa_ref, wr_ref, wo_ref, nw_ref, wrt_ref, brt_ref,
                  tri_ref, x1_ref, h_ref, comb_ref, cnt_ref, carry_ref):
    @pl.when(pl.program_id(0) == 0)
    def _():
        carry_ref[...] = jnp.zeros_like(carry_ref)

    sub = tri_ref.shape[0]
    carry = carry_ref[...]
    for r in range(x_ref.shape[0] // sub):
        carry = _merge_rows(slice(r * sub, (r + 1) * sub), carry, x_ref, ya_ref, yr_ref, gta_ref, gtr_ref,
                            wa_ref, wr_ref, wo_ref, nw_ref, wrt_ref, brt_ref, tri_ref, x1_ref, h_ref, comb_ref)
    carry_ref[...] = carry
    cnt_ref[...] = jnp.broadcast_to(carry, cnt_ref.shape)


def _merge_rows(rows, carry, x_ref, ya_ref, yr_ref, gta_ref, gtr_ref, wa_ref, wr_ref, wo_ref, nw_ref, wrt_ref,
                brt_ref, tri_ref, x1_ref, h_ref, comb_ref):
    merged = gta_ref[rows] * _dot(ya_ref[rows], wa_ref[...]) + gtr_ref[rows] * _dot(yr_ref[rows], wr_ref[...])
    x1 = x_ref[rows] + _dot(merged.astype(BF16), wo_ref[...])
    x1_ref[rows] = x1
    h = (x1 * lax.rsqrt(jnp.mean(x1 * x1, axis=-1, keepdims=True) + EPS) * nw_ref[...]).astype(BF16)
    h_ref[rows] = _pack_bf16_pairs(h.astype(F32))
    logits = _dot(h, wrt_ref[...]) + brt_ref[...]
    lane = lax.broadcasted_iota(jnp.int32, logits.shape, 1)
    lane_f = lane.astype(F32)
    big = float(LANE)
    is_grp = (lane >= N_EXPERTS) & (lane < N_EXPERTS + N_GROUPS)
    lg = jnp.where(is_grp, logits, -jnp.inf)
    eg = jnp.where(is_grp, jnp.exp(lg - jnp.max(lg, axis=-1, keepdims=True)), 0.0)
    pg = eg / jnp.sum(eg, axis=-1, keepdims=True)
    p_grp = jnp.max(pg, axis=-1, keepdims=True)
    grp = jnp.min(jnp.where(is_grp & (pg == p_grp), lane_f, big), axis=-1, keepdims=True) - N_EXPERTS
    in_grp = (lane_f >= grp * EXPERTS_PER_GROUP) & (lane_f < (grp + 1.0) * EXPERTS_PER_GROUP)
    le = jnp.where(in_grp, logits, -jnp.inf)
    m1 = jnp.max(le, axis=-1, keepdims=True)
    i1 = jnp.min(jnp.where(le == m1, lane_f, big), axis=-1, keepdims=True)
    le2 = jnp.where(lane_f == i1, -jnp.inf, le)
    m2 = jnp.max(le2, axis=-1, keepdims=True)
    i2 = jnp.min(jnp.where(le2 == m2, lane_f, big), axis=-1, keepdims=True)
    e2 = jnp.exp(m2 - m1)
    den = 1.0 + e2
    onehot = jnp.where(lane_f == grp, 1.0, 0.0)
    before = _dot(tri_ref[...], onehot.astype(BF16)) + carry
    rank = jnp.sum(onehot * before, axis=-1, keepdims=True)
    comb_ref[rows] = (jnp.where(lane_f == i1, (1.0 / den) * p_grp, 0.0)
                      + jnp.where(lane_f == i2, (e2 / den) * p_grp, 0.0)
                      + jnp.where(lane == GRP_LANE, grp, 0.0)
                      + jnp.where(lane == RANK_LANE, rank, 0.0))
    return carry + jnp.sum(onehot, axis=0, keepdims=True)


def _pack_bf16_pairs(x):
    n = x.shape[1] // 2
    return pltpu.pack_elementwise([x[:, :n], x[:, n:]], packed_dtype=BF16)


def _unpack_bf16_pairs(w):
    return tuple(pltpu.unpack_elementwise(w, index=k, packed_dtype=BF16, unpacked_dtype=F32) for k in range(2))


def _merge(x2d, ya, yr, gta, gtr, wa, wr, wo, nw, wrt, brt):
    S = x2d.shape[0]
    tm = MERGE_TM
    row = lambda w: pl.BlockSpec((tm, w), lambda i: (i, 0))
    full = lambda a: _const_spec(a.shape)
    sub = MERGE_SUB_ROWS
    tri = jnp.asarray(np.tril(np.ones((sub, sub), np.float32), -1), BF16)
    return pl.pallas_call(
        _merge_kernel,
        grid=(S // tm,),
        in_specs=[row(D_MODEL), row(NSA_Q_W), row(RET_V_W), row(D_MODEL), row(D_MODEL),
                  full(wa), full(wr), full(wo), full(nw), full(wrt), full(brt), full(tri)],
        out_specs=[row(D_MODEL), row(D_MODEL // 2), row(LANE), pl.BlockSpec((8, LANE), lambda i: (0, 0))],
        out_shape=[jax.ShapeDtypeStruct((S, D_MODEL), F32), jax.ShapeDtypeStruct((S, D_MODEL // 2), jnp.uint32),
                   jax.ShapeDtypeStruct((S, LANE), F32), jax.ShapeDtypeStruct((8, LANE), F32)],
        scratch_shapes=[pltpu.VMEM((1, LANE), F32)],
        compiler_params=_params("arbitrary"),
        name="merge_router",
    )(x2d, ya, yr, gta, gtr, wa, wr, wo, nw, wrt, brt, tri)


MOE_TM = 1024
MOE_SUB_TILES = 2
PERMUTE_TM = 512
HALF = D_MODEL // 2


def _permute_kernel(pos_ref, h_ref, comb_ref, hs_ref, cs_ref):
    i = pl.program_id(0)

    @pl.when(i == 0)
    def _():
        hs_ref[...] = jnp.zeros_like(hs_ref)
        cs_ref[...] = jnp.zeros_like(cs_ref)

    tm = h_ref.shape[0]

    def body(r, _):
        d = pos_ref[i * tm + r]
        hs_ref[pl.ds(d, 1), :] = h_ref[pl.ds(r, 1), :]
        cs_ref[pl.ds(d, 1), :] = comb_ref[pl.ds(r, 1), :]
        return 0

    lax.fori_loop(0, tm, body, 0, unroll=8)


def _permute(pos, hu, comb, n_sorted):
    S = hu.shape[0]
    tm = PERMUTE_TM
    resident = lambda w: pl.BlockSpec((n_sorted, w), lambda i, p: (0, 0), pipeline_mode=pl.Buffered(1))
    return pl.pallas_call(
        _permute_kernel,
        grid_spec=pltpu.PrefetchScalarGridSpec(
            num_scalar_prefetch=1, grid=(S // tm,),
            in_specs=[pl.BlockSpec((tm, HALF), lambda i, p: (i, 0)), pl.BlockSpec((tm, LANE), lambda i, p: (i, 0))],
            out_specs=[resident(HALF), resident(LANE)]),
        out_shape=[jax.ShapeDtypeStruct((n_sorted, HALF), jnp.uint32), jax.ShapeDtypeStruct((n_sorted, LANE), F32)],
        compiler_params=_params("arbitrary"),
        name="moe_permute",
    )(pos, hu, comb)


def _moe_kernel(tg_ref, tv_ref, hs_ref, cs_ref, wg_ref, wu_ref, wd_ref, ys_ref, lo_ref, hi_ref, acc_ref):
    i = pl.program_id(0)
    e = pl.program_id(1)

    @pl.when(e == 0)
    def _():
        lo, hi = _unpack_bf16_pairs(hs_ref[...])
        lo_ref[...] = lo.astype(BF16)
        hi_ref[...] = hi.astype(BF16)
        acc_ref[...] = jnp.zeros_like(acc_ref)

    @pl.when(tv_ref[i] > 0)
    def _():
        wg = wg_ref[0].astype(BF16)
        wu = wu_ref[0].astype(BF16)
        wd = wd_ref[0].astype(BF16)
        sub = lo_ref.shape[0] // MOE_SUB_TILES
        for r in range(MOE_SUB_TILES):
            rows = slice(r * sub, (r + 1) * sub)
            lo, hi = lo_ref[rows], hi_ref[rows]
            gate = _dot(lo, wg[:HALF]) + _dot(hi, wg[HALF:])
            up = _dot(lo, wu[:HALF]) + _dot(hi, wu[HALF:])
            comb = cs_ref[rows]
            lane = lax.broadcasted_iota(jnp.int32, comb.shape, 1)
            c = jnp.sum(jnp.where(lane == tg_ref[i] * EXPERTS_PER_GROUP + e, comb, 0.0), axis=-1, keepdims=True)
            acc_ref[rows] += _dot((jax.nn.silu(gate) * up).astype(BF16), wd) * c

    @pl.when(e == pl.num_programs(1) - 1)
    def _():
        ys_ref[...] = _pack_bf16_pairs(acc_ref[...].astype(BF16).astype(F32))


def _moe(tile_group, tile_valid, hs, cs, wg, wu, wd):
    n_sorted = hs.shape[0]
    tm = MOE_TM
    row = lambda w: pl.BlockSpec((tm, w), lambda i, e, tg, tv: (i, 0))
    wspec = lambda a, b: pl.BlockSpec((1, a, b), lambda i, e, tg, tv: (tg[i] * EXPERTS_PER_GROUP + e, 0, 0))
    return pl.pallas_call(
        _moe_kernel,
        grid_spec=pltpu.PrefetchScalarGridSpec(
            num_scalar_prefetch=2, grid=(n_sorted // tm, EXPERTS_PER_GROUP),
            in_specs=[row(HALF), row(LANE), wspec(D_MODEL, EXPERT_FF), wspec(D_MODEL, EXPERT_FF),
                      wspec(EXPERT_FF, D_MODEL)],
            out_specs=row(HALF),
            scratch_shapes=[pltpu.VMEM((tm, HALF), BF16), pltpu.VMEM((tm, HALF), BF16),
                            pltpu.VMEM((tm, D_MODEL), F32)]),
        out_shape=jax.ShapeDtypeStruct((n_sorted, HALF), jnp.uint32),
        compiler_params=_params("parallel", "arbitrary"),
        name="moe_experts",
    )(tile_group, tile_valid, hs, cs, wg, wu, wd)


def _unpermute_kernel(pos_ref, ys_ref, x1_ref, nw_ref, out_ref, stage_ref):
    i = pl.program_id(0)
    tm = x1_ref.shape[0]

    def body(r, _):
        stage_ref[pl.ds(r, 1), :] = ys_ref[pl.ds(pos_ref[i * tm + r], 1), :]
        return 0

    lax.fori_loop(0, tm, body, 0, unroll=8)
    lo, hi = _unpack_bf16_pairs(stage_ref[...])
    x2 = x1_ref[...] + jnp.concatenate([lo, hi], axis=1)
    out_ref[...] = x2 * lax.rsqrt(jnp.mean(x2 * x2, axis=-1, keepdims=True) + EPS) * nw_ref[...]


def _unpermute_residual_norm(pos, ys, x1, nw):
    S = x1.shape[0]
    tm = PERMUTE_TM
    return pl.pallas_call(
        _unpermute_kernel,
        grid_spec=pltpu.PrefetchScalarGridSpec(
            num_scalar_prefetch=1, grid=(S // tm,),
            in_specs=[pl.BlockSpec(ys.shape, lambda i, p: (0, 0), pipeline_mode=pl.Buffered(1)),
                      pl.BlockSpec((tm, D_MODEL), lambda i, p: (i, 0)),
                      pl.BlockSpec((1, D_MODEL), lambda i, p: (0, 0))],
            out_specs=pl.BlockSpec((tm, D_MODEL), lambda i, p: (i, 0)),
            scratch_shapes=[pltpu.VMEM((tm, HALF), jnp.uint32)]),
        out_shape=jax.ShapeDtypeStruct((S, D_MODEL), F32),
        compiler_params=_params("parallel"),
        name="moe_unpermute_norm",
    )(pos, ys, x1, nw)


def _hierarchical_moe(hu, comb, cnt, x1, wg, wu, wd, final_norm_w):
    S = hu.shape[0]
    tm = MOE_TM
    n_sorted = S + N_GROUPS * tm
    assert RANK_LANE == GRP_LANE + 1
    grp_rank = comb[:, GRP_LANE:RANK_LANE + 1].astype(jnp.int32)
    grp, rank = grp_rank[:, 0], grp_rank[:, 1]
    counts = cnt[0, :N_GROUPS].astype(jnp.int32)
    padded = (counts + tm - 1) // tm * tm
    ends = jnp.cumsum(padded)
    starts = ends - padded
    pos = rank + jnp.sum(jnp.where(grp[:, None] == jnp.arange(N_GROUPS)[None, :], starts[None, :], 0), axis=1)
    tile_start = jnp.arange(n_sorted // tm, dtype=jnp.int32) * tm
    tile_group = jnp.minimum(jnp.sum((tile_start[:, None] >= ends[None, :]).astype(jnp.int32), axis=1), N_GROUPS - 1)
    tile_valid = (tile_start < ends[N_GROUPS - 1]).astype(jnp.int32)
    hs, cs = _permute(pos, hu, comb, n_sorted)
    ys = _moe(tile_group, tile_valid, hs, cs, wg, wu, wd)
    return _unpermute_residual_norm(pos, ys, x1, final_norm_w)


def _positional_tables():
    half = RET_QK_DIM // 2
    inv = ROPE_BASE ** (-jnp.arange(half, dtype=F32) / half)
    inv2 = jnp.concatenate([inv, inv]).reshape(1, RET_QK_DIM)
    C, H = RET_CHUNK, RET_HEADS
    log_gamma = jnp.log(1.0 - 2.0 ** (-5.0 - jnp.arange(H, dtype=F32)))
    idx = jnp.arange(C, dtype=F32)
    diff = idx[:, None] - idx[None, :]
    dmask = jnp.where(diff >= 0, jnp.exp(jnp.maximum(diff, 0.0)[None] * log_gamma[:, None, None]), 0.0)
    qd = jnp.exp((idx + 1.0)[None, :] * log_gamma[:, None])
    kd = jnp.exp((C - 1.0 - idx)[None, :] * log_gamma[:, None])
    cd = jnp.exp(C * log_gamma)
    qd = jnp.broadcast_to(qd[:, :, None], (H, C, RET_V_DIM))
    kd = jnp.broadcast_to(kd[:, :, None], (H, C, RET_QK_DIM))
    cd = jnp.broadcast_to(cd[:, None, None], (H, 1, RET_V_DIM))
    return inv2, dmask, qd, kd, cd


def _attention_tables(rel_bias):
    r = np.arange(Q_BLOCK)[None, :]
    cc = np.arange(3 * CMP_TILE)[:, None] - CMP_TILE
    d = r - CMP_STRIDE * (cc - (CMP_TILE - CMP_TOK_PER_QB)) - (CMP_LEN - 1)
    d = np.where(cc < 0, -2, np.where((cc >= CMP_TILE) | (d < 0), -1, d))
    tct = _bias_table_t(rel_bias, d, shift=True)
    d = r + Q_BLOCK - np.arange(3 * Q_BLOCK)[:, None]
    tnt = _bias_table_t(rel_bias, np.where(d < 0, -1, d), shift=True)
    d = r + WINDOW - np.arange(WIN_KEYS + WINDOW)[:, None]
    twt = _bias_table_t(rel_bias, np.where((d < 0) | (d >= WINDOW), -1, d), shift=False)
    kk = np.arange(SEL_WIN * SEL_BLOCK)[:, None] // SEL_BLOCK
    et = jnp.where(jnp.asarray(kk == np.arange(SEL_WIN)[None, :]), -NEG, 0.0).astype(BF16)
    return tct, tnt, twt, et


def _layer(x2d, rel_bias, norm_mix, w_in, cmp_pos_k, cmp_k_w1, cmp_k_w2, cmp_pos_v, cmp_v_w1, cmp_v_w2,
           ret_norm, w_nsa_up, w_ret_up, w_out, norm_ffn, w_rg, b_rg, w_re, b_re, moe_wg, moe_wu, moe_wd,
           final_norm_w):
    S = x2d.shape[0]
    G, dh = NSA_KV_GROUPS, NSA_HEAD_DIM
    assert S % (CMP_TILE * CMP_STRIDE) == 0 and S // SEL_BLOCK >= SEL_TOP_N

    o = np.cumsum((0,) + (NSA_Q_W, 6 * NSA_KV_W, 3 * NSA_HEADS, RET_QK_W, RET_QK_W, RET_V_W, RET_V_W,
                          D_MODEL, D_MODEL))
    seg = lambda n: w_in[:, o[n]:o[n + 1]]
    w_cat = jnp.concatenate([seg(0), seg(1), seg(3), seg(4), seg(5), seg(6), seg(7), seg(8), seg(2),
                             jnp.zeros((D_MODEL, LANE - 3 * NSA_HEADS), F32)], axis=1).astype(BF16)
    inv2, dmask, qd, kd, cd = _positional_tables()

    (qa_t, kcmp, vcmp, ksel, vsel_t, kwin, vwin_t, ga_t, qr, kr, vr, gr, gta, gtr) = _input_projection(
        x2d, norm_mix.reshape(1, D_MODEL), w_cat, inv2)

    nch = S // CMP_STRIDE
    kc = _compress(kcmp, cmp_pos_k, cmp_k_w1, cmp_k_w2).astype(BF16)
    vct = _compress(vcmp, cmp_pos_v, cmp_v_w1, cmp_v_w2).T.reshape(G, dh, nch).astype(BF16)

    tct, tnt, twt, et = _attention_tables(rel_bias)
    ya = _nsa_attention(qa_t.reshape(NSA_HEADS, dh, S), ga_t, kc, vct, tct,
                        ksel, vsel_t.reshape(G, PV_ROWS, S), et, tnt,
                        kwin, vwin_t.reshape(G, PV_ROWS, S), twt)

    yr = _retention(qr, kr, vr, gr, ret_norm.reshape(1, RET_V_W), dmask, qd, kd, cd)

    wrt = jnp.concatenate([w_re, w_rg, jnp.zeros((D_MODEL, LANE - N_EXPERTS - N_GROUPS), F32)], axis=1).astype(BF16)
    brt = jnp.concatenate([b_re, b_rg, jnp.zeros((LANE - N_EXPERTS - N_GROUPS,), F32)]).reshape(1, LANE)
    x1, h2, comb, cnt = _merge(x2d, ya, yr, gta, gtr, w_nsa_up.astype(BF16), w_ret_up.astype(BF16),
                               w_out.astype(BF16), norm_ffn.reshape(1, D_MODEL), wrt, brt)
    return _hierarchical_moe(h2, comb, cnt, x1, moe_wg, moe_wu, moe_wd, final_norm_w)


def kernel(x, rel_bias, norm_mix, w_in, cmp_pos_k, cmp_k_w1, cmp_k_w2, cmp_pos_v, cmp_v_w1, cmp_v_w2, ret_norm, w_nsa_up, w_ret_up, w_out, norm_ffn, w_router_group, b_router_group, w_router_expert, b_router_expert, moe_w_gate, moe_w_up, moe_w_down, norm_final):
    B, S, D = x.shape
    assert B == 1 and norm_mix.shape[0] == 1, "single sequence, depth 1"
    out = _layer(x[0], rel_bias, norm_mix[0], w_in[0], cmp_pos_k[0], cmp_k_w1[0], cmp_k_w2[0],
                 cmp_pos_v[0], cmp_v_w1[0], cmp_v_w2[0], ret_norm[0], w_nsa_up[0], w_ret_up[0], w_out[0],
                 norm_ffn[0], w_router_group[0], b_router_group[0], w_router_expert[0], b_router_expert[0],
                 moe_w_gate[0], moe_w_up[0], moe_w_down[0], norm_final.reshape(1, D))
    return out[None]
```
